```python
import math
import jax, jax.numpy as jnp
from jax import lax
import numpy as np

D_MODEL = 1024
BATCH = 16
SEQ = 256
DEPTH = 4
DEC_BATCH = 2
DEC_SEQ = 4096
PAST_LEN = 256

GRID_W = 64
HG_HEADS = 4
HG_DK = 128
HG_DV = 128
HG_WIDTH = HG_HEADS * HG_DK
CHUNK = 64
DA_HEADS = 4
DA_QK = 64
DA_V = 2 * DA_QK
DA_WIDTH = DA_HEADS * DA_V
QBLK = 128
ROPE_BASE = 10000.0
MIX_WIDTH = HG_WIDTH + DA_WIDTH
IN_DIM = 5 * HG_WIDTH + 2 * DA_HEADS * 2 * DA_QK + DA_WIDTH
N_EXPERTS = 32
TOP_K = 4
D_FF = D_MODEL
SWIGLU_ALPHA = 1.702
SWIGLU_LIMIT = 7.0
MOE_BLK = 128
EPS = 1e-6

kernel_name = "hymba_hgrn2_diffattn_moe_dit_step"


def rmsnorm(x, w):
    xf = x.astype(jnp.float32)
    y = xf * lax.rsqrt(jnp.mean(xf * xf, axis=-1, keepdims=True) + EPS)
    return (y * w.astype(jnp.float32)).astype(x.dtype)


def adaln_mod(cond, w, b):
    m = jax.nn.silu(cond) @ w + b
    return jnp.split(m[:, None, :], 6, axis=-1)


def split_in_proj(p):
    sizes = [HG_WIDTH] * 5 + [DA_HEADS * 2 * DA_QK] * 2 + [DA_WIDTH]
    idx = [int(v) for v in np.cumsum(sizes)[:-1]]
    return jnp.split(p, idx, axis=-1)


def rope2d(x, rope):
    cos_r, sin_r, cos_c, sin_c = rope
    half = DA_QK // 2

    def rot(xp, cos, sin):
        cos = cos.astype(xp.dtype)
        sin = sin.astype(xp.dtype)
        x1, x2 = jnp.split(xp, 2, axis=-1)
        return jnp.concatenate([x1 * cos - x2 * sin, x2 * cos + x1 * sin], axis=-1)

    return jnp.concatenate([rot(x[..., :half], cos_r, sin_r), rot(x[..., half:], cos_c, sin_c)], axis=-1)


def gla_scan(q, k, v, log_f, s0):
    b, L, h, _ = q.shape
    dv = v.shape[-1]
    n = L // CHUNK

    def chunks(t):
        return jnp.moveaxis(t.astype(jnp.float32).reshape(b, n, CHUNK, h, t.shape[-1]), 1, 0)

    causal = jnp.tril(jnp.ones((CHUNK, CHUNK), bool))[None, :, :, None, None]

    def step(S, inp):
        qc, kc, vc, gc = inp
        A = jnp.cumsum(gc, axis=1)
        A_last = A[:, -1]
        o_inter = jnp.einsum('bthd,bhde->bthe', qc * jnp.exp(A), S)
        decay = jnp.exp(jnp.where(causal, A[:, :, None] - A[:, None, :], -jnp.inf))
        attn = jnp.einsum('bthd,bshd,btshd->bths', qc, kc, decay)
        o_intra = jnp.einsum('bths,bshe->bthe', attn, vc)
        k_dec = kc * jnp.exp(A_last[:, None] - A)
        S_new = jnp.exp(A_last)[..., None] * S + jnp.einsum('bshd,bshe->bhde', k_dec, vc)
        return S_new, o_inter + o_intra

    S_fin, o = lax.scan(step, s0.astype(jnp.float32), (chunks(q), chunks(k), chunks(v), chunks(log_f)))
    o = jnp.moveaxis(o, 0, 1).reshape(b, L, h, dv)
    return o, S_fin


def hgrn_mix(q, i, fx_fw, fx_bw, g, lb, s0_fw, s0_bw, norm_w):
    b, L, _ = q.shape
    hd = lambda t: t.reshape(b, L, HG_HEADS, HG_DK)
    q, i, g = hd(q), hd(i), hd(g)
    out = None
    finals = []
    for fx, lbd, s0, rev in ((fx_fw, lb[0], s0_fw, False), (fx_bw, lb[1], s0_bw, True)):
        lbd = lbd.reshape(HG_HEADS, HG_DK)
        fxf = hd(fx).astype(jnp.float32)
        log_f = jnp.logaddexp(jnp.log(lbd), jnp.log1p(-lbd) + jax.nn.log_sigmoid(fxf))
        k = (1.0 - lbd) * jax.nn.sigmoid(-fxf)
        qd, kd, vd, gd = q, k, i, log_f
        if rev:
            qd, kd, vd, gd = (jnp.flip(t, axis=1) for t in (qd, kd, vd, gd))
        o, s_fin = gla_scan(qd, kd, vd, gd, s0)
        if rev:
            o = jnp.flip(o, axis=1)
        out = o if out is None else out + o
        finals.append(s_fin.astype(q.dtype))
    o = rmsnorm(out, norm_w).astype(q.dtype) * jax.nn.silu(g)
    return o.reshape(b, L, HG_WIDTH), finals


def diff_attention(q, k, v, lam):
    b, Lq, h, _, dqk = q.shape
    nb = Lq // QBLK
    qb = jnp.moveaxis(q.reshape(b, nb, QBLK, h, 2, dqk), 1, 0)
    scale = 1.0 / math.sqrt(dqk)

    def one_block(qblk):
        s = jnp.einsum('bqhmd,bkhmd->bhmqk', qblk, k, preferred_element_type=jnp.float32) * scale
        p = jax.nn.softmax(s, axis=-1)
        w = (p[:, :, 0] - lam * p[:, :, 1]).astype(v.dtype)
        return jnp.einsum('bhqk,bkhe->bqhe', w, v)

    o = lax.map(one_block, qb)
    return jnp.moveaxis(o, 0, 1).reshape(b, Lq, h, v.shape[-1])


def token_mixer(h, w_in_l, lb_l, hg_norm_l, lam, lam_init, da_norm_l, w_out_l, rope, ctx):
    b, L, _ = h.shape
    proj = h @ w_in_l
    q_hg, fx_fw, fx_bw, i_hg, g_hg, q_da, k_da, v_da = split_in_proj(proj)
    q_da = q_da.reshape(b, L, DA_HEADS, 2, DA_QK)
    k_da = k_da.reshape(b, L, DA_HEADS, 2, DA_QK)
    v_da = v_da.reshape(b, L, DA_HEADS, DA_V)
    if ctx is None:
        zeros = jnp.zeros((b, HG_HEADS, HG_DK, HG_DV), jnp.float32)
        o_hg, (s_fw, s_bw) = hgrn_mix(q_hg, i_hg, fx_fw, fx_bw, g_hg, lb_l, zeros, zeros, hg_norm_l)
        keys, vals = k_da, v_da
        new_ctx = (k_da.reshape(b, L, DA_HEADS, 2 * DA_QK), v_da, jnp.stack([s_fw, s_bw], axis=1))
    else:
        k_ctx, v_ctx, s_ctx = ctx
        q_da = rope2d(q_da, rope)
        k_da = rope2d(k_da, rope)
        o_hg, _ = hgrn_mix(q_hg, i_hg, fx_fw, fx_bw, g_hg, lb_l, s_ctx[:, 0], s_ctx[:, 1], hg_norm_l)
        keys = jnp.concatenate([k_da, k_ctx.reshape(b, -1, DA_HEADS, 2, DA_QK)], axis=1)
        vals = jnp.concatenate([v_da, v_ctx], axis=1)
        new_ctx = None
    o_da = diff_attention(q_da, keys, vals, lam)
    o_da = rmsnorm(o_da, da_norm_l) * (1.0 - lam_init)
    out = jnp.concatenate([o_hg, o_da.reshape(b, L, DA_WIDTH)], axis=-1) @ w_out_l
    return out, new_ctx


def moe_ffn(h, w_router_l, b_router_l, w_gu_l, b_gu_l, w_dn_l, b_dn_l):
    b, L, d = h.shape
    n_tok = b * L
    xf = h.reshape(n_tok, d)
    logits = jnp.matmul(xf, w_router_l, preferred_element_type=jnp.float32) + b_router_l.astype(jnp.float32)
    top_val, top_exp = lax.top_k(logits, TOP_K)
    gates = jax.nn.softmax(top_val, axis=-1).astype(h.dtype)
    n_asg = n_tok * TOP_K
    n_blk = -(-n_asg // MOE_BLK) + N_EXPERTS
    flat_e = top_exp.reshape(-1)
    order = jnp.argsort(flat_e)
    sorted_e = flat_e[order]
    tok = order // TOP_K
    counts = jnp.bincount(flat_e, length=N_EXPERTS)
    padded = (counts + MOE_BLK - 1) // MOE_BLK * MOE_BLK
    pad_end = jnp.cumsum(padded)
    pad_start = pad_end - padded
    start = jnp.cumsum(counts) - counts
    dest = pad_start[sorted_e] + jnp.arange(n_asg, dtype=jnp.int32) - start[sorted_e]
    slot_tok = jnp.zeros((n_blk * MOE_BLK,), jnp.int32).at[dest].set(tok.astype(jnp.int32))
    blk_exp = jnp.minimum(jnp.searchsorted(pad_end, jnp.arange(n_blk) * MOE_BLK, side='right'), N_EXPERTS - 1)
    xs = xf[slot_tok].reshape(n_blk, MOE_BLK, d)

    def expert_block(args):
        xb, e = args
        gu = xb @ w_gu_l[e] + b_gu_l[e]
        gate, lin = jnp.split(gu, 2, axis=-1)
        gate = jnp.minimum(gate, SWIGLU_LIMIT)
        lin = jnp.clip(lin, -SWIGLU_LIMIT, SWIGLU_LIMIT)
        act = (lin + 1.0) * gate * jax.nn.sigmoid(SWIGLU_ALPHA * gate)
        return act @ w_dn_l[e] + b_dn_l[e]

    ys = lax.map(expert_block, (xs, blk_exp)).reshape(n_blk * MOE_BLK, d)
    contrib = ys[dest] * gates.reshape(-1)[order][:, None]
    return jax.ops.segment_sum(contrib, tok, num_segments=n_tok).reshape(b, L, d)


def trunk_layer(x, mod, l, lam, lam_init, lb_l, rope, ctx, norm1_w, norm2_w, w_in, hg_norm_w, da_norm_w,
                w_out, w_router, b_router, w_gu, b_gu, w_dn, b_dn):
    sh1, sc1, g1, sh2, sc2, g2 = mod
    h = rmsnorm(x, norm1_w[l]) * (1.0 + sc1) + sh1
    mix, new_ctx = token_mixer(h, w_in[l], lb_l, hg_norm_w[l], lam, lam_init, da_norm_w[l], w_out[l], rope, ctx)
    x = x + g1 * mix
    h = rmsnorm(x, norm2_w[l]) * (1.0 + sc2) + sh2
    x = x + g2 * moe_ffn(h, w_router[l], b_router[l], w_gu[l], b_gu[l], w_dn[l], b_dn[l])
    return x, new_ctx


def setup_inputs(seed: int = 0) -> dict:
    key = jax.random.key(seed)
    ks = jax.random.split(key, 32)
    nrm = lambda k, shape, s: jax.random.normal(k, shape, jnp.float32) * s
    return {
        "x_prompt": nrm(ks[0], (BATCH, SEQ, D_MODEL), 1.0),
        "x_sample": nrm(ks[1], (DEC_BATCH, DEC_SEQ, D_MODEL), 1.0),
        "c": nrm(ks[2], (DEC_BATCH, D_MODEL), 1.0),
        "cache_k": nrm(ks[3], (DEC_BATCH, DEPTH, PAST_LEN, DA_HEADS, 2 * DA_QK), 1.0),
        "cache_v": nrm(ks[4], (DEC_BATCH, DEPTH, PAST_LEN, DA_HEADS, DA_V), 1.0),
        "state_hgrn": nrm(ks[5], (DEC_BATCH, DEPTH, 2, HG_HEADS, HG_DK, HG_DV), 0.5),
        "c_ctx": nrm(ks[6], (D_MODEL,), 1.0),
        "norm1_w": 1.0 + nrm(ks[7], (DEPTH, D_MODEL), 0.05),
        "norm2_w": 1.0 + nrm(ks[8], (DEPTH, D_MODEL), 0.05),
        "w_ada": nrm(ks[9], (DEPTH, D_MODEL, 6 * D_MODEL), 0.5 * D_MODEL ** -0.5),
        "b_ada": nrm(ks[10], (DEPTH, 6 * D_MODEL), 0.02),
        "w_in": nrm(ks[11], (DEPTH, D_MODEL, IN_DIM), D_MODEL ** -0.5),
        "hg_lb": nrm(ks[12], (DEPTH, 2, HG_WIDTH), 1.0),
        "hg_norm_w": 1.0 + nrm(ks[13], (DEPTH, HG_DV), 0.05),
        "lam_q1": nrm(ks[14], (DEPTH, DA_QK), 0.1),
        "lam_k1": nrm(ks[15], (DEPTH, DA_QK), 0.1),
        "lam_q2": nrm(ks[16], (DEPTH, DA_QK), 0.1),
        "lam_k2": nrm(ks[17], (DEPTH, DA_QK), 0.1),
        "da_norm_w": 1.0 + nrm(ks[18], (DEPTH, DA_V), 0.05),
        "w_out": nrm(ks[19], (DEPTH, MIX_WIDTH, D_MODEL), MIX_WIDTH ** -0.5),
        "w_router": nrm(ks[20], (DEPTH, D_MODEL, N_EXPERTS), D_MODEL ** -0.5),
        "b_router": nrm(ks[21], (DEPTH, N_EXPERTS), 0.01),
        "w_gu": nrm(ks[22], (DEPTH, N_EXPERTS, D_MODEL, 2 * D_FF), D_MODEL ** -0.5),
        "b_gu": nrm(ks[23], (DEPTH, N_EXPERTS, 2 * D_FF), 0.01),
        "w_dn": nrm(ks[24], (DEPTH, N_EXPERTS, D_FF, D_MODEL), D_FF ** -0.5),
        "b_dn": nrm(ks[25], (DEPTH, N_EXPERTS, D_MODEL), 0.01),
        "norm_f_w": 1.0 + nrm(ks[26], (D_MODEL,), 0.05),
    }


def reference(x_prompt, x_sample, c, cache_k, cache_v, state_hgrn, c_ctx, norm1_w, norm2_w, w_ada, b_ada,
              w_in, hg_lb, hg_norm_w, lam_q1, lam_k1, lam_q2, lam_k2, da_norm_w, w_out, w_router, b_router,
              w_gu, b_gu, w_dn, b_dn, norm_f_w):
    f32 = jnp.float32
    lb_all = jnp.cumsum(jax.nn.softmax(hg_lb.astype(f32), axis=0), axis=0)
    lb_all = lb_all - lb_all[0:1]
    lams, lam_inits = [], []
    for l in range(DEPTH):
        lam_init = 0.8 - 0.6 * math.exp(-0.3 * l)
        lam = (jnp.exp(jnp.sum(lam_q1[l].astype(f32) * lam_k1[l].astype(f32)))
               - jnp.exp(jnp.sum(lam_q2[l].astype(f32) * lam_k2[l].astype(f32))) + lam_init)
        lams.append(lam)
        lam_inits.append(lam_init)

    x = x_prompt
    ks_l, vs_l, ss_l = [], [], []
    for l in range(DEPTH):
        mod = adaln_mod(c_ctx[None, :], w_ada[l], b_ada[l])
        x, (k_l, v_l, s_l) = trunk_layer(x, mod, l, lams[l], lam_inits[l], lb_all[l], None, None,
                                         norm1_w, norm2_w, w_in, hg_norm_w, da_norm_w, w_out,
                                         w_router, b_router, w_gu, b_gu, w_dn, b_dn)
        ks_l.append(k_l)
        vs_l.append(v_l)
        ss_l.append(s_l)
    y_prompt = rmsnorm(x, norm_f_w)
    new_cache_k = jnp.stack(ks_l, axis=1)
    new_cache_v = jnp.stack(vs_l, axis=1)
    new_state_hgrn = jnp.stack(ss_l, axis=1)

    L = x_sample.shape[1]
    n_rows = L // GRID_W
    row_idx = jnp.repeat(jnp.arange(n_rows), GRID_W).astype(f32)
    col_idx = jnp.tile(jnp.arange(GRID_W), n_rows).astype(f32)
    n_pairs = DA_QK // 4
    inv_freq = ROPE_BASE ** (-jnp.arange(n_pairs, dtype=f32) / n_pairs)
    ang_r = row_idx[:, None] * inv_freq[None, :]
    ang_c = col_idx[:, None] * inv_freq[None, :]
    rope = (jnp.cos(ang_r)[:, None, None, :], jnp.sin(ang_r)[:, None, None, :],
            jnp.cos(ang_c)[:, None, None, :], jnp.sin(ang_c)[:, None, None, :])
    x = x_sample
    for l in range(DEPTH):
        mod = adaln_mod(c, w_ada[l], b_ada[l])
        ctx = (cache_k[:, l], cache_v[:, l], state_hgrn[:, l])
        x, _ = trunk_layer(x, mod, l, lams[l], lam_inits[l], lb_all[l], rope, ctx,
                           norm1_w, norm2_w, w_in, hg_norm_w, da_norm_w, w_out,
                           w_router, b_router, w_gu, b_gu, w_dn, b_dn)
    y_sample = rmsnorm(x, norm_f_w)
    return (y_prompt, y_sample, new_cache_k, new_cache_v, new_state_hgrn)
```

```python
import functools
import math

import jax
import jax.numpy as jnp
from jax import lax
from jax.experimental import pallas as pl
from jax.experimental.pallas import tpu as pltpu

F32 = jnp.float32
BF16 = jnp.bfloat16

D_MODEL = 1024
BATCH = 16
SEQ = 256
DEPTH = 4
DEC_BATCH = 2
DEC_SEQ = 4096
PAST_LEN = 256
GRID_W = 64
HG_HEADS = 4
HG_DK = 128
HG_DV = 128
HG_WIDTH = HG_HEADS * HG_DK
DA_HEADS = 4
DA_QK = 64
DA_V = 2 * DA_QK
DA_WIDTH = DA_HEADS * DA_V
ROPE_BASE = 10000.0
IN_DIM = 5 * HG_WIDTH + 3 * DA_WIDTH
N_EXPERTS = 32
TOP_K = 4
D_FF = D_MODEL
SWIGLU_ALPHA = 1.702
SWIGLU_LIMIT = 7.0
EPS = 1e-6

N_CTX_TOK = BATCH * SEQ
N_TOK = N_CTX_TOK + DEC_BATCH * DEC_SEQ
COND_GROUP = DEC_SEQ
assert N_CTX_TOK == COND_GROUP
N_COND = N_TOK // COND_GROUP
COND_ROWS = 8

COL_Q_HG, COL_F_FW, COL_F_BW, COL_I_HG, COL_G_HG, COL_Q_DA, COL_K_DA, COL_V_DA = range(8)
W512 = 512
LANES = 128

HG_CHUNK = 64
HG_HALF = HG_CHUNK // 2
HG_ROWS = 256
HG_EXP_CLAMP = 80.0

TM = 512
TN_IN = 2048
ATT_TQ = 128
ATT_TK = 256
MOE_BLK = 256
N_ASG = N_TOK * TOP_K
MOE_NBLK = N_ASG // MOE_BLK + N_EXPERTS
VMEM_LIMIT = 56 * 1024 * 1024


def _dot(a, b):
    return jnp.dot(a, b, preferred_element_type=F32)


def _dot_nt(a, b):
    return lax.dot_general(a, b, (((1,), (1,)), ((), ())), preferred_element_type=F32)


def _dot_tn(a, b):
    return lax.dot_general(a, b, (((0,), (0,)), ((), ())), preferred_element_type=F32)


def _split2(x):
    hi = x.astype(BF16)
    lo = (x - hi.astype(F32)).astype(BF16)
    return hi, lo


def _split3(x):
    hi = x.astype(BF16)
    r = x - hi.astype(F32)
    mid = r.astype(BF16)
    lo = (r - mid.astype(F32)).astype(BF16)
    return hi, mid, lo


def _dot_precise(a, b):
    a_hi, a_lo = _split2(a)
    b_hi, b_lo = _split2(b)
    return _dot(a_hi, b_hi) + (_dot(a_hi, b_lo) + _dot(a_lo, b_hi))


def _rms(x, w):
    ms = jnp.mean(x * x, axis=-1, keepdims=True)
    return x * lax.rsqrt(ms + EPS) * w


def _sigmoid(x):
    return 1.0 / (1.0 + jnp.exp(-x))


def _ada_kernel(cond_ref, w_ref, b_ref, o_ref):
    cnd = cond_ref[...]
    s = cnd * _sigmoid(cnd)
    o_ref[...] = _dot_precise(s, w_ref[...]) + b_ref[...]


def _ada_mod(conds, w_ada, b_ada):
    nj = 6 * D_MODEL // D_MODEL
    return pl.pallas_call(
        _ada_kernel,
        grid=(DEPTH, nj),
        in_specs=[
            pl.BlockSpec((COND_ROWS, D_MODEL), lambda l, j: (0, 0)),
            pl.BlockSpec((None, D_MODEL, D_MODEL), lambda l, j: (l, 0, j)),
            pl.BlockSpec((None, 1, D_MODEL), lambda l, j: (l, 0, j)),
        ],
        out_specs=pl.BlockSpec((None, COND_ROWS, D_MODEL), lambda l, j: (l, 0, j)),
        out_shape=jax.ShapeDtypeStruct((DEPTH, COND_ROWS, 6 * D_MODEL), F32),
        compiler_params=pltpu.CompilerParams(dimension_semantics=("arbitrary", "arbitrary"),
                                             vmem_limit_bytes=VMEM_LIMIT),
        name="ada_mod",
    )(conds, w_ada, b_ada.reshape(DEPTH, 1, 6 * D_MODEL))


def _inproj_kernel(x_ref, nw_ref, mod_ref, w_ref, o_ref, wbf_ref):
    @pl.when(pl.program_id(1) == 0)
    def _():
        wbf_ref[...] = w_ref[...].astype(BF16)

    x = x_ref[...]
    sh = mod_ref[:, 0:D_MODEL]
    sc = mod_ref[:, D_MODEL:2 * D_MODEL]
    h = _rms(x, nw_ref[...]) * (1.0 + sc) + sh
    o_ref[...] = _dot(h.astype(BF16), wbf_ref[...])


def _inproj(x, norm_w_l, mods_l, w_in, l):
    n_i = N_TOK // TM
    return pl.pallas_call(
        _inproj_kernel,
        grid=(IN_DIM // TN_IN, n_i),
        in_specs=[
            pl.BlockSpec((TM, D_MODEL), lambda n, i: (i, 0)),
            pl.BlockSpec((1, D_MODEL), lambda n, i: (0, 0)),
            pl.BlockSpec((None, 1, 6 * D_MODEL), lambda n, i: (i * TM // COND_GROUP, 0, 0)),
            pl.BlockSpec((None, D_MODEL, TN_IN), lambda n, i: (l, 0, n)),
        ],
        out_specs=pl.BlockSpec((TM, TN_IN), lambda n, i: (i, n)),
        out_shape=jax.ShapeDtypeStruct((N_TOK, IN_DIM), F32),
        scratch_shapes=[pltpu.VMEM((D_MODEL, TN_IN), BF16)],
        compiler_params=pltpu.CompilerParams(dimension_semantics=("arbitrary", "arbitrary"),
                                             vmem_limit_bytes=VMEM_LIMIT),
        name="norm_inproj",
    )(x, norm_w_l, mods_l, w_in)


N_HG_BLK = N_TOK // HG_ROWS
N_HG_CTX_BLK = N_CTX_TOK // HG_ROWS
HG_BPS_CTX = SEQ // HG_ROWS
HG_BPS_LAT = DEC_SEQ // HG_ROWS


def _hg_block(step, rev):
    return (N_HG_BLK - 1 - step) if rev else step


def _hg_lat_seq(p):
    return jnp.clip((p - N_HG_CTX_BLK) // HG_BPS_LAT, 0, DEC_BATCH - 1)


def _hg_ctx_seq(p):
    return jnp.minimum(p // HG_BPS_CTX, BATCH - 1)


def _hgrn_kernel(*refs, rev):
    if rev:
        (q_ref, f_ref, v_ref, g_ref, ofw_ref, lb_ref, s0_ref, nw_ref, o_ref, sfin_ref, st_ref) = refs
    else:
        (q_ref, f_ref, v_ref, lb_ref, s0_ref, o_ref, sfin_ref, st_ref) = refs
    p = _hg_block(pl.program_id(0), rev)
    is_ctx = p < N_HG_CTX_BLK
    pos = jnp.where(is_ctx, p % HG_BPS_CTX, (p - N_HG_CTX_BLK) % HG_BPS_LAT)
    bps = jnp.where(is_ctx, HG_BPS_CTX, HG_BPS_LAT)
    first_pos = (bps - 1) if rev else 0
    last_pos = 0 if rev else (bps - 1)

    @pl.when(jnp.logical_and(pos == first_pos, is_ctx))
    def _():
        st_ref[...] = jnp.zeros_like(st_ref)

    @pl.when(jnp.logical_and(pos == first_pos, jnp.logical_not(is_ctx)))
    def _():
        for h in range(HG_HEADS):
            st_ref[h] = s0_ref[h].T

    lb = lb_ref[...]
    oml = 1.0 - lb
    log_oml = jnp.log1p(-lb)
    row = lax.broadcasted_iota(jnp.int32, (HG_CHUNK, HG_CHUNK), 0)
    col = lax.broadcasted_iota(jnp.int32, (HG_CHUNK, HG_CHUNK), 1)
    tri = ((col >= row) if rev else (col <= row)).astype(BF16)
    same_half = (row >= HG_HALF) == (col >= HG_HALF)
    if rev:
        m_diag = jnp.logical_and(same_half, col >= row)
        m_off = jnp.logical_and(row < HG_HALF, col >= HG_HALF)
    else:
        m_diag = jnp.logical_and(same_half, col <= row)
        m_off = jnp.logical_and(row >= HG_HALF, col < HG_HALF)
    row1 = lax.broadcasted_iota(jnp.int32, (HG_CHUNK, 1), 0)
    n_chunks = HG_ROWS // HG_CHUNK

    def chunk(ci, carry):
        c = (n_chunks - 1 - ci) if rev else ci
        r0 = pl.multiple_of(c * HG_CHUNK, HG_CHUNK)
        rows = pl.ds(r0, HG_CHUNK)
        fx = f_ref[rows, :]
        q = q_ref[rows, :]
        v = v_ref[rows, :]
        e = jnp.exp(-jnp.abs(fx))
        r = 1.0 / (1.0 + e)
        er = e * r
        nonneg = fx >= 0.0
        sig = jnp.where(nonneg, r, er)
        nsig = jnp.where(nonneg, er, r)
        f = lb + oml * sig
        logf = jnp.maximum(jnp.log(f), log_oml + (jnp.minimum(fx, 0.0) - jnp.log1p(e)))
        k = oml * nsig
        hi, mid, lo = _split3(logf)
        a = _dot(tri, hi) + (_dot(tri, mid) + _dot(tri, lo))
        a_mid = jnp.where(row1 < HG_HALF, a[HG_HALF // 2:HG_HALF // 2 + 1, :],
                          a[HG_HALF + HG_HALF // 2:HG_HALF + HG_HALF // 2 + 1, :])
        if rev:
            a_b = a[HG_HALF:HG_HALF + 1, :]
            a_last = a[0:1, :]
        else:
            a_b = a[HG_HALF - 1:HG_HALF, :]
            a_last = a[HG_CHUNK - 1:HG_CHUNK, :]
        qd = (q * jnp.exp(jnp.minimum(a - a_mid, HG_EXP_CLAMP))).astype(BF16)
        kd = (k * jnp.exp(jnp.minimum(a_mid - a, HG_EXP_CLAMP))).astype(BF16)
        qo = (q * jnp.exp(jnp.minimum(a - a_b, 0.0))).astype(BF16)
        ko = (k * jnp.exp(jnp.minimum(a_b - a, 0.0))).astype(BF16)
        qa = (q * jnp.exp(a)).astype(BF16)
        kdec = (k * jnp.exp(a_last - a)).astype(BF16)
        dec = jnp.exp(a_last)
        vb = v.astype(BF16)
        for h in range(HG_HEADS):
            hs = slice(h * HG_DK, (h + 1) * HG_DK)
            p1 = _dot_nt(qd[:, hs], kd[:, hs])
            p2 = _dot_nt(qo[:, hs], ko[:, hs])
            attn = jnp.where(m_diag, p1, jnp.where(m_off, p2, 0.0)).astype(BF16)
            st = st_ref[h]
            o_h = _dot_nt(qa[:, hs], st.astype(BF16)) + _dot(attn, vb[:, hs])
            st_ref[h] = dec[:, hs] * st + _dot_tn(vb[:, hs], kdec[:, hs])
            if rev:
                tot = ofw_ref[rows, hs] + o_h
                g = g_ref[rows, hs]
                y = _rms(tot, nw_ref[...]) * (g * _sigmoid(g))
                o_ref[rows, hs] = y.astype(o_ref.dtype)
            else:
                o_ref[rows, hs] = o_h
        return carry

    lax.fori_loop(0, n_chunks, chunk, 0)

    @pl.when(jnp.logical_and(pos == last_pos, is_ctx))
    def _():
        for h in range(HG_HEADS):
            sfin_ref[h] = st_ref[h].T


def _hgrn_dir(proj, o_fw, lb_l, state_hgrn, hg_norm_l, l, rev):
    d = 1 if rev else 0

    def tok(colblk):
        return pl.BlockSpec((HG_ROWS, W512), lambda s: (_hg_block(s, rev), colblk))

    in_specs = [tok(COL_Q_HG), tok(COL_F_BW if rev else COL_F_FW), tok(COL_I_HG)]
    args = [proj, proj, proj]
    if rev:
        in_specs += [tok(COL_G_HG), pl.BlockSpec((HG_ROWS, W512), lambda s: (_hg_block(s, rev), 0))]
        args += [proj, o_fw]
    in_specs += [
        pl.BlockSpec((None, 1, W512), lambda s: (d, 0, 0)),
        pl.BlockSpec((None, None, None, HG_HEADS, HG_DK, HG_DV),
                     lambda s: (_hg_lat_seq(_hg_block(s, rev)), l, d, 0, 0, 0)),
    ]
    args += [lb_l.reshape(2, 1, W512), state_hgrn]
    if rev:
        in_specs.append(pl.BlockSpec((1, HG_DV), lambda s: (0, 0)))
        args.append(hg_norm_l)
    return pl.pallas_call(
        functools.partial(_hgrn_kernel, rev=rev),
        grid=(N_HG_BLK,),
        in_specs=in_specs,
        out_specs=[
            pl.BlockSpec((HG_ROWS, W512), lambda s: (_hg_block(s, rev), 0)),
            pl.BlockSpec((None, HG_HEADS, HG_DK, HG_DV), lambda s: (_hg_ctx_seq(_hg_block(s, rev)), 0, 0, 0)),
        ],
        out_shape=[
            jax.ShapeDtypeStruct((N_TOK, W512), BF16 if rev else F32),
            jax.ShapeDtypeStruct((BATCH, HG_HEADS, HG_DK, HG_DV), F32),
        ],
        scratch_shapes=[pltpu.VMEM((HG_HEADS, HG_DV, HG_DK), F32)],
        compiler_params=pltpu.CompilerParams(dimension_semantics=("arbitrary",),
                                             vmem_limit_bytes=VMEM_LIMIT),
        name="hgrn_bwd" if rev else "hgrn_fwd",
    )(*args)


def _rope_tables():
    n_pairs = DA_QK // 4
    inv_freq = ROPE_BASE ** (-jnp.arange(n_pairs, dtype=F32) / n_pairs)
    t = jnp.arange(DEC_SEQ)
    row_idx = (t // GRID_W).astype(F32)
    col_idx = (t % GRID_W).astype(F32)
    lane = jnp.arange(LANES)
    jj = lane % DA_QK
    use_col = (jj // (DA_QK // 2)) == 1
    second = (jj % (DA_QK // 2)) >= n_pairs
    fr = inv_freq[jj % n_pairs]
    ang_r = row_idx[:, None] * inv_freq[None, :]
    ang_c = col_idx[:, None] * inv_freq[None, :]
    cos_r, sin_r, cos_c, sin_c = jnp.cos(ang_r), jnp.sin(ang_r), jnp.cos(ang_c), jnp.sin(ang_c)
    fi = jj % n_pairs
    cos_t = jnp.where(use_col[None, :], cos_c[:, fi], cos_r[:, fi])
    sin_t = jnp.where(use_col[None, :], sin_c[:, fi], sin_r[:, fi])
    sin_t = jnp.where(second[None, :], sin_t, -sin_t)
    del fr
    cos_all = jnp.concatenate([jnp.ones((N_CTX_TOK, LANES), F32)] + [cos_t] * DEC_BATCH, axis=0)
    sin_all = jnp.concatenate([jnp.zeros((N_CTX_TOK, LANES), F32)] + [sin_t] * DEC_BATCH, axis=0)
    return cos_all, sin_all


def _da_prep_kernel(q_ref, k_ref, v_ref, cos_ref, sin_ref, qo_ref, ko_ref, vo_ref):
    cos = cos_ref[...]
    sin = sin_ref[...]
    lane = lax.broadcasted_iota(jnp.int32, (1, LANES), 1)
    first = (lane % (DA_QK // 2)) < (DA_QK // 4)
    scale = 1.0 / math.sqrt(DA_QK)

    def rope(x):
        partner = jnp.where(first, pltpu.roll(x, LANES - DA_QK // 4, 1), pltpu.roll(x, DA_QK // 4, 1))
        return x * cos + partner * sin

    for h in range(DA_HEADS):
        hs = slice(h * DA_V, (h + 1) * DA_V)
        qo_ref[:, hs] = (rope(q_ref[:, hs]) * scale).astype(BF16)
        ko_ref[:, hs] = rope(k_ref[:, hs]).astype(BF16)
    vo_ref[...] = v_ref[...].astype(BF16)


def _da_prep(proj, cos_all, sin_all):
    def tok(colblk):
        return pl.BlockSpec((TM, W512), lambda i: (i, colblk))

    tab = pl.BlockSpec((TM, LANES), lambda i: (i, 0))
    out = pl.BlockSpec((TM, W512), lambda i: (i, 0))
    shp = jax.ShapeDtypeStruct((N_TOK, W512), BF16)
    return pl.pallas_call(
        _da_prep_kernel,
        grid=(N_TOK // TM,),
        in_specs=[tok(COL_Q_DA), tok(COL_K_DA), tok(COL_V_DA), tab, tab],
        out_specs=[out, out, out],
        out_shape=[shp, shp, shp],
        compiler_params=pltpu.CompilerParams(dimension_semantics=("arbitrary",),
                                             vmem_limit_bytes=VMEM_LIMIT),
        name="da_prep",
    )(proj, proj, proj, cos_all, sin_all)


def _attn_kernel(*refs, n_tiles, has_ctx):
    if has_ctx:
        scal_ref, q_ref, k_ref, v_ref, kc_ref, vc_ref, nw_ref, o_ref = refs
    else:
        scal_ref, q_ref, k_ref, v_ref, nw_ref, o_ref = refs
    lam = scal_ref[0]
    out_scale = scal_ref[1]
    lane = lax.broadcasted_iota(jnp.int32, (1, LANES), 1)
    tq = q_ref.shape[0]

    def update(carry, q_c, kt, vt):
        m, l, acc = carry
        s = _dot_nt(q_c, kt)
        m_new = jnp.maximum(m, jnp.max(s, axis=-1, keepdims=True))
        alpha = jnp.exp(m - m_new)
        p = jnp.exp(s - m_new)
        l_new = alpha * l + jnp.sum(p, axis=-1, keepdims=True)
        acc_new = alpha * acc + _dot(p.astype(BF16), vt)
        return m_new, l_new, acc_new

    for h in range(DA_HEADS):
        hs = slice(h * DA_V, (h + 1) * DA_V)
        qh = q_ref[:, hs]
        zero = jnp.zeros_like(qh)
        q1 = jnp.where(lane < DA_QK, qh, zero)
        q2 = jnp.where(lane >= DA_QK, qh, zero)
        init = (jnp.full((tq, 1), -1e30, F32), jnp.zeros((tq, 1), F32), jnp.zeros((tq, DA_V), F32))

        def body(i, carry):
            c1, c2 = carry
            rows = pl.ds(pl.multiple_of(i * ATT_TK, ATT_TK), ATT_TK)
            kt = k_ref[rows, hs]
            vt = v_ref[rows, hs]
            return update(c1, q1, kt, vt), update(c2, q2, kt, vt)

        c1, c2 = lax.fori_loop(0, n_tiles, body, (init, init))
        if has_ctx:
            kt = kc_ref[:, hs].astype(BF16)
            vt = vc_ref[:, hs].astype(BF16)
            c1 = update(c1, q1, kt, vt)
            c2 = update(c2, q2, kt, vt)
        o = c1[2] / c1[1] - lam * (c2[2] / c2[1])
        o_ref[:, hs] = (_rms(o, nw_ref[...]) * out_scale).astype(o_ref.dtype)


def _attention(scal, q, k, v, cache_k, cache_v, da_norm_l, l):
    smem = pl.BlockSpec(memory_space=pltpu.SMEM)
    nw = pl.BlockSpec((1, DA_V), lambda *a: (0, 0))
    params = pltpu.CompilerParams(dimension_semantics=("arbitrary", "arbitrary"), vmem_limit_bytes=VMEM_LIMIT)
    tq_ctx = SEQ
    o_ctx = pl.pallas_call(
        functools.partial(_attn_kernel, n_tiles=SEQ // ATT_TK, has_ctx=False),
        grid=(BATCH, SEQ // tq_ctx),
        in_specs=[smem,
                  pl.BlockSpec((tq_ctx, W512), lambda b, i: (b * (SEQ // tq_ctx) + i, 0)),
                  pl.BlockSpec((SEQ, W512), lambda b, i: (b, 0)),
                  pl.BlockSpec((SEQ, W512), lambda b, i: (b, 0)),
                  nw],
        out_specs=pl.BlockSpec((tq_ctx, W512), lambda b, i: (b * (SEQ // tq_ctx) + i, 0)),
        out_shape=jax.ShapeDtypeStruct((N_CTX_TOK, W512), BF16),
        compiler_params=params,
        name="attn_ctx",
    )(scal, q, k, v, da_norm_l)
    nq = DEC_SEQ // ATT_TQ
    off_q = N_CTX_TOK // ATT_TQ
    off_k = N_CTX_TOK // DEC_SEQ
    ck = cache_k.reshape(DEC_BATCH, DEPTH, PAST_LEN, W512)
    cv = cache_v.reshape(DEC_BATCH, DEPTH, PAST_LEN, W512)
    o_lat = pl.pallas_call(
        functools.partial(_attn_kernel, n_tiles=DEC_SEQ // ATT_TK, has_ctx=True),
        grid=(DEC_BATCH, nq),
        in_specs=[smem,
                  pl.BlockSpec((ATT_TQ, W512), lambda b, i: (off_q + b * nq + i, 0)),
                  pl.BlockSpec((DEC_SEQ, W512), lambda b, i: (off_k + b, 0)),
                  pl.BlockSpec((DEC_SEQ, W512), lambda b, i: (off_k + b, 0)),
                  pl.BlockSpec((None, None, PAST_LEN, W512), lambda b, i: (b, l, 0, 0)),
                  pl.BlockSpec((None, None, PAST_LEN, W512), lambda b, i: (b, l, 0, 0)),
                  nw],
        out_specs=pl.BlockSpec((ATT_TQ, W512), lambda b, i: (b * nq + i, 0)),
        out_shape=jax.ShapeDtypeStruct((DEC_BATCH * DEC_SEQ, W512), BF16),
        compiler_params=params,
        name="attn_lat",
    )(scal, q, k, v, ck, cv, da_norm_l)
    return o_ctx, o_lat


def _outproj_kernel(hg_ref, dac_ref, dal_ref, x_ref, mod_ref, w_ref, nw_ref, wr_ref, br_ref,
                    xo_ref, h_ref, lg_ref, wbf_ref):
    i = pl.program_id(0)

    @pl.when(i == 0)
    def _():
        wbf_ref[...] = w_ref[...].astype(BF16)

    is_ctx = i < N_CTX_TOK // TM
    da = jnp.where(is_ctx, dac_ref[...], dal_ref[...])
    mix = _dot(hg_ref[...], wbf_ref[0:HG_WIDTH, :]) + _dot(da, wbf_ref[HG_WIDTH:, :])
    g1 = mod_ref[:, 2 * D_MODEL:3 * D_MODEL]
    sh2 = mod_ref[:, 3 * D_MODEL:4 * D_MODEL]
    sc2 = mod_ref[:, 4 * D_MODEL:5 * D_MODEL]
    x = x_ref[...] + g1 * mix
    xo_ref[...] = x
    h = _rms(x, nw_ref[...]) * (1.0 + sc2) + sh2
    h_ref[...] = h.astype(BF16)
    lg_ref[...] = _dot_precise(h, wr_ref[...]) + br_ref[...]


def _outproj(o_hg, o_da_ctx, o_da_lat, x, mods_l, w_out, norm2_l, w_router_l, b_router_l, l):
    n_ctx_t = N_CTX_TOK // TM
    n_lat_t = (N_TOK - N_CTX_TOK) // TM
    tokspec = lambda w: pl.BlockSpec((TM, w), lambda i: (i, 0))
    return pl.pallas_call(
        _outproj_kernel,
        grid=(N_TOK // TM,),
        in_specs=[
            tokspec(W512),
            pl.BlockSpec((TM, W512), lambda i: (jnp.minimum(i, n_ctx_t - 1), 0)),
            pl.BlockSpec((TM, W512), lambda i: (jnp.clip(i - n_ctx_t, 0, n_lat_t - 1), 0)),
            tokspec(D_MODEL),
            pl.BlockSpec((None, 1, 6 * D_MODEL), lambda i: (i * TM // COND_GROUP, 0, 0)),
            pl.BlockSpec((None, D_MODEL, D_MODEL), lambda i: (l, 0, 0)),
            pl.BlockSpec((1, D_MODEL), lambda i: (0, 0)),
            pl.BlockSpec((D_MODEL, N_EXPERTS), lambda i: (0, 0)),
            pl.BlockSpec((1, N_EXPERTS), lambda i: (0, 0)),
        ],
        out_specs=[tokspec(D_MODEL), tokspec(D_MODEL), tokspec(N_EXPERTS)],
        out_shape=[jax.ShapeDtypeStruct((N_TOK, D_MODEL), F32),
                   jax.ShapeDtypeStruct((N_TOK, D_MODEL), BF16),
                   jax.ShapeDtypeStruct((N_TOK, N_EXPERTS), F32)],
        scratch_shapes=[pltpu.VMEM((D_MODEL, D_MODEL), BF16)],
        compiler_params=pltpu.CompilerParams(dimension_semantics=("arbitrary",),
                                             vmem_limit_bytes=VMEM_LIMIT),
        name="outproj_norm_router",
    )(o_hg, o_da_ctx, o_da_lat, x, mods_l, w_out, norm2_l, w_router_l, b_router_l)


def _moe_kernel(be_ref, nu_ref, x_ref, gate_ref, wgu_ref, bgu_ref, wdn_ref, bdn_ref, o_ref, wgu_bf, wdn_bf):
    i = pl.program_id(0)
    used = i < nu_ref[0]
    prev = be_ref[jnp.maximum(i - 1, 0)]
    fresh = jnp.logical_or(i == 0, be_ref[i] != prev)

    @pl.when(jnp.logical_and(used, fresh))
    def _():
        wgu_bf[...] = wgu_ref[...].astype(BF16)
        wdn_bf[...] = wdn_ref[...].astype(BF16)

    @pl.when(used)
    def _():
        gu = _dot(x_ref[...], wgu_bf[...]) + bgu_ref[...]
        gate = jnp.minimum(gu[:, :D_FF], SWIGLU_LIMIT)
        lin = jnp.clip(gu[:, D_FF:], -SWIGLU_LIMIT, SWIGLU_LIMIT)
        act = (lin + 1.0) * gate * _sigmoid(SWIGLU_ALPHA * gate)
        y = _dot(act.astype(BF16), wdn_bf[...]) + bdn_ref[...]
        o_ref[...] = (y * gate_ref[...]).astype(o_ref.dtype)


def _moe_experts(blk_exp, n_used, xs, slot_gate, w_gu, b_gu, w_dn, b_dn, l):
    def rowblk(i, be, nu):
        return (jnp.minimum(i, nu[0] - 1), 0)

    grid_spec = pltpu.PrefetchScalarGridSpec(
        num_scalar_prefetch=2,
        grid=(MOE_NBLK,),
        in_specs=[
            pl.BlockSpec((MOE_BLK, D_MODEL), rowblk),
            pl.BlockSpec((MOE_BLK, 1), rowblk),
            pl.BlockSpec((None, None, D_MODEL, 2 * D_FF), lambda i, be, nu: (l, be[i], 0, 0)),
            pl.BlockSpec((None, None, 1, 2 * D_FF), lambda i, be, nu: (l, be[i], 0, 0)),
            pl.BlockSpec((None, None, D_FF, D_MODEL), lambda i, be, nu: (l, be[i], 0, 0)),
            pl.BlockSpec((None, None, 1, D_MODEL), lambda i, be, nu: (l, be[i], 0, 0)),
        ],
        out_specs=pl.BlockSpec((MOE_BLK, D_MODEL), rowblk),
        scratch_shapes=[pltpu.VMEM((D_MODEL, 2 * D_FF), BF16), pltpu.VMEM((D_FF, D_MODEL), BF16)],
    )
    return pl.pallas_call(
        _moe_kernel,
        grid_spec=grid_spec,
        out_shape=jax.ShapeDtypeStruct((MOE_NBLK * MOE_BLK, D_MODEL), BF16),
        compiler_params=pltpu.CompilerParams(dimension_semantics=("arbitrary",),
                                             vmem_limit_bytes=VMEM_LIMIT),
        name="moe_experts",
    )(blk_exp, n_used, xs, slot_gate, w_gu, b_gu.reshape(DEPTH, N_EXPERTS, 1, 2 * D_FF),
      w_dn, b_dn.reshape(DEPTH, N_EXPERTS, 1, D_MODEL))


def _route(logits):
    top_val, top_exp = lax.top_k(logits, TOP_K)
    gates = jax.nn.softmax(top_val, axis=-1)
    flat_e = top_exp.reshape(-1)
    onehot = (flat_e[:, None] == jnp.arange(N_EXPERTS)[None, :]).astype(jnp.int32)
    csum = jnp.cumsum(onehot, axis=0)
    rank = jnp.sum((csum - onehot) * onehot, axis=-1)
    counts = csum[-1]
    padded = (counts + MOE_BLK - 1) // MOE_BLK * MOE_BLK
    pad_end = jnp.cumsum(padded)
    pad_start = pad_end - padded
    dest = (pad_start[flat_e] + rank).astype(jnp.int32)
    tok = jnp.arange(N_ASG, dtype=jnp.int32) // TOP_K
    n_slots = MOE_NBLK * MOE_BLK
    slot_tok = jnp.zeros((n_slots,), jnp.int32).at[dest].set(tok)
    slot_gate = jnp.zeros((n_slots,), F32).at[dest].set(gates.reshape(-1))
    blk_exp = jnp.minimum(jnp.searchsorted(pad_end, jnp.arange(MOE_NBLK) * MOE_BLK, side='right'),
                          N_EXPERTS - 1).astype(jnp.int32)
    n_used = (pad_end[-1:] // MOE_BLK).astype(jnp.int32)
    return slot_tok, slot_gate.reshape(n_slots, 1), dest, blk_exp, n_used


def _final_norm_kernel(x_ref, w_ref, o_ref):
    o_ref[...] = _rms(x_ref[...], w_ref[...])


def _final_norm(x, w):
    return pl.pallas_call(
        _final_norm_kernel,
        grid=(N_TOK // TM,),
        in_specs=[pl.BlockSpec((TM, D_MODEL), lambda i: (i, 0)), pl.BlockSpec((1, D_MODEL), lambda i: (0, 0))],
        out_specs=pl.BlockSpec((TM, D_MODEL), lambda i: (i, 0)),
        out_shape=jax.ShapeDtypeStruct((N_TOK, D_MODEL), F32),
        compiler_params=pltpu.CompilerParams(dimension_semantics=("arbitrary",)),
        name="final_norm",
    )(x, w)


def kernel(x_prompt, x_sample, c, cache_k, cache_v, state_hgrn, c_ctx, norm1_w, norm2_w, w_ada, b_ada, w_in, hg_lb, hg_norm_w, lam_q1, lam_k1, lam_q2, lam_k2, da_norm_w, w_out, w_router, b_router, w_gu, b_gu, w_dn, b_dn, norm_f_w):
    lb_all = jnp.cumsum(jax.nn.softmax(hg_lb.astype(F32), axis=0), axis=0)
    lb_all = lb_all - lb_all[0:1]
    lam_init = [0.8 - 0.6 * math.exp(-0.3 * l) for l in range(DEPTH)]
    lam = (jnp.exp(jnp.sum(lam_q1 * lam_k1, axis=-1)) - jnp.exp(jnp.sum(lam_q2 * lam_k2, axis=-1))
           + jnp.asarray(lam_init, F32))
    cos_all, sin_all = _rope_tables()

    conds = jnp.concatenate([c_ctx[None, :], c, jnp.zeros((COND_ROWS - 1 - DEC_BATCH, D_MODEL), F32)], axis=0)
    mods = _ada_mod(conds, w_ada, b_ada)
    mods = mods[:, :N_COND].reshape(DEPTH, N_COND, 1, 6 * D_MODEL)

    x = jnp.concatenate([x_prompt.reshape(N_CTX_TOK, D_MODEL), x_sample.reshape(-1, D_MODEL)], axis=0)
    ks_l, vs_l, ss_l = [], [], []
    for l in range(DEPTH):
        proj = _inproj(x, norm1_w[l][None, :], mods[l], w_in, l)
        ks_l.append(proj[:N_CTX_TOK, COL_K_DA * W512:(COL_K_DA + 1) * W512].reshape(BATCH, SEQ, DA_HEADS, 2 * DA_QK))
        vs_l.append(proj[:N_CTX_TOK, COL_V_DA * W512:(COL_V_DA + 1) * W512].reshape(BATCH, SEQ, DA_HEADS, DA_V))
        o_fw, s_fw = _hgrn_dir(proj, None, lb_all[l], state_hgrn, None, l, rev=False)
        o_hg, s_bw = _hgrn_dir(proj, o_fw, lb_all[l], state_hgrn, hg_norm_w[l][None, :], l, rev=True)
        ss_l.append(jnp.stack([s_fw, s_bw], axis=1))
        q_r, k_r, v_b = _da_prep(proj, cos_all, sin_all)
        scal = jnp.stack([lam[l], jnp.asarray(1.0 - lam_init[l], F32)])
        o_da_ctx, o_da_lat = _attention(scal, q_r, k_r, v_b, cache_k, cache_v, da_norm_w[l][None, :], l)
        x_mid, h2, logits = _outproj(o_hg, o_da_ctx, o_da_lat, x, mods[l], w_out, norm2_w[l][None, :],
                                     w_router[l], b_router[l][None, :], l)
        slot_tok, slot_gate, dest, blk_exp, n_used = _route(logits)
        xs = jnp.take(h2, slot_tok, axis=0)
        ys = _moe_experts(blk_exp, n_used, xs, slot_gate, w_gu, b_gu, w_dn, b_dn, l)
        moe = jnp.take(ys, dest, axis=0).astype(F32).reshape(N_TOK, TOP_K, D_MODEL).sum(axis=1)
        g2 = mods[l][:, :, 5 * D_MODEL:]
        x = (x_mid.reshape(N_COND, COND_GROUP, D_MODEL) + g2 * moe.reshape(N_COND, COND_GROUP, D_MODEL)
             ).reshape(N_TOK, D_MODEL)
    y = _final_norm(x, norm_f_w[None, :])
    y_prompt = y[:N_CTX_TOK].reshape(BATCH, SEQ, D_MODEL)
    y_sample = y[N_CTX_TOK:].reshape(DEC_BATCH, DEC_SEQ, D_MODEL)
    return (y_prompt, y_sample, jnp.stack(ks_l, axis=1), jnp.stack(vs_l, axis=1), jnp.stack(ss_l, axis=1))
```

```python
import functools
import math

import jax
import jax.numpy as jnp
from jax import lax
from jax.experimental import pallas as pl
from jax.experimental.pallas import tpu as pltpu

F32 = jnp.float32
BF16 = jnp.bfloat16

D_MODEL = 1024
BATCH = 16
SEQ = 256
DEPTH = 4
DEC_BATCH = 2
DEC_SEQ = 4096
PAST_LEN = 256
GRID_W = 64
HG_HEADS = 4
HG_DK = 128
HG_DV = 128
HG_WIDTH = HG_HEADS * HG_DK
DA_HEADS = 4
DA_QK = 64
DA_V = 2 * DA_QK
DA_WIDTH = DA_HEADS * DA_V
ROPE_BASE = 10000.0
IN_DIM = 5 * HG_WIDTH + 3 * DA_WIDTH
N_EXPERTS = 32
TOP_K = 4
D_FF = D_MODEL
SWIGLU_ALPHA = 1.702
SWIGLU_LIMIT = 7.0
EPS = 1e-6

N_CTX_TOK = BATCH * SEQ
N_TOK = N_CTX_TOK + DEC_BATCH * DEC_SEQ
COND_GROUP = DEC_SEQ
assert N_CTX_TOK == COND_GROUP
N_COND = N_TOK // COND_GROUP
COND_ROWS = 8

COL_Q_HG, COL_F_FW, COL_F_BW, COL_I_HG, COL_G_HG, COL_Q_DA, COL_K_DA, COL_V_DA = range(8)
W512 = 512
LANES = 128

HG_CHUNK = 64
HG_HALF = HG_CHUNK // 2
HG_ROWS = 256
HG_EXP_CLAMP = 80.0

TM = 512
TN_IN = 2048
ATT_TQ = 256
Q_SCALE = math.log2(math.e) / math.sqrt(DA_QK)
MOE_BLK = 256
N_ASG = N_TOK * TOP_K
MOE_NBLK = N_ASG // MOE_BLK + N_EXPERTS
VMEM_LIMIT = 56 * 1024 * 1024


def _dot(a, b):
    return jnp.dot(a, b, preferred_element_type=F32)


def _dot_nt(a, b):
    return lax.dot_general(a, b, (((1,), (1,)), ((), ())), preferred_element_type=F32)


def _dot_tn(a, b):
    return lax.dot_general(a, b, (((0,), (0,)), ((), ())), preferred_element_type=F32)


def _split2(x):
    hi = x.astype(BF16)
    lo = (x - hi.astype(F32)).astype(BF16)
    return hi, lo


def _split3(x):
    hi = x.astype(BF16)
    r = x - hi.astype(F32)
    mid = r.astype(BF16)
    lo = (r - mid.astype(F32)).astype(BF16)
    return hi, mid, lo


def _dot_precise(a, b):
    a_hi, a_lo = _split2(a)
    b_hi, b_lo = _split2(b)
    return _dot(a_hi, b_hi) + (_dot(a_hi, b_lo) + _dot(a_lo, b_hi))


def _rms(x, w):
    ms = jnp.mean(x * x, axis=-1, keepdims=True)
    return x * lax.rsqrt(ms + EPS) * w


def _sigmoid(x):
    return 1.0 / (1.0 + jnp.exp(-x))


def _ada_kernel(cond_ref, w_ref, b_ref, o_ref):
    cnd = cond_ref[...]
    s = cnd * _sigmoid(cnd)
    o_ref[...] = _dot_precise(s, w_ref[...]) + b_ref[...]


def _ada_mod(conds, w_ada, b_ada):
    nj = 6 * D_MODEL // D_MODEL
    return pl.pallas_call(
        _ada_kernel,
        grid=(DEPTH, nj),
        in_specs=[
            pl.BlockSpec((COND_ROWS, D_MODEL), lambda l, j: (0, 0)),
            pl.BlockSpec((None, D_MODEL, D_MODEL), lambda l, j: (l, 0, j)),
            pl.BlockSpec((None, 1, D_MODEL), lambda l, j: (l, 0, j)),
        ],
        out_specs=pl.BlockSpec((None, COND_ROWS, D_MODEL), lambda l, j: (l, 0, j)),
        out_shape=jax.ShapeDtypeStruct((DEPTH, COND_ROWS, 6 * D_MODEL), F32),
        compiler_params=pltpu.CompilerParams(dimension_semantics=("arbitrary", "arbitrary"),
                                             vmem_limit_bytes=VMEM_LIMIT),
        name="ada_mod",
    )(conds, w_ada, b_ada.reshape(DEPTH, 1, 6 * D_MODEL))


def _inproj_kernel(x_ref, nw_ref, mod_ref, w_ref, o_ref, wbf_ref):
    @pl.when(pl.program_id(1) == 0)
    def _():
        wbf_ref[...] = w_ref[...].astype(BF16)

    x = x_ref[...]
    sh = mod_ref[:, 0:D_MODEL]
    sc = mod_ref[:, D_MODEL:2 * D_MODEL]
    h = _rms(x, nw_ref[...]) * (1.0 + sc) + sh
    o_ref[...] = _dot(h.astype(BF16), wbf_ref[...])


def _inproj(x, norm_w_l, mods_l, w_in, l):
    n_i = N_TOK // TM
    return pl.pallas_call(
        _inproj_kernel,
        grid=(IN_DIM // TN_IN, n_i),
        in_specs=[
            pl.BlockSpec((TM, D_MODEL), lambda n, i: (i, 0)),
            pl.BlockSpec((1, D_MODEL), lambda n, i: (0, 0)),
            pl.BlockSpec((None, 1, 6 * D_MODEL), lambda n, i: (i * TM // COND_GROUP, 0, 0)),
            pl.BlockSpec((None, D_MODEL, TN_IN), lambda n, i: (l, 0, n)),
        ],
        out_specs=pl.BlockSpec((TM, TN_IN), lambda n, i: (i, n)),
        out_shape=jax.ShapeDtypeStruct((N_TOK, IN_DIM), F32),
        scratch_shapes=[pltpu.VMEM((D_MODEL, TN_IN), BF16)],
        compiler_params=pltpu.CompilerParams(dimension_semantics=("arbitrary", "arbitrary"),
                                             vmem_limit_bytes=VMEM_LIMIT),
        name="norm_inproj",
    )(x, norm_w_l, mods_l, w_in)


N_HG_BLK = N_TOK // HG_ROWS
N_HG_CTX_BLK = N_CTX_TOK // HG_ROWS
HG_BPS_CTX = SEQ // HG_ROWS
HG_BPS_LAT = DEC_SEQ // HG_ROWS


def _hg_block(step, rev):
    return (N_HG_BLK - 1 - step) if rev else step


def _hg_lat_seq(p):
    return jnp.clip((p - N_HG_CTX_BLK) // HG_BPS_LAT, 0, DEC_BATCH - 1)


def _hg_ctx_seq(p):
    return jnp.minimum(p // HG_BPS_CTX, BATCH - 1)


def _hgrn_kernel(*refs, rev):
    if rev:
        (q_ref, f_ref, v_ref, g_ref, ofw_ref, lb_ref, s0_ref, nw_ref, o_ref, sfin_ref, st_ref) = refs
    else:
        (q_ref, f_ref, v_ref, lb_ref, s0_ref, o_ref, sfin_ref, st_ref) = refs
    p = _hg_block(pl.program_id(0), rev)
    is_ctx = p < N_HG_CTX_BLK
    pos = jnp.where(is_ctx, p % HG_BPS_CTX, (p - N_HG_CTX_BLK) % HG_BPS_LAT)
    bps = jnp.where(is_ctx, HG_BPS_CTX, HG_BPS_LAT)
    first_pos = (bps - 1) if rev else 0
    last_pos = 0 if rev else (bps - 1)

    @pl.when(jnp.logical_and(pos == first_pos, is_ctx))
    def _():
        st_ref[...] = jnp.zeros_like(st_ref)

    @pl.when(jnp.logical_and(pos == first_pos, jnp.logical_not(is_ctx)))
    def _():
        for h in range(HG_HEADS):
            st_ref[h] = s0_ref[h].T

    lb = lb_ref[...]
    oml = 1.0 - lb
    log_oml = jnp.log1p(-lb)
    row = lax.broadcasted_iota(jnp.int32, (HG_CHUNK, HG_CHUNK), 0)
    col = lax.broadcasted_iota(jnp.int32, (HG_CHUNK, HG_CHUNK), 1)
    tri = ((col >= row) if rev else (col <= row)).astype(BF16)
    same_half = (row >= HG_HALF) == (col >= HG_HALF)
    if rev:
        m_diag = jnp.logical_and(same_half, col >= row)
        m_off = jnp.logical_and(row < HG_HALF, col >= HG_HALF)
    else:
        m_diag = jnp.logical_and(same_half, col <= row)
        m_off = jnp.logical_and(row >= HG_HALF, col < HG_HALF)
    row1 = lax.broadcasted_iota(jnp.int32, (HG_CHUNK, 1), 0)
    n_chunks = HG_ROWS // HG_CHUNK

    def chunk(ci, carry):
        c = (n_chunks - 1 - ci) if rev else ci
        r0 = pl.multiple_of(c * HG_CHUNK, HG_CHUNK)
        rows = pl.ds(r0, HG_CHUNK)
        fx = f_ref[rows, :]
        q = q_ref[rows, :]
        v = v_ref[rows, :]
        e = jnp.exp(-jnp.abs(fx))
        r = 1.0 / (1.0 + e)
        er = e * r
        nonneg = fx >= 0.0
        sig = jnp.where(nonneg, r, er)
        nsig = jnp.where(nonneg, er, r)
        f = lb + oml * sig
        logf = jnp.maximum(jnp.log(f), log_oml + (jnp.minimum(fx, 0.0) - jnp.log1p(e)))
        k = oml * nsig
        hi, mid, lo = _split3(logf)
        a = _dot(tri, hi) + (_dot(tri, mid) + _dot(tri, lo))
        a_mid = jnp.where(row1 < HG_HALF, a[HG_HALF // 2:HG_HALF // 2 + 1, :],
                          a[HG_HALF + HG_HALF // 2:HG_HALF + HG_HALF // 2 + 1, :])
        if rev:
            a_b = a[HG_HALF:HG_HALF + 1, :]
            a_last = a[0:1, :]
        else:
            a_b = a[HG_HALF - 1:HG_HALF, :]
            a_last = a[HG_CHUNK - 1:HG_CHUNK, :]
        qd = (q * jnp.exp(jnp.minimum(a - a_mid, HG_EXP_CLAMP))).astype(BF16)
        kd = (k * jnp.exp(jnp.minimum(a_mid - a, HG_EXP_CLAMP))).astype(BF16)
        qo = (q * jnp.exp(jnp.minimum(a - a_b, 0.0))).astype(BF16)
        ko = (k * jnp.exp(jnp.minimum(a_b - a, 0.0))).astype(BF16)
        qa = (q * jnp.exp(a)).astype(BF16)
        kdec = (k * jnp.exp(a_last - a)).astype(BF16)
        dec = jnp.exp(a_last)
        vb = v.astype(BF16)
        for h in range(HG_HEADS):
            hs = slice(h * HG_DK, (h + 1) * HG_DK)
            p1 = _dot_nt(qd[:, hs], kd[:, hs])
            p2 = _dot_nt(qo[:, hs], ko[:, hs])
            attn = jnp.where(m_diag, p1, jnp.where(m_off, p2, 0.0)).astype(BF16)
            st = st_ref[h]
            o_h = _dot_nt(qa[:, hs], st.astype(BF16)) + _dot(attn, vb[:, hs])
            st_ref[h] = dec[:, hs] * st + _dot_tn(vb[:, hs], kdec[:, hs])
            if rev:
                tot = ofw_ref[rows, hs] + o_h
                g = g_ref[rows, hs]
                y = _rms(tot, nw_ref[...]) * (g * _sigmoid(g))
                o_ref[rows, hs] = y.astype(o_ref.dtype)
            else:
                o_ref[rows, hs] = o_h
        return carry

    lax.fori_loop(0, n_chunks, chunk, 0)

    @pl.when(jnp.logical_and(pos == last_pos, is_ctx))
    def _():
        for h in range(HG_HEADS):
            sfin_ref[h] = st_ref[h].T


def _hgrn_dir(proj, o_fw, lb_l, state_hgrn, hg_norm_l, l, rev):
    d = 1 if rev else 0

    def tok(colblk):
        return pl.BlockSpec((HG_ROWS, W512), lambda s: (_hg_block(s, rev), colblk))

    in_specs = [tok(COL_Q_HG), tok(COL_F_BW if rev else COL_F_FW), tok(COL_I_HG)]
    args = [proj, proj, proj]
    if rev:
        in_specs += [tok(COL_G_HG), pl.BlockSpec((HG_ROWS, W512), lambda s: (_hg_block(s, rev), 0))]
        args += [proj, o_fw]
    in_specs += [
        pl.BlockSpec((None, 1, W512), lambda s: (d, 0, 0)),
        pl.BlockSpec((None, None, None, HG_HEADS, HG_DK, HG_DV),
                     lambda s: (_hg_lat_seq(_hg_block(s, rev)), l, d, 0, 0, 0)),
    ]
    args += [lb_l.reshape(2, 1, W512), state_hgrn]
    if rev:
        in_specs.append(pl.BlockSpec((1, HG_DV), lambda s: (0, 0)))
        args.append(hg_norm_l)
    return pl.pallas_call(
        functools.partial(_hgrn_kernel, rev=rev),
        grid=(N_HG_BLK,),
        in_specs=in_specs,
        out_specs=[
            pl.BlockSpec((HG_ROWS, W512), lambda s: (_hg_block(s, rev), 0)),
            pl.BlockSpec((None, HG_HEADS, HG_DK, HG_DV), lambda s: (_hg_ctx_seq(_hg_block(s, rev)), 0, 0, 0)),
        ],
        out_shape=[
            jax.ShapeDtypeStruct((N_TOK, W512), BF16 if rev else F32),
            jax.ShapeDtypeStruct((BATCH, HG_HEADS, HG_DK, HG_DV), F32),
        ],
        scratch_shapes=[pltpu.VMEM((HG_HEADS, HG_DV, HG_DK), F32)],
        compiler_params=pltpu.CompilerParams(dimension_semantics=("arbitrary",),
                                             vmem_limit_bytes=VMEM_LIMIT),
        name="hgrn_bwd" if rev else "hgrn_fwd",
    )(*args)


def _rope_tables():
    n_pairs = DA_QK // 4
    inv_freq = ROPE_BASE ** (-jnp.arange(n_pairs, dtype=F32) / n_pairs)
    t = jnp.arange(DEC_SEQ)
    row_idx = (t // GRID_W).astype(F32)
    col_idx = (t % GRID_W).astype(F32)
    lane = jnp.arange(LANES)
    jj = lane % DA_QK
    use_col = (jj // (DA_QK // 2)) == 1
    second = (jj % (DA_QK // 2)) >= n_pairs
    fr = inv_freq[jj % n_pairs]
    ang_r = row_idx[:, None] * inv_freq[None, :]
    ang_c = col_idx[:, None] * inv_freq[None, :]
    cos_r, sin_r, cos_c, sin_c = jnp.cos(ang_r), jnp.sin(ang_r), jnp.cos(ang_c), jnp.sin(ang_c)
    fi = jj % n_pairs
    cos_t = jnp.where(use_col[None, :], cos_c[:, fi], cos_r[:, fi])
    sin_t = jnp.where(use_col[None, :], sin_c[:, fi], sin_r[:, fi])
    sin_t = jnp.where(second[None, :], sin_t, -sin_t)
    del fr
    cos_all = jnp.concatenate([jnp.ones((N_CTX_TOK, LANES), F32)] + [cos_t] * DEC_BATCH, axis=0)
    sin_all = jnp.concatenate([jnp.zeros((N_CTX_TOK, LANES), F32)] + [sin_t] * DEC_BATCH, axis=0)
    return cos_all, sin_all


def _da_prep_kernel(q_ref, k_ref, v_ref, cos_ref, sin_ref, qo_ref, ko_ref, vo_ref):
    cos = cos_ref[...]
    sin = sin_ref[...]
    lane = lax.broadcasted_iota(jnp.int32, (1, LANES), 1)
    first = (lane % (DA_QK // 2)) < (DA_QK // 4)

    def rope(x):
        partner = jnp.where(first, pltpu.roll(x, LANES - DA_QK // 4, 1), pltpu.roll(x, DA_QK // 4, 1))
        return x * cos + partner * sin

    for h in range(DA_HEADS):
        hs = slice(h * DA_V, (h + 1) * DA_V)
        qo_ref[h] = (rope(q_ref[:, hs]) * Q_SCALE).astype(BF16)
        ko_ref[h] = rope(k_ref[:, hs]).astype(BF16)
        vo_ref[h] = v_ref[:, hs].astype(BF16)


def _da_prep(proj, cos_all, sin_all):
    def tok(colblk):
        return pl.BlockSpec((TM, W512), lambda i: (i, colblk))

    tab = pl.BlockSpec((TM, LANES), lambda i: (i, 0))
    out = pl.BlockSpec((DA_HEADS, TM, DA_V), lambda i: (0, i, 0))
    shp = jax.ShapeDtypeStruct((DA_HEADS, N_TOK, DA_V), BF16)
    return pl.pallas_call(
        _da_prep_kernel,
        grid=(N_TOK // TM,),
        in_specs=[tok(COL_Q_DA), tok(COL_K_DA), tok(COL_V_DA), tab, tab],
        out_specs=[out, out, out],
        out_shape=[shp, shp, shp],
        compiler_params=pltpu.CompilerParams(dimension_semantics=("arbitrary",),
                                             vmem_limit_bytes=VMEM_LIMIT),
        name="da_prep",
    )(proj, proj, proj, cos_all, sin_all)


def _attn_kernel(*refs, has_ctx):
    if has_ctx:
        scal_ref, q_ref, k_ref, v_ref, kc_ref, vc_ref, nw_ref, o_ref = refs
    else:
        scal_ref, q_ref, k_ref, v_ref, nw_ref, o_ref = refs
    lam = scal_ref[0]
    out_scale = scal_ref[1]
    lane = lax.broadcasted_iota(jnp.int32, (1, LANES), 1)
    n_heads = q_ref.shape[0]
    if has_ctx:
        kc = kc_ref[...].astype(BF16)
        vc = vc_ref[...].astype(BF16)

    def component(q_c, k, v):
        s = _dot_nt(q_c, k)
        m = jnp.max(s, axis=-1, keepdims=True)
        if has_ctx:
            s_c = _dot_nt(q_c, kc)
            m = jnp.maximum(m, jnp.max(s_c, axis=-1, keepdims=True))
        p = jnp.exp2(s - m)
        l = jnp.sum(p, axis=-1, keepdims=True)
        acc = _dot(p.astype(BF16), v)
        if has_ctx:
            p_c = jnp.exp2(s_c - m)
            l = l + jnp.sum(p_c, axis=-1, keepdims=True)
            acc = acc + _dot(p_c.astype(BF16), vc)
        return acc / l

    for h in range(n_heads):
        qh = q_ref[h]
        zero = jnp.zeros_like(qh)
        q1 = jnp.where(lane < DA_QK, qh, zero)
        q2 = jnp.where(lane >= DA_QK, qh, zero)
        o = component(q1, k_ref[h], v_ref[h]) - lam * component(q2, k_ref[h], v_ref[h])
        y = (_rms(o, nw_ref[...]) * out_scale).astype(o_ref.dtype)
        if n_heads == 1:
            o_ref[...] = y
        else:
            o_ref[:, h * DA_V:(h + 1) * DA_V] = y


def _attention(scal, q, k, v, cache_k, cache_v, da_norm_l, l):
    smem = pl.BlockSpec(memory_space=pltpu.SMEM)
    nw = pl.BlockSpec((1, DA_V), lambda *a: (0, 0))
    blk = pl.BlockSpec((DA_HEADS, SEQ, DA_V), lambda b: (0, b, 0))
    o_ctx = pl.pallas_call(
        functools.partial(_attn_kernel, has_ctx=False),
        grid=(BATCH,),
        in_specs=[smem, blk, blk, blk, nw],
        out_specs=pl.BlockSpec((SEQ, W512), lambda b: (b, 0)),
        out_shape=jax.ShapeDtypeStruct((N_CTX_TOK, W512), BF16),
        compiler_params=pltpu.CompilerParams(dimension_semantics=("arbitrary",), vmem_limit_bytes=VMEM_LIMIT),
        name="attn_ctx",
    )(scal, q, k, v, da_norm_l)
    nq = DEC_SEQ // ATT_TQ
    off_q = N_CTX_TOK // ATT_TQ
    off_k = N_CTX_TOK // DEC_SEQ
    ck = cache_k.reshape(DEC_BATCH, DEPTH, PAST_LEN, W512)
    cv = cache_v.reshape(DEC_BATCH, DEPTH, PAST_LEN, W512)
    kv = pl.BlockSpec((1, DEC_SEQ, DA_V), lambda b, h, i: (h, off_k + b, 0))
    ckv = pl.BlockSpec((None, None, PAST_LEN, DA_V), lambda b, h, i: (b, l, 0, h))
    o_lat = pl.pallas_call(
        functools.partial(_attn_kernel, has_ctx=True),
        grid=(DEC_BATCH, DA_HEADS, nq),
        in_specs=[smem,
                  pl.BlockSpec((1, ATT_TQ, DA_V), lambda b, h, i: (h, off_q + b * nq + i, 0)),
                  kv, kv, ckv, ckv, nw],
        out_specs=pl.BlockSpec((ATT_TQ, DA_V), lambda b, h, i: (b * nq + i, h)),
        out_shape=jax.ShapeDtypeStruct((DEC_BATCH * DEC_SEQ, W512), BF16),
        compiler_params=pltpu.CompilerParams(dimension_semantics=("arbitrary", "arbitrary", "arbitrary"),
                                             vmem_limit_bytes=VMEM_LIMIT),
        name="attn_lat",
    )(scal, q, k, v, ck, cv, da_norm_l)
    return o_ctx, o_lat


def _outproj_kernel(hg_ref, dac_ref, dal_ref, x_ref, mod_ref, w_ref, nw_ref, wr_ref, br_ref,
                    xo_ref, h_ref, lg_ref, wbf_ref):
    i = pl.program_id(0)

    @pl.when(i == 0)
    def _():
        wbf_ref[...] = w_ref[...].astype(BF16)

    is_ctx = i < N_CTX_TOK // TM
    da = jnp.where(is_ctx, dac_ref[...], dal_ref[...])
    mix = _dot(hg_ref[...], wbf_ref[0:HG_WIDTH, :]) + _dot(da, wbf_ref[HG_WIDTH:, :])
    g1 = mod_ref[:, 2 * D_MODEL:3 * D_MODEL]
    sh2 = mod_ref[:, 3 * D_MODEL:4 * D_MODEL]
    sc2 = mod_ref[:, 4 * D_MODEL:5 * D_MODEL]
    x = x_ref[...] + g1 * mix
    xo_ref[...] = x
    h = _rms(x, nw_ref[...]) * (1.0 + sc2) + sh2
    h_ref[...] = h.astype(BF16)
    lg_ref[...] = _dot_precise(h, wr_ref[...]) + br_ref[...]


def _outproj(o_hg, o_da_ctx, o_da_lat, x, mods_l, w_out, norm2_l, w_router_l, b_router_l, l):
    n_ctx_t = N_CTX_TOK // TM
    n_lat_t = (N_TOK - N_CTX_TOK) // TM
    tokspec = lambda w: pl.BlockSpec((TM, w), lambda i: (i, 0))
    return pl.pallas_call(
        _outproj_kernel,
        grid=(N_TOK // TM,),
        in_specs=[
            tokspec(W512),
            pl.BlockSpec((TM, W512), lambda i: (jnp.minimum(i, n_ctx_t - 1), 0)),
            pl.BlockSpec((TM, W512), lambda i: (jnp.clip(i - n_ctx_t, 0, n_lat_t - 1), 0)),
            tokspec(D_MODEL),
            pl.BlockSpec((None, 1, 6 * D_MODEL), lambda i: (i * TM // COND_GROUP, 0, 0)),
            pl.BlockSpec((None, D_MODEL, D_MODEL), lambda i: (l, 0, 0)),
            pl.BlockSpec((1, D_MODEL), lambda i: (0, 0)),
            pl.BlockSpec((D_MODEL, N_EXPERTS), lambda i: (0, 0)),
            pl.BlockSpec((1, N_EXPERTS), lambda i: (0, 0)),
        ],
        out_specs=[tokspec(D_MODEL), tokspec(D_MODEL), tokspec(N_EXPERTS)],
        out_shape=[jax.ShapeDtypeStruct((N_TOK, D_MODEL), F32),
                   jax.ShapeDtypeStruct((N_TOK, D_MODEL), BF16),
                   jax.ShapeDtypeStruct((N_TOK, N_EXPERTS), F32)],
        scratch_shapes=[pltpu.VMEM((D_MODEL, D_MODEL), BF16)],
        compiler_params=pltpu.CompilerParams(dimension_semantics=("arbitrary",),
                                             vmem_limit_bytes=VMEM_LIMIT),
        name="outproj_norm_router",
    )(o_hg, o_da_ctx, o_da_lat, x, mods_l, w_out, norm2_l, w_router_l, b_router_l)


def _moe_kernel(be_ref, nu_ref, x_ref, wgu_ref, bgu_ref, wdn_ref, bdn_ref, o_ref, wgu_bf, wdn_bf):
    i = pl.program_id(0)
    used = i < nu_ref[0]
    prev = be_ref[jnp.maximum(i - 1, 0)]
    fresh = jnp.logical_or(i == 0, be_ref[i] != prev)

    @pl.when(jnp.logical_and(used, fresh))
    def _():
        wgu_bf[...] = wgu_ref[...].astype(BF16)
        wdn_bf[...] = wdn_ref[...].astype(BF16)

    @pl.when(used)
    def _():
        gu = _dot(x_ref[...], wgu_bf[...]) + bgu_ref[...]
        gate = jnp.minimum(gu[:, :D_FF], SWIGLU_LIMIT)
        lin = jnp.clip(gu[:, D_FF:], -SWIGLU_LIMIT, SWIGLU_LIMIT)
        act = (lin + 1.0) * gate * _sigmoid(SWIGLU_ALPHA * gate)
        y = _dot(act.astype(BF16), wdn_bf[...]) + bdn_ref[...]
        o_ref[...] = y.astype(o_ref.dtype)


def _moe_experts(blk_exp, n_used, xs, w_gu, b_gu, w_dn, b_dn, l):
    def rowblk(i, be, nu):
        return (jnp.minimum(i, nu[0] - 1), 0)

    grid_spec = pltpu.PrefetchScalarGridSpec(
        num_scalar_prefetch=2,
        grid=(MOE_NBLK,),
        in_specs=[
            pl.BlockSpec((MOE_BLK, D_MODEL), rowblk),
            pl.BlockSpec((None, None, D_MODEL, 2 * D_FF), lambda i, be, nu: (l, be[i], 0, 0)),
            pl.BlockSpec((None, None, 1, 2 * D_FF), lambda i, be, nu: (l, be[i], 0, 0)),
            pl.BlockSpec((None, None, D_FF, D_MODEL), lambda i, be, nu: (l, be[i], 0, 0)),
            pl.BlockSpec((None, None, 1, D_MODEL), lambda i, be, nu: (l, be[i], 0, 0)),
        ],
        out_specs=pl.BlockSpec((MOE_BLK, D_MODEL), rowblk),
        scratch_shapes=[pltpu.VMEM((D_MODEL, 2 * D_FF), BF16), pltpu.VMEM((D_FF, D_MODEL), BF16)],
    )
    return pl.pallas_call(
        _moe_kernel,
        grid_spec=grid_spec,
        out_shape=jax.ShapeDtypeStruct((MOE_NBLK * MOE_BLK, D_MODEL), BF16),
        compiler_params=pltpu.CompilerParams(dimension_semantics=("arbitrary",),
                                             vmem_limit_bytes=VMEM_LIMIT),
        name="moe_experts",
    )(blk_exp, n_used, xs, w_gu, b_gu.reshape(DEPTH, N_EXPERTS, 1, 2 * D_FF),
      w_dn, b_dn.reshape(DEPTH, N_EXPERTS, 1, D_MODEL))


def _route_kernel(lg_ref, exp_ref, gate_ref, rank_ref, cnt_ref, run_ref):
    i = pl.program_id(0)

    @pl.when(i == 0)
    def _():
        run_ref[...] = jnp.zeros_like(run_ref)

    lg = lg_ref[...]
    lane = lax.broadcasted_iota(jnp.int32, (1, N_EXPERTS), 1)
    lane_k = lax.broadcasted_iota(jnp.int32, (1, TOP_K), 1)
    vals, idxs, hots = [], [], []
    for _ in range(TOP_K):
        m = jnp.max(lg, axis=-1, keepdims=True)
        idx = jnp.min(jnp.where(lg == m, lane, N_EXPERTS), axis=-1, keepdims=True)
        hot = lane == idx
        vals.append(m)
        idxs.append(idx)
        hots.append(hot)
        lg = jnp.where(hot, -jnp.inf, lg)
    exps = [jnp.exp(v - vals[0]) for v in vals]
    denom = exps[0] + exps[1] + exps[2] + exps[3]
    tokhot = (hots[0] | hots[1] | hots[2] | hots[3]).astype(BF16)
    r = lax.broadcasted_iota(jnp.int32, (TM, TM), 0)
    cidx = lax.broadcasted_iota(jnp.int32, (TM, TM), 1)
    before = _dot((cidx < r).astype(BF16), tokhot) + run_ref[...]
    e_out = jnp.zeros((TM, TOP_K), jnp.int32)
    g_out = jnp.zeros((TM, TOP_K), F32)
    r_out = jnp.zeros((TM, TOP_K), jnp.int32)
    for k in range(TOP_K):
        rk = jnp.sum(jnp.where(hots[k], before, 0.0), axis=-1, keepdims=True).astype(jnp.int32)
        e_out = jnp.where(lane_k == k, idxs[k], e_out)
        g_out = jnp.where(lane_k == k, exps[k] / denom, g_out)
        r_out = jnp.where(lane_k == k, rk, r_out)
    exp_ref[...] = e_out
    gate_ref[...] = g_out
    rank_ref[...] = r_out
    run_ref[...] += jnp.sum(tokhot.astype(F32), axis=0, keepdims=True)
    cnt_ref[...] = run_ref[...].astype(jnp.int32)


def _route(logits):
    tk = pl.BlockSpec((TM, TOP_K), lambda i: (i, 0))
    top_exp, gates, rank, counts = pl.pallas_call(
        _route_kernel,
        grid=(N_TOK // TM,),
        in_specs=[pl.BlockSpec((TM, N_EXPERTS), lambda i: (i, 0))],
        out_specs=[tk, tk, tk, pl.BlockSpec((1, N_EXPERTS), lambda i: (0, 0))],
        out_shape=[jax.ShapeDtypeStruct((N_TOK, TOP_K), jnp.int32),
                   jax.ShapeDtypeStruct((N_TOK, TOP_K), F32),
                   jax.ShapeDtypeStruct((N_TOK, TOP_K), jnp.int32),
                   jax.ShapeDtypeStruct((1, N_EXPERTS), jnp.int32)],
        scratch_shapes=[pltpu.VMEM((1, N_EXPERTS), F32)],
        compiler_params=pltpu.CompilerParams(dimension_semantics=("arbitrary",)),
        name="route",
    )(logits)
    counts = counts[0]
    padded = (counts + MOE_BLK - 1) // MOE_BLK * MOE_BLK
    pad_end = jnp.cumsum(padded)
    pad_start = pad_end - padded
    start = jnp.cumsum(counts) - counts
    eq = top_exp[:, :, None] == jnp.arange(N_EXPERTS, dtype=jnp.int32)[None, None, :]
    dest = (jnp.sum(jnp.where(eq, pad_start[None, None, :], 0), axis=-1) + rank).reshape(-1)
    blk_exp = jnp.minimum(
        jnp.sum((pad_end[None, :] <= (jnp.arange(MOE_NBLK, dtype=jnp.int32) * MOE_BLK)[:, None]).astype(jnp.int32),
                axis=-1), N_EXPERTS - 1).astype(jnp.int32)
    n_used = (pad_end[-1:] // MOE_BLK).astype(jnp.int32)
    tok_ids = jnp.arange(N_ASG, dtype=jnp.int32) // TOP_K
    _, tok_sorted = lax.sort((top_exp.reshape(-1), tok_ids), num_keys=1, is_stable=True)
    slot = jnp.arange(MOE_NBLK * MOE_BLK, dtype=jnp.int32)
    e_slot = jnp.repeat(blk_exp, MOE_BLK)
    src = jnp.clip(start[e_slot] + slot - pad_start[e_slot], 0, N_ASG - 1)
    slot_tok = tok_sorted[src]
    return slot_tok, gates, dest, blk_exp, n_used


def _combine_kernel(ys_ref, gate_ref, x_ref, mod_ref, o_ref):
    g = gate_ref[...]
    acc = g[:, 0:1] * ys_ref[:, 0:D_MODEL].astype(F32)
    for k in range(1, TOP_K):
        acc = acc + g[:, k:k + 1] * ys_ref[:, k * D_MODEL:(k + 1) * D_MODEL].astype(F32)
    o_ref[...] = x_ref[...] + mod_ref[:, 5 * D_MODEL:] * acc


def _combine(ys_tok, gates, x_mid, mods_l):
    return pl.pallas_call(
        _combine_kernel,
        grid=(N_TOK // TM,),
        in_specs=[pl.BlockSpec((TM, TOP_K * D_MODEL), lambda i: (i, 0)),
                  pl.BlockSpec((TM, TOP_K), lambda i: (i, 0)),
                  pl.BlockSpec((TM, D_MODEL), lambda i: (i, 0)),
                  pl.BlockSpec((None, 1, 6 * D_MODEL), lambda i: (i * TM // COND_GROUP, 0, 0))],
        out_specs=pl.BlockSpec((TM, D_MODEL), lambda i: (i, 0)),
        out_shape=jax.ShapeDtypeStruct((N_TOK, D_MODEL), F32),
        compiler_params=pltpu.CompilerParams(dimension_semantics=("arbitrary",), vmem_limit_bytes=VMEM_LIMIT),
        name="moe_combine",
    )(ys_tok, gates, x_mid, mods_l)


def _final_norm_kernel(x_ref, w_ref, o_ref):
    o_ref[...] = _rms(x_ref[...], w_ref[...])


def _final_norm(x, w):
    return pl.pallas_call(
        _final_norm_kernel,
        grid=(N_TOK // TM,),
        in_specs=[pl.BlockSpec((TM, D_MODEL), lambda i: (i, 0)), pl.BlockSpec((1, D_MODEL), lambda i: (0, 0))],
        out_specs=pl.BlockSpec((TM, D_MODEL), lambda i: (i, 0)),
        out_shape=jax.ShapeDtypeStruct((N_TOK, D_MODEL), F32),
        compiler_params=pltpu.CompilerParams(dimension_semantics=("arbitrary",)),
        name="final_norm",
    )(x, w)


def kernel(x_prompt, x_sample, c, cache_k, cache_v, state_hgrn, c_ctx, norm1_w, norm2_w, w_ada, b_ada, w_in, hg_lb, hg_norm_w, lam_q1, lam_k1, lam_q2, lam_k2, da_norm_w, w_out, w_router, b_router, w_gu, b_gu, w_dn, b_dn, norm_f_w):
    lb_all = jnp.cumsum(jax.nn.softmax(hg_lb.astype(F32), axis=0), axis=0)
    lb_all = lb_all - lb_all[0:1]
    lam_init = [0.8 - 0.6 * math.exp(-0.3 * l) for l in range(DEPTH)]
    lam = (jnp.exp(jnp.sum(lam_q1 * lam_k1, axis=-1)) - jnp.exp(jnp.sum(lam_q2 * lam_k2, axis=-1))
           + jnp.asarray(lam_init, F32))
    cos_all, sin_all = _rope_tables()

    conds = jnp.concatenate([c_ctx[None, :], c, jnp.zeros((COND_ROWS - 1 - DEC_BATCH, D_MODEL), F32)], axis=0)
    mods = _ada_mod(conds, w_ada, b_ada)
    mods = mods[:, :N_COND].reshape(DEPTH, N_COND, 1, 6 * D_MODEL)

    x = jnp.concatenate([x_prompt.reshape(N_CTX_TOK, D_MODEL), x_sample.reshape(-1, D_MODEL)], axis=0)
    ks_l, vs_l, ss_l = [], [], []
    for l in range(DEPTH):
        proj = _inproj(x, norm1_w[l][None, :], mods[l], w_in, l)
        ks_l.append(proj[:N_CTX_TOK, COL_K_DA * W512:(COL_K_DA + 1) * W512].reshape(BATCH, SEQ, DA_HEADS, 2 * DA_QK))
        vs_l.append(proj[:N_CTX_TOK, COL_V_DA * W512:(COL_V_DA + 1) * W512].reshape(BATCH, SEQ, DA_HEADS, DA_V))
        o_fw, s_fw = _hgrn_dir(proj, None, lb_all[l], state_hgrn, None, l, rev=False)
        o_hg, s_bw = _hgrn_dir(proj, o_fw, lb_all[l], state_hgrn, hg_norm_w[l][None, :], l, rev=True)
        ss_l.append(jnp.stack([s_fw, s_bw], axis=1))
        q_r, k_r, v_b = _da_prep(proj, cos_all, sin_all)
        scal = jnp.stack([lam[l], jnp.asarray(1.0 - lam_init[l], F32)])
        o_da_ctx, o_da_lat = _attention(scal, q_r, k_r, v_b, cache_k, cache_v, da_norm_w[l][None, :], l)
        x_mid, h2, logits = _outproj(o_hg, o_da_ctx, o_da_lat, x, mods[l], w_out, norm2_w[l][None, :],
                                     w_router[l], b_router[l][None, :], l)
        slot_tok, gates, dest, blk_exp, n_used = _route(logits)
        xs = h2.at[slot_tok].get(mode="promise_in_bounds")
        ys = _moe_experts(blk_exp, n_used, xs, w_gu, b_gu, w_dn, b_dn, l)
        ys_tok = ys.at[dest].get(mode="promise_in_bounds").reshape(N_TOK, TOP_K * D_MODEL)
        x = _combine(ys_tok, gates, x_mid, mods[l])
    y = _final_norm(x, norm_f_w[None, :])
    y_prompt = y[:N_CTX_TOK].reshape(BATCH, SEQ, D_MODEL)
    y_sample = y[N_CTX_TOK:].reshape(DEC_BATCH, DEC_SEQ, D_MODEL)
    return (y_prompt, y_sample, jnp.stack(ks_l, axis=1), jnp.stack(vs_l, axis=1), jnp.stack(ss_l, axis=1))
```

```python
import functools
import math

import jax
import jax.numpy as jnp
from jax import lax
from jax.experimental import pallas as pl
from jax.experimental.pallas import tpu as pltpu

F32 = jnp.float32
BF16 = jnp.bfloat16

D_MODEL = 1024
BATCH = 16
SEQ = 256
DEPTH = 4
DEC_BATCH = 2
DEC_SEQ = 4096
PAST_LEN = 256
GRID_W = 64
HG_HEADS = 4
HG_DK = 128
HG_DV = 128
HG_WIDTH = HG_HEADS * HG_DK
DA_HEADS = 4
DA_QK = 64
DA_V = 2 * DA_QK
DA_WIDTH = DA_HEADS * DA_V
ROPE_BASE = 10000.0
IN_DIM = 5 * HG_WIDTH + 3 * DA_WIDTH
N_EXPERTS = 32
TOP_K = 4
D_FF = D_MODEL
SWIGLU_ALPHA = 1.702
SWIGLU_LIMIT = 7.0
EPS = 1e-6

N_CTX_TOK = BATCH * SEQ
N_TOK = N_CTX_TOK + DEC_BATCH * DEC_SEQ
COND_GROUP = DEC_SEQ
assert N_CTX_TOK == COND_GROUP
N_COND = N_TOK // COND_GROUP
COND_ROWS = 8

COL_Q_HG, COL_F_FW, COL_F_BW, COL_I_HG, COL_G_HG, COL_Q_DA, COL_K_DA, COL_V_DA = range(8)
W512 = 512
LANES = 128

HG_CHUNK = 64
HG_HALF = HG_CHUNK // 2
HG_ROWS = 256
HG_EXP_CLAMP = 80.0

TM = 512
TN_IN = 2048
ATT_TQ = 256
Q_SCALE = math.log2(math.e) / math.sqrt(DA_QK)
MOE_BLK = 256
N_ASG = N_TOK * TOP_K
MOE_NBLK = N_ASG // MOE_BLK + N_EXPERTS
VMEM_LIMIT = 56 * 1024 * 1024


def _dot(a, b):
    return jnp.dot(a, b, preferred_element_type=F32)


def _dot_nt(a, b):
    return lax.dot_general(a, b, (((1,), (1,)), ((), ())), preferred_element_type=F32)


def _dot_tn(a, b):
    return lax.dot_general(a, b, (((0,), (0,)), ((), ())), preferred_element_type=F32)


def _split2(x):
    hi = x.astype(BF16)
    lo = (x - hi.astype(F32)).astype(BF16)
    return hi, lo


def _split3(x):
    hi = x.astype(BF16)
    r = x - hi.astype(F32)
    mid = r.astype(BF16)
    lo = (r - mid.astype(F32)).astype(BF16)
    return hi, mid, lo


def _dot_precise(a, b):
    a_hi, a_lo = _split2(a)
    b_hi, b_lo = _split2(b)
    return _dot(a_hi, b_hi) + (_dot(a_hi, b_lo) + _dot(a_lo, b_hi))


def _pack_rows(x):
    n = x.shape[1] // 2
    lo = pltpu.bitcast(x[:, :n].astype(BF16).astype(F32), jnp.uint32)
    hi = pltpu.bitcast(x[:, n:].astype(BF16).astype(F32), jnp.uint32)
    return (lo >> 16) | hi


def _unpack_rows(u):
    lo = pltpu.bitcast(u << 16, F32)
    hi = pltpu.bitcast(u & jnp.uint32(0xFFFF0000), F32)
    return lo, hi


def _rms(x, w):
    ms = jnp.mean(x * x, axis=-1, keepdims=True)
    return x * lax.rsqrt(ms + EPS) * w


def _sigmoid(x):
    return 1.0 / (1.0 + jnp.exp(-x))


def _ada_kernel(cond_ref, w_ref, b_ref, o_ref):
    cnd = cond_ref[...]
    s = cnd * _sigmoid(cnd)
    o_ref[...] = _dot_precise(s, w_ref[...]) + b_ref[...]


def _ada_mod(conds, w_ada, b_ada):
    nj = 6 * D_MODEL // D_MODEL
    return pl.pallas_call(
        _ada_kernel,
        grid=(DEPTH, nj),
        in_specs=[
            pl.BlockSpec((COND_ROWS, D_MODEL), lambda l, j: (0, 0)),
            pl.BlockSpec((None, D_MODEL, D_MODEL), lambda l, j: (l, 0, j)),
            pl.BlockSpec((None, 1, D_MODEL), lambda l, j: (l, 0, j)),
        ],
        out_specs=pl.BlockSpec((None, COND_ROWS, D_MODEL), lambda l, j: (l, 0, j)),
        out_shape=jax.ShapeDtypeStruct((DEPTH, COND_ROWS, 6 * D_MODEL), F32),
        compiler_params=pltpu.CompilerParams(dimension_semantics=("arbitrary", "arbitrary"),
                                             vmem_limit_bytes=VMEM_LIMIT),
        name="ada_mod",
    )(conds, w_ada, b_ada.reshape(DEPTH, 1, 6 * D_MODEL))


def _inproj_kernel(x_ref, nw_ref, mod_ref, w_ref, o_ref, wbf_ref):
    @pl.when(pl.program_id(1) == 0)
    def _():
        wbf_ref[...] = w_ref[...].astype(BF16)

    x = x_ref[...]
    sh = mod_ref[:, 0:D_MODEL]
    sc = mod_ref[:, D_MODEL:2 * D_MODEL]
    h = _rms(x, nw_ref[...]) * (1.0 + sc) + sh
    o_ref[...] = _dot(h.astype(BF16), wbf_ref[...])


def _inproj(x, norm_w_l, mods_l, w_in, l):
    n_i = N_TOK // TM
    return pl.pallas_call(
        _inproj_kernel,
        grid=(IN_DIM // TN_IN, n_i),
        in_specs=[
            pl.BlockSpec((TM, D_MODEL), lambda n, i: (i, 0)),
            pl.BlockSpec((1, D_MODEL), lambda n, i: (0, 0)),
            pl.BlockSpec((None, 1, 6 * D_MODEL), lambda n, i: (i * TM // COND_GROUP, 0, 0)),
            pl.BlockSpec((None, D_MODEL, TN_IN), lambda n, i: (l, 0, n)),
        ],
        out_specs=pl.BlockSpec((TM, TN_IN), lambda n, i: (i, n)),
        out_shape=jax.ShapeDtypeStruct((N_TOK, IN_DIM), F32),
        scratch_shapes=[pltpu.VMEM((D_MODEL, TN_IN), BF16)],
        compiler_params=pltpu.CompilerParams(dimension_semantics=("arbitrary", "arbitrary"),
                                             vmem_limit_bytes=VMEM_LIMIT),
        name="norm_inproj",
    )(x, norm_w_l, mods_l, w_in)


N_HG_BLK = N_TOK // HG_ROWS
N_HG_CTX_BLK = N_CTX_TOK // HG_ROWS
HG_BPS_CTX = SEQ // HG_ROWS
HG_BPS_LAT = DEC_SEQ // HG_ROWS


def _hg_block(step, rev):
    return (N_HG_BLK - 1 - step) if rev else step


def _hg_lat_seq(p):
    return jnp.clip((p - N_HG_CTX_BLK) // HG_BPS_LAT, 0, DEC_BATCH - 1)


def _hg_ctx_seq(p):
    return jnp.minimum(p // HG_BPS_CTX, BATCH - 1)


def _hgrn_kernel(*refs, rev):
    if rev:
        (q_ref, f_ref, v_ref, g_ref, ofw_ref, lb_ref, s0_ref, nw_ref, o_ref, sfin_ref, st_ref) = refs
    else:
        (q_ref, f_ref, v_ref, lb_ref, s0_ref, o_ref, sfin_ref, st_ref) = refs
    p = _hg_block(pl.program_id(0), rev)
    is_ctx = p < N_HG_CTX_BLK
    pos = jnp.where(is_ctx, p % HG_BPS_CTX, (p - N_HG_CTX_BLK) % HG_BPS_LAT)
    bps = jnp.where(is_ctx, HG_BPS_CTX, HG_BPS_LAT)
    first_pos = (bps - 1) if rev else 0
    last_pos = 0 if rev else (bps - 1)

    @pl.when(jnp.logical_and(pos == first_pos, is_ctx))
    def _():
        st_ref[...] = jnp.zeros_like(st_ref)

    @pl.when(jnp.logical_and(pos == first_pos, jnp.logical_not(is_ctx)))
    def _():
        for h in range(HG_HEADS):
            st_ref[h] = s0_ref[h].T

    lb = lb_ref[...]
    oml = 1.0 - lb
    log_oml = jnp.log1p(-lb)
    row = lax.broadcasted_iota(jnp.int32, (HG_CHUNK, HG_CHUNK), 0)
    col = lax.broadcasted_iota(jnp.int32, (HG_CHUNK, HG_CHUNK), 1)
    tri = ((col >= row) if rev else (col <= row)).astype(BF16)
    same_half = (row >= HG_HALF) == (col >= HG_HALF)
    if rev:
        m_diag = jnp.logical_and(same_half, col >= row)
        m_off = jnp.logical_and(row < HG_HALF, col >= HG_HALF)
    else:
        m_diag = jnp.logical_and(same_half, col <= row)
        m_off = jnp.logical_and(row >= HG_HALF, col < HG_HALF)
    row1 = lax.broadcasted_iota(jnp.int32, (HG_CHUNK, 1), 0)
    n_chunks = HG_ROWS // HG_CHUNK

    def chunk(ci, carry):
        c = (n_chunks - 1 - ci) if rev else ci
        r0 = pl.multiple_of(c * HG_CHUNK, HG_CHUNK)
        rows = pl.ds(r0, HG_CHUNK)
        fx = f_ref[rows, :]
        q = q_ref[rows, :]
        v = v_ref[rows, :]
        e = jnp.exp(-jnp.abs(fx))
        r = 1.0 / (1.0 + e)
        er = e * r
        nonneg = fx >= 0.0
        sig = jnp.where(nonneg, r, er)
        nsig = jnp.where(nonneg, er, r)
        f = lb + oml * sig
        logf = jnp.maximum(jnp.log(f), log_oml + (jnp.minimum(fx, 0.0) - jnp.log1p(e)))
        k = oml * nsig
        hi, mid, lo = _split3(logf)
        a = _dot(tri, hi) + (_dot(tri, mid) + _dot(tri, lo))
        a_mid = jnp.where(row1 < HG_HALF, a[HG_HALF // 2:HG_HALF // 2 + 1, :],
                          a[HG_HALF + HG_HALF // 2:HG_HALF + HG_HALF // 2 + 1, :])
        if rev:
            a_b = a[HG_HALF:HG_HALF + 1, :]
            a_last = a[0:1, :]
        else:
            a_b = a[HG_HALF - 1:HG_HALF, :]
            a_last = a[HG_CHUNK - 1:HG_CHUNK, :]
        qd = (q * jnp.exp(jnp.minimum(a - a_mid, HG_EXP_CLAMP))).astype(BF16)
        kd = (k * jnp.exp(jnp.minimum(a_mid - a, HG_EXP_CLAMP))).astype(BF16)
        qo = (q * jnp.exp(jnp.minimum(a - a_b, 0.0))).astype(BF16)
        ko = (k * jnp.exp(jnp.minimum(a_b - a, 0.0))).astype(BF16)
        qa = (q * jnp.exp(a)).astype(BF16)
        kdec = (k * jnp.exp(a_last - a)).astype(BF16)
        dec = jnp.exp(a_last)
        vb = v.astype(BF16)
        for h in range(HG_HEADS):
            hs = slice(h * HG_DK, (h + 1) * HG_DK)
            p1 = _dot_nt(qd[:, hs], kd[:, hs])
            p2 = _dot_nt(qo[:, hs], ko[:, hs])
            attn = jnp.where(m_diag, p1, jnp.where(m_off, p2, 0.0)).astype(BF16)
            st = st_ref[h]
            o_h = _dot_nt(qa[:, hs], st.astype(BF16)) + _dot(attn, vb[:, hs])
            st_ref[h] = dec[:, hs] * st + _dot_tn(vb[:, hs], kdec[:, hs])
            if rev:
                tot = ofw_ref[rows, hs] + o_h
                g = g_ref[rows, hs]
                y = _rms(tot, nw_ref[...]) * (g * _sigmoid(g))
                o_ref[rows, hs] = y.astype(o_ref.dtype)
            else:
                o_ref[rows, hs] = o_h
        return carry

    lax.fori_loop(0, n_chunks, chunk, 0)

    @pl.when(jnp.logical_and(pos == last_pos, is_ctx))
    def _():
        for h in range(HG_HEADS):
            sfin_ref[h] = st_ref[h].T


def _hgrn_dir(proj, o_fw, lb_l, state_hgrn, hg_norm_l, l, rev):
    d = 1 if rev else 0

    def tok(colblk):
        return pl.BlockSpec((HG_ROWS, W512), lambda s: (_hg_block(s, rev), colblk))

    in_specs = [tok(COL_Q_HG), tok(COL_F_BW if rev else COL_F_FW), tok(COL_I_HG)]
    args = [proj, proj, proj]
    if rev:
        in_specs += [tok(COL_G_HG), pl.BlockSpec((HG_ROWS, W512), lambda s: (_hg_block(s, rev), 0))]
        args += [proj, o_fw]
    in_specs += [
        pl.BlockSpec((None, 1, W512), lambda s: (d, 0, 0)),
        pl.BlockSpec((None, None, None, HG_HEADS, HG_DK, HG_DV),
                     lambda s: (_hg_lat_seq(_hg_block(s, rev)), l, d, 0, 0, 0)),
    ]
    args += [lb_l.reshape(2, 1, W512), state_hgrn]
    if rev:
        in_specs.append(pl.BlockSpec((1, HG_DV), lambda s: (0, 0)))
        args.append(hg_norm_l)
    return pl.pallas_call(
        functools.partial(_hgrn_kernel, rev=rev),
        grid=(N_HG_BLK,),
        in_specs=in_specs,
        out_specs=[
            pl.BlockSpec((HG_ROWS, W512), lambda s: (_hg_block(s, rev), 0)),
            pl.BlockSpec((None, HG_HEADS, HG_DK, HG_DV), lambda s: (_hg_ctx_seq(_hg_block(s, rev)), 0, 0, 0)),
        ],
        out_shape=[
            jax.ShapeDtypeStruct((N_TOK, W512), BF16 if rev else F32),
            jax.ShapeDtypeStruct((BATCH, HG_HEADS, HG_DK, HG_DV), F32),
        ],
        scratch_shapes=[pltpu.VMEM((HG_HEADS, HG_DV, HG_DK), F32)],
        compiler_params=pltpu.CompilerParams(dimension_semantics=("arbitrary",),
                                             vmem_limit_bytes=VMEM_LIMIT),
        name="hgrn_bwd" if rev else "hgrn_fwd",
    )(*args)


def _rope_tables():
    n_pairs = DA_QK // 4
    inv_freq = ROPE_BASE ** (-jnp.arange(n_pairs, dtype=F32) / n_pairs)
    t = jnp.arange(DEC_SEQ)
    row_idx = (t // GRID_W).astype(F32)
    col_idx = (t % GRID_W).astype(F32)
    lane = jnp.arange(LANES)
    jj = lane % DA_QK
    use_col = (jj // (DA_QK // 2)) == 1
    second = (jj % (DA_QK // 2)) >= n_pairs
    fr = inv_freq[jj % n_pairs]
    ang_r = row_idx[:, None] * inv_freq[None, :]
    ang_c = col_idx[:, None] * inv_freq[None, :]
    cos_r, sin_r, cos_c, sin_c = jnp.cos(ang_r), jnp.sin(ang_r), jnp.cos(ang_c), jnp.sin(ang_c)
    fi = jj % n_pairs
    cos_t = jnp.where(use_col[None, :], cos_c[:, fi], cos_r[:, fi])
    sin_t = jnp.where(use_col[None, :], sin_c[:, fi], sin_r[:, fi])
    sin_t = jnp.where(second[None, :], sin_t, -sin_t)
    del fr
    cos_all = jnp.concatenate([jnp.ones((N_CTX_TOK, LANES), F32)] + [cos_t] * DEC_BATCH, axis=0)
    sin_all = jnp.concatenate([jnp.zeros((N_CTX_TOK, LANES), F32)] + [sin_t] * DEC_BATCH, axis=0)
    return cos_all, sin_all


def _da_prep_kernel(q_ref, k_ref, v_ref, cos_ref, sin_ref, qo_ref, ko_ref, vo_ref):
    cos = cos_ref[...]
    sin = sin_ref[...]
    lane = lax.broadcasted_iota(jnp.int32, (1, LANES), 1)
    first = (lane % (DA_QK // 2)) < (DA_QK // 4)

    def rope(x):
        partner = jnp.where(first, pltpu.roll(x, LANES - DA_QK // 4, 1), pltpu.roll(x, DA_QK // 4, 1))
        return x * cos + partner * sin

    for h in range(DA_HEADS):
        hs = slice(h * DA_V, (h + 1) * DA_V)
        qo_ref[h] = (rope(q_ref[:, hs]) * Q_SCALE).astype(BF16)
        ko_ref[h] = rope(k_ref[:, hs]).astype(BF16)
        vo_ref[h] = v_ref[:, hs].astype(BF16)


def _da_prep(proj, cos_all, sin_all):
    def tok(colblk):
        return pl.BlockSpec((TM, W512), lambda i: (i, colblk))

    tab = pl.BlockSpec((TM, LANES), lambda i: (i, 0))
    out = pl.BlockSpec((DA_HEADS, TM, DA_V), lambda i: (0, i, 0))
    shp = jax.ShapeDtypeStruct((DA_HEADS, N_TOK, DA_V), BF16)
    return pl.pallas_call(
        _da_prep_kernel,
        grid=(N_TOK // TM,),
        in_specs=[tok(COL_Q_DA), tok(COL_K_DA), tok(COL_V_DA), tab, tab],
        out_specs=[out, out, out],
        out_shape=[shp, shp, shp],
        compiler_params=pltpu.CompilerParams(dimension_semantics=("arbitrary",),
                                             vmem_limit_bytes=VMEM_LIMIT),
        name="da_prep",
    )(proj, proj, proj, cos_all, sin_all)


def _attn_kernel(*refs, has_ctx):
    if has_ctx:
        scal_ref, q_ref, k_ref, v_ref, kc_ref, vc_ref, nw_ref, o_ref = refs
    else:
        scal_ref, q_ref, k_ref, v_ref, nw_ref, o_ref = refs
    lam = scal_ref[0]
    out_scale = scal_ref[1]
    lane = lax.broadcasted_iota(jnp.int32, (1, LANES), 1)
    n_heads = q_ref.shape[0]
    if has_ctx:
        kc = kc_ref[...].astype(BF16)
        vc = vc_ref[...].astype(BF16)

    def component(q_c, k, v):
        s = _dot_nt(q_c, k)
        m = jnp.max(s, axis=-1, keepdims=True)
        if has_ctx:
            s_c = _dot_nt(q_c, kc)
            m = jnp.maximum(m, jnp.max(s_c, axis=-1, keepdims=True))
        p = jnp.exp2(s - m)
        l = jnp.sum(p, axis=-1, keepdims=True)
        acc = _dot(p.astype(BF16), v)
        if has_ctx:
            p_c = jnp.exp2(s_c - m)
            l = l + jnp.sum(p_c, axis=-1, keepdims=True)
            acc = acc + _dot(p_c.astype(BF16), vc)
        return acc / l

    for h in range(n_heads):
        qh = q_ref[h]
        zero = jnp.zeros_like(qh)
        q1 = jnp.where(lane < DA_QK, qh, zero)
        q2 = jnp.where(lane >= DA_QK, qh, zero)
        o = component(q1, k_ref[h], v_ref[h]) - lam * component(q2, k_ref[h], v_ref[h])
        y = (_rms(o, nw_ref[...]) * out_scale).astype(o_ref.dtype)
        if n_heads == 1:
            o_ref[...] = y
        else:
            o_ref[:, h * DA_V:(h + 1) * DA_V] = y


def _attention(scal, q, k, v, cache_k, cache_v, da_norm_l, l):
    smem = pl.BlockSpec(memory_space=pltpu.SMEM)
    nw = pl.BlockSpec((1, DA_V), lambda *a: (0, 0))
    blk = pl.BlockSpec((DA_HEADS, SEQ, DA_V), lambda b: (0, b, 0))
    o_ctx = pl.pallas_call(
        functools.partial(_attn_kernel, has_ctx=False),
        grid=(BATCH,),
        in_specs=[smem, blk, blk, blk, nw],
        out_specs=pl.BlockSpec((SEQ, W512), lambda b: (b, 0)),
        out_shape=jax.ShapeDtypeStruct((N_CTX_TOK, W512), BF16),
        compiler_params=pltpu.CompilerParams(dimension_semantics=("arbitrary",), vmem_limit_bytes=VMEM_LIMIT),
        name="attn_ctx",
    )(scal, q, k, v, da_norm_l)
    nq = DEC_SEQ // ATT_TQ
    off_q = N_CTX_TOK // ATT_TQ
    off_k = N_CTX_TOK // DEC_SEQ
    ck = cache_k.reshape(DEC_BATCH, DEPTH, PAST_LEN, W512)
    cv = cache_v.reshape(DEC_BATCH, DEPTH, PAST_LEN, W512)
    kv = pl.BlockSpec((1, DEC_SEQ, DA_V), lambda b, h, i: (h, off_k + b, 0))
    ckv = pl.BlockSpec((None, None, PAST_LEN, DA_V), lambda b, h, i: (b, l, 0, h))
    o_lat = pl.pallas_call(
        functools.partial(_attn_kernel, has_ctx=True),
        grid=(DEC_BATCH, DA_HEADS, nq),
        in_specs=[smem,
                  pl.BlockSpec((1, ATT_TQ, DA_V), lambda b, h, i: (h, off_q + b * nq + i, 0)),
                  kv, kv, ckv, ckv, nw],
        out_specs=pl.BlockSpec((ATT_TQ, DA_V), lambda b, h, i: (b * nq + i, h)),
        out_shape=jax.ShapeDtypeStruct((DEC_BATCH * DEC_SEQ, W512), BF16),
        compiler_params=pltpu.CompilerParams(dimension_semantics=("arbitrary", "arbitrary", "arbitrary"),
                                             vmem_limit_bytes=VMEM_LIMIT),
        name="attn_lat",
    )(scal, q, k, v, ck, cv, da_norm_l)
    return o_ctx, o_lat


def _outproj_kernel(hg_ref, dac_ref, dal_ref, x_ref, mod_ref, w_ref, nw_ref, wr_ref, br_ref,
                    xo_ref, h_ref, lg_ref, wbf_ref):
    i = pl.program_id(0)

    @pl.when(i == 0)
    def _():
        wbf_ref[...] = w_ref[...].astype(BF16)

    is_ctx = i < N_CTX_TOK // TM
    da = jnp.where(is_ctx, dac_ref[...], dal_ref[...])
    mix = _dot(hg_ref[...], wbf_ref[0:HG_WIDTH, :]) + _dot(da, wbf_ref[HG_WIDTH:, :])
    g1 = mod_ref[:, 2 * D_MODEL:3 * D_MODEL]
    sh2 = mod_ref[:, 3 * D_MODEL:4 * D_MODEL]
    sc2 = mod_ref[:, 4 * D_MODEL:5 * D_MODEL]
    x = x_ref[...] + g1 * mix
    xo_ref[...] = x
    h = _rms(x, nw_ref[...]) * (1.0 + sc2) + sh2
    h_ref[...] = _pack_rows(h)
    lg_ref[...] = _dot_precise(h, wr_ref[...]) + br_ref[...]


def _outproj(o_hg, o_da_ctx, o_da_lat, x, mods_l, w_out, norm2_l, w_router_l, b_router_l, l):
    n_ctx_t = N_CTX_TOK // TM
    n_lat_t = (N_TOK - N_CTX_TOK) // TM
    tokspec = lambda w: pl.BlockSpec((TM, w), lambda i: (i, 0))
    return pl.pallas_call(
        _outproj_kernel,
        grid=(N_TOK // TM,),
        in_specs=[
            tokspec(W512),
            pl.BlockSpec((TM, W512), lambda i: (jnp.minimum(i, n_ctx_t - 1), 0)),
            pl.BlockSpec((TM, W512), lambda i: (jnp.clip(i - n_ctx_t, 0, n_lat_t - 1), 0)),
            tokspec(D_MODEL),
            pl.BlockSpec((None, 1, 6 * D_MODEL), lambda i: (i * TM // COND_GROUP, 0, 0)),
            pl.BlockSpec((None, D_MODEL, D_MODEL), lambda i: (l, 0, 0)),
            pl.BlockSpec((1, D_MODEL), lambda i: (0, 0)),
            pl.BlockSpec((D_MODEL, N_EXPERTS), lambda i: (0, 0)),
            pl.BlockSpec((1, N_EXPERTS), lambda i: (0, 0)),
        ],
        out_specs=[tokspec(D_MODEL), tokspec(D_MODEL // 2), tokspec(N_EXPERTS)],
        out_shape=[jax.ShapeDtypeStruct((N_TOK, D_MODEL), F32),
                   jax.ShapeDtypeStruct((N_TOK, D_MODEL // 2), jnp.uint32),
                   jax.ShapeDtypeStruct((N_TOK, N_EXPERTS), F32)],
        scratch_shapes=[pltpu.VMEM((D_MODEL, D_MODEL), BF16)],
        compiler_params=pltpu.CompilerParams(dimension_semantics=("arbitrary",),
                                             vmem_limit_bytes=VMEM_LIMIT),
        name="outproj_norm_router",
    )(o_hg, o_da_ctx, o_da_lat, x, mods_l, w_out, norm2_l, w_router_l, b_router_l)


def _moe_kernel(be_ref, nu_ref, x_ref, wgu_ref, bgu_ref, wdn_ref, bdn_ref, o_ref, wgu_bf, wdn_bf):
    i = pl.program_id(0)
    used = i < nu_ref[0]
    prev = be_ref[jnp.maximum(i - 1, 0)]
    fresh = jnp.logical_or(i == 0, be_ref[i] != prev)

    @pl.when(jnp.logical_and(used, fresh))
    def _():
        wgu_bf[...] = wgu_ref[...].astype(BF16)
        wdn_bf[...] = wdn_ref[...].astype(BF16)

    @pl.when(used)
    def _():
        x_lo, x_hi = _unpack_rows(x_ref[...])
        x = jnp.concatenate([x_lo.astype(BF16), x_hi.astype(BF16)], axis=1)
        gu = _dot(x, wgu_bf[...]) + bgu_ref[...]
        gate = jnp.minimum(gu[:, :D_FF], SWIGLU_LIMIT)
        lin = jnp.clip(gu[:, D_FF:], -SWIGLU_LIMIT, SWIGLU_LIMIT)
        act = (lin + 1.0) * gate * _sigmoid(SWIGLU_ALPHA * gate)
        y = _dot(act.astype(BF16), wdn_bf[...]) + bdn_ref[...]
        o_ref[...] = _pack_rows(y)


def _moe_experts(blk_exp, n_used, xs, w_gu, b_gu, w_dn, b_dn, l):
    def rowblk(i, be, nu):
        return (jnp.minimum(i, nu[0] - 1), 0)

    grid_spec = pltpu.PrefetchScalarGridSpec(
        num_scalar_prefetch=2,
        grid=(MOE_NBLK,),
        in_specs=[
            pl.BlockSpec((MOE_BLK, D_MODEL // 2), rowblk),
            pl.BlockSpec((None, None, D_MODEL, 2 * D_FF), lambda i, be, nu: (l, be[i], 0, 0)),
            pl.BlockSpec((None, None, 1, 2 * D_FF), lambda i, be, nu: (l, be[i], 0, 0)),
            pl.BlockSpec((None, None, D_FF, D_MODEL), lambda i, be, nu: (l, be[i], 0, 0)),
            pl.BlockSpec((None, None, 1, D_MODEL), lambda i, be, nu: (l, be[i], 0, 0)),
        ],
        out_specs=pl.BlockSpec((MOE_BLK, D_MODEL // 2), rowblk),
        scratch_shapes=[pltpu.VMEM((D_MODEL, 2 * D_FF), BF16), pltpu.VMEM((D_FF, D_MODEL), BF16)],
    )
    return pl.pallas_call(
        _moe_kernel,
        grid_spec=grid_spec,
        out_shape=jax.ShapeDtypeStruct((MOE_NBLK * MOE_BLK, D_MODEL // 2), jnp.uint32),
        compiler_params=pltpu.CompilerParams(dimension_semantics=("arbitrary",),
                                             vmem_limit_bytes=VMEM_LIMIT),
        name="moe_experts",
    )(blk_exp, n_used, xs, w_gu, b_gu.reshape(DEPTH, N_EXPERTS, 1, 2 * D_FF),
      w_dn, b_dn.reshape(DEPTH, N_EXPERTS, 1, D_MODEL))


def _route_kernel(lg_ref, exp_ref, gate_ref, rank_ref, cnt_ref, run_ref):
    i = pl.program_id(0)

    @pl.when(i == 0)
    def _():
        run_ref[...] = jnp.zeros_like(run_ref)

    lg = lg_ref[...]
    lane = lax.broadcasted_iota(jnp.int32, (1, N_EXPERTS), 1)
    lane_k = lax.broadcasted_iota(jnp.int32, (1, TOP_K), 1)
    vals, idxs, hots = [], [], []
    for _ in range(TOP_K):
        m = jnp.max(lg, axis=-1, keepdims=True)
        idx = jnp.min(jnp.where(lg == m, lane, N_EXPERTS), axis=-1, keepdims=True)
        hot = lane == idx
        vals.append(m)
        idxs.append(idx)
        hots.append(hot)
        lg = jnp.where(hot, -jnp.inf, lg)
    exps = [jnp.exp(v - vals[0]) for v in vals]
    denom = exps[0] + exps[1] + exps[2] + exps[3]
    tokhot = (hots[0] | hots[1] | hots[2] | hots[3]).astype(BF16)
    r = lax.broadcasted_iota(jnp.int32, (TM, TM), 0)
    cidx = lax.broadcasted_iota(jnp.int32, (TM, TM), 1)
    before = _dot((cidx < r).astype(BF16), tokhot) + run_ref[...]
    e_out = jnp.zeros((TM, TOP_K), jnp.int32)
    g_out = jnp.zeros((TM, TOP_K), F32)
    r_out = jnp.zeros((TM, TOP_K), jnp.int32)
    for k in range(TOP_K):
        rk = jnp.sum(jnp.where(hots[k], before, 0.0), axis=-1, keepdims=True).astype(jnp.int32)
        e_out = jnp.where(lane_k == k, idxs[k], e_out)
        g_out = jnp.where(lane_k == k, exps[k] / denom, g_out)
        r_out = jnp.where(lane_k == k, rk, r_out)
    exp_ref[...] = e_out
    gate_ref[...] = g_out
    rank_ref[...] = r_out
    run_ref[...] += jnp.sum(tokhot.astype(F32), axis=0, keepdims=True)
    cnt_ref[...] = run_ref[...].astype(jnp.int32)


def _route(logits):
    tk = pl.BlockSpec((TM, TOP_K), lambda i: (i, 0))
    top_exp, gates, rank, counts = pl.pallas_call(
        _route_kernel,
        grid=(N_TOK // TM,),
        in_specs=[pl.BlockSpec((TM, N_EXPERTS), lambda i: (i, 0))],
        out_specs=[tk, tk, tk, pl.BlockSpec((1, N_EXPERTS), lambda i: (0, 0))],
        out_shape=[jax.ShapeDtypeStruct((N_TOK, TOP_K), jnp.int32),
                   jax.ShapeDtypeStruct((N_TOK, TOP_K), F32),
                   jax.ShapeDtypeStruct((N_TOK, TOP_K), jnp.int32),
                   jax.ShapeDtypeStruct((1, N_EXPERTS), jnp.int32)],
        scratch_shapes=[pltpu.VMEM((1, N_EXPERTS), F32)],
        compiler_params=pltpu.CompilerParams(dimension_semantics=("arbitrary",)),
        name="route",
    )(logits)
    counts = counts[0]
    padded = (counts + MOE_BLK - 1) // MOE_BLK * MOE_BLK
    pad_end = jnp.cumsum(padded)
    pad_start = pad_end - padded
    start = jnp.cumsum(counts) - counts
    eq = top_exp[:, :, None] == jnp.arange(N_EXPERTS, dtype=jnp.int32)[None, None, :]
    dest = (jnp.sum(jnp.where(eq, pad_start[None, None, :], 0), axis=-1) + rank).T.reshape(-1)
    blk_exp = jnp.minimum(
        jnp.sum((pad_end[None, :] <= (jnp.arange(MOE_NBLK, dtype=jnp.int32) * MOE_BLK)[:, None]).astype(jnp.int32),
                axis=-1), N_EXPERTS - 1).astype(jnp.int32)
    n_used = (pad_end[-1:] // MOE_BLK).astype(jnp.int32)
    tok_ids = jnp.arange(N_ASG, dtype=jnp.int32) // TOP_K
    _, tok_sorted = lax.sort((top_exp.reshape(-1), tok_ids), num_keys=1, is_stable=True)
    slot = jnp.arange(MOE_NBLK * MOE_BLK, dtype=jnp.int32)
    e_slot = jnp.repeat(blk_exp, MOE_BLK)
    src = jnp.clip(start[e_slot] + slot - pad_start[e_slot], 0, N_ASG - 1)
    slot_tok = tok_sorted[src]
    return slot_tok, gates, dest, blk_exp, n_used


def _combine_kernel(y0_ref, y1_ref, y2_ref, y3_ref, gate_ref, x_ref, mod_ref, o_ref):
    g = gate_ref[...]
    half = D_MODEL // 2
    acc_lo = jnp.zeros((TM, half), F32)
    acc_hi = jnp.zeros((TM, half), F32)
    for k, y_ref in enumerate((y0_ref, y1_ref, y2_ref, y3_ref)):
        lo, hi = _unpack_rows(y_ref[...])
        acc_lo = acc_lo + g[:, k:k + 1] * lo
        acc_hi = acc_hi + g[:, k:k + 1] * hi
    g2 = mod_ref[:, 5 * D_MODEL:]
    o_ref[:, :half] = x_ref[:, :half] + g2[:, :half] * acc_lo
    o_ref[:, half:] = x_ref[:, half:] + g2[:, half:] * acc_hi


def _combine(ys_tok, gates, x_mid, mods_l):
    n_i = N_TOK // TM

    def ysblk(k):
        return pl.BlockSpec((TM, D_MODEL // 2), lambda i: (k * n_i + i, 0))

    return pl.pallas_call(
        _combine_kernel,
        grid=(n_i,),
        in_specs=[ysblk(0), ysblk(1), ysblk(2), ysblk(3),
                  pl.BlockSpec((TM, TOP_K), lambda i: (i, 0)),
                  pl.BlockSpec((TM, D_MODEL), lambda i: (i, 0)),
                  pl.BlockSpec((None, 1, 6 * D_MODEL), lambda i: (i * TM // COND_GROUP, 0, 0))],
        out_specs=pl.BlockSpec((TM, D_MODEL), lambda i: (i, 0)),
        out_shape=jax.ShapeDtypeStruct((N_TOK, D_MODEL), F32),
        compiler_params=pltpu.CompilerParams(dimension_semantics=("arbitrary",), vmem_limit_bytes=VMEM_LIMIT),
        name="moe_combine",
    )(ys_tok, ys_tok, ys_tok, ys_tok, gates, x_mid, mods_l)


def _final_norm_kernel(x_ref, w_ref, o_ref):
    o_ref[...] = _rms(x_ref[...], w_ref[...])


def _final_norm(x, w):
    return pl.pallas_call(
        _final_norm_kernel,
        grid=(N_TOK // TM,),
        in_specs=[pl.BlockSpec((TM, D_MODEL), lambda i: (i, 0)), pl.BlockSpec((1, D_MODEL), lambda i: (0, 0))],
        out_specs=pl.BlockSpec((TM, D_MODEL), lambda i: (i, 0)),
        out_shape=jax.ShapeDtypeStruct((N_TOK, D_MODEL), F32),
        compiler_params=pltpu.CompilerParams(dimension_semantics=("arbitrary",)),
        name="final_norm",
    )(x, w)


def kernel(x_prompt, x_sample, c, cache_k, cache_v, state_hgrn, c_ctx, norm1_w, norm2_w, w_ada, b_ada, w_in, hg_lb, hg_norm_w, lam_q1, lam_k1, lam_q2, lam_k2, da_norm_w, w_out, w_router, b_router, w_gu, b_gu, w_dn, b_dn, norm_f_w):
    lb_all = jnp.cumsum(jax.nn.softmax(hg_lb.astype(F32), axis=0), axis=0)
    lb_all = lb_all - lb_all[0:1]
    lam_init = [0.8 - 0.6 * math.exp(-0.3 * l) for l in range(DEPTH)]
    lam = (jnp.exp(jnp.sum(lam_q1 * lam_k1, axis=-1)) - jnp.exp(jnp.sum(lam_q2 * lam_k2, axis=-1))
           + jnp.asarray(lam_init, F32))
    cos_all, sin_all = _rope_tables()

    conds = jnp.concatenate([c_ctx[None, :], c, jnp.zeros((COND_ROWS - 1 - DEC_BATCH, D_MODEL), F32)], axis=0)
    mods = _ada_mod(conds, w_ada, b_ada)
    mods = mods[:, :N_COND].reshape(DEPTH, N_COND, 1, 6 * D_MODEL)

    x = jnp.concatenate([x_prompt.reshape(N_CTX_TOK, D_MODEL), x_sample.reshape(-1, D_MODEL)], axis=0)
    ks_l, vs_l, ss_l = [], [], []
    for l in range(DEPTH):
        proj = _inproj(x, norm1_w[l][None, :], mods[l], w_in, l)
        ks_l.append(proj[:N_CTX_TOK, COL_K_DA * W512:(COL_K_DA + 1) * W512].reshape(BATCH, SEQ, DA_HEADS, 2 * DA_QK))
        vs_l.append(proj[:N_CTX_TOK, COL_V_DA * W512:(COL_V_DA + 1) * W512].reshape(BATCH, SEQ, DA_HEADS, DA_V))
        o_fw, s_fw = _hgrn_dir(proj, None, lb_all[l], state_hgrn, None, l, rev=False)
        o_hg, s_bw = _hgrn_dir(proj, o_fw, lb_all[l], state_hgrn, hg_norm_w[l][None, :], l, rev=True)
        ss_l.append(jnp.stack([s_fw, s_bw], axis=1))
        q_r, k_r, v_b = _da_prep(proj, cos_all, sin_all)
        scal = jnp.stack([lam[l], jnp.asarray(1.0 - lam_init[l], F32)])
        o_da_ctx, o_da_lat = _attention(scal, q_r, k_r, v_b, cache_k, cache_v, da_norm_w[l][None, :], l)
        x_mid, h2, logits = _outproj(o_hg, o_da_ctx, o_da_lat, x, mods[l], w_out, norm2_w[l][None, :],
                                     w_router[l], b_router[l][None, :], l)
        slot_tok, gates, dest, blk_exp, n_used = _route(logits)
        xs = h2.at[slot_tok].get(mode="promise_in_bounds")
        ys = _moe_experts(blk_exp, n_used, xs, w_gu, b_gu, w_dn, b_dn, l)
        ys_tok = ys.at[dest].get(mode="promise_in_bounds")
        x = _combine(ys_tok, gates, x_mid, mods[l])
    y = _final_norm(x, norm_f_w[None, :])
    y_prompt = y[:N_CTX_TOK].reshape(BATCH, SEQ, D_MODEL)
    y_sample = y[N_CTX_TOK:].reshape(DEC_BATCH, DEC_SEQ, D_MODEL)
    return (y_prompt, y_sample, jnp.stack(ks_l, axis=1), jnp.stack(vs_l, axis=1), jnp.stack(ss_l, axis=1))
```

```python
import functools
import math

import jax
import jax.numpy as jnp
from jax import lax
from jax.experimental import pallas as pl
from jax.experimental.pallas import tpu as pltpu

F32 = jnp.float32
BF16 = jnp.bfloat16

D_MODEL = 1024
BATCH = 16
SEQ = 256
DEPTH = 4
DEC_BATCH = 2
DEC_SEQ = 4096
PAST_LEN = 256
GRID_W = 64
HG_HEADS = 4
HG_DK = 128
HG_DV = 128
HG_WIDTH = HG_HEADS * HG_DK
DA_HEADS = 4
DA_QK = 64
DA_V = 2 * DA_QK
DA_WIDTH = DA_HEADS * DA_V
ROPE_BASE = 10000.0
IN_DIM = 5 * HG_WIDTH + 3 * DA_WIDTH
N_EXPERTS = 32
TOP_K = 4
D_FF = D_MODEL
SWIGLU_ALPHA = 1.702
SWIGLU_LIMIT = 7.0
EPS = 1e-6

N_CTX_TOK = BATCH * SEQ
N_TOK = N_CTX_TOK + DEC_BATCH * DEC_SEQ
COND_GROUP = DEC_SEQ
assert N_CTX_TOK == COND_GROUP
N_COND = N_TOK // COND_GROUP
COND_ROWS = 8

COL_Q_HG, COL_F_FW, COL_F_BW, COL_I_HG, COL_G_HG, COL_Q_DA, COL_K_DA, COL_V_DA = range(8)
W512 = 512
LANES = 128

HG_CHUNK = 64
HG_HALF = HG_CHUNK // 2
HG_ROWS = 256
HG_EXP_CLAMP = 80.0

TM = 512
TN_IN = 2048
ATT_TQ = 256
Q_SCALE = math.log2(math.e) / math.sqrt(DA_QK)
ATT_KC = 512
ATT_MIN_SUM = 2.0 ** -40
MOE_BLK = 512
N_ASG = N_TOK * TOP_K
MOE_NBLK = N_ASG // MOE_BLK + N_EXPERTS
VMEM_LIMIT = 56 * 1024 * 1024


def _dot(a, b):
    return jnp.dot(a, b, preferred_element_type=F32)


def _dot_nt(a, b):
    return lax.dot_general(a, b, (((1,), (1,)), ((), ())), preferred_element_type=F32)


def _dot_tn(a, b):
    return lax.dot_general(a, b, (((0,), (0,)), ((), ())), preferred_element_type=F32)


def _split2(x):
    hi = x.astype(BF16)
    lo = (x - hi.astype(F32)).astype(BF16)
    return hi, lo


def _split3(x):
    hi = x.astype(BF16)
    r = x - hi.astype(F32)
    mid = r.astype(BF16)
    lo = (r - mid.astype(F32)).astype(BF16)
    return hi, mid, lo


def _dot_precise(a, b):
    a_hi, a_lo = _split2(a)
    b_hi, b_lo = _split2(b)
    return _dot(a_hi, b_hi) + (_dot(a_hi, b_lo) + _dot(a_lo, b_hi))


def _pack_rows(x):
    n = x.shape[1] // 2
    lo = pltpu.bitcast(x[:, :n].astype(BF16).astype(F32), jnp.uint32)
    hi = pltpu.bitcast(x[:, n:].astype(BF16).astype(F32), jnp.uint32)
    return (lo >> 16) | hi


def _unpack_rows(u):
    lo = pltpu.bitcast(u << 16, F32)
    hi = pltpu.bitcast(u & jnp.uint32(0xFFFF0000), F32)
    return lo, hi


def _rms(x, w):
    ms = jnp.mean(x * x, axis=-1, keepdims=True)
    return x * lax.rsqrt(ms + EPS) * w


def _sigmoid(x):
    return 1.0 / (1.0 + jnp.exp(-x))


def _ada_kernel(cond_ref, w_ref, b_ref, o_ref):
    cnd = cond_ref[...]
    s = cnd * _sigmoid(cnd)
    o_ref[...] = _dot_precise(s, w_ref[...]) + b_ref[...]


def _ada_mod(conds, w_ada, b_ada):
    nj = 6 * D_MODEL // D_MODEL
    return pl.pallas_call(
        _ada_kernel,
        grid=(DEPTH, nj),
        in_specs=[
            pl.BlockSpec((COND_ROWS, D_MODEL), lambda l, j: (0, 0)),
            pl.BlockSpec((None, D_MODEL, D_MODEL), lambda l, j: (l, 0, j)),
            pl.BlockSpec((None, 1, D_MODEL), lambda l, j: (l, 0, j)),
        ],
        out_specs=pl.BlockSpec((None, COND_ROWS, D_MODEL), lambda l, j: (l, 0, j)),
        out_shape=jax.ShapeDtypeStruct((DEPTH, COND_ROWS, 6 * D_MODEL), F32),
        compiler_params=pltpu.CompilerParams(dimension_semantics=("arbitrary", "arbitrary"),
                                             vmem_limit_bytes=VMEM_LIMIT),
        name="ada_mod",
    )(conds, w_ada, b_ada.reshape(DEPTH, 1, 6 * D_MODEL))


def _inproj_kernel(x_ref, nw_ref, mod_ref, w_ref, o_ref, wbf_ref):
    @pl.when(pl.program_id(1) == 0)
    def _():
        wbf_ref[...] = w_ref[...].astype(BF16)

    x = x_ref[...]
    sh = mod_ref[:, 0:D_MODEL]
    sc = mod_ref[:, D_MODEL:2 * D_MODEL]
    h = _rms(x, nw_ref[...]) * (1.0 + sc) + sh
    o_ref[...] = _dot(h.astype(BF16), wbf_ref[...])


def _inproj(x, norm_w_l, mods_l, w_in, l):
    n_i = N_TOK // TM
    return pl.pallas_call(
        _inproj_kernel,
        grid=(IN_DIM // TN_IN, n_i),
        in_specs=[
            pl.BlockSpec((TM, D_MODEL), lambda n, i: (i, 0)),
            pl.BlockSpec((1, D_MODEL), lambda n, i: (0, 0)),
            pl.BlockSpec((None, 1, 6 * D_MODEL), lambda n, i: (i * TM // COND_GROUP, 0, 0)),
            pl.BlockSpec((None, D_MODEL, TN_IN), lambda n, i: (l, 0, n)),
        ],
        out_specs=pl.BlockSpec((TM, TN_IN), lambda n, i: (i, n)),
        out_shape=jax.ShapeDtypeStruct((N_TOK, IN_DIM), F32),
        scratch_shapes=[pltpu.VMEM((D_MODEL, TN_IN), BF16)],
        compiler_params=pltpu.CompilerParams(dimension_semantics=("arbitrary", "arbitrary"),
                                             vmem_limit_bytes=VMEM_LIMIT),
        name="norm_inproj",
    )(x, norm_w_l, mods_l, w_in)


N_HG_BLK = N_TOK // HG_ROWS
N_HG_CTX_BLK = N_CTX_TOK // HG_ROWS
HG_BPS_CTX = SEQ // HG_ROWS
HG_BPS_LAT = DEC_SEQ // HG_ROWS


def _hg_block(step, rev):
    return (N_HG_BLK - 1 - step) if rev else step


def _hg_lat_seq(p):
    return jnp.clip((p - N_HG_CTX_BLK) // HG_BPS_LAT, 0, DEC_BATCH - 1)


def _hg_ctx_seq(p):
    return jnp.minimum(p // HG_BPS_CTX, BATCH - 1)


def _hgrn_kernel(*refs, rev):
    if rev:
        (q_ref, f_ref, v_ref, g_ref, ofw_ref, lb_ref, s0_ref, nw_ref, o_ref, sfin_ref, st_ref) = refs
    else:
        (q_ref, f_ref, v_ref, lb_ref, s0_ref, o_ref, sfin_ref, st_ref) = refs
    p = _hg_block(pl.program_id(0), rev)
    is_ctx = p < N_HG_CTX_BLK
    pos = jnp.where(is_ctx, p % HG_BPS_CTX, (p - N_HG_CTX_BLK) % HG_BPS_LAT)
    bps = jnp.where(is_ctx, HG_BPS_CTX, HG_BPS_LAT)
    first_pos = (bps - 1) if rev else 0
    last_pos = 0 if rev else (bps - 1)

    @pl.when(jnp.logical_and(pos == first_pos, is_ctx))
    def _():
        st_ref[...] = jnp.zeros_like(st_ref)

    @pl.when(jnp.logical_and(pos == first_pos, jnp.logical_not(is_ctx)))
    def _():
        for h in range(HG_HEADS):
            st_ref[h] = s0_ref[h].T

    lb = lb_ref[...]
    oml = 1.0 - lb
    log_oml = jnp.log1p(-lb)
    row = lax.broadcasted_iota(jnp.int32, (HG_CHUNK, HG_CHUNK), 0)
    col = lax.broadcasted_iota(jnp.int32, (HG_CHUNK, HG_CHUNK), 1)
    tri = ((col >= row) if rev else (col <= row)).astype(BF16)
    same_half = (row >= HG_HALF) == (col >= HG_HALF)
    if rev:
        m_diag = jnp.logical_and(same_half, col >= row)
        m_off = jnp.logical_and(row < HG_HALF, col >= HG_HALF)
    else:
        m_diag = jnp.logical_and(same_half, col <= row)
        m_off = jnp.logical_and(row >= HG_HALF, col < HG_HALF)
    row1 = lax.broadcasted_iota(jnp.int32, (HG_CHUNK, 1), 0)
    n_chunks = HG_ROWS // HG_CHUNK

    def chunk(ci, carry):
        c = (n_chunks - 1 - ci) if rev else ci
        r0 = pl.multiple_of(c * HG_CHUNK, HG_CHUNK)
        rows = pl.ds(r0, HG_CHUNK)
        fx = f_ref[rows, :]
        q = q_ref[rows, :]
        v = v_ref[rows, :]
        e = jnp.exp(-jnp.abs(fx))
        r = 1.0 / (1.0 + e)
        er = e * r
        nonneg = fx >= 0.0
        sig = jnp.where(nonneg, r, er)
        nsig = jnp.where(nonneg, er, r)
        f = lb + oml * sig
        logf = jnp.maximum(jnp.log(f), log_oml + (jnp.minimum(fx, 0.0) - jnp.log1p(e)))
        k = oml * nsig
        hi, mid, lo = _split3(logf)
        a = _dot(tri, hi) + (_dot(tri, mid) + _dot(tri, lo))
        a_mid = jnp.where(row1 < HG_HALF, a[HG_HALF // 2:HG_HALF // 2 + 1, :],
                          a[HG_HALF + HG_HALF // 2:HG_HALF + HG_HALF // 2 + 1, :])
        if rev:
            a_b = a[HG_HALF:HG_HALF + 1, :]
            a_last = a[0:1, :]
        else:
            a_b = a[HG_HALF - 1:HG_HALF, :]
            a_last = a[HG_CHUNK - 1:HG_CHUNK, :]
        qd = (q * jnp.exp(jnp.minimum(a - a_mid, HG_EXP_CLAMP))).astype(BF16)
        kd = (k * jnp.exp(jnp.minimum(a_mid - a, HG_EXP_CLAMP))).astype(BF16)
        qo = (q * jnp.exp(jnp.minimum(a - a_b, 0.0))).astype(BF16)
        ko = (k * jnp.exp(jnp.minimum(a_b - a, 0.0))).astype(BF16)
        qa = (q * jnp.exp(a)).astype(BF16)
        kdec = (k * jnp.exp(a_last - a)).astype(BF16)
        dec = jnp.exp(a_last)
        vb = v.astype(BF16)
        for h in range(HG_HEADS):
            hs = slice(h * HG_DK, (h + 1) * HG_DK)
            p1 = _dot_nt(qd[:, hs], kd[:, hs])
            p2 = _dot_nt(qo[:, hs], ko[:, hs])
            attn = jnp.where(m_diag, p1, jnp.where(m_off, p2, 0.0)).astype(BF16)
            st = st_ref[h]
            o_h = _dot_nt(qa[:, hs], st.astype(BF16)) + _dot(attn, vb[:, hs])
            st_ref[h] = dec[:, hs] * st + _dot_tn(vb[:, hs], kdec[:, hs])
            if rev:
                tot = ofw_ref[rows, hs] + o_h
                g = g_ref[rows, hs]
                y = _rms(tot, nw_ref[...]) * (g * _sigmoid(g))
                o_ref[rows, hs] = y.astype(o_ref.dtype)
            else:
                o_ref[rows, hs] = o_h
        return carry

    lax.fori_loop(0, n_chunks, chunk, 0)

    @pl.when(jnp.logical_and(pos == last_pos, is_ctx))
    def _():
        for h in range(HG_HEADS):
            sfin_ref[h] = st_ref[h].T


def _hgrn_dir(proj, o_fw, lb_l, state_hgrn, hg_norm_l, l, rev):
    d = 1 if rev else 0

    def tok(colblk):
        return pl.BlockSpec((HG_ROWS, W512), lambda s: (_hg_block(s, rev), colblk))

    in_specs = [tok(COL_Q_HG), tok(COL_F_BW if rev else COL_F_FW), tok(COL_I_HG)]
    args = [proj, proj, proj]
    if rev:
        in_specs += [tok(COL_G_HG), pl.BlockSpec((HG_ROWS, W512), lambda s: (_hg_block(s, rev), 0))]
        args += [proj, o_fw]
    in_specs += [
        pl.BlockSpec((None, 1, W512), lambda s: (d, 0, 0)),
        pl.BlockSpec((None, None, None, HG_HEADS, HG_DK, HG_DV),
                     lambda s: (_hg_lat_seq(_hg_block(s, rev)), l, d, 0, 0, 0)),
    ]
    args += [lb_l.reshape(2, 1, W512), state_hgrn]
    if rev:
        in_specs.append(pl.BlockSpec((1, HG_DV), lambda s: (0, 0)))
        args.append(hg_norm_l)
    return pl.pallas_call(
        functools.partial(_hgrn_kernel, rev=rev),
        grid=(N_HG_BLK,),
        in_specs=in_specs,
        out_specs=[
            pl.BlockSpec((HG_ROWS, W512), lambda s: (_hg_block(s, rev), 0)),
            pl.BlockSpec((None, HG_HEADS, HG_DK, HG_DV), lambda s: (_hg_ctx_seq(_hg_block(s, rev)), 0, 0, 0)),
        ],
        out_shape=[
            jax.ShapeDtypeStruct((N_TOK, W512), BF16 if rev else F32),
            jax.ShapeDtypeStruct((BATCH, HG_HEADS, HG_DK, HG_DV), F32),
        ],
        scratch_shapes=[pltpu.VMEM((HG_HEADS, HG_DV, HG_DK), F32)],
        compiler_params=pltpu.CompilerParams(dimension_semantics=("arbitrary",),
                                             vmem_limit_bytes=VMEM_LIMIT),
        name="hgrn_bwd" if rev else "hgrn_fwd",
    )(*args)


def _rope_tables():
    n_pairs = DA_QK // 4
    inv_freq = ROPE_BASE ** (-jnp.arange(n_pairs, dtype=F32) / n_pairs)
    t = jnp.arange(DEC_SEQ)
    row_idx = (t // GRID_W).astype(F32)
    col_idx = (t % GRID_W).astype(F32)
    lane = jnp.arange(LANES)
    jj = lane % DA_QK
    use_col = (jj // (DA_QK // 2)) == 1
    second = (jj % (DA_QK // 2)) >= n_pairs
    fr = inv_freq[jj % n_pairs]
    ang_r = row_idx[:, None] * inv_freq[None, :]
    ang_c = col_idx[:, None] * inv_freq[None, :]
    cos_r, sin_r, cos_c, sin_c = jnp.cos(ang_r), jnp.sin(ang_r), jnp.cos(ang_c), jnp.sin(ang_c)
    fi = jj % n_pairs
    cos_t = jnp.where(use_col[None, :], cos_c[:, fi], cos_r[:, fi])
    sin_t = jnp.where(use_col[None, :], sin_c[:, fi], sin_r[:, fi])
    sin_t = jnp.where(second[None, :], sin_t, -sin_t)
    del fr
    cos_all = jnp.concatenate([jnp.ones((N_CTX_TOK, LANES), F32)] + [cos_t] * DEC_BATCH, axis=0)
    sin_all = jnp.concatenate([jnp.zeros((N_CTX_TOK, LANES), F32)] + [sin_t] * DEC_BATCH, axis=0)
    return cos_all, sin_all


def _da_prep_kernel(q_ref, k_ref, v_ref, cos_ref, sin_ref, qo_ref, ko_ref, vo_ref, kn_ref):
    cos = cos_ref[...]
    sin = sin_ref[...]
    lane = lax.broadcasted_iota(jnp.int32, (1, LANES), 1)
    first = (lane % (DA_QK // 2)) < (DA_QK // 4)
    lane_n = lax.broadcasted_iota(jnp.int32, (1, 2 * DA_HEADS), 1)

    def rope(x):
        partner = jnp.where(first, pltpu.roll(x, LANES - DA_QK // 4, 1), pltpu.roll(x, DA_QK // 4, 1))
        return x * cos + partner * sin

    kn = jnp.zeros((TM, 2 * DA_HEADS), F32)
    ones = jnp.ones((TM, DA_V), BF16)
    for h in range(DA_HEADS):
        hs = slice(h * DA_V, (h + 1) * DA_V)
        qo_ref[h] = (rope(q_ref[:, hs]) * Q_SCALE).astype(BF16)
        kr = rope(k_ref[:, hs])
        ko_ref[h] = kr.astype(BF16)
        ksq = kr * kr
        n1 = jnp.sum(jnp.where(lane < DA_QK, ksq, 0.0), axis=-1, keepdims=True)
        n2 = jnp.sum(jnp.where(lane >= DA_QK, ksq, 0.0), axis=-1, keepdims=True)
        kn = jnp.where(lane_n == 2 * h, n1, jnp.where(lane_n == 2 * h + 1, n2, kn))
        vo_ref[h, :, 0:DA_V] = v_ref[:, hs].astype(BF16)
        vo_ref[h, :, DA_V:] = ones
    kn_ref[...] = kn


def _da_prep(proj, cos_all, sin_all):
    def tok(colblk):
        return pl.BlockSpec((TM, W512), lambda i: (i, colblk))

    tab = pl.BlockSpec((TM, LANES), lambda i: (i, 0))
    out = pl.BlockSpec((DA_HEADS, TM, DA_V), lambda i: (0, i, 0))
    shp = jax.ShapeDtypeStruct((DA_HEADS, N_TOK, DA_V), BF16)
    return pl.pallas_call(
        _da_prep_kernel,
        grid=(N_TOK // TM,),
        in_specs=[tok(COL_Q_DA), tok(COL_K_DA), tok(COL_V_DA), tab, tab],
        out_specs=[out, out, pl.BlockSpec((DA_HEADS, TM, 2 * DA_V), lambda i: (0, i, 0)),
                   pl.BlockSpec((TM, 2 * DA_HEADS), lambda i: (i, 0))],
        out_shape=[shp, shp, jax.ShapeDtypeStruct((DA_HEADS, N_TOK, 2 * DA_V), BF16),
                   jax.ShapeDtypeStruct((N_TOK, 2 * DA_HEADS), F32)],
        compiler_params=pltpu.CompilerParams(dimension_semantics=("arbitrary",),
                                             vmem_limit_bytes=VMEM_LIMIT),
        name="da_prep",
    )(proj, proj, proj, cos_all, sin_all)


def _attn_kernel(*refs, has_ctx):
    if has_ctx:
        scal_ref, kmax_ref, q_ref, k_ref, v_ref, kc_ref, vc_ref, nw_ref, o_ref, p_ref = refs
    else:
        scal_ref, kmax_ref, q_ref, k_ref, v_ref, nw_ref, o_ref, p_ref = refs
    lam = scal_ref[0]
    out_scale = scal_ref[1]
    lane = lax.broadcasted_iota(jnp.int32, (1, LANES), 1)
    n_heads = q_ref.shape[0]
    n_keys = k_ref.shape[1]
    chunk = min(ATT_KC, n_keys)
    b = pl.program_id(0)
    if has_ctx:
        kc = kc_ref[...].astype(BF16)
        vc = vc_ref[...].astype(BF16)
        vc_ext = jnp.concatenate([vc, jnp.ones((PAST_LEN, DA_V), BF16)], axis=1)

    def shifted(comp, q_c, h, shift):
        for c in range(n_keys // chunk):
            ks = slice(c * chunk, (c + 1) * chunk)
            p_ref[comp, :, ks] = jnp.exp2(_dot_nt(q_c, k_ref[h, ks, :]) - shift).astype(BF16)
        acc = _dot(p_ref[comp], v_ref[h])
        if has_ctx:
            acc = acc + _dot(jnp.exp2(_dot_nt(q_c, kc) - shift).astype(BF16), vc_ext)
        l = acc[:, DA_V:DA_V + 1]
        return acc[:, :DA_V] / l, l

    def exact(q_c, h):
        s = _dot_nt(q_c, k_ref[h])
        m = jnp.max(s, axis=-1, keepdims=True)
        if has_ctx:
            s_c = _dot_nt(q_c, kc)
            m = jnp.maximum(m, jnp.max(s_c, axis=-1, keepdims=True))
        p = jnp.exp2(s - m)
        l = jnp.sum(p, axis=-1, keepdims=True)
        acc = _dot(p.astype(BF16), v_ref[h, :, 0:DA_V])
        if has_ctx:
            p_c = jnp.exp2(s_c - m)
            l = l + jnp.sum(p_c, axis=-1, keepdims=True)
            acc = acc + _dot(p_c.astype(BF16), vc)
        return acc / l

    def finish(o1, o2):
        return (_rms(o1 - lam * o2, nw_ref[...]) * out_scale).astype(o_ref.dtype)

    for h in range(n_heads):
        head = h if n_heads > 1 else pl.program_id(1)
        cols = slice(None) if n_heads == 1 else slice(h * DA_V, (h + 1) * DA_V)
        qh = q_ref[h]
        zero = jnp.zeros_like(qh)
        qs = (jnp.where(lane < DA_QK, qh, zero), jnp.where(lane >= DA_QK, qh, zero))
        outs, sums = [], []
        for comp, q_c in enumerate(qs):
            qf = q_c.astype(F32)
            qn = jnp.sqrt(jnp.sum(qf * qf, axis=-1, keepdims=True))
            shift = qn * (kmax_ref[(b * DA_HEADS + head) * 2 + comp] * 1.01) + 1e-3
            o_c, l_c = shifted(comp, q_c, h, shift)
            outs.append(o_c)
            sums.append(l_c)
        o_ref[:, cols] = finish(outs[0], outs[1])
        ok = jnp.min(jnp.minimum(sums[0], sums[1])) >= ATT_MIN_SUM

        @pl.when(jnp.logical_not(ok))
        def _():
            o_ref[:, cols] = finish(exact(qs[0], h), exact(qs[1], h))


def _attention(scal, q, k, v, knorm, cache_k, cache_v, da_norm_l, l):
    smem = pl.BlockSpec(memory_space=pltpu.SMEM)
    nw = pl.BlockSpec((1, DA_V), lambda *a: (0, 0))
    kmax_ctx = jnp.sqrt(jnp.max(knorm[:N_CTX_TOK].reshape(BATCH, SEQ, 2 * DA_HEADS), axis=1)).reshape(-1)
    ckn = jnp.sum(jnp.square(cache_k[:, l].reshape(DEC_BATCH, PAST_LEN, 2 * DA_HEADS, DA_QK)), axis=-1)
    kmax_lat = jnp.sqrt(jnp.maximum(jnp.max(knorm[N_CTX_TOK:].reshape(DEC_BATCH, DEC_SEQ, 2 * DA_HEADS), axis=1),
                                    jnp.max(ckn, axis=1))).reshape(-1)
    blk = pl.BlockSpec((DA_HEADS, SEQ, DA_V), lambda b: (0, b, 0))
    o_ctx = pl.pallas_call(
        functools.partial(_attn_kernel, has_ctx=False),
        grid=(BATCH,),
        in_specs=[smem, smem, blk, blk, pl.BlockSpec((DA_HEADS, SEQ, 2 * DA_V), lambda b: (0, b, 0)), nw],
        out_specs=pl.BlockSpec((SEQ, W512), lambda b: (b, 0)),
        out_shape=jax.ShapeDtypeStruct((N_CTX_TOK, W512), BF16),
        scratch_shapes=[pltpu.VMEM((2, SEQ, SEQ), BF16)],
        compiler_params=pltpu.CompilerParams(dimension_semantics=("arbitrary",), vmem_limit_bytes=VMEM_LIMIT),
        name="attn_ctx",
    )(scal, kmax_ctx, q, k, v, da_norm_l)
    nq = DEC_SEQ // ATT_TQ
    off_q = N_CTX_TOK // ATT_TQ
    off_k = N_CTX_TOK // DEC_SEQ
    ck = cache_k.reshape(DEC_BATCH, DEPTH, PAST_LEN, W512)
    cv = cache_v.reshape(DEC_BATCH, DEPTH, PAST_LEN, W512)
    ckv = pl.BlockSpec((None, None, PAST_LEN, DA_V), lambda b, h, i: (b, l, 0, h))
    o_lat = pl.pallas_call(
        functools.partial(_attn_kernel, has_ctx=True),
        grid=(DEC_BATCH, DA_HEADS, nq),
        in_specs=[smem, smem,
                  pl.BlockSpec((1, ATT_TQ, DA_V), lambda b, h, i: (h, off_q + b * nq + i, 0)),
                  pl.BlockSpec((1, DEC_SEQ, DA_V), lambda b, h, i: (h, off_k + b, 0)),
                  pl.BlockSpec((1, DEC_SEQ, 2 * DA_V), lambda b, h, i: (h, off_k + b, 0)),
                  ckv, ckv, nw],
        out_specs=pl.BlockSpec((ATT_TQ, DA_V), lambda b, h, i: (b * nq + i, h)),
        out_shape=jax.ShapeDtypeStruct((DEC_BATCH * DEC_SEQ, W512), BF16),
        scratch_shapes=[pltpu.VMEM((2, ATT_TQ, DEC_SEQ), BF16)],
        compiler_params=pltpu.CompilerParams(dimension_semantics=("arbitrary", "arbitrary", "arbitrary"),
                                             vmem_limit_bytes=VMEM_LIMIT),
        name="attn_lat",
    )(scal, kmax_lat, q, k, v, ck, cv, da_norm_l)
    return o_ctx, o_lat


def _outproj_kernel(hg_ref, dac_ref, dal_ref, x_ref, mod_ref, w_ref, nw_ref, wr_ref, br_ref,
                    xo_ref, h_ref, lg_ref, wbf_ref):
    i = pl.program_id(0)

    @pl.when(i == 0)
    def _():
        wbf_ref[...] = w_ref[...].astype(BF16)

    is_ctx = i < N_CTX_TOK // TM
    da = jnp.where(is_ctx, dac_ref[...], dal_ref[...])
    mix = _dot(hg_ref[...], wbf_ref[0:HG_WIDTH, :]) + _dot(da, wbf_ref[HG_WIDTH:, :])
    g1 = mod_ref[:, 2 * D_MODEL:3 * D_MODEL]
    sh2 = mod_ref[:, 3 * D_MODEL:4 * D_MODEL]
    sc2 = mod_ref[:, 4 * D_MODEL:5 * D_MODEL]
    x = x_ref[...] + g1 * mix
    xo_ref[...] = x
    h = _rms(x, nw_ref[...]) * (1.0 + sc2) + sh2
    h_ref[...] = _pack_rows(h)
    lg_ref[...] = _dot_precise(h, wr_ref[...]) + br_ref[...]


def _outproj(o_hg, o_da_ctx, o_da_lat, x, mods_l, w_out, norm2_l, w_router_l, b_router_l, l):
    n_ctx_t = N_CTX_TOK // TM
    n_lat_t = (N_TOK - N_CTX_TOK) // TM
    tokspec = lambda w: pl.BlockSpec((TM, w), lambda i: (i, 0))
    return pl.pallas_call(
        _outproj_kernel,
        grid=(N_TOK // TM,),
        in_specs=[
            tokspec(W512),
            pl.BlockSpec((TM, W512), lambda i: (jnp.minimum(i, n_ctx_t - 1), 0)),
            pl.BlockSpec((TM, W512), lambda i: (jnp.clip(i - n_ctx_t, 0, n_lat_t - 1), 0)),
            tokspec(D_MODEL),
            pl.BlockSpec((None, 1, 6 * D_MODEL), lambda i: (i * TM // COND_GROUP, 0, 0)),
            pl.BlockSpec((None, D_MODEL, D_MODEL), lambda i: (l, 0, 0)),
            pl.BlockSpec((1, D_MODEL), lambda i: (0, 0)),
            pl.BlockSpec((D_MODEL, N_EXPERTS), lambda i: (0, 0)),
            pl.BlockSpec((1, N_EXPERTS), lambda i: (0, 0)),
        ],
        out_specs=[tokspec(D_MODEL), tokspec(D_MODEL // 2), tokspec(N_EXPERTS)],
        out_shape=[jax.ShapeDtypeStruct((N_TOK, D_MODEL), F32),
                   jax.ShapeDtypeStruct((N_TOK, D_MODEL // 2), jnp.uint32),
                   jax.ShapeDtypeStruct((N_TOK, N_EXPERTS), F32)],
        scratch_shapes=[pltpu.VMEM((D_MODEL, D_MODEL), BF16)],
        compiler_params=pltpu.CompilerParams(dimension_semantics=("arbitrary",),
                                             vmem_limit_bytes=VMEM_LIMIT),
        name="outproj_norm_router",
    )(o_hg, o_da_ctx, o_da_lat, x, mods_l, w_out, norm2_l, w_router_l, b_router_l)


def _moe_kernel(be_ref, nu_ref, x_ref, wgu_ref, bgu_ref, wdn_ref, bdn_ref, o_ref, wgu_bf, wdn_bf):
    i = pl.program_id(0)
    used = i < nu_ref[0]
    prev = be_ref[jnp.maximum(i - 1, 0)]
    fresh = jnp.logical_or(i == 0, be_ref[i] != prev)

    @pl.when(jnp.logical_and(used, fresh))
    def _():
        wgu_bf[...] = wgu_ref[...].astype(BF16)
        wdn_bf[...] = wdn_ref[...].astype(BF16)

    @pl.when(used)
    def _():
        x_lo, x_hi = _unpack_rows(x_ref[...])
        x = jnp.concatenate([x_lo.astype(BF16), x_hi.astype(BF16)], axis=1)
        gu = _dot(x, wgu_bf[...]) + bgu_ref[...]
        gate = jnp.minimum(gu[:, :D_FF], SWIGLU_LIMIT)
        lin = jnp.clip(gu[:, D_FF:], -SWIGLU_LIMIT, SWIGLU_LIMIT)
        act = (lin + 1.0) * gate * _sigmoid(SWIGLU_ALPHA * gate)
        y = _dot(act.astype(BF16), wdn_bf[...]) + bdn_ref[...]
        o_ref[...] = _pack_rows(y)


def _moe_experts(blk_exp, n_used, xs, w_gu, b_gu, w_dn, b_dn, l):
    def rowblk(i, be, nu):
        return (jnp.minimum(i, nu[0] - 1), 0)

    grid_spec = pltpu.PrefetchScalarGridSpec(
        num_scalar_prefetch=2,
        grid=(MOE_NBLK,),
        in_specs=[
            pl.BlockSpec((MOE_BLK, D_MODEL // 2), rowblk),
            pl.BlockSpec((None, None, D_MODEL, 2 * D_FF), lambda i, be, nu: (l, be[i], 0, 0)),
            pl.BlockSpec((None, None, 1, 2 * D_FF), lambda i, be, nu: (l, be[i], 0, 0)),
            pl.BlockSpec((None, None, D_FF, D_MODEL), lambda i, be, nu: (l, be[i], 0, 0)),
            pl.BlockSpec((None, None, 1, D_MODEL), lambda i, be, nu: (l, be[i], 0, 0)),
        ],
        out_specs=pl.BlockSpec((MOE_BLK, D_MODEL // 2), rowblk),
        scratch_shapes=[pltpu.VMEM((D_MODEL, 2 * D_FF), BF16), pltpu.VMEM((D_FF, D_MODEL), BF16)],
    )
    return pl.pallas_call(
        _moe_kernel,
        grid_spec=grid_spec,
        out_shape=jax.ShapeDtypeStruct((MOE_NBLK * MOE_BLK, D_MODEL // 2), jnp.uint32),
        compiler_params=pltpu.CompilerParams(dimension_semantics=("arbitrary",),
                                             vmem_limit_bytes=VMEM_LIMIT),
        name="moe_experts",
    )(blk_exp, n_used, xs, w_gu, b_gu.reshape(DEPTH, N_EXPERTS, 1, 2 * D_FF),
      w_dn, b_dn.reshape(DEPTH, N_EXPERTS, 1, D_MODEL))


def _route_kernel(lg_ref, exp_ref, gate_ref, rank_ref, cnt_ref, run_ref):
    i = pl.program_id(0)

    @pl.when(i == 0)
    def _():
        run_ref[...] = jnp.zeros_like(run_ref)

    lg = lg_ref[...]
    lane = lax.broadcasted_iota(jnp.int32, (1, N_EXPERTS), 1)
    lane_k = lax.broadcasted_iota(jnp.int32, (1, TOP_K), 1)
    vals, idxs, hots = [], [], []
    for _ in range(TOP_K):
        m = jnp.max(lg, axis=-1, keepdims=True)
        idx = jnp.min(jnp.where(lg == m, lane, N_EXPERTS), axis=-1, keepdims=True)
        hot = lane == idx
        vals.append(m)
        idxs.append(idx)
        hots.append(hot)
        lg = jnp.where(hot, -jnp.inf, lg)
    exps = [jnp.exp(v - vals[0]) for v in vals]
    denom = exps[0] + exps[1] + exps[2] + exps[3]
    tokhot = (hots[0] | hots[1] | hots[2] | hots[3]).astype(BF16)
    r = lax.broadcasted_iota(jnp.int32, (TM, TM), 0)
    cidx = lax.broadcasted_iota(jnp.int32, (TM, TM), 1)
    before = _dot((cidx < r).astype(BF16), tokhot) + run_ref[...]
    e_out = jnp.zeros((TM, TOP_K), jnp.int32)
    g_out = jnp.zeros((TM, TOP_K), F32)
    r_out = jnp.zeros((TM, TOP_K), jnp.int32)
    for k in range(TOP_K):
        rk = jnp.sum(jnp.where(hots[k], before, 0.0), axis=-1, keepdims=True).astype(jnp.int32)
        e_out = jnp.where(lane_k == k, idxs[k], e_out)
        g_out = jnp.where(lane_k == k, exps[k] / denom, g_out)
        r_out = jnp.where(lane_k == k, rk, r_out)
    exp_ref[...] = e_out
    gate_ref[...] = g_out
    rank_ref[...] = r_out
    run_ref[...] += jnp.sum(tokhot.astype(F32), axis=0, keepdims=True)
    cnt_ref[...] = run_ref[...].astype(jnp.int32)


def _route(logits):
    tk = pl.BlockSpec((TM, TOP_K), lambda i: (i, 0))
    top_exp, gates, rank, counts = pl.pallas_call(
        _route_kernel,
        grid=(N_TOK // TM,),
        in_specs=[pl.BlockSpec((TM, N_EXPERTS), lambda i: (i, 0))],
        out_specs=[tk, tk, tk, pl.BlockSpec((1, N_EXPERTS), lambda i: (0, 0))],
        out_shape=[jax.ShapeDtypeStruct((N_TOK, TOP_K), jnp.int32),
                   jax.ShapeDtypeStruct((N_TOK, TOP_K), F32),
                   jax.ShapeDtypeStruct((N_TOK, TOP_K), jnp.int32),
                   jax.ShapeDtypeStruct((1, N_EXPERTS), jnp.int32)],
        scratch_shapes=[pltpu.VMEM((1, N_EXPERTS), F32)],
        compiler_params=pltpu.CompilerParams(dimension_semantics=("arbitrary",)),
        name="route",
    )(logits)
    counts = counts[0]
    padded = (counts + MOE_BLK - 1) // MOE_BLK * MOE_BLK
    pad_end = jnp.cumsum(padded)
    pad_start = pad_end - padded
    start = jnp.cumsum(counts) - counts
    eq = top_exp[:, :, None] == jnp.arange(N_EXPERTS, dtype=jnp.int32)[None, None, :]
    dest = (jnp.sum(jnp.where(eq, pad_start[None, None, :], 0), axis=-1) + rank).T.reshape(-1)
    blk_exp = jnp.minimum(
        jnp.sum((pad_end[None, :] <= (jnp.arange(MOE_NBLK, dtype=jnp.int32) * MOE_BLK)[:, None]).astype(jnp.int32),
                axis=-1), N_EXPERTS - 1).astype(jnp.int32)
    n_used = (pad_end[-1:] // MOE_BLK).astype(jnp.int32)
    tok_ids = jnp.arange(N_ASG, dtype=jnp.int32) // TOP_K
    _, tok_sorted = lax.sort((top_exp.reshape(-1), tok_ids), num_keys=1, is_stable=True)
    slot = jnp.arange(MOE_NBLK * MOE_BLK, dtype=jnp.int32)
    e_slot = jnp.repeat(blk_exp, MOE_BLK)
    src = jnp.clip(start[e_slot] + slot - pad_start[e_slot], 0, N_ASG - 1)
    slot_tok = tok_sorted[src]
    return slot_tok, gates, dest, blk_exp, n_used


def _combine_kernel(y0_ref, y1_ref, y2_ref, y3_ref, gate_ref, x_ref, mod_ref, o_ref):
    g = gate_ref[...]
    half = D_MODEL // 2
    acc_lo = jnp.zeros((TM, half), F32)
    acc_hi = jnp.zeros((TM, half), F32)
    for k, y_ref in enumerate((y0_ref, y1_ref, y2_ref, y3_ref)):
        lo, hi = _unpack_rows(y_ref[...])
        acc_lo = acc_lo + g[:, k:k + 1] * lo
        acc_hi = acc_hi + g[:, k:k + 1] * hi
    g2 = mod_ref[:, 5 * D_MODEL:]
    o_ref[:, :half] = x_ref[:, :half] + g2[:, :half] * acc_lo
    o_ref[:, half:] = x_ref[:, half:] + g2[:, half:] * acc_hi


def _combine(ys_tok, gates, x_mid, mods_l):
    n_i = N_TOK // TM

    def ysblk(k):
        return pl.BlockSpec((TM, D_MODEL // 2), lambda i: (k * n_i + i, 0))

    return pl.pallas_call(
        _combine_kernel,
        grid=(n_i,),
        in_specs=[ysblk(0), ysblk(1), ysblk(2), ysblk(3),
                  pl.BlockSpec((TM, TOP_K), lambda i: (i, 0)),
                  pl.BlockSpec((TM, D_MODEL), lambda i: (i, 0)),
                  pl.BlockSpec((None, 1, 6 * D_MODEL), lambda i: (i * TM // COND_GROUP, 0, 0))],
        out_specs=pl.BlockSpec((TM, D_MODEL), lambda i: (i, 0)),
        out_shape=jax.ShapeDtypeStruct((N_TOK, D_MODEL), F32),
        compiler_params=pltpu.CompilerParams(dimension_semantics=("arbitrary",), vmem_limit_bytes=VMEM_LIMIT),
        name="moe_combine",
    )(ys_tok, ys_tok, ys_tok, ys_tok, gates, x_mid, mods_l)


def _final_norm_kernel(x_ref, w_ref, o_ref):
    o_ref[...] = _rms(x_ref[...], w_ref[...])


def _final_norm(x, w):
    return pl.pallas_call(
        _final_norm_kernel,
        grid=(N_TOK // TM,),
        in_specs=[pl.BlockSpec((TM, D_MODEL), lambda i: (i, 0)), pl.BlockSpec((1, D_MODEL), lambda i: (0, 0))],
        out_specs=pl.BlockSpec((TM, D_MODEL), lambda i: (i, 0)),
        out_shape=jax.ShapeDtypeStruct((N_TOK, D_MODEL), F32),
        compiler_params=pltpu.CompilerParams(dimension_semantics=("arbitrary",)),
        name="final_norm",
    )(x, w)


def kernel(x_prompt, x_sample, c, cache_k, cache_v, state_hgrn, c_ctx, norm1_w, norm2_w, w_ada, b_ada, w_in, hg_lb, hg_norm_w, lam_q1, lam_k1, lam_q2, lam_k2, da_norm_w, w_out, w_router, b_router, w_gu, b_gu, w_dn, b_dn, norm_f_w):
    lb_all = jnp.cumsum(jax.nn.softmax(hg_lb.astype(F32), axis=0), axis=0)
    lb_all = lb_all - lb_all[0:1]
    lam_init = [0.8 - 0.6 * math.exp(-0.3 * l) for l in range(DEPTH)]
    lam = (jnp.exp(jnp.sum(lam_q1 * lam_k1, axis=-1)) - jnp.exp(jnp.sum(lam_q2 * lam_k2, axis=-1))
           + jnp.asarray(lam_init, F32))
    cos_all, sin_all = _rope_tables()

    conds = jnp.concatenate([c_ctx[None, :], c, jnp.zeros((COND_ROWS - 1 - DEC_BATCH, D_MODEL), F32)], axis=0)
    mods = _ada_mod(conds, w_ada, b_ada)
    mods = mods[:, :N_COND].reshape(DEPTH, N_COND, 1, 6 * D_MODEL)

    x = jnp.concatenate([x_prompt.reshape(N_CTX_TOK, D_MODEL), x_sample.reshape(-1, D_MODEL)], axis=0)
    ks_l, vs_l, ss_l = [], [], []
    for l in range(DEPTH):
        proj = _inproj(x, norm1_w[l][None, :], mods[l], w_in, l)
        ks_l.append(proj[:N_CTX_TOK, COL_K_DA * W512:(COL_K_DA + 1) * W512].reshape(BATCH, SEQ, DA_HEADS, 2 * DA_QK))
        vs_l.append(proj[:N_CTX_TOK, COL_V_DA * W512:(COL_V_DA + 1) * W512].reshape(BATCH, SEQ, DA_HEADS, DA_V))
        o_fw, s_fw = _hgrn_dir(proj, None, lb_all[l], state_hgrn, None, l, rev=False)
        o_hg, s_bw = _hgrn_dir(proj, o_fw, lb_all[l], state_hgrn, hg_norm_w[l][None, :], l, rev=True)
        ss_l.append(jnp.stack([s_fw, s_bw], axis=1))
        q_r, k_r, v_b, knorm = _da_prep(proj, cos_all, sin_all)
        scal = jnp.stack([lam[l], jnp.asarray(1.0 - lam_init[l], F32)])
        o_da_ctx, o_da_lat = _attention(scal, q_r, k_r, v_b, knorm, cache_k, cache_v, da_norm_w[l][None, :], l)
        x_mid, h2, logits = _outproj(o_hg, o_da_ctx, o_da_lat, x, mods[l], w_out, norm2_w[l][None, :],
                                     w_router[l], b_router[l][None, :], l)
        slot_tok, gates, dest, blk_exp, n_used = _route(logits)
        xs = h2.at[slot_tok].get(mode="promise_in_bounds")
        ys = _moe_experts(blk_exp, n_used, xs, w_gu, b_gu, w_dn, b_dn, l)
        ys_tok = ys.at[dest].get(mode="promise_in_bounds")
        x = _combine(ys_tok, gates, x_mid, mods[l])
    y = _final_norm(x, norm_f_w[None, :])
    y_prompt = y[:N_CTX_TOK].reshape(BATCH, SEQ, D_MODEL)
    y_sample = y[N_CTX_TOK:].reshape(DEC_BATCH, DEC_SEQ, D_MODEL)
    return (y_prompt, y_sample, jnp.stack(ks_l, axis=1), jnp.stack(vs_l, axis=1), jnp.stack(ss_l, axis=1))
```

```python
import functools
import math

import jax
import jax.numpy as jnp
from jax import lax
from jax.experimental import pallas as pl
from jax.experimental.pallas import tpu as pltpu

F32 = jnp.float32
BF16 = jnp.bfloat16

D_MODEL = 1024
BATCH = 16
SEQ = 256
DEPTH = 4
DEC_BATCH = 2
DEC_SEQ = 4096
PAST_LEN = 256
GRID_W = 64
HG_HEADS = 4
HG_DK = 128
HG_DV = 128
HG_WIDTH = HG_HEADS * HG_DK
DA_HEADS = 4
DA_QK = 64
DA_V = 2 * DA_QK
DA_WIDTH = DA_HEADS * DA_V
ROPE_BASE = 10000.0
IN_DIM = 5 * HG_WIDTH + 3 * DA_WIDTH
N_EXPERTS = 32
TOP_K = 4
D_FF = D_MODEL
SWIGLU_ALPHA = 1.702
SWIGLU_LIMIT = 7.0
EPS = 1e-6

N_CTX_TOK = BATCH * SEQ
N_TOK = N_CTX_TOK + DEC_BATCH * DEC_SEQ
COND_GROUP = DEC_SEQ
assert N_CTX_TOK == COND_GROUP
N_COND = N_TOK // COND_GROUP
COND_ROWS = 8

COL_Q_HG, COL_F_FW, COL_F_BW, COL_I_HG, COL_G_HG, COL_Q_DA, COL_K_DA, COL_V_DA = range(8)
W512 = 512
LANES = 128

HG_CHUNK = 64
HG_HALF = HG_CHUNK // 2
HG_ROWS = 256
HG_EXP_CLAMP = 80.0

TM = 512
TN_IN = 2048
ATT_TQ = 256
Q_SCALE = math.log2(math.e) / math.sqrt(DA_QK)
ATT_KC = 512
ATT_MIN_SUM = 2.0 ** -40
MOE_BLK = 512
N_ASG = N_TOK * TOP_K
MOE_NBLK = N_ASG // MOE_BLK + N_EXPERTS
VMEM_LIMIT = 56 * 1024 * 1024


def _dot(a, b):
    return jnp.dot(a, b, preferred_element_type=F32)


def _dot_nt(a, b):
    return lax.dot_general(a, b, (((1,), (1,)), ((), ())), preferred_element_type=F32)


def _dot_tn(a, b):
    return lax.dot_general(a, b, (((0,), (0,)), ((), ())), preferred_element_type=F32)


def _split2(x):
    hi = x.astype(BF16)
    lo = (x - hi.astype(F32)).astype(BF16)
    return hi, lo


def _split3(x):
    hi = x.astype(BF16)
    r = x - hi.astype(F32)
    mid = r.astype(BF16)
    lo = (r - mid.astype(F32)).astype(BF16)
    return hi, mid, lo


def _dot_precise(a, b):
    a_hi, a_lo = _split2(a)
    b_hi, b_lo = _split2(b)
    return _dot(a_hi, b_hi) + (_dot(a_hi, b_lo) + _dot(a_lo, b_hi))


def _pack_rows(x):
    n = x.shape[1] // 2
    lo = pltpu.bitcast(x[:, :n].astype(BF16).astype(F32), jnp.uint32)
    hi = pltpu.bitcast(x[:, n:].astype(BF16).astype(F32), jnp.uint32)
    return (lo >> 16) | hi


def _unpack_rows(u):
    lo = pltpu.bitcast(u << 16, F32)
    hi = pltpu.bitcast(u & jnp.uint32(0xFFFF0000), F32)
    return lo, hi


def _rms(x, w):
    ms = jnp.mean(x * x, axis=-1, keepdims=True)
    return x * lax.rsqrt(ms + EPS) * w


def _sigmoid(x):
    return 1.0 / (1.0 + jnp.exp(-x))


def _ada_kernel(cond_ref, w_ref, b_ref, o_ref):
    cnd = cond_ref[...]
    s = cnd * _sigmoid(cnd)
    o_ref[...] = _dot_precise(s, w_ref[...]) + b_ref[...]


def _ada_mod(conds, w_ada, b_ada):
    nj = 6 * D_MODEL // D_MODEL
    return pl.pallas_call(
        _ada_kernel,
        grid=(DEPTH, nj),
        in_specs=[
            pl.BlockSpec((COND_ROWS, D_MODEL), lambda l, j: (0, 0)),
            pl.BlockSpec((None, D_MODEL, D_MODEL), lambda l, j: (l, 0, j)),
            pl.BlockSpec((None, 1, D_MODEL), lambda l, j: (l, 0, j)),
        ],
        out_specs=pl.BlockSpec((None, COND_ROWS, D_MODEL), lambda l, j: (l, 0, j)),
        out_shape=jax.ShapeDtypeStruct((DEPTH, COND_ROWS, 6 * D_MODEL), F32),
        compiler_params=pltpu.CompilerParams(dimension_semantics=("arbitrary", "arbitrary"),
                                             vmem_limit_bytes=VMEM_LIMIT),
        name="ada_mod",
    )(conds, w_ada, b_ada.reshape(DEPTH, 1, 6 * D_MODEL))


def _inproj_kernel(x_ref, nw_ref, mod_ref, w_ref, o_ref, wbf_ref):
    @pl.when(pl.program_id(1) == 0)
    def _():
        wbf_ref[...] = w_ref[...].astype(BF16)

    x = x_ref[...]
    sh = mod_ref[:, 0:D_MODEL]
    sc = mod_ref[:, D_MODEL:2 * D_MODEL]
    h = _rms(x, nw_ref[...]) * (1.0 + sc) + sh
    o_ref[...] = _dot(h.astype(BF16), wbf_ref[...])


def _inproj(x, norm_w_l, mods_l, w_in, l):
    n_i = N_TOK // TM
    return pl.pallas_call(
        _inproj_kernel,
        grid=(IN_DIM // TN_IN, n_i),
        in_specs=[
            pl.BlockSpec((TM, D_MODEL), lambda n, i: (i, 0)),
            pl.BlockSpec((1, D_MODEL), lambda n, i: (0, 0)),
            pl.BlockSpec((None, 1, 6 * D_MODEL), lambda n, i: (i * TM // COND_GROUP, 0, 0)),
            pl.BlockSpec((None, D_MODEL, TN_IN), lambda n, i: (l, 0, n)),
        ],
        out_specs=pl.BlockSpec((TM, TN_IN), lambda n, i: (i, n)),
        out_shape=jax.ShapeDtypeStruct((N_TOK, IN_DIM), F32),
        scratch_shapes=[pltpu.VMEM((D_MODEL, TN_IN), BF16)],
        compiler_params=pltpu.CompilerParams(dimension_semantics=("arbitrary", "arbitrary"),
                                             vmem_limit_bytes=VMEM_LIMIT),
        name="norm_inproj",
    )(x, norm_w_l, mods_l, w_in)


N_HG_BLK = N_TOK // HG_ROWS
N_HG_CTX_BLK = N_CTX_TOK // HG_ROWS
HG_BPS_CTX = SEQ // HG_ROWS
HG_BPS_LAT = DEC_SEQ // HG_ROWS


def _hg_block(step, rev):
    return (N_HG_BLK - 1 - step) if rev else step


def _hg_lat_seq(p):
    return jnp.clip((p - N_HG_CTX_BLK) // HG_BPS_LAT, 0, DEC_BATCH - 1)


def _hg_ctx_seq(p):
    return jnp.minimum(p // HG_BPS_CTX, BATCH - 1)


def _hgrn_kernel(*refs, rev):
    if rev:
        (q_ref, f_ref, v_ref, g_ref, ofw_ref, lb_ref, s0_ref, nw_ref, o_ref, sfin_ref, st_ref) = refs
    else:
        (q_ref, f_ref, v_ref, lb_ref, s0_ref, o_ref, sfin_ref, st_ref) = refs
    p = _hg_block(pl.program_id(0), rev)
    is_ctx = p < N_HG_CTX_BLK
    pos = jnp.where(is_ctx, p % HG_BPS_CTX, (p - N_HG_CTX_BLK) % HG_BPS_LAT)
    bps = jnp.where(is_ctx, HG_BPS_CTX, HG_BPS_LAT)
    first_pos = (bps - 1) if rev else 0
    last_pos = 0 if rev else (bps - 1)

    @pl.when(jnp.logical_and(pos == first_pos, is_ctx))
    def _():
        st_ref[...] = jnp.zeros_like(st_ref)

    @pl.when(jnp.logical_and(pos == first_pos, jnp.logical_not(is_ctx)))
    def _():
        for h in range(HG_HEADS):
            st_ref[h] = s0_ref[h].T

    lb = lb_ref[...]
    oml = 1.0 - lb
    log_oml = jnp.log1p(-lb)
    row = lax.broadcasted_iota(jnp.int32, (HG_CHUNK, HG_CHUNK), 0)
    col = lax.broadcasted_iota(jnp.int32, (HG_CHUNK, HG_CHUNK), 1)
    tri = ((col >= row) if rev else (col <= row)).astype(BF16)
    same_half = (row >= HG_HALF) == (col >= HG_HALF)
    if rev:
        m_diag = jnp.logical_and(same_half, col >= row)
        m_off = jnp.logical_and(row < HG_HALF, col >= HG_HALF)
    else:
        m_diag = jnp.logical_and(same_half, col <= row)
        m_off = jnp.logical_and(row >= HG_HALF, col < HG_HALF)
    row1 = lax.broadcasted_iota(jnp.int32, (HG_CHUNK, 1), 0)
    n_chunks = HG_ROWS // HG_CHUNK

    def chunk(ci):
        c = (n_chunks - 1 - ci) if rev else ci
        rows = slice(c * HG_CHUNK, (c + 1) * HG_CHUNK)
        fx = f_ref[rows, :]
        q = q_ref[rows, :]
        v = v_ref[rows, :]
        e = jnp.exp(-jnp.abs(fx))
        r = 1.0 / (1.0 + e)
        er = e * r
        nonneg = fx >= 0.0
        sig = jnp.where(nonneg, r, er)
        nsig = jnp.where(nonneg, er, r)
        f = lb + oml * sig
        logf = jnp.maximum(jnp.log(f), log_oml + (jnp.minimum(fx, 0.0) - jnp.log1p(e)))
        k = oml * nsig
        hi, mid, lo = _split3(logf)
        a = _dot(tri, hi) + (_dot(tri, mid) + _dot(tri, lo))
        a_mid = jnp.where(row1 < HG_HALF, a[HG_HALF // 2:HG_HALF // 2 + 1, :],
                          a[HG_HALF + HG_HALF // 2:HG_HALF + HG_HALF // 2 + 1, :])
        if rev:
            a_b = a[HG_HALF:HG_HALF + 1, :]
            a_last = a[0:1, :]
        else:
            a_b = a[HG_HALF - 1:HG_HALF, :]
            a_last = a[HG_CHUNK - 1:HG_CHUNK, :]
        qd = (q * jnp.exp(jnp.minimum(a - a_mid, HG_EXP_CLAMP))).astype(BF16)
        kd = (k * jnp.exp(jnp.minimum(a_mid - a, HG_EXP_CLAMP))).astype(BF16)
        qo = (q * jnp.exp(jnp.minimum(a - a_b, 0.0))).astype(BF16)
        ko = (k * jnp.exp(jnp.minimum(a_b - a, 0.0))).astype(BF16)
        qa = (q * jnp.exp(a)).astype(BF16)
        kdec = (k * jnp.exp(a_last - a)).astype(BF16)
        dec = jnp.exp(a_last)
        vb = v.astype(BF16)
        for h in range(HG_HEADS):
            hs = slice(h * HG_DK, (h + 1) * HG_DK)
            p1 = _dot_nt(qd[:, hs], kd[:, hs])
            p2 = _dot_nt(qo[:, hs], ko[:, hs])
            attn = jnp.where(m_diag, p1, jnp.where(m_off, p2, 0.0)).astype(BF16)
            st = st_ref[h]
            o_h = _dot_nt(qa[:, hs], st.astype(BF16)) + _dot(attn, vb[:, hs])
            st_ref[h] = dec[:, hs] * st + _dot_tn(vb[:, hs], kdec[:, hs])
            if rev:
                tot = ofw_ref[rows, hs] + o_h
                g = g_ref[rows, hs]
                y = _rms(tot, nw_ref[...]) * (g * _sigmoid(g))
                o_ref[rows, hs] = y.astype(o_ref.dtype)
            else:
                o_ref[rows, hs] = o_h

    for ci in range(n_chunks):
        chunk(ci)

    @pl.when(jnp.logical_and(pos == last_pos, is_ctx))
    def _():
        for h in range(HG_HEADS):
            sfin_ref[h] = st_ref[h].T


def _hgrn_dir(proj, o_fw, lb_l, state_hgrn, hg_norm_l, l, rev):
    d = 1 if rev else 0

    def tok(colblk):
        return pl.BlockSpec((HG_ROWS, W512), lambda s: (_hg_block(s, rev), colblk))

    in_specs = [tok(COL_Q_HG), tok(COL_F_BW if rev else COL_F_FW), tok(COL_I_HG)]
    args = [proj, proj, proj]
    if rev:
        in_specs += [tok(COL_G_HG), pl.BlockSpec((HG_ROWS, W512), lambda s: (_hg_block(s, rev), 0))]
        args += [proj, o_fw]
    in_specs += [
        pl.BlockSpec((None, 1, W512), lambda s: (d, 0, 0)),
        pl.BlockSpec((None, None, None, HG_HEADS, HG_DK, HG_DV),
                     lambda s: (_hg_lat_seq(_hg_block(s, rev)), l, d, 0, 0, 0)),
    ]
    args += [lb_l.reshape(2, 1, W512), state_hgrn]
    if rev:
        in_specs.append(pl.BlockSpec((1, HG_DV), lambda s: (0, 0)))
        args.append(hg_norm_l)
    return pl.pallas_call(
        functools.partial(_hgrn_kernel, rev=rev),
        grid=(N_HG_BLK,),
        in_specs=in_specs,
        out_specs=[
            pl.BlockSpec((HG_ROWS, W512), lambda s: (_hg_block(s, rev), 0)),
            pl.BlockSpec((None, HG_HEADS, HG_DK, HG_DV), lambda s: (_hg_ctx_seq(_hg_block(s, rev)), 0, 0, 0)),
        ],
        out_shape=[
            jax.ShapeDtypeStruct((N_TOK, W512), BF16 if rev else F32),
            jax.ShapeDtypeStruct((BATCH, HG_HEADS, HG_DK, HG_DV), F32),
        ],
        scratch_shapes=[pltpu.VMEM((HG_HEADS, HG_DV, HG_DK), F32)],
        compiler_params=pltpu.CompilerParams(dimension_semantics=("arbitrary",),
                                             vmem_limit_bytes=VMEM_LIMIT),
        name="hgrn_bwd" if rev else "hgrn_fwd",
    )(*args)


def _rope_tables():
    n_pairs = DA_QK // 4
    inv_freq = ROPE_BASE ** (-jnp.arange(n_pairs, dtype=F32) / n_pairs)
    t = jnp.arange(DEC_SEQ)
    row_idx = (t // GRID_W).astype(F32)
    col_idx = (t % GRID_W).astype(F32)
    lane = jnp.arange(LANES)
    jj = lane % DA_QK
    use_col = (jj // (DA_QK // 2)) == 1
    second = (jj % (DA_QK // 2)) >= n_pairs
    fr = inv_freq[jj % n_pairs]
    ang_r = row_idx[:, None] * inv_freq[None, :]
    ang_c = col_idx[:, None] * inv_freq[None, :]
    cos_r, sin_r, cos_c, sin_c = jnp.cos(ang_r), jnp.sin(ang_r), jnp.cos(ang_c), jnp.sin(ang_c)
    fi = jj % n_pairs
    cos_t = jnp.where(use_col[None, :], cos_c[:, fi], cos_r[:, fi])
    sin_t = jnp.where(use_col[None, :], sin_c[:, fi], sin_r[:, fi])
    sin_t = jnp.where(second[None, :], sin_t, -sin_t)
    del fr
    cos_all = jnp.concatenate([jnp.ones((N_CTX_TOK, LANES), F32)] + [cos_t] * DEC_BATCH, axis=0)
    sin_all = jnp.concatenate([jnp.zeros((N_CTX_TOK, LANES), F32)] + [sin_t] * DEC_BATCH, axis=0)
    return cos_all, sin_all


def _da_prep_kernel(q_ref, k_ref, v_ref, cos_ref, sin_ref, qo_ref, ko_ref, vo_ref, kn_ref):
    cos = cos_ref[...]
    sin = sin_ref[...]
    lane = lax.broadcasted_iota(jnp.int32, (1, LANES), 1)
    first = (lane % (DA_QK // 2)) < (DA_QK // 4)
    lane_n = lax.broadcasted_iota(jnp.int32, (1, 2 * DA_HEADS), 1)

    def rope(x):
        partner = jnp.where(first, pltpu.roll(x, LANES - DA_QK // 4, 1), pltpu.roll(x, DA_QK // 4, 1))
        return x * cos + partner * sin

    kn = jnp.zeros((TM, 2 * DA_HEADS), F32)
    ones = jnp.ones((TM, DA_V), BF16)
    for h in range(DA_HEADS):
        hs = slice(h * DA_V, (h + 1) * DA_V)
        qo_ref[h] = (rope(q_ref[:, hs]) * Q_SCALE).astype(BF16)
        kr = rope(k_ref[:, hs])
        ko_ref[h] = kr.astype(BF16)
        ksq = kr * kr
        n1 = jnp.sum(jnp.where(lane < DA_QK, ksq, 0.0), axis=-1, keepdims=True)
        n2 = jnp.sum(jnp.where(lane >= DA_QK, ksq, 0.0), axis=-1, keepdims=True)
        kn = jnp.where(lane_n == 2 * h, n1, jnp.where(lane_n == 2 * h + 1, n2, kn))
        vo_ref[h, :, 0:DA_V] = v_ref[:, hs].astype(BF16)
        vo_ref[h, :, DA_V:] = ones
    kn_ref[...] = kn


def _da_prep(proj, cos_all, sin_all):
    def tok(colblk):
        return pl.BlockSpec((TM, W512), lambda i: (i, colblk))

    tab = pl.BlockSpec((TM, LANES), lambda i: (i, 0))
    out = pl.BlockSpec((DA_HEADS, TM, DA_V), lambda i: (0, i, 0))
    shp = jax.ShapeDtypeStruct((DA_HEADS, N_TOK, DA_V), BF16)
    return pl.pallas_call(
        _da_prep_kernel,
        grid=(N_TOK // TM,),
        in_specs=[tok(COL_Q_DA), tok(COL_K_DA), tok(COL_V_DA), tab, tab],
        out_specs=[out, out, pl.BlockSpec((DA_HEADS, TM, 2 * DA_V), lambda i: (0, i, 0)),
                   pl.BlockSpec((TM, 2 * DA_HEADS), lambda i: (i, 0))],
        out_shape=[shp, shp, jax.ShapeDtypeStruct((DA_HEADS, N_TOK, 2 * DA_V), BF16),
                   jax.ShapeDtypeStruct((N_TOK, 2 * DA_HEADS), F32)],
        compiler_params=pltpu.CompilerParams(dimension_semantics=("arbitrary",),
                                             vmem_limit_bytes=VMEM_LIMIT),
        name="da_prep",
    )(proj, proj, proj, cos_all, sin_all)


def _attn_kernel(*refs, has_ctx):
    if has_ctx:
        scal_ref, kmax_ref, q_ref, k_ref, v_ref, kc_ref, vc_ref, nw_ref, o_ref = refs
    else:
        scal_ref, kmax_ref, q_ref, k_ref, v_ref, nw_ref, o_ref = refs
    lam = scal_ref[0]
    out_scale = scal_ref[1]
    lane = lax.broadcasted_iota(jnp.int32, (1, LANES), 1)
    n_heads = q_ref.shape[0]
    n_keys = k_ref.shape[1]
    chunk = min(ATT_KC, n_keys)
    b = pl.program_id(0)
    if has_ctx:
        kc = kc_ref[...].astype(BF16)
        vc = vc_ref[...].astype(BF16)
        vc_ext = jnp.concatenate([vc, jnp.ones((PAST_LEN, DA_V), BF16)], axis=1)

    def shifted(q_c, h, shift):
        acc = None
        for c in range(n_keys // chunk):
            ks = slice(c * chunk, (c + 1) * chunk)
            p = jnp.exp2(_dot_nt(q_c, k_ref[h, ks, :]) - shift).astype(BF16)
            part = _dot(p, v_ref[h, ks, :])
            acc = part if acc is None else acc + part
        if has_ctx:
            acc = acc + _dot(jnp.exp2(_dot_nt(q_c, kc) - shift).astype(BF16), vc_ext)
        l = acc[:, DA_V:DA_V + 1]
        return acc[:, :DA_V] / l, l

    def exact(q_c, h):
        s = _dot_nt(q_c, k_ref[h])
        m = jnp.max(s, axis=-1, keepdims=True)
        if has_ctx:
            s_c = _dot_nt(q_c, kc)
            m = jnp.maximum(m, jnp.max(s_c, axis=-1, keepdims=True))
        p = jnp.exp2(s - m)
        l = jnp.sum(p, axis=-1, keepdims=True)
        acc = _dot(p.astype(BF16), v_ref[h, :, 0:DA_V])
        if has_ctx:
            p_c = jnp.exp2(s_c - m)
            l = l + jnp.sum(p_c, axis=-1, keepdims=True)
            acc = acc + _dot(p_c.astype(BF16), vc)
        return acc / l

    def finish(o1, o2):
        return (_rms(o1 - lam * o2, nw_ref[...]) * out_scale).astype(o_ref.dtype)

    for h in range(n_heads):
        head = h if n_heads > 1 else pl.program_id(1)
        cols = slice(None) if n_heads == 1 else slice(h * DA_V, (h + 1) * DA_V)
        qh = q_ref[h]
        zero = jnp.zeros_like(qh)
        qs = (jnp.where(lane < DA_QK, qh, zero), jnp.where(lane >= DA_QK, qh, zero))
        outs, sums = [], []
        for comp, q_c in enumerate(qs):
            qf = q_c.astype(F32)
            qn = jnp.sqrt(jnp.sum(qf * qf, axis=-1, keepdims=True))
            shift = qn * (kmax_ref[(b * DA_HEADS + head) * 2 + comp] * 1.01) + 1e-3
            o_c, l_c = shifted(q_c, h, shift)
            outs.append(o_c)
            sums.append(l_c)
        o_ref[:, cols] = finish(outs[0], outs[1])
        ok = jnp.min(jnp.minimum(sums[0], sums[1])) >= ATT_MIN_SUM

        @pl.when(jnp.logical_not(ok))
        def _():
            o_ref[:, cols] = finish(exact(qs[0], h), exact(qs[1], h))


def _attention(scal, q, k, v, knorm, cache_k, cache_v, da_norm_l, l):
    smem = pl.BlockSpec(memory_space=pltpu.SMEM)
    nw = pl.BlockSpec((1, DA_V), lambda *a: (0, 0))
    kmax_ctx = jnp.sqrt(jnp.max(knorm[:N_CTX_TOK].reshape(BATCH, SEQ, 2 * DA_HEADS), axis=1)).reshape(-1)
    ckn = jnp.sum(jnp.square(cache_k[:, l].reshape(DEC_BATCH, PAST_LEN, 2 * DA_HEADS, DA_QK)), axis=-1)
    kmax_lat = jnp.sqrt(jnp.maximum(jnp.max(knorm[N_CTX_TOK:].reshape(DEC_BATCH, DEC_SEQ, 2 * DA_HEADS), axis=1),
                                    jnp.max(ckn, axis=1))).reshape(-1)
    blk = pl.BlockSpec((DA_HEADS, SEQ, DA_V), lambda b: (0, b, 0))
    o_ctx = pl.pallas_call(
        functools.partial(_attn_kernel, has_ctx=False),
        grid=(BATCH,),
        in_specs=[smem, smem, blk, blk, pl.BlockSpec((DA_HEADS, SEQ, 2 * DA_V), lambda b: (0, b, 0)), nw],
        out_specs=pl.BlockSpec((SEQ, W512), lambda b: (b, 0)),
        out_shape=jax.ShapeDtypeStruct((N_CTX_TOK, W512), BF16),
        compiler_params=pltpu.CompilerParams(dimension_semantics=("arbitrary",), vmem_limit_bytes=VMEM_LIMIT),
        name="attn_ctx",
    )(scal, kmax_ctx, q, k, v, da_norm_l)
    nq = DEC_SEQ // ATT_TQ
    off_q = N_CTX_TOK // ATT_TQ
    off_k = N_CTX_TOK // DEC_SEQ
    ck = cache_k.reshape(DEC_BATCH, DEPTH, PAST_LEN, W512)
    cv = cache_v.reshape(DEC_BATCH, DEPTH, PAST_LEN, W512)
    ckv = pl.BlockSpec((None, None, PAST_LEN, DA_V), lambda b, h, i: (b, l, 0, h))
    o_lat = pl.pallas_call(
        functools.partial(_attn_kernel, has_ctx=True),
        grid=(DEC_BATCH, DA_HEADS, nq),
        in_specs=[smem, smem,
                  pl.BlockSpec((1, ATT_TQ, DA_V), lambda b, h, i: (h, off_q + b * nq + i, 0)),
                  pl.BlockSpec((1, DEC_SEQ, DA_V), lambda b, h, i: (h, off_k + b, 0)),
                  pl.BlockSpec((1, DEC_SEQ, 2 * DA_V), lambda b, h, i: (h, off_k + b, 0)),
                  ckv, ckv, nw],
        out_specs=pl.BlockSpec((ATT_TQ, DA_V), lambda b, h, i: (b * nq + i, h)),
        out_shape=jax.ShapeDtypeStruct((DEC_BATCH * DEC_SEQ, W512), BF16),
        compiler_params=pltpu.CompilerParams(dimension_semantics=("arbitrary", "arbitrary", "arbitrary"),
                                             vmem_limit_bytes=VMEM_LIMIT),
        name="attn_lat",
    )(scal, kmax_lat, q, k, v, ck, cv, da_norm_l)
    return o_ctx, o_lat


def _outproj_kernel(hg_ref, dac_ref, dal_ref, x_ref, mod_ref, w_ref, nw_ref, wr_ref, br_ref,
                    xo_ref, h_ref, lg_ref, wbf_ref):
    i = pl.program_id(0)

    @pl.when(i == 0)
    def _():
        wbf_ref[...] = w_ref[...].astype(BF16)

    is_ctx = i < N_CTX_TOK // TM
    da = jnp.where(is_ctx, dac_ref[...], dal_ref[...])
    mix = _dot(hg_ref[...], wbf_ref[0:HG_WIDTH, :]) + _dot(da, wbf_ref[HG_WIDTH:, :])
    g1 = mod_ref[:, 2 * D_MODEL:3 * D_MODEL]
    sh2 = mod_ref[:, 3 * D_MODEL:4 * D_MODEL]
    sc2 = mod_ref[:, 4 * D_MODEL:5 * D_MODEL]
    x = x_ref[...] + g1 * mix
    xo_ref[...] = x
    h = _rms(x, nw_ref[...]) * (1.0 + sc2) + sh2
    h_ref[...] = _pack_rows(h)
    lg_ref[...] = _dot_precise(h, wr_ref[...]) + br_ref[...]


def _outproj(o_hg, o_da_ctx, o_da_lat, x, mods_l, w_out, norm2_l, w_router_l, b_router_l, l):
    n_ctx_t = N_CTX_TOK // TM
    n_lat_t = (N_TOK - N_CTX_TOK) // TM
    tokspec = lambda w: pl.BlockSpec((TM, w), lambda i: (i, 0))
    return pl.pallas_call(
        _outproj_kernel,
        grid=(N_TOK // TM,),
        in_specs=[
            tokspec(W512),
            pl.BlockSpec((TM, W512), lambda i: (jnp.minimum(i, n_ctx_t - 1), 0)),
            pl.BlockSpec((TM, W512), lambda i: (jnp.clip(i - n_ctx_t, 0, n_lat_t - 1), 0)),
            tokspec(D_MODEL),
            pl.BlockSpec((None, 1, 6 * D_MODEL), lambda i: (i * TM // COND_GROUP, 0, 0)),
            pl.BlockSpec((None, D_MODEL, D_MODEL), lambda i: (l, 0, 0)),
            pl.BlockSpec((1, D_MODEL), lambda i: (0, 0)),
            pl.BlockSpec((D_MODEL, N_EXPERTS), lambda i: (0, 0)),
            pl.BlockSpec((1, N_EXPERTS), lambda i: (0, 0)),
        ],
        out_specs=[tokspec(D_MODEL), tokspec(D_MODEL // 2), tokspec(N_EXPERTS)],
        out_shape=[jax.ShapeDtypeStruct((N_TOK, D_MODEL), F32),
                   jax.ShapeDtypeStruct((N_TOK, D_MODEL // 2), jnp.uint32),
                   jax.ShapeDtypeStruct((N_TOK, N_EXPERTS), F32)],
        scratch_shapes=[pltpu.VMEM((D_MODEL, D_MODEL), BF16)],
        compiler_params=pltpu.CompilerParams(dimension_semantics=("arbitrary",),
                                             vmem_limit_bytes=VMEM_LIMIT),
        name="outproj_norm_router",
    )(o_hg, o_da_ctx, o_da_lat, x, mods_l, w_out, norm2_l, w_router_l, b_router_l)


def _moe_kernel(be_ref, nu_ref, x_ref, wgu_ref, bgu_ref, wdn_ref, bdn_ref, o_ref, wgu_bf, wdn_bf):
    i = pl.program_id(0)
    used = i < nu_ref[0]
    prev = be_ref[jnp.maximum(i - 1, 0)]
    fresh = jnp.logical_or(i == 0, be_ref[i] != prev)

    @pl.when(jnp.logical_and(used, fresh))
    def _():
        wgu_bf[...] = wgu_ref[...].astype(BF16)
        wdn_bf[...] = wdn_ref[...].astype(BF16)

    @pl.when(used)
    def _():
        x_lo, x_hi = _unpack_rows(x_ref[...])
        x = jnp.concatenate([x_lo.astype(BF16), x_hi.astype(BF16)], axis=1)
        gu = _dot(x, wgu_bf[...]) + bgu_ref[...]
        gate = jnp.minimum(gu[:, :D_FF], SWIGLU_LIMIT)
        lin = jnp.clip(gu[:, D_FF:], -SWIGLU_LIMIT, SWIGLU_LIMIT)
        act = (lin + 1.0) * gate * _sigmoid(SWIGLU_ALPHA * gate)
        y = _dot(act.astype(BF16), wdn_bf[...]) + bdn_ref[...]
        o_ref[...] = _pack_rows(y)


def _moe_experts(blk_exp, n_used, xs, w_gu, b_gu, w_dn, b_dn, l):
    def rowblk(i, be, nu):
        return (jnp.minimum(i, nu[0] - 1), 0)

    grid_spec = pltpu.PrefetchScalarGridSpec(
        num_scalar_prefetch=2,
        grid=(MOE_NBLK,),
        in_specs=[
            pl.BlockSpec((MOE_BLK, D_MODEL // 2), rowblk),
            pl.BlockSpec((None, None, D_MODEL, 2 * D_FF), lambda i, be, nu: (l, be[i], 0, 0)),
            pl.BlockSpec((None, None, 1, 2 * D_FF), lambda i, be, nu: (l, be[i], 0, 0)),
            pl.BlockSpec((None, None, D_FF, D_MODEL), lambda i, be, nu: (l, be[i], 0, 0)),
            pl.BlockSpec((None, None, 1, D_MODEL), lambda i, be, nu: (l, be[i], 0, 0)),
        ],
        out_specs=pl.BlockSpec((MOE_BLK, D_MODEL // 2), rowblk),
        scratch_shapes=[pltpu.VMEM((D_MODEL, 2 * D_FF), BF16), pltpu.VMEM((D_FF, D_MODEL), BF16)],
    )
    return pl.pallas_call(
        _moe_kernel,
        grid_spec=grid_spec,
        out_shape=jax.ShapeDtypeStruct((MOE_NBLK * MOE_BLK, D_MODEL // 2), jnp.uint32),
        compiler_params=pltpu.CompilerParams(dimension_semantics=("arbitrary",),
                                             vmem_limit_bytes=VMEM_LIMIT),
        name="moe_experts",
    )(blk_exp, n_used, xs, w_gu, b_gu.reshape(DEPTH, N_EXPERTS, 1, 2 * D_FF),
      w_dn, b_dn.reshape(DEPTH, N_EXPERTS, 1, D_MODEL))


def _route_kernel(lg_ref, exp_ref, gate_ref, rank_ref, cnt_ref, run_ref):
    i = pl.program_id(0)

    @pl.when(i == 0)
    def _():
        run_ref[...] = jnp.zeros_like(run_ref)

    lg = lg_ref[...]
    lane = lax.broadcasted_iota(jnp.int32, (1, N_EXPERTS), 1)
    lane_k = lax.broadcasted_iota(jnp.int32, (1, TOP_K), 1)
    vals, idxs, hots = [], [], []
    for _ in range(TOP_K):
        m = jnp.max(lg, axis=-1, keepdims=True)
        idx = jnp.min(jnp.where(lg == m, lane, N_EXPERTS), axis=-1, keepdims=True)
        hot = lane == idx
        vals.append(m)
        idxs.append(idx)
        hots.append(hot)
        lg = jnp.where(hot, -jnp.inf, lg)
    exps = [jnp.exp(v - vals[0]) for v in vals]
    denom = exps[0] + exps[1] + exps[2] + exps[3]
    tokhot = (hots[0] | hots[1] | hots[2] | hots[3]).astype(BF16)
    r = lax.broadcasted_iota(jnp.int32, (TM, TM), 0)
    cidx = lax.broadcasted_iota(jnp.int32, (TM, TM), 1)
    before = _dot((cidx < r).astype(BF16), tokhot) + run_ref[...]
    e_out = jnp.zeros((TM, TOP_K), jnp.int32)
    g_out = jnp.zeros((TM, TOP_K), F32)
    r_out = jnp.zeros((TM, TOP_K), jnp.int32)
    for k in range(TOP_K):
        rk = jnp.sum(jnp.where(hots[k], before, 0.0), axis=-1, keepdims=True).astype(jnp.int32)
        e_out = jnp.where(lane_k == k, idxs[k], e_out)
        g_out = jnp.where(lane_k == k, exps[k] / denom, g_out)
        r_out = jnp.where(lane_k == k, rk, r_out)
    exp_ref[...] = e_out
    gate_ref[...] = g_out
    rank_ref[...] = r_out
    run_ref[...] += jnp.sum(tokhot.astype(F32), axis=0, keepdims=True)
    cnt_ref[...] = run_ref[...].astype(jnp.int32)


def _route(logits):
    tk = pl.BlockSpec((TM, TOP_K), lambda i: (i, 0))
    top_exp, gates, rank, counts = pl.pallas_call(
        _route_kernel,
        grid=(N_TOK // TM,),
        in_specs=[pl.BlockSpec((TM, N_EXPERTS), lambda i: (i, 0))],
        out_specs=[tk, tk, tk, pl.BlockSpec((1, N_EXPERTS), lambda i: (0, 0))],
        out_shape=[jax.ShapeDtypeStruct((N_TOK, TOP_K), jnp.int32),
                   jax.ShapeDtypeStruct((N_TOK, TOP_K), F32),
                   jax.ShapeDtypeStruct((N_TOK, TOP_K), jnp.int32),
                   jax.ShapeDtypeStruct((1, N_EXPERTS), jnp.int32)],
        scratch_shapes=[pltpu.VMEM((1, N_EXPERTS), F32)],
        compiler_params=pltpu.CompilerParams(dimension_semantics=("arbitrary",)),
        name="route",
    )(logits)
    counts = counts[0]
    padded = (counts + MOE_BLK - 1) // MOE_BLK * MOE_BLK
    pad_end = jnp.cumsum(padded)
    pad_start = pad_end - padded
    start = jnp.cumsum(counts) - counts
    eq = top_exp[:, :, None] == jnp.arange(N_EXPERTS, dtype=jnp.int32)[None, None, :]
    dest = (jnp.sum(jnp.where(eq, pad_start[None, None, :], 0), axis=-1) + rank).T.reshape(-1)
    blk_exp = jnp.minimum(
        jnp.sum((pad_end[None, :] <= (jnp.arange(MOE_NBLK, dtype=jnp.int32) * MOE_BLK)[:, None]).astype(jnp.int32),
                axis=-1), N_EXPERTS - 1).astype(jnp.int32)
    n_used = (pad_end[-1:] // MOE_BLK).astype(jnp.int32)
    tok_ids = jnp.arange(N_ASG, dtype=jnp.int32) // TOP_K
    _, tok_sorted = lax.sort((top_exp.reshape(-1), tok_ids), num_keys=1, is_stable=True)
    slot = jnp.arange(MOE_NBLK * MOE_BLK, dtype=jnp.int32)
    e_slot = jnp.repeat(blk_exp, MOE_BLK)
    src = jnp.clip(start[e_slot] + slot - pad_start[e_slot], 0, N_ASG - 1)
    slot_tok = tok_sorted[src]
    return slot_tok, gates, dest, blk_exp, n_used


def _combine_kernel(y0_ref, y1_ref, y2_ref, y3_ref, gate_ref, x_ref, mod_ref, o_ref):
    g = gate_ref[...]
    half = D_MODEL // 2
    acc_lo = jnp.zeros((TM, half), F32)
    acc_hi = jnp.zeros((TM, half), F32)
    for k, y_ref in enumerate((y0_ref, y1_ref, y2_ref, y3_ref)):
        lo, hi = _unpack_rows(y_ref[...])
        acc_lo = acc_lo + g[:, k:k + 1] * lo
        acc_hi = acc_hi + g[:, k:k + 1] * hi
    g2 = mod_ref[:, 5 * D_MODEL:]
    o_ref[:, :half] = x_ref[:, :half] + g2[:, :half] * acc_lo
    o_ref[:, half:] = x_ref[:, half:] + g2[:, half:] * acc_hi


def _combine(ys_tok, gates, x_mid, mods_l):
    n_i = N_TOK // TM

    def ysblk(k):
        return pl.BlockSpec((TM, D_MODEL // 2), lambda i: (k * n_i + i, 0))

    return pl.pallas_call(
        _combine_kernel,
        grid=(n_i,),
        in_specs=[ysblk(0), ysblk(1), ysblk(2), ysblk(3),
                  pl.BlockSpec((TM, TOP_K), lambda i: (i, 0)),
                  pl.BlockSpec((TM, D_MODEL), lambda i: (i, 0)),
                  pl.BlockSpec((None, 1, 6 * D_MODEL), lambda i: (i * TM // COND_GROUP, 0, 0))],
        out_specs=pl.BlockSpec((TM, D_MODEL), lambda i: (i, 0)),
        out_shape=jax.ShapeDtypeStruct((N_TOK, D_MODEL), F32),
        compiler_params=pltpu.CompilerParams(dimension_semantics=("arbitrary",), vmem_limit_bytes=VMEM_LIMIT),
        name="moe_combine",
    )(ys_tok, ys_tok, ys_tok, ys_tok, gates, x_mid, mods_l)


def _final_norm_kernel(x_ref, w_ref, o_ref):
    o_ref[...] = _rms(x_ref[...], w_ref[...])


def _final_norm(x, w):
    return pl.pallas_call(
        _final_norm_kernel,
        grid=(N_TOK // TM,),
        in_specs=[pl.BlockSpec((TM, D_MODEL), lambda i: (i, 0)), pl.BlockSpec((1, D_MODEL), lambda i: (0, 0))],
        out_specs=pl.BlockSpec((TM, D_MODEL), lambda i: (i, 0)),
        out_shape=jax.ShapeDtypeStruct((N_TOK, D_MODEL), F32),
        compiler_params=pltpu.CompilerParams(dimension_semantics=("arbitrary",)),
        name="final_norm",
    )(x, w)


def kernel(x_prompt, x_sample, c, cache_k, cache_v, state_hgrn, c_ctx, norm1_w, norm2_w, w_ada, b_ada, w_in, hg_lb, hg_norm_w, lam_q1, lam_k1, lam_q2, lam_k2, da_norm_w, w_out, w_router, b_router, w_gu, b_gu, w_dn, b_dn, norm_f_w):
    lb_all = jnp.cumsum(jax.nn.softmax(hg_lb.astype(F32), axis=0), axis=0)
    lb_all = lb_all - lb_all[0:1]
    lam_init = [0.8 - 0.6 * math.exp(-0.3 * l) for l in range(DEPTH)]
    lam = (jnp.exp(jnp.sum(lam_q1 * lam_k1, axis=-1)) - jnp.exp(jnp.sum(lam_q2 * lam_k2, axis=-1))
           + jnp.asarray(lam_init, F32))
    cos_all, sin_all = _rope_tables()

    conds = jnp.concatenate([c_ctx[None, :], c, jnp.zeros((COND_ROWS - 1 - DEC_BATCH, D_MODEL), F32)], axis=0)
    mods = _ada_mod(conds, w_ada, b_ada)
    mods = mods[:, :N_COND].reshape(DEPTH, N_COND, 1, 6 * D_MODEL)

    x = jnp.concatenate([x_prompt.reshape(N_CTX_TOK, D_MODEL), x_sample.reshape(-1, D_MODEL)], axis=0)
    ks_l, vs_l, ss_l = [], [], []
    for l in range(DEPTH):
        proj = _inproj(x, norm1_w[l][None, :], mods[l], w_in, l)
        ks_l.append(proj[:N_CTX_TOK, COL_K_DA * W512:(COL_K_DA + 1) * W512].reshape(BATCH, SEQ, DA_HEADS, 2 * DA_QK))
        vs_l.append(proj[:N_CTX_TOK, COL_V_DA * W512:(COL_V_DA + 1) * W512].reshape(BATCH, SEQ, DA_HEADS, DA_V))
        o_fw, s_fw = _hgrn_dir(proj, None, lb_all[l], state_hgrn, None, l, rev=False)
        o_hg, s_bw = _hgrn_dir(proj, o_fw, lb_all[l], state_hgrn, hg_norm_w[l][None, :], l, rev=True)
        ss_l.append(jnp.stack([s_fw, s_bw], axis=1))
        q_r, k_r, v_b, knorm = _da_prep(proj, cos_all, sin_all)
        scal = jnp.stack([lam[l], jnp.asarray(1.0 - lam_init[l], F32)])
        o_da_ctx, o_da_lat = _attention(scal, q_r, k_r, v_b, knorm, cache_k, cache_v, da_norm_w[l][None, :], l)
        x_mid, h2, logits = _outproj(o_hg, o_da_ctx, o_da_lat, x, mods[l], w_out, norm2_w[l][None, :],
                                     w_router[l], b_router[l][None, :], l)
        slot_tok, gates, dest, blk_exp, n_used = _route(logits)
        xs = h2.at[slot_tok].get(mode="promise_in_bounds")
        ys = _moe_experts(blk_exp, n_used, xs, w_gu, b_gu, w_dn, b_dn, l)
        ys_tok = ys.at[dest].get(mode="promise_in_bounds")
        x = _combine(ys_tok, gates, x_mid, mods[l])
    y = _final_norm(x, norm_f_w[None, :])
    y_prompt = y[:N_CTX_TOK].reshape(BATCH, SEQ, D_MODEL)
    y_sample = y[N_CTX_TOK:].reshape(DEC_BATCH, DEC_SEQ, D_MODEL)
    return (y_prompt, y_sample, jnp.stack(ks_l, axis=1), jnp.stack(vs_l, axis=1), jnp.stack(ss_l, axis=1))
```

```python
import functools
import math

import jax
import jax.numpy as jnp
from jax import lax
from jax.experimental import pallas as pl
from jax.experimental.pallas import tpu as pltpu

F32 = jnp.float32
BF16 = jnp.bfloat16

D_MODEL = 1024
BATCH = 16
SEQ = 256
DEPTH = 4
DEC_BATCH = 2
DEC_SEQ = 4096
PAST_LEN = 256
GRID_W = 64
HG_HEADS = 4
HG_DK = 128
HG_DV = 128
HG_WIDTH = HG_HEADS * HG_DK
DA_HEADS = 4
DA_QK = 64
DA_V = 2 * DA_QK
DA_WIDTH = DA_HEADS * DA_V
ROPE_BASE = 10000.0
IN_DIM = 5 * HG_WIDTH + 3 * DA_WIDTH
N_EXPERTS = 32
TOP_K = 4
D_FF = D_MODEL
SWIGLU_ALPHA = 1.702
SWIGLU_LIMIT = 7.0
EPS = 1e-6

N_CTX_TOK = BATCH * SEQ
N_TOK = N_CTX_TOK + DEC_BATCH * DEC_SEQ
COND_GROUP = DEC_SEQ
assert N_CTX_TOK == COND_GROUP
N_COND = N_TOK // COND_GROUP
COND_ROWS = 8

COL_Q_HG, COL_F_FW, COL_F_BW, COL_I_HG, COL_G_HG, COL_Q_DA, COL_K_DA, COL_V_DA = range(8)
W512 = 512
LANES = 128

HG_CHUNK = 64
HG_HALF = HG_CHUNK // 2
HG_ROWS = 256
HG_EXP_CLAMP = 80.0

TM = 512
TN_IN = 2048
ATT_TQ = 256
Q_SCALE = math.log2(math.e) / math.sqrt(DA_QK)
ATT_KC = 512
ATT_MIN_SUM = 2.0 ** -40
MOE_BLK = 512
N_ASG = N_TOK * TOP_K
MOE_NBLK = N_ASG // MOE_BLK + N_EXPERTS
VMEM_LIMIT = 56 * 1024 * 1024


def _dot(a, b):
    return jnp.dot(a, b, preferred_element_type=F32)


def _dot_nt(a, b):
    return lax.dot_general(a, b, (((1,), (1,)), ((), ())), preferred_element_type=F32)


def _dot_tn(a, b):
    return lax.dot_general(a, b, (((0,), (0,)), ((), ())), preferred_element_type=F32)


def _split2(x):
    hi = x.astype(BF16)
    lo = (x - hi.astype(F32)).astype(BF16)
    return hi, lo


def _split3(x):
    hi = x.astype(BF16)
    r = x - hi.astype(F32)
    mid = r.astype(BF16)
    lo = (r - mid.astype(F32)).astype(BF16)
    return hi, mid, lo


def _dot_precise(a, b):
    a_hi, a_lo = _split2(a)
    b_hi, b_lo = _split2(b)
    return _dot(a_hi, b_hi) + (_dot(a_hi, b_lo) + _dot(a_lo, b_hi))


def _pack_rows(x):
    n = x.shape[1] // 2
    lo = pltpu.bitcast(x[:, :n].astype(BF16).astype(F32), jnp.uint32)
    hi = pltpu.bitcast(x[:, n:].astype(BF16).astype(F32), jnp.uint32)
    return (lo >> 16) | hi


def _unpack_rows(u):
    lo = pltpu.bitcast(u << 16, F32)
    hi = pltpu.bitcast(u & jnp.uint32(0xFFFF0000), F32)
    return lo, hi


def _rms(x, w):
    ms = jnp.mean(x * x, axis=-1, keepdims=True)
    return x * lax.rsqrt(ms + EPS) * w


def _sigmoid(x):
    return 1.0 / (1.0 + jnp.exp(-x))


def _ada_kernel(cond_ref, w_ref, b_ref, o_ref):
    cnd = cond_ref[...]
    s = cnd * _sigmoid(cnd)
    o_ref[...] = _dot_precise(s, w_ref[...]) + b_ref[...]


def _ada_mod(conds, w_ada, b_ada):
    nj = 6 * D_MODEL // D_MODEL
    return pl.pallas_call(
        _ada_kernel,
        grid=(DEPTH, nj),
        in_specs=[
            pl.BlockSpec((COND_ROWS, D_MODEL), lambda l, j: (0, 0)),
            pl.BlockSpec((None, D_MODEL, D_MODEL), lambda l, j: (l, 0, j)),
            pl.BlockSpec((None, 1, D_MODEL), lambda l, j: (l, 0, j)),
        ],
        out_specs=pl.BlockSpec((None, COND_ROWS, D_MODEL), lambda l, j: (l, 0, j)),
        out_shape=jax.ShapeDtypeStruct((DEPTH, COND_ROWS, 6 * D_MODEL), F32),
        compiler_params=pltpu.CompilerParams(dimension_semantics=("arbitrary", "arbitrary"),
                                             vmem_limit_bytes=VMEM_LIMIT),
        name="ada_mod",
    )(conds, w_ada, b_ada.reshape(DEPTH, 1, 6 * D_MODEL))


SEQ_PER_TILE = TM // SEQ
N_CTX_TILES = N_CTX_TOK // TM
KV_COL_BLK = (COL_K_DA * W512) // TN_IN
assert COL_K_DA * W512 == KV_COL_BLK * TN_IN + TN_IN - 2 * W512


def _inproj_kernel(x_ref, nw_ref, mod_ref, w_ref, kin_ref, vin_ref, o_ref, kc_ref, vc_ref, wbf_ref):
    del kin_ref, vin_ref
    n = pl.program_id(0)
    i = pl.program_id(1)

    @pl.when(i == 0)
    def _():
        wbf_ref[...] = w_ref[...].astype(BF16)

    x = x_ref[...]
    sh = mod_ref[:, 0:D_MODEL]
    sc = mod_ref[:, D_MODEL:2 * D_MODEL]
    h = _rms(x, nw_ref[...]) * (1.0 + sc) + sh
    o = _dot(h.astype(BF16), wbf_ref[...])
    o_ref[...] = o

    @pl.when(jnp.logical_and(n == KV_COL_BLK, i < N_CTX_TILES))
    def _():
        for s in range(SEQ_PER_TILE):
            kc_ref[s] = o[s * SEQ:(s + 1) * SEQ, TN_IN - 2 * W512:TN_IN - W512]
            vc_ref[s] = o[s * SEQ:(s + 1) * SEQ, TN_IN - W512:]


def _inproj(x, norm_w_l, mods_l, w_in, new_k, new_v, l):
    n_i = N_TOK // TM

    def cache_idx(n, i):
        return (jnp.where(n == KV_COL_BLK, jnp.minimum(i, N_CTX_TILES - 1), 0), l, 0, 0)

    cache_spec = pl.BlockSpec((SEQ_PER_TILE, None, SEQ, W512), cache_idx)
    cache_shape = jax.ShapeDtypeStruct((BATCH, DEPTH, SEQ, W512), F32)
    return pl.pallas_call(
        _inproj_kernel,
        grid=(IN_DIM // TN_IN, n_i),
        in_specs=[
            pl.BlockSpec((TM, D_MODEL), lambda n, i: (i, 0)),
            pl.BlockSpec((1, D_MODEL), lambda n, i: (0, 0)),
            pl.BlockSpec((None, 1, 6 * D_MODEL), lambda n, i: (i * TM // COND_GROUP, 0, 0)),
            pl.BlockSpec((None, D_MODEL, TN_IN), lambda n, i: (l, 0, n)),
            pl.BlockSpec(memory_space=pl.ANY),
            pl.BlockSpec(memory_space=pl.ANY),
        ],
        out_specs=[pl.BlockSpec((TM, TN_IN), lambda n, i: (i, n)), cache_spec, cache_spec],
        out_shape=[jax.ShapeDtypeStruct((N_TOK, IN_DIM), F32), cache_shape, cache_shape],
        input_output_aliases={4: 1, 5: 2},
        scratch_shapes=[pltpu.VMEM((D_MODEL, TN_IN), BF16)],
        compiler_params=pltpu.CompilerParams(dimension_semantics=("arbitrary", "arbitrary"),
                                             vmem_limit_bytes=VMEM_LIMIT),
        name="norm_inproj",
    )(x, norm_w_l, mods_l, w_in, new_k, new_v)


N_HG_BLK = N_TOK // HG_ROWS
N_HG_CTX_BLK = N_CTX_TOK // HG_ROWS
HG_BPS_CTX = SEQ // HG_ROWS
HG_BPS_LAT = DEC_SEQ // HG_ROWS


def _hg_block(step, rev):
    return (N_HG_BLK - 1 - step) if rev else step


def _hg_lat_seq(p):
    return jnp.clip((p - N_HG_CTX_BLK) // HG_BPS_LAT, 0, DEC_BATCH - 1)


def _hg_ctx_seq(p):
    return jnp.minimum(p // HG_BPS_CTX, BATCH - 1)


def _hgrn_kernel(*refs, rev):
    if rev:
        (q_ref, f_ref, v_ref, g_ref, ofw_ref, lb_ref, s0_ref, nw_ref, _, o_ref, sfin_ref, st_ref) = refs
    else:
        (q_ref, f_ref, v_ref, lb_ref, s0_ref, _, o_ref, sfin_ref, st_ref) = refs
    p = _hg_block(pl.program_id(0), rev)
    is_ctx = p < N_HG_CTX_BLK
    pos = jnp.where(is_ctx, p % HG_BPS_CTX, (p - N_HG_CTX_BLK) % HG_BPS_LAT)
    bps = jnp.where(is_ctx, HG_BPS_CTX, HG_BPS_LAT)
    first_pos = (bps - 1) if rev else 0
    last_pos = 0 if rev else (bps - 1)

    @pl.when(jnp.logical_and(pos == first_pos, is_ctx))
    def _():
        st_ref[...] = jnp.zeros_like(st_ref)

    @pl.when(jnp.logical_and(pos == first_pos, jnp.logical_not(is_ctx)))
    def _():
        for h in range(HG_HEADS):
            st_ref[h] = s0_ref[h].T

    lb = lb_ref[...]
    oml = 1.0 - lb
    log_oml = jnp.log1p(-lb)
    row = lax.broadcasted_iota(jnp.int32, (HG_CHUNK, HG_CHUNK), 0)
    col = lax.broadcasted_iota(jnp.int32, (HG_CHUNK, HG_CHUNK), 1)
    tri = ((col >= row) if rev else (col <= row)).astype(BF16)
    same_half = (row >= HG_HALF) == (col >= HG_HALF)
    if rev:
        m_diag = jnp.logical_and(same_half, col >= row)
        m_off = jnp.logical_and(row < HG_HALF, col >= HG_HALF)
    else:
        m_diag = jnp.logical_and(same_half, col <= row)
        m_off = jnp.logical_and(row >= HG_HALF, col < HG_HALF)
    row1 = lax.broadcasted_iota(jnp.int32, (HG_CHUNK, 1), 0)
    n_chunks = HG_ROWS // HG_CHUNK

    def chunk(ci):
        c = (n_chunks - 1 - ci) if rev else ci
        rows = slice(c * HG_CHUNK, (c + 1) * HG_CHUNK)
        fx = f_ref[rows, :]
        q = q_ref[rows, :]
        v = v_ref[rows, :]
        e = jnp.exp(-jnp.abs(fx))
        r = 1.0 / (1.0 + e)
        er = e * r
        nonneg = fx >= 0.0
        sig = jnp.where(nonneg, r, er)
        nsig = jnp.where(nonneg, er, r)
        f = lb + oml * sig
        logf = jnp.maximum(jnp.log(f), log_oml + (jnp.minimum(fx, 0.0) - jnp.log1p(e)))
        k = oml * nsig
        hi, mid, lo = _split3(logf)
        a = _dot(tri, hi) + (_dot(tri, mid) + _dot(tri, lo))
        a_mid = jnp.where(row1 < HG_HALF, a[HG_HALF // 2:HG_HALF // 2 + 1, :],
                          a[HG_HALF + HG_HALF // 2:HG_HALF + HG_HALF // 2 + 1, :])
        if rev:
            a_b = a[HG_HALF:HG_HALF + 1, :]
            a_last = a[0:1, :]
        else:
            a_b = a[HG_HALF - 1:HG_HALF, :]
            a_last = a[HG_CHUNK - 1:HG_CHUNK, :]
        qd = (q * jnp.exp(jnp.minimum(a - a_mid, HG_EXP_CLAMP))).astype(BF16)
        kd = (k * jnp.exp(jnp.minimum(a_mid - a, HG_EXP_CLAMP))).astype(BF16)
        qo = (q * jnp.exp(jnp.minimum(a - a_b, 0.0))).astype(BF16)
        ko = (k * jnp.exp(jnp.minimum(a_b - a, 0.0))).astype(BF16)
        qa = (q * jnp.exp(a)).astype(BF16)
        kdec = (k * jnp.exp(a_last - a)).astype(BF16)
        dec = jnp.exp(a_last)
        vb = v.astype(BF16)
        for h in range(HG_HEADS):
            hs = slice(h * HG_DK, (h + 1) * HG_DK)
            p1 = _dot_nt(qd[:, hs], kd[:, hs])
            p2 = _dot_nt(qo[:, hs], ko[:, hs])
            attn = jnp.where(m_diag, p1, jnp.where(m_off, p2, 0.0)).astype(BF16)
            st = st_ref[h]
            o_h = _dot_nt(qa[:, hs], st.astype(BF16)) + _dot(attn, vb[:, hs])
            st_ref[h] = dec[:, hs] * st + _dot_tn(vb[:, hs], kdec[:, hs])
            if rev:
                tot = ofw_ref[rows, hs] + o_h
                g = g_ref[rows, hs]
                y = _rms(tot, nw_ref[...]) * (g * _sigmoid(g))
                o_ref[rows, hs] = y.astype(o_ref.dtype)
            else:
                o_ref[rows, hs] = o_h

    for ci in range(n_chunks):
        chunk(ci)

    @pl.when(jnp.logical_and(pos == last_pos, is_ctx))
    def _():
        for h in range(HG_HEADS):
            sfin_ref[h] = st_ref[h].T


def _hgrn_dir(proj, o_fw, lb_l, state_hgrn, hg_norm_l, new_state, l, rev):
    d = 1 if rev else 0

    def tok(colblk):
        return pl.BlockSpec((HG_ROWS, W512), lambda s: (_hg_block(s, rev), colblk))

    in_specs = [tok(COL_Q_HG), tok(COL_F_BW if rev else COL_F_FW), tok(COL_I_HG)]
    args = [proj, proj, proj]
    if rev:
        in_specs += [tok(COL_G_HG), pl.BlockSpec((HG_ROWS, W512), lambda s: (_hg_block(s, rev), 0))]
        args += [proj, o_fw]
    in_specs += [
        pl.BlockSpec((None, 1, W512), lambda s: (d, 0, 0)),
        pl.BlockSpec((None, None, None, HG_HEADS, HG_DK, HG_DV),
                     lambda s: (_hg_lat_seq(_hg_block(s, rev)), l, d, 0, 0, 0)),
    ]
    args += [lb_l.reshape(2, 1, W512), state_hgrn]
    if rev:
        in_specs.append(pl.BlockSpec((1, HG_DV), lambda s: (0, 0)))
        args.append(hg_norm_l)
    in_specs.append(pl.BlockSpec(memory_space=pl.ANY))
    args.append(new_state)
    return pl.pallas_call(
        functools.partial(_hgrn_kernel, rev=rev),
        grid=(N_HG_BLK,),
        in_specs=in_specs,
        out_specs=[
            pl.BlockSpec((HG_ROWS, W512), lambda s: (_hg_block(s, rev), 0)),
            pl.BlockSpec((None, None, None, HG_HEADS, HG_DK, HG_DV),
                         lambda s: (_hg_ctx_seq(_hg_block(s, rev)), l, d, 0, 0, 0)),
        ],
        out_shape=[
            jax.ShapeDtypeStruct((N_TOK, W512), BF16 if rev else F32),
            jax.ShapeDtypeStruct((BATCH, DEPTH, 2, HG_HEADS, HG_DK, HG_DV), F32),
        ],
        input_output_aliases={len(args) - 1: 1},
        scratch_shapes=[pltpu.VMEM((HG_HEADS, HG_DV, HG_DK), F32)],
        compiler_params=pltpu.CompilerParams(dimension_semantics=("arbitrary",),
                                             vmem_limit_bytes=VMEM_LIMIT),
        name="hgrn_bwd" if rev else "hgrn_fwd",
    )(*args)


def _rope_tables():
    n_pairs = DA_QK // 4
    inv_freq = ROPE_BASE ** (-jnp.arange(n_pairs, dtype=F32) / n_pairs)
    t = jnp.arange(DEC_SEQ)
    row_idx = (t // GRID_W).astype(F32)
    col_idx = (t % GRID_W).astype(F32)
    lane = jnp.arange(LANES)
    jj = lane % DA_QK
    use_col = (jj // (DA_QK // 2)) == 1
    second = (jj % (DA_QK // 2)) >= n_pairs
    fr = inv_freq[jj % n_pairs]
    ang_r = row_idx[:, None] * inv_freq[None, :]
    ang_c = col_idx[:, None] * inv_freq[None, :]
    cos_r, sin_r, cos_c, sin_c = jnp.cos(ang_r), jnp.sin(ang_r), jnp.cos(ang_c), jnp.sin(ang_c)
    fi = jj % n_pairs
    cos_t = jnp.where(use_col[None, :], cos_c[:, fi], cos_r[:, fi])
    sin_t = jnp.where(use_col[None, :], sin_c[:, fi], sin_r[:, fi])
    sin_t = jnp.where(second[None, :], sin_t, -sin_t)
    del fr
    cos_all = jnp.concatenate([jnp.ones((N_CTX_TOK, LANES), F32)] + [cos_t] * DEC_BATCH, axis=0)
    sin_all = jnp.concatenate([jnp.zeros((N_CTX_TOK, LANES), F32)] + [sin_t] * DEC_BATCH, axis=0)
    return cos_all, sin_all


def _da_prep_kernel(q_ref, k_ref, v_ref, cos_ref, sin_ref, qo_ref, ko_ref, vo_ref, kn_ref):
    cos = cos_ref[...]
    sin = sin_ref[...]
    lane = lax.broadcasted_iota(jnp.int32, (1, LANES), 1)
    first = (lane % (DA_QK // 2)) < (DA_QK // 4)
    lane_n = lax.broadcasted_iota(jnp.int32, (1, 2 * DA_HEADS), 1)

    def rope(x):
        partner = jnp.where(first, pltpu.roll(x, LANES - DA_QK // 4, 1), pltpu.roll(x, DA_QK // 4, 1))
        return x * cos + partner * sin

    kn = jnp.zeros((TM, 2 * DA_HEADS), F32)
    ones = jnp.ones((TM, DA_V), BF16)
    for h in range(DA_HEADS):
        hs = slice(h * DA_V, (h + 1) * DA_V)
        qo_ref[h] = (rope(q_ref[:, hs]) * Q_SCALE).astype(BF16)
        kr = rope(k_ref[:, hs])
        ko_ref[h] = kr.astype(BF16)
        ksq = kr * kr
        n1 = jnp.sum(jnp.where(lane < DA_QK, ksq, 0.0), axis=-1, keepdims=True)
        n2 = jnp.sum(jnp.where(lane >= DA_QK, ksq, 0.0), axis=-1, keepdims=True)
        kn = jnp.where(lane_n == 2 * h, n1, jnp.where(lane_n == 2 * h + 1, n2, kn))
        vo_ref[h, :, 0:DA_V] = v_ref[:, hs].astype(BF16)
        vo_ref[h, :, DA_V:] = ones
    kn_ref[...] = kn


def _da_prep(proj, cos_all, sin_all):
    def tok(colblk):
        return pl.BlockSpec((TM, W512), lambda i: (i, colblk))

    tab = pl.BlockSpec((TM, LANES), lambda i: (i, 0))
    out = pl.BlockSpec((DA_HEADS, TM, DA_V), lambda i: (0, i, 0))
    shp = jax.ShapeDtypeStruct((DA_HEADS, N_TOK, DA_V), BF16)
    return pl.pallas_call(
        _da_prep_kernel,
        grid=(N_TOK // TM,),
        in_specs=[tok(COL_Q_DA), tok(COL_K_DA), tok(COL_V_DA), tab, tab],
        out_specs=[out, out, pl.BlockSpec((DA_HEADS, TM, 2 * DA_V), lambda i: (0, i, 0)),
                   pl.BlockSpec((TM, 2 * DA_HEADS), lambda i: (i, 0))],
        out_shape=[shp, shp, jax.ShapeDtypeStruct((DA_HEADS, N_TOK, 2 * DA_V), BF16),
                   jax.ShapeDtypeStruct((N_TOK, 2 * DA_HEADS), F32)],
        compiler_params=pltpu.CompilerParams(dimension_semantics=("arbitrary",),
                                             vmem_limit_bytes=VMEM_LIMIT),
        name="da_prep",
    )(proj, proj, proj, cos_all, sin_all)


def _attn_kernel(*refs, has_ctx):
    if has_ctx:
        scal_ref, kmax_ref, q_ref, k_ref, v_ref, kc_ref, vc_ref, nw_ref, o_ref = refs
    else:
        scal_ref, kmax_ref, q_ref, k_ref, v_ref, nw_ref, o_ref = refs
    lam = scal_ref[0]
    out_scale = scal_ref[1]
    lane = lax.broadcasted_iota(jnp.int32, (1, LANES), 1)
    n_heads = q_ref.shape[0]
    n_keys = k_ref.shape[1]
    chunk = min(ATT_KC, n_keys)
    b = pl.program_id(0)
    if has_ctx:
        kc = kc_ref[...].astype(BF16)
        vc = vc_ref[...].astype(BF16)
        vc_ext = jnp.concatenate([vc, jnp.ones((PAST_LEN, DA_V), BF16)], axis=1)

    def shifted(q_c, h, shift):
        acc = None
        for c in range(n_keys // chunk):
            ks = slice(c * chunk, (c + 1) * chunk)
            p = jnp.exp2(_dot_nt(q_c, k_ref[h, ks, :]) - shift).astype(BF16)
            part = _dot(p, v_ref[h, ks, :])
            acc = part if acc is None else acc + part
        if has_ctx:
            acc = acc + _dot(jnp.exp2(_dot_nt(q_c, kc) - shift).astype(BF16), vc_ext)
        l = acc[:, DA_V:DA_V + 1]
        return acc[:, :DA_V] / l, l

    def exact(q_c, h):
        s = _dot_nt(q_c, k_ref[h])
        m = jnp.max(s, axis=-1, keepdims=True)
        if has_ctx:
            s_c = _dot_nt(q_c, kc)
            m = jnp.maximum(m, jnp.max(s_c, axis=-1, keepdims=True))
        p = jnp.exp2(s - m)
        l = jnp.sum(p, axis=-1, keepdims=True)
        acc = _dot(p.astype(BF16), v_ref[h, :, 0:DA_V])
        if has_ctx:
            p_c = jnp.exp2(s_c - m)
            l = l + jnp.sum(p_c, axis=-1, keepdims=True)
            acc = acc + _dot(p_c.astype(BF16), vc)
        return acc / l

    def finish(o1, o2):
        return (_rms(o1 - lam * o2, nw_ref[...]) * out_scale).astype(o_ref.dtype)

    for h in range(n_heads):
        head = h if n_heads > 1 else pl.program_id(1)
        cols = slice(None) if n_heads == 1 else slice(h * DA_V, (h + 1) * DA_V)
        qh = q_ref[h]
        zero = jnp.zeros_like(qh)
        qs = (jnp.where(lane < DA_QK, qh, zero), jnp.where(lane >= DA_QK, qh, zero))
        outs, sums = [], []
        for comp, q_c in enumerate(qs):
            qf = q_c.astype(F32)
            qn = jnp.sqrt(jnp.sum(qf * qf, axis=-1, keepdims=True))
            shift = qn * (kmax_ref[(b * DA_HEADS + head) * 2 + comp] * 1.01) + 1e-3
            o_c, l_c = shifted(q_c, h, shift)
            outs.append(o_c)
            sums.append(l_c)
        o_ref[:, cols] = finish(outs[0], outs[1])
        ok = jnp.min(jnp.minimum(sums[0], sums[1])) >= ATT_MIN_SUM

        @pl.when(jnp.logical_not(ok))
        def _():
            o_ref[:, cols] = finish(exact(qs[0], h), exact(qs[1], h))


def _attention(scal, q, k, v, knorm, cache_k, cache_v, da_norm_l, l):
    smem = pl.BlockSpec(memory_space=pltpu.SMEM)
    nw = pl.BlockSpec((1, DA_V), lambda *a: (0, 0))
    kmax_ctx = jnp.sqrt(jnp.max(knorm[:N_CTX_TOK].reshape(BATCH, SEQ, 2 * DA_HEADS), axis=1)).reshape(-1)
    ckn = jnp.sum(jnp.square(cache_k[:, l].reshape(DEC_BATCH, PAST_LEN, 2 * DA_HEADS, DA_QK)), axis=-1)
    kmax_lat = jnp.sqrt(jnp.maximum(jnp.max(knorm[N_CTX_TOK:].reshape(DEC_BATCH, DEC_SEQ, 2 * DA_HEADS), axis=1),
                                    jnp.max(ckn, axis=1))).reshape(-1)
    blk = pl.BlockSpec((DA_HEADS, SEQ, DA_V), lambda b: (0, b, 0))
    o_ctx = pl.pallas_call(
        functools.partial(_attn_kernel, has_ctx=False),
        grid=(BATCH,),
        in_specs=[smem, smem, blk, blk, pl.BlockSpec((DA_HEADS, SEQ, 2 * DA_V), lambda b: (0, b, 0)), nw],
        out_specs=pl.BlockSpec((SEQ, W512), lambda b: (b, 0)),
        out_shape=jax.ShapeDtypeStruct((N_CTX_TOK, W512), BF16),
        compiler_params=pltpu.CompilerParams(dimension_semantics=("arbitrary",), vmem_limit_bytes=VMEM_LIMIT),
        name="attn_ctx",
    )(scal, kmax_ctx, q, k, v, da_norm_l)
    nq = DEC_SEQ // ATT_TQ
    off_q = N_CTX_TOK // ATT_TQ
    off_k = N_CTX_TOK // DEC_SEQ
    ck = cache_k.reshape(DEC_BATCH, DEPTH, PAST_LEN, W512)
    cv = cache_v.reshape(DEC_BATCH, DEPTH, PAST_LEN, W512)
    ckv = pl.BlockSpec((None, None, PAST_LEN, DA_V), lambda b, h, i: (b, l, 0, h))
    o_lat = pl.pallas_call(
        functools.partial(_attn_kernel, has_ctx=True),
        grid=(DEC_BATCH, DA_HEADS, nq),
        in_specs=[smem, smem,
                  pl.BlockSpec((1, ATT_TQ, DA_V), lambda b, h, i: (h, off_q + b * nq + i, 0)),
                  pl.BlockSpec((1, DEC_SEQ, DA_V), lambda b, h, i: (h, off_k + b, 0)),
                  pl.BlockSpec((1, DEC_SEQ, 2 * DA_V), lambda b, h, i: (h, off_k + b, 0)),
                  ckv, ckv, nw],
        out_specs=pl.BlockSpec((ATT_TQ, DA_V), lambda b, h, i: (b * nq + i, h)),
        out_shape=jax.ShapeDtypeStruct((DEC_BATCH * DEC_SEQ, W512), BF16),
        compiler_params=pltpu.CompilerParams(dimension_semantics=("arbitrary", "arbitrary", "arbitrary"),
                                             vmem_limit_bytes=VMEM_LIMIT),
        name="attn_lat",
    )(scal, kmax_lat, q, k, v, ck, cv, da_norm_l)
    return o_ctx, o_lat


def _outproj_kernel(hg_ref, dac_ref, dal_ref, x_ref, mod_ref, w_ref, nw_ref, wr_ref, br_ref,
                    xo_ref, h_ref, lg_ref, wbf_ref):
    i = pl.program_id(0)

    @pl.when(i == 0)
    def _():
        wbf_ref[...] = w_ref[...].astype(BF16)

    is_ctx = i < N_CTX_TOK // TM
    da = jnp.where(is_ctx, dac_ref[...], dal_ref[...])
    mix = _dot(hg_ref[...], wbf_ref[0:HG_WIDTH, :]) + _dot(da, wbf_ref[HG_WIDTH:, :])
    g1 = mod_ref[:, 2 * D_MODEL:3 * D_MODEL]
    sh2 = mod_ref[:, 3 * D_MODEL:4 * D_MODEL]
    sc2 = mod_ref[:, 4 * D_MODEL:5 * D_MODEL]
    x = x_ref[...] + g1 * mix
    xo_ref[...] = x
    h = _rms(x, nw_ref[...]) * (1.0 + sc2) + sh2
    h_ref[...] = _pack_rows(h)
    lg_ref[...] = _dot_precise(h, wr_ref[...]) + br_ref[...]


def _outproj(o_hg, o_da_ctx, o_da_lat, x, mods_l, w_out, norm2_l, w_router_l, b_router_l, l):
    n_ctx_t = N_CTX_TOK // TM
    n_lat_t = (N_TOK - N_CTX_TOK) // TM
    tokspec = lambda w: pl.BlockSpec((TM, w), lambda i: (i, 0))
    return pl.pallas_call(
        _outproj_kernel,
        grid=(N_TOK // TM,),
        in_specs=[
            tokspec(W512),
            pl.BlockSpec((TM, W512), lambda i: (jnp.minimum(i, n_ctx_t - 1), 0)),
            pl.BlockSpec((TM, W512), lambda i: (jnp.clip(i - n_ctx_t, 0, n_lat_t - 1), 0)),
            tokspec(D_MODEL),
            pl.BlockSpec((None, 1, 6 * D_MODEL), lambda i: (i * TM // COND_GROUP, 0, 0)),
            pl.BlockSpec((None, D_MODEL, D_MODEL), lambda i: (l, 0, 0)),
            pl.BlockSpec((1, D_MODEL), lambda i: (0, 0)),
            pl.BlockSpec((D_MODEL, N_EXPERTS), lambda i: (0, 0)),
            pl.BlockSpec((1, N_EXPERTS), lambda i: (0, 0)),
        ],
        out_specs=[tokspec(D_MODEL), tokspec(D_MODEL // 2), tokspec(N_EXPERTS)],
        out_shape=[jax.ShapeDtypeStruct((N_TOK, D_MODEL), F32),
                   jax.ShapeDtypeStruct((N_TOK, D_MODEL // 2), jnp.uint32),
                   jax.ShapeDtypeStruct((N_TOK, N_EXPERTS), F32)],
        scratch_shapes=[pltpu.VMEM((D_MODEL, D_MODEL), BF16)],
        compiler_params=pltpu.CompilerParams(dimension_semantics=("arbitrary",),
                                             vmem_limit_bytes=VMEM_LIMIT),
        name="outproj_norm_router",
    )(o_hg, o_da_ctx, o_da_lat, x, mods_l, w_out, norm2_l, w_router_l, b_router_l)


def _moe_kernel(be_ref, nu_ref, x_ref, wgu_ref, bgu_ref, wdn_ref, bdn_ref, o_ref, wgu_bf, wdn_bf):
    i = pl.program_id(0)
    used = i < nu_ref[0]
    prev = be_ref[jnp.maximum(i - 1, 0)]
    fresh = jnp.logical_or(i == 0, be_ref[i] != prev)

    @pl.when(jnp.logical_and(used, fresh))
    def _():
        wgu_bf[...] = wgu_ref[...].astype(BF16)
        wdn_bf[...] = wdn_ref[...].astype(BF16)

    @pl.when(used)
    def _():
        x_lo, x_hi = _unpack_rows(x_ref[...])
        x = jnp.concatenate([x_lo.astype(BF16), x_hi.astype(BF16)], axis=1)
        gu = _dot(x, wgu_bf[...]) + bgu_ref[...]
        gate = jnp.minimum(gu[:, :D_FF], SWIGLU_LIMIT)
        lin = jnp.clip(gu[:, D_FF:], -SWIGLU_LIMIT, SWIGLU_LIMIT)
        act = (lin + 1.0) * gate * _sigmoid(SWIGLU_ALPHA * gate)
        y = _dot(act.astype(BF16), wdn_bf[...]) + bdn_ref[...]
        o_ref[...] = _pack_rows(y)

    @pl.when(jnp.logical_not(used))
    def _():
        o_ref[...] = jnp.zeros_like(o_ref)


def _moe_experts(blk_exp, n_used, xs, w_gu, b_gu, w_dn, b_dn, l):
    def rowblk(i, be, nu):
        return (jnp.minimum(i, nu[0] - 1), 0)

    grid_spec = pltpu.PrefetchScalarGridSpec(
        num_scalar_prefetch=2,
        grid=(MOE_NBLK,),
        in_specs=[
            pl.BlockSpec((MOE_BLK, D_MODEL // 2), rowblk),
            pl.BlockSpec((None, None, D_MODEL, 2 * D_FF), lambda i, be, nu: (l, be[i], 0, 0)),
            pl.BlockSpec((None, None, 1, 2 * D_FF), lambda i, be, nu: (l, be[i], 0, 0)),
            pl.BlockSpec((None, None, D_FF, D_MODEL), lambda i, be, nu: (l, be[i], 0, 0)),
            pl.BlockSpec((None, None, 1, D_MODEL), lambda i, be, nu: (l, be[i], 0, 0)),
        ],
        out_specs=pl.BlockSpec((MOE_BLK, D_MODEL // 2), lambda i, be, nu: (i, 0)),
        scratch_shapes=[pltpu.VMEM((D_MODEL, 2 * D_FF), BF16), pltpu.VMEM((D_FF, D_MODEL), BF16)],
    )
    return pl.pallas_call(
        _moe_kernel,
        grid_spec=grid_spec,
        out_shape=jax.ShapeDtypeStruct((MOE_NBLK * MOE_BLK, D_MODEL // 2), jnp.uint32),
        compiler_params=pltpu.CompilerParams(dimension_semantics=("arbitrary",),
                                             vmem_limit_bytes=VMEM_LIMIT),
        name="moe_experts",
    )(blk_exp, n_used, xs, w_gu, b_gu.reshape(DEPTH, N_EXPERTS, 1, 2 * D_FF),
      w_dn, b_dn.reshape(DEPTH, N_EXPERTS, 1, D_MODEL))


def _route_kernel(lg_ref, exp_ref, gate_ref, rank_ref, cnt_ref, run_ref):
    i = pl.program_id(0)

    @pl.when(i == 0)
    def _():
        run_ref[...] = jnp.zeros_like(run_ref)

    lg = lg_ref[...]
    lane = lax.broadcasted_iota(jnp.int32, (1, N_EXPERTS), 1)
    lane_k = lax.broadcasted_iota(jnp.int32, (1, TOP_K), 1)
    vals, idxs, hots = [], [], []
    for _ in range(TOP_K):
        m = jnp.max(lg, axis=-1, keepdims=True)
        idx = jnp.min(jnp.where(lg == m, lane, N_EXPERTS), axis=-1, keepdims=True)
        hot = lane == idx
        vals.append(m)
        idxs.append(idx)
        hots.append(hot)
        lg = jnp.where(hot, -jnp.inf, lg)
    exps = [jnp.exp(v - vals[0]) for v in vals]
    denom = exps[0] + exps[1] + exps[2] + exps[3]
    tokhot = (hots[0] | hots[1] | hots[2] | hots[3]).astype(BF16)
    r = lax.broadcasted_iota(jnp.int32, (TM, TM), 0)
    cidx = lax.broadcasted_iota(jnp.int32, (TM, TM), 1)
    before = _dot((cidx < r).astype(BF16), tokhot) + run_ref[...]
    e_out = jnp.zeros((TM, TOP_K), jnp.int32)
    g_out = jnp.zeros((TM, TOP_K), F32)
    r_out = jnp.zeros((TM, TOP_K), jnp.int32)
    for k in range(TOP_K):
        rk = jnp.sum(jnp.where(hots[k], before, 0.0), axis=-1, keepdims=True).astype(jnp.int32)
        e_out = jnp.where(lane_k == k, idxs[k], e_out)
        g_out = jnp.where(lane_k == k, exps[k] / denom, g_out)
        r_out = jnp.where(lane_k == k, rk, r_out)
    exp_ref[...] = e_out
    gate_ref[...] = g_out
    rank_ref[...] = r_out
    run_ref[...] += jnp.sum(tokhot.astype(F32), axis=0, keepdims=True)
    cnt_ref[...] = run_ref[...].astype(jnp.int32)


def _route(logits):
    tk = pl.BlockSpec((TM, TOP_K), lambda i: (i, 0))
    top_exp, gates, rank, counts = pl.pallas_call(
        _route_kernel,
        grid=(N_TOK // TM,),
        in_specs=[pl.BlockSpec((TM, N_EXPERTS), lambda i: (i, 0))],
        out_specs=[tk, tk, tk, pl.BlockSpec((1, N_EXPERTS), lambda i: (0, 0))],
        out_shape=[jax.ShapeDtypeStruct((N_TOK, TOP_K), jnp.int32),
                   jax.ShapeDtypeStruct((N_TOK, TOP_K), F32),
                   jax.ShapeDtypeStruct((N_TOK, TOP_K), jnp.int32),
                   jax.ShapeDtypeStruct((1, N_EXPERTS), jnp.int32)],
        scratch_shapes=[pltpu.VMEM((1, N_EXPERTS), F32)],
        compiler_params=pltpu.CompilerParams(dimension_semantics=("arbitrary",)),
        name="route",
    )(logits)
    counts = counts[0]
    padded = (counts + MOE_BLK - 1) // MOE_BLK * MOE_BLK
    pad_end = jnp.cumsum(padded)
    pad_start = pad_end - padded
    start = jnp.cumsum(counts) - counts
    eq = top_exp[:, :, None] == jnp.arange(N_EXPERTS, dtype=jnp.int32)[None, None, :]
    dest = (jnp.sum(jnp.where(eq, pad_start[None, None, :], 0), axis=-1) + rank).T.reshape(-1)
    blk_exp = jnp.minimum(
        jnp.sum((pad_end[None, :] <= (jnp.arange(MOE_NBLK, dtype=jnp.int32) * MOE_BLK)[:, None]).astype(jnp.int32),
                axis=-1), N_EXPERTS - 1).astype(jnp.int32)
    n_used = (pad_end[-1:] // MOE_BLK).astype(jnp.int32)
    tok_ids = jnp.arange(N_ASG, dtype=jnp.int32) // TOP_K
    _, tok_sorted = lax.sort((top_exp.reshape(-1), tok_ids), num_keys=1, is_stable=True)
    slot = jnp.arange(MOE_NBLK * MOE_BLK, dtype=jnp.int32)
    e_slot = jnp.repeat(blk_exp, MOE_BLK)
    src = jnp.clip(start[e_slot] + slot - pad_start[e_slot], 0, N_ASG - 1)
    slot_tok = tok_sorted[src]
    return slot_tok, gates, dest, blk_exp, n_used


def _combine_kernel(y0_ref, y1_ref, y2_ref, y3_ref, gate_ref, x_ref, mod_ref, o_ref):
    g = gate_ref[...]
    half = D_MODEL // 2
    acc_lo = jnp.zeros((TM, half), F32)
    acc_hi = jnp.zeros((TM, half), F32)
    for k, y_ref in enumerate((y0_ref, y1_ref, y2_ref, y3_ref)):
        lo, hi = _unpack_rows(y_ref[...])
        acc_lo = acc_lo + g[:, k:k + 1] * lo
        acc_hi = acc_hi + g[:, k:k + 1] * hi
    g2 = mod_ref[:, 5 * D_MODEL:]
    o_ref[:, :half] = x_ref[:, :half] + g2[:, :half] * acc_lo
    o_ref[:, half:] = x_ref[:, half:] + g2[:, half:] * acc_hi


def _combine(ys_tok, gates, x_mid, mods_l):
    n_i = N_TOK // TM

    def ysblk(k):
        return pl.BlockSpec((TM, D_MODEL // 2), lambda i: (k * n_i + i, 0))

    return pl.pallas_call(
        _combine_kernel,
        grid=(n_i,),
        in_specs=[ysblk(0), ysblk(1), ysblk(2), ysblk(3),
                  pl.BlockSpec((TM, TOP_K), lambda i: (i, 0)),
                  pl.BlockSpec((TM, D_MODEL), lambda i: (i, 0)),
                  pl.BlockSpec((None, 1, 6 * D_MODEL), lambda i: (i * TM // COND_GROUP, 0, 0))],
        out_specs=pl.BlockSpec((TM, D_MODEL), lambda i: (i, 0)),
        out_shape=jax.ShapeDtypeStruct((N_TOK, D_MODEL), F32),
        compiler_params=pltpu.CompilerParams(dimension_semantics=("arbitrary",), vmem_limit_bytes=VMEM_LIMIT),
        name="moe_combine",
    )(ys_tok, ys_tok, ys_tok, ys_tok, gates, x_mid, mods_l)


def _final_norm_kernel(x_ref, w_ref, oc_ref, ol_ref):
    y = _rms(x_ref[...], w_ref[...])
    is_ctx = pl.program_id(0) < N_CTX_TILES

    @pl.when(is_ctx)
    def _():
        oc_ref[...] = y

    @pl.when(jnp.logical_not(is_ctx))
    def _():
        ol_ref[...] = y


def _final_norm(x, w):
    n_lat_tiles = N_TOK // TM - N_CTX_TILES
    return pl.pallas_call(
        _final_norm_kernel,
        grid=(N_TOK // TM,),
        in_specs=[pl.BlockSpec((TM, D_MODEL), lambda i: (i, 0)), pl.BlockSpec((1, D_MODEL), lambda i: (0, 0))],
        out_specs=[pl.BlockSpec((TM, D_MODEL), lambda i: (jnp.minimum(i, N_CTX_TILES - 1), 0)),
                   pl.BlockSpec((TM, D_MODEL), lambda i: (jnp.clip(i - N_CTX_TILES, 0, n_lat_tiles - 1), 0))],
        out_shape=[jax.ShapeDtypeStruct((N_CTX_TOK, D_MODEL), F32),
                   jax.ShapeDtypeStruct((N_TOK - N_CTX_TOK, D_MODEL), F32)],
        compiler_params=pltpu.CompilerParams(dimension_semantics=("arbitrary",)),
        name="final_norm",
    )(x, w)


def kernel(x_prompt, x_sample, c, cache_k, cache_v, state_hgrn, c_ctx, norm1_w, norm2_w, w_ada, b_ada, w_in, hg_lb, hg_norm_w, lam_q1, lam_k1, lam_q2, lam_k2, da_norm_w, w_out, w_router, b_router, w_gu, b_gu, w_dn, b_dn, norm_f_w):
    lb_all = jnp.cumsum(jax.nn.softmax(hg_lb.astype(F32), axis=0), axis=0)
    lb_all = lb_all - lb_all[0:1]
    lam_init = [0.8 - 0.6 * math.exp(-0.3 * l) for l in range(DEPTH)]
    lam = (jnp.exp(jnp.sum(lam_q1 * lam_k1, axis=-1)) - jnp.exp(jnp.sum(lam_q2 * lam_k2, axis=-1))
           + jnp.asarray(lam_init, F32))
    cos_all, sin_all = _rope_tables()

    conds = jnp.concatenate([c_ctx[None, :], c, jnp.zeros((COND_ROWS - 1 - DEC_BATCH, D_MODEL), F32)], axis=0)
    mods = _ada_mod(conds, w_ada, b_ada)
    mods = mods[:, :N_COND].reshape(DEPTH, N_COND, 1, 6 * D_MODEL)

    x = jnp.concatenate([x_prompt.reshape(N_CTX_TOK, D_MODEL), x_sample.reshape(-1, D_MODEL)], axis=0)
    new_k = jnp.zeros((BATCH, DEPTH, SEQ, W512), F32)
    new_v = jnp.zeros((BATCH, DEPTH, SEQ, W512), F32)
    new_state = jnp.zeros((BATCH, DEPTH, 2, HG_HEADS, HG_DK, HG_DV), F32)
    for l in range(DEPTH):
        proj, new_k, new_v = _inproj(x, norm1_w[l][None, :], mods[l], w_in, new_k, new_v, l)
        o_fw, new_state = _hgrn_dir(proj, None, lb_all[l], state_hgrn, None, new_state, l, rev=False)
        o_hg, new_state = _hgrn_dir(proj, o_fw, lb_all[l], state_hgrn, hg_norm_w[l][None, :], new_state, l,
                                    rev=True)
        q_r, k_r, v_b, knorm = _da_prep(proj, cos_all, sin_all)
        scal = jnp.stack([lam[l], jnp.asarray(1.0 - lam_init[l], F32)])
        o_da_ctx, o_da_lat = _attention(scal, q_r, k_r, v_b, knorm, cache_k, cache_v, da_norm_w[l][None, :], l)
        x_mid, h2, logits = _outproj(o_hg, o_da_ctx, o_da_lat, x, mods[l], w_out, norm2_w[l][None, :],
                                     w_router[l], b_router[l][None, :], l)
        slot_tok, gates, dest, blk_exp, n_used = _route(logits)
        xs = h2.at[slot_tok].get(mode="promise_in_bounds")
        ys = _moe_experts(blk_exp, n_used, xs, w_gu, b_gu, w_dn, b_dn, l)
        ys_tok = ys.at[dest].get(mode="promise_in_bounds")
        x = _combine(ys_tok, gates, x_mid, mods[l])
    y_ctx, y_lat = _final_norm(x, norm_f_w[None, :])
    return (y_ctx.reshape(BATCH, SEQ, D_MODEL), y_lat.reshape(DEC_BATCH, DEC_SEQ, D_MODEL),
            new_k.reshape(BATCH, DEPTH, SEQ, DA_HEADS, 2 * DA_QK), new_v.reshape(BATCH, DEPTH, SEQ, DA_HEADS, DA_V),
            new_state)
```

```python
import functools
import math

import jax
import jax.numpy as jnp
from jax import lax
from jax.experimental import pallas as pl
from jax.experimental.pallas import tpu as pltpu
from jax.experimental.pallas import tpu_sc as plsc

F32 = jnp.float32
BF16 = jnp.bfloat16

D_MODEL = 1024
BATCH = 16
SEQ = 256
DEPTH = 4
DEC_BATCH = 2
DEC_SEQ = 4096
PAST_LEN = 256
GRID_W = 64
HG_HEADS = 4
HG_DK = 128
HG_DV = 128
HG_WIDTH = HG_HEADS * HG_DK
DA_HEADS = 4
DA_QK = 64
DA_V = 2 * DA_QK
DA_WIDTH = DA_HEADS * DA_V
ROPE_BASE = 10000.0
IN_DIM = 5 * HG_WIDTH + 3 * DA_WIDTH
N_EXPERTS = 32
TOP_K = 4
D_FF = D_MODEL
SWIGLU_ALPHA = 1.702
SWIGLU_LIMIT = 7.0
EPS = 1e-6

N_CTX_TOK = BATCH * SEQ
N_TOK = N_CTX_TOK + DEC_BATCH * DEC_SEQ
COND_GROUP = DEC_SEQ
assert N_CTX_TOK == COND_GROUP
N_COND = N_TOK // COND_GROUP
COND_ROWS = 8

COL_Q_HG, COL_F_FW, COL_F_BW, COL_I_HG, COL_G_HG, COL_Q_DA, COL_K_DA, COL_V_DA = range(8)
W512 = 512
LANES = 128

HG_CHUNK = 64
HG_HALF = HG_CHUNK // 2
HG_ROWS = 256
HG_EXP_CLAMP = 80.0

TM = 512
TN_IN = 2048
ATT_TQ = 256
Q_SCALE = math.log2(math.e) / math.sqrt(DA_QK)
ATT_KC = 512
ATT_MIN_SUM = 2.0 ** -40
MOE_BLK = 512
N_ASG = N_TOK * TOP_K
MOE_NBLK = N_ASG // MOE_BLK + N_EXPERTS
VMEM_LIMIT = 56 * 1024 * 1024


def _dot(a, b):
    return jnp.dot(a, b, preferred_element_type=F32)


def _dot_nt(a, b):
    return lax.dot_general(a, b, (((1,), (1,)), ((), ())), preferred_element_type=F32)


def _dot_tn(a, b):
    return lax.dot_general(a, b, (((0,), (0,)), ((), ())), preferred_element_type=F32)


def _split2(x):
    hi = x.astype(BF16)
    lo = (x - hi.astype(F32)).astype(BF16)
    return hi, lo


def _split3(x):
    hi = x.astype(BF16)
    r = x - hi.astype(F32)
    mid = r.astype(BF16)
    lo = (r - mid.astype(F32)).astype(BF16)
    return hi, mid, lo


def _dot_precise(a, b):
    a_hi, a_lo = _split2(a)
    b_hi, b_lo = _split2(b)
    return _dot(a_hi, b_hi) + (_dot(a_hi, b_lo) + _dot(a_lo, b_hi))


def _pack_rows(x):
    n = x.shape[1] // 2
    lo = pltpu.bitcast(x[:, :n].astype(BF16).astype(F32), jnp.uint32)
    hi = pltpu.bitcast(x[:, n:].astype(BF16).astype(F32), jnp.uint32)
    return (lo >> 16) | hi


def _unpack_rows(u):
    lo = pltpu.bitcast(u << 16, F32)
    hi = pltpu.bitcast(u & jnp.uint32(0xFFFF0000), F32)
    return lo, hi


def _rms(x, w):
    ms = jnp.mean(x * x, axis=-1, keepdims=True)
    return x * lax.rsqrt(ms + EPS) * w


def _sigmoid(x):
    return 1.0 / (1.0 + jnp.exp(-x))


def _ada_kernel(cond_ref, w_ref, b_ref, o_ref):
    cnd = cond_ref[...]
    s = cnd * _sigmoid(cnd)
    o_ref[...] = _dot_precise(s, w_ref[...]) + b_ref[...]


def _ada_mod(conds, w_ada, b_ada):
    nj = 6 * D_MODEL // D_MODEL
    return pl.pallas_call(
        _ada_kernel,
        grid=(DEPTH, nj),
        in_specs=[
            pl.BlockSpec((COND_ROWS, D_MODEL), lambda l, j: (0, 0)),
            pl.BlockSpec((None, D_MODEL, D_MODEL), lambda l, j: (l, 0, j)),
            pl.BlockSpec((None, 1, D_MODEL), lambda l, j: (l, 0, j)),
        ],
        out_specs=pl.BlockSpec((None, COND_ROWS, D_MODEL), lambda l, j: (l, 0, j)),
        out_shape=jax.ShapeDtypeStruct((DEPTH, COND_ROWS, 6 * D_MODEL), F32),
        compiler_params=pltpu.CompilerParams(dimension_semantics=("arbitrary", "arbitrary"),
                                             vmem_limit_bytes=VMEM_LIMIT),
        name="ada_mod",
    )(conds, w_ada, b_ada.reshape(DEPTH, 1, 6 * D_MODEL))


SEQ_PER_TILE = TM // SEQ
N_CTX_TILES = N_CTX_TOK // TM
KV_COL_BLK = (COL_K_DA * W512) // TN_IN
assert COL_K_DA * W512 == KV_COL_BLK * TN_IN + TN_IN - 2 * W512


def _inproj_kernel(x_ref, nw_ref, mod_ref, w_ref, kin_ref, vin_ref, o_ref, kc_ref, vc_ref, wbf_ref):
    del kin_ref, vin_ref
    n = pl.program_id(0)
    i = pl.program_id(1)

    @pl.when(i == 0)
    def _():
        wbf_ref[...] = w_ref[...].astype(BF16)

    x = x_ref[...]
    sh = mod_ref[:, 0:D_MODEL]
    sc = mod_ref[:, D_MODEL:2 * D_MODEL]
    h = _rms(x, nw_ref[...]) * (1.0 + sc) + sh
    o = _dot(h.astype(BF16), wbf_ref[...])
    o_ref[...] = o

    @pl.when(jnp.logical_and(n == KV_COL_BLK, i < N_CTX_TILES))
    def _():
        for s in range(SEQ_PER_TILE):
            kc_ref[s] = o[s * SEQ:(s + 1) * SEQ, TN_IN - 2 * W512:TN_IN - W512]
            vc_ref[s] = o[s * SEQ:(s + 1) * SEQ, TN_IN - W512:]


def _inproj(x, norm_w_l, mods_l, w_in, new_k, new_v, l):
    n_i = N_TOK // TM

    def cache_idx(n, i):
        return (jnp.where(n == KV_COL_BLK, jnp.minimum(i, N_CTX_TILES - 1), 0), l, 0, 0)

    cache_spec = pl.BlockSpec((SEQ_PER_TILE, None, SEQ, W512), cache_idx)
    cache_shape = jax.ShapeDtypeStruct((BATCH, DEPTH, SEQ, W512), F32)
    return pl.pallas_call(
        _inproj_kernel,
        grid=(IN_DIM // TN_IN, n_i),
        in_specs=[
            pl.BlockSpec((TM, D_MODEL), lambda n, i: (i, 0)),
            pl.BlockSpec((1, D_MODEL), lambda n, i: (0, 0)),
            pl.BlockSpec((None, 1, 6 * D_MODEL), lambda n, i: (i * TM // COND_GROUP, 0, 0)),
            pl.BlockSpec((None, D_MODEL, TN_IN), lambda n, i: (l, 0, n)),
            pl.BlockSpec(memory_space=pl.ANY),
            pl.BlockSpec(memory_space=pl.ANY),
        ],
        out_specs=[pl.BlockSpec((TM, TN_IN), lambda n, i: (i, n)), cache_spec, cache_spec],
        out_shape=[jax.ShapeDtypeStruct((N_TOK, IN_DIM), F32), cache_shape, cache_shape],
        input_output_aliases={4: 1, 5: 2},
        scratch_shapes=[pltpu.VMEM((D_MODEL, TN_IN), BF16)],
        compiler_params=pltpu.CompilerParams(dimension_semantics=("arbitrary", "arbitrary"),
                                             vmem_limit_bytes=VMEM_LIMIT),
        name="norm_inproj",
    )(x, norm_w_l, mods_l, w_in, new_k, new_v)


N_HG_BLK = N_TOK // HG_ROWS
N_HG_CTX_BLK = N_CTX_TOK // HG_ROWS
HG_BPS_CTX = SEQ // HG_ROWS
HG_BPS_LAT = DEC_SEQ // HG_ROWS


def _hg_block(step, rev):
    return (N_HG_BLK - 1 - step) if rev else step


def _hg_lat_seq(p):
    return jnp.clip((p - N_HG_CTX_BLK) // HG_BPS_LAT, 0, DEC_BATCH - 1)


def _hg_ctx_seq(p):
    return jnp.minimum(p // HG_BPS_CTX, BATCH - 1)


def _hgrn_kernel(*refs, rev):
    if rev:
        (q_ref, f_ref, v_ref, g_ref, ofw_ref, lb_ref, s0_ref, nw_ref, _, o_ref, sfin_ref, st_ref) = refs
    else:
        (q_ref, f_ref, v_ref, lb_ref, s0_ref, _, o_ref, sfin_ref, st_ref) = refs
    p = _hg_block(pl.program_id(0), rev)
    is_ctx = p < N_HG_CTX_BLK
    pos = jnp.where(is_ctx, p % HG_BPS_CTX, (p - N_HG_CTX_BLK) % HG_BPS_LAT)
    bps = jnp.where(is_ctx, HG_BPS_CTX, HG_BPS_LAT)
    first_pos = (bps - 1) if rev else 0
    last_pos = 0 if rev else (bps - 1)

    @pl.when(jnp.logical_and(pos == first_pos, is_ctx))
    def _():
        st_ref[...] = jnp.zeros_like(st_ref)

    @pl.when(jnp.logical_and(pos == first_pos, jnp.logical_not(is_ctx)))
    def _():
        for h in range(HG_HEADS):
            st_ref[h] = s0_ref[h].T

    lb = lb_ref[...]
    oml = 1.0 - lb
    log_oml = jnp.log1p(-lb)
    row = lax.broadcasted_iota(jnp.int32, (HG_CHUNK, HG_CHUNK), 0)
    col = lax.broadcasted_iota(jnp.int32, (HG_CHUNK, HG_CHUNK), 1)
    tri = ((col >= row) if rev else (col <= row)).astype(BF16)
    same_half = (row >= HG_HALF) == (col >= HG_HALF)
    if rev:
        m_diag = jnp.logical_and(same_half, col >= row)
        m_off = jnp.logical_and(row < HG_HALF, col >= HG_HALF)
    else:
        m_diag = jnp.logical_and(same_half, col <= row)
        m_off = jnp.logical_and(row >= HG_HALF, col < HG_HALF)
    row1 = lax.broadcasted_iota(jnp.int32, (HG_CHUNK, 1), 0)
    n_chunks = HG_ROWS // HG_CHUNK

    def chunk(ci):
        c = (n_chunks - 1 - ci) if rev else ci
        rows = slice(c * HG_CHUNK, (c + 1) * HG_CHUNK)
        fx = f_ref[rows, :]
        q = q_ref[rows, :]
        v = v_ref[rows, :]
        e = jnp.exp(-jnp.abs(fx))
        r = 1.0 / (1.0 + e)
        er = e * r
        nonneg = fx >= 0.0
        sig = jnp.where(nonneg, r, er)
        nsig = jnp.where(nonneg, er, r)
        f = lb + oml * sig
        logf = jnp.maximum(jnp.log(f), log_oml + (jnp.minimum(fx, 0.0) - jnp.log1p(e)))
        k = oml * nsig
        hi, mid, lo = _split3(logf)
        a = _dot(tri, hi) + (_dot(tri, mid) + _dot(tri, lo))
        a_mid = jnp.where(row1 < HG_HALF, a[HG_HALF // 2:HG_HALF // 2 + 1, :],
                          a[HG_HALF + HG_HALF // 2:HG_HALF + HG_HALF // 2 + 1, :])
        if rev:
            a_b = a[HG_HALF:HG_HALF + 1, :]
            a_last = a[0:1, :]
        else:
            a_b = a[HG_HALF - 1:HG_HALF, :]
            a_last = a[HG_CHUNK - 1:HG_CHUNK, :]
        qd = (q * jnp.exp(jnp.minimum(a - a_mid, HG_EXP_CLAMP))).astype(BF16)
        kd = (k * jnp.exp(jnp.minimum(a_mid - a, HG_EXP_CLAMP))).astype(BF16)
        qo = (q * jnp.exp(jnp.minimum(a - a_b, 0.0))).astype(BF16)
        ko = (k * jnp.exp(jnp.minimum(a_b - a, 0.0))).astype(BF16)
        qa = (q * jnp.exp(a)).astype(BF16)
        kdec = (k * jnp.exp(a_last - a)).astype(BF16)
        dec = jnp.exp(a_last)
        vb = v.astype(BF16)
        for h in range(HG_HEADS):
            hs = slice(h * HG_DK, (h + 1) * HG_DK)
            p1 = _dot_nt(qd[:, hs], kd[:, hs])
            p2 = _dot_nt(qo[:, hs], ko[:, hs])
            attn = jnp.where(m_diag, p1, jnp.where(m_off, p2, 0.0)).astype(BF16)
            st = st_ref[h]
            o_h = _dot_nt(qa[:, hs], st.astype(BF16)) + _dot(attn, vb[:, hs])
            st_ref[h] = dec[:, hs] * st + _dot_tn(vb[:, hs], kdec[:, hs])
            if rev:
                tot = ofw_ref[rows, hs] + o_h
                g = g_ref[rows, hs]
                y = _rms(tot, nw_ref[...]) * (g * _sigmoid(g))
                o_ref[rows, hs] = y.astype(o_ref.dtype)
            else:
                o_ref[rows, hs] = o_h

    for ci in range(n_chunks):
        chunk(ci)

    @pl.when(jnp.logical_and(pos == last_pos, is_ctx))
    def _():
        for h in range(HG_HEADS):
            sfin_ref[h] = st_ref[h].T


def _hgrn_dir(proj, o_fw, lb_l, state_hgrn, hg_norm_l, new_state, l, rev):
    d = 1 if rev else 0

    def tok(colblk):
        return pl.BlockSpec((HG_ROWS, W512), lambda s: (_hg_block(s, rev), colblk))

    in_specs = [tok(COL_Q_HG), tok(COL_F_BW if rev else COL_F_FW), tok(COL_I_HG)]
    args = [proj, proj, proj]
    if rev:
        in_specs += [tok(COL_G_HG), pl.BlockSpec((HG_ROWS, W512), lambda s: (_hg_block(s, rev), 0))]
        args += [proj, o_fw]
    in_specs += [
        pl.BlockSpec((None, 1, W512), lambda s: (d, 0, 0)),
        pl.BlockSpec((None, None, None, HG_HEADS, HG_DK, HG_DV),
                     lambda s: (_hg_lat_seq(_hg_block(s, rev)), l, d, 0, 0, 0)),
    ]
    args += [lb_l.reshape(2, 1, W512), state_hgrn]
    if rev:
        in_specs.append(pl.BlockSpec((1, HG_DV), lambda s: (0, 0)))
        args.append(hg_norm_l)
    in_specs.append(pl.BlockSpec(memory_space=pl.ANY))
    args.append(new_state)
    return pl.pallas_call(
        functools.partial(_hgrn_kernel, rev=rev),
        grid=(N_HG_BLK,),
        in_specs=in_specs,
        out_specs=[
            pl.BlockSpec((HG_ROWS, W512), lambda s: (_hg_block(s, rev), 0)),
            pl.BlockSpec((None, None, None, HG_HEADS, HG_DK, HG_DV),
                         lambda s: (_hg_ctx_seq(_hg_block(s, rev)), l, d, 0, 0, 0)),
        ],
        out_shape=[
            jax.ShapeDtypeStruct((N_TOK, W512), BF16 if rev else F32),
            jax.ShapeDtypeStruct((BATCH, DEPTH, 2, HG_HEADS, HG_DK, HG_DV), F32),
        ],
        input_output_aliases={len(args) - 1: 1},
        scratch_shapes=[pltpu.VMEM((HG_HEADS, HG_DV, HG_DK), F32)],
        compiler_params=pltpu.CompilerParams(dimension_semantics=("arbitrary",),
                                             vmem_limit_bytes=VMEM_LIMIT),
        name="hgrn_bwd" if rev else "hgrn_fwd",
    )(*args)


def _rope_tables():
    n_pairs = DA_QK // 4
    inv_freq = ROPE_BASE ** (-jnp.arange(n_pairs, dtype=F32) / n_pairs)
    t = jnp.arange(DEC_SEQ)
    row_idx = (t // GRID_W).astype(F32)
    col_idx = (t % GRID_W).astype(F32)
    lane = jnp.arange(LANES)
    jj = lane % DA_QK
    use_col = (jj // (DA_QK // 2)) == 1
    second = (jj % (DA_QK // 2)) >= n_pairs
    fr = inv_freq[jj % n_pairs]
    ang_r = row_idx[:, None] * inv_freq[None, :]
    ang_c = col_idx[:, None] * inv_freq[None, :]
    cos_r, sin_r, cos_c, sin_c = jnp.cos(ang_r), jnp.sin(ang_r), jnp.cos(ang_c), jnp.sin(ang_c)
    fi = jj % n_pairs
    cos_t = jnp.where(use_col[None, :], cos_c[:, fi], cos_r[:, fi])
    sin_t = jnp.where(use_col[None, :], sin_c[:, fi], sin_r[:, fi])
    sin_t = jnp.where(second[None, :], sin_t, -sin_t)
    del fr
    cos_all = jnp.concatenate([jnp.ones((N_CTX_TOK, LANES), F32)] + [cos_t] * DEC_BATCH, axis=0)
    sin_all = jnp.concatenate([jnp.zeros((N_CTX_TOK, LANES), F32)] + [sin_t] * DEC_BATCH, axis=0)
    return cos_all, sin_all


def _da_prep_kernel(q_ref, k_ref, v_ref, cos_ref, sin_ref, qo_ref, ko_ref, vo_ref, kn_ref):
    cos = cos_ref[...]
    sin = sin_ref[...]
    lane = lax.broadcasted_iota(jnp.int32, (1, LANES), 1)
    first = (lane % (DA_QK // 2)) < (DA_QK // 4)
    lane_n = lax.broadcasted_iota(jnp.int32, (1, 2 * DA_HEADS), 1)

    def rope(x):
        partner = jnp.where(first, pltpu.roll(x, LANES - DA_QK // 4, 1), pltpu.roll(x, DA_QK // 4, 1))
        return x * cos + partner * sin

    kn = jnp.zeros((TM, 2 * DA_HEADS), F32)
    ones = jnp.ones((TM, DA_V), BF16)
    for h in range(DA_HEADS):
        hs = slice(h * DA_V, (h + 1) * DA_V)
        qo_ref[h] = (rope(q_ref[:, hs]) * Q_SCALE).astype(BF16)
        kr = rope(k_ref[:, hs])
        ko_ref[h] = kr.astype(BF16)
        ksq = kr * kr
        n1 = jnp.sum(jnp.where(lane < DA_QK, ksq, 0.0), axis=-1, keepdims=True)
        n2 = jnp.sum(jnp.where(lane >= DA_QK, ksq, 0.0), axis=-1, keepdims=True)
        kn = jnp.where(lane_n == 2 * h, n1, jnp.where(lane_n == 2 * h + 1, n2, kn))
        vo_ref[h, :, 0:DA_V] = v_ref[:, hs].astype(BF16)
        vo_ref[h, :, DA_V:] = ones
    kn_ref[...] = kn


def _da_prep(proj, cos_all, sin_all):
    def tok(colblk):
        return pl.BlockSpec((TM, W512), lambda i: (i, colblk))

    tab = pl.BlockSpec((TM, LANES), lambda i: (i, 0))
    out = pl.BlockSpec((DA_HEADS, TM, DA_V), lambda i: (0, i, 0))
    shp = jax.ShapeDtypeStruct((DA_HEADS, N_TOK, DA_V), BF16)
    return pl.pallas_call(
        _da_prep_kernel,
        grid=(N_TOK // TM,),
        in_specs=[tok(COL_Q_DA), tok(COL_K_DA), tok(COL_V_DA), tab, tab],
        out_specs=[out, out, pl.BlockSpec((DA_HEADS, TM, 2 * DA_V), lambda i: (0, i, 0)),
                   pl.BlockSpec((TM, 2 * DA_HEADS), lambda i: (i, 0))],
        out_shape=[shp, shp, jax.ShapeDtypeStruct((DA_HEADS, N_TOK, 2 * DA_V), BF16),
                   jax.ShapeDtypeStruct((N_TOK, 2 * DA_HEADS), F32)],
        compiler_params=pltpu.CompilerParams(dimension_semantics=("arbitrary",),
                                             vmem_limit_bytes=VMEM_LIMIT),
        name="da_prep",
    )(proj, proj, proj, cos_all, sin_all)


def _attn_kernel(*refs, has_ctx):
    if has_ctx:
        scal_ref, kmax_ref, q_ref, k_ref, v_ref, kc_ref, vc_ref, nw_ref, o_ref = refs
    else:
        scal_ref, kmax_ref, q_ref, k_ref, v_ref, nw_ref, o_ref = refs
    lam = scal_ref[0]
    out_scale = scal_ref[1]
    lane = lax.broadcasted_iota(jnp.int32, (1, LANES), 1)
    n_heads = q_ref.shape[0]
    n_keys = k_ref.shape[1]
    chunk = min(ATT_KC, n_keys)
    b = pl.program_id(0)
    if has_ctx:
        kc = kc_ref[...].astype(BF16)
        vc = vc_ref[...].astype(BF16)
        vc_ext = jnp.concatenate([vc, jnp.ones((PAST_LEN, DA_V), BF16)], axis=1)

    def shifted(q_c, h, shift):
        acc = None
        for c in range(n_keys // chunk):
            ks = slice(c * chunk, (c + 1) * chunk)
            p = jnp.exp2(_dot_nt(q_c, k_ref[h, ks, :]) - shift).astype(BF16)
            part = _dot(p, v_ref[h, ks, :])
            acc = part if acc is None else acc + part
        if has_ctx:
            acc = acc + _dot(jnp.exp2(_dot_nt(q_c, kc) - shift).astype(BF16), vc_ext)
        l = acc[:, DA_V:DA_V + 1]
        return acc[:, :DA_V] / l, l

    def exact(q_c, h):
        s = _dot_nt(q_c, k_ref[h])
        m = jnp.max(s, axis=-1, keepdims=True)
        if has_ctx:
            s_c = _dot_nt(q_c, kc)
            m = jnp.maximum(m, jnp.max(s_c, axis=-1, keepdims=True))
        p = jnp.exp2(s - m)
        l = jnp.sum(p, axis=-1, keepdims=True)
        acc = _dot(p.astype(BF16), v_ref[h, :, 0:DA_V])
        if has_ctx:
            p_c = jnp.exp2(s_c - m)
            l = l + jnp.sum(p_c, axis=-1, keepdims=True)
            acc = acc + _dot(p_c.astype(BF16), vc)
        return acc / l

    def finish(o1, o2):
        return (_rms(o1 - lam * o2, nw_ref[...]) * out_scale).astype(o_ref.dtype)

    for h in range(n_heads):
        head = h if n_heads > 1 else pl.program_id(1)
        cols = slice(None) if n_heads == 1 else slice(h * DA_V, (h + 1) * DA_V)
        qh = q_ref[h]
        zero = jnp.zeros_like(qh)
        qs = (jnp.where(lane < DA_QK, qh, zero), jnp.where(lane >= DA_QK, qh, zero))
        outs, sums = [], []
        for comp, q_c in enumerate(qs):
            qf = q_c.astype(F32)
            qn = jnp.sqrt(jnp.sum(qf * qf, axis=-1, keepdims=True))
            shift = qn * (kmax_ref[(b * DA_HEADS + head) * 2 + comp] * 1.01) + 1e-3
            o_c, l_c = shifted(q_c, h, shift)
            outs.append(o_c)
            sums.append(l_c)
        o_ref[:, cols] = finish(outs[0], outs[1])
        ok = jnp.min(jnp.minimum(sums[0], sums[1])) >= ATT_MIN_SUM

        @pl.when(jnp.logical_not(ok))
        def _():
            o_ref[:, cols] = finish(exact(qs[0], h), exact(qs[1], h))


def _attention(scal, q, k, v, knorm, cache_k, cache_v, da_norm_l, l):
    smem = pl.BlockSpec(memory_space=pltpu.SMEM)
    nw = pl.BlockSpec((1, DA_V), lambda *a: (0, 0))
    kmax_ctx = jnp.sqrt(jnp.max(knorm[:N_CTX_TOK].reshape(BATCH, SEQ, 2 * DA_HEADS), axis=1)).reshape(-1)
    ckn = jnp.sum(jnp.square(cache_k[:, l].reshape(DEC_BATCH, PAST_LEN, 2 * DA_HEADS, DA_QK)), axis=-1)
    kmax_lat = jnp.sqrt(jnp.maximum(jnp.max(knorm[N_CTX_TOK:].reshape(DEC_BATCH, DEC_SEQ, 2 * DA_HEADS), axis=1),
                                    jnp.max(ckn, axis=1))).reshape(-1)
    blk = pl.BlockSpec((DA_HEADS, SEQ, DA_V), lambda b: (0, b, 0))
    o_ctx = pl.pallas_call(
        functools.partial(_attn_kernel, has_ctx=False),
        grid=(BATCH,),
        in_specs=[smem, smem, blk, blk, pl.BlockSpec((DA_HEADS, SEQ, 2 * DA_V), lambda b: (0, b, 0)), nw],
        out_specs=pl.BlockSpec((SEQ, W512), lambda b: (b, 0)),
        out_shape=jax.ShapeDtypeStruct((N_CTX_TOK, W512), BF16),
        compiler_params=pltpu.CompilerParams(dimension_semantics=("arbitrary",), vmem_limit_bytes=VMEM_LIMIT),
        name="attn_ctx",
    )(scal, kmax_ctx, q, k, v, da_norm_l)
    nq = DEC_SEQ // ATT_TQ
    off_q = N_CTX_TOK // ATT_TQ
    off_k = N_CTX_TOK // DEC_SEQ
    ck = cache_k.reshape(DEC_BATCH, DEPTH, PAST_LEN, W512)
    cv = cache_v.reshape(DEC_BATCH, DEPTH, PAST_LEN, W512)
    ckv = pl.BlockSpec((None, None, PAST_LEN, DA_V), lambda b, h, i: (b, l, 0, h))
    o_lat = pl.pallas_call(
        functools.partial(_attn_kernel, has_ctx=True),
        grid=(DEC_BATCH, DA_HEADS, nq),
        in_specs=[smem, smem,
                  pl.BlockSpec((1, ATT_TQ, DA_V), lambda b, h, i: (h, off_q + b * nq + i, 0)),
                  pl.BlockSpec((1, DEC_SEQ, DA_V), lambda b, h, i: (h, off_k + b, 0)),
                  pl.BlockSpec((1, DEC_SEQ, 2 * DA_V), lambda b, h, i: (h, off_k + b, 0)),
                  ckv, ckv, nw],
        out_specs=pl.BlockSpec((ATT_TQ, DA_V), lambda b, h, i: (b * nq + i, h)),
        out_shape=jax.ShapeDtypeStruct((DEC_BATCH * DEC_SEQ, W512), BF16),
        compiler_params=pltpu.CompilerParams(dimension_semantics=("arbitrary", "arbitrary", "arbitrary"),
                                             vmem_limit_bytes=VMEM_LIMIT),
        name="attn_lat",
    )(scal, kmax_lat, q, k, v, ck, cv, da_norm_l)
    return o_ctx, o_lat


def _outproj_kernel(hg_ref, dac_ref, dal_ref, x_ref, mod_ref, w_ref, nw_ref, wr_ref, br_ref,
                    xo_ref, h_ref, lg_ref, wbf_ref):
    i = pl.program_id(0)

    @pl.when(i == 0)
    def _():
        wbf_ref[...] = w_ref[...].astype(BF16)

    is_ctx = i < N_CTX_TOK // TM
    da = jnp.where(is_ctx, dac_ref[...], dal_ref[...])
    mix = _dot(hg_ref[...], wbf_ref[0:HG_WIDTH, :]) + _dot(da, wbf_ref[HG_WIDTH:, :])
    g1 = mod_ref[:, 2 * D_MODEL:3 * D_MODEL]
    sh2 = mod_ref[:, 3 * D_MODEL:4 * D_MODEL]
    sc2 = mod_ref[:, 4 * D_MODEL:5 * D_MODEL]
    x = x_ref[...] + g1 * mix
    xo_ref[...] = x
    h = _rms(x, nw_ref[...]) * (1.0 + sc2) + sh2
    h_ref[...] = _pack_rows(h)
    lg_ref[...] = _dot_precise(h, wr_ref[...]) + br_ref[...]


def _outproj(o_hg, o_da_ctx, o_da_lat, x, mods_l, w_out, norm2_l, w_router_l, b_router_l, l):
    n_ctx_t = N_CTX_TOK // TM
    n_lat_t = (N_TOK - N_CTX_TOK) // TM
    tokspec = lambda w: pl.BlockSpec((TM, w), lambda i: (i, 0))
    return pl.pallas_call(
        _outproj_kernel,
        grid=(N_TOK // TM,),
        in_specs=[
            tokspec(W512),
            pl.BlockSpec((TM, W512), lambda i: (jnp.minimum(i, n_ctx_t - 1), 0)),
            pl.BlockSpec((TM, W512), lambda i: (jnp.clip(i - n_ctx_t, 0, n_lat_t - 1), 0)),
            tokspec(D_MODEL),
            pl.BlockSpec((None, 1, 6 * D_MODEL), lambda i: (i * TM // COND_GROUP, 0, 0)),
            pl.BlockSpec((None, D_MODEL, D_MODEL), lambda i: (l, 0, 0)),
            pl.BlockSpec((1, D_MODEL), lambda i: (0, 0)),
            pl.BlockSpec((D_MODEL, N_EXPERTS), lambda i: (0, 0)),
            pl.BlockSpec((1, N_EXPERTS), lambda i: (0, 0)),
        ],
        out_specs=[tokspec(D_MODEL), tokspec(D_MODEL // 2), tokspec(N_EXPERTS)],
        out_shape=[jax.ShapeDtypeStruct((N_TOK, D_MODEL), F32),
                   jax.ShapeDtypeStruct((N_TOK, D_MODEL // 2), jnp.uint32),
                   jax.ShapeDtypeStruct((N_TOK, N_EXPERTS), F32)],
        scratch_shapes=[pltpu.VMEM((D_MODEL, D_MODEL), BF16)],
        compiler_params=pltpu.CompilerParams(dimension_semantics=("arbitrary",),
                                             vmem_limit_bytes=VMEM_LIMIT),
        name="outproj_norm_router",
    )(o_hg, o_da_ctx, o_da_lat, x, mods_l, w_out, norm2_l, w_router_l, b_router_l)


def _moe_kernel(be_ref, nu_ref, x_ref, wgu_ref, bgu_ref, wdn_ref, bdn_ref, o_ref, wgu_bf, wdn_bf):
    i = pl.program_id(0)
    used = i < nu_ref[0]
    prev = be_ref[jnp.maximum(i - 1, 0)]
    fresh = jnp.logical_or(i == 0, be_ref[i] != prev)

    @pl.when(jnp.logical_and(used, fresh))
    def _():
        wgu_bf[...] = wgu_ref[...].astype(BF16)
        wdn_bf[...] = wdn_ref[...].astype(BF16)

    @pl.when(used)
    def _():
        x_lo, x_hi = _unpack_rows(x_ref[...])
        x = jnp.concatenate([x_lo.astype(BF16), x_hi.astype(BF16)], axis=1)
        gu = _dot(x, wgu_bf[...]) + bgu_ref[...]
        gate = jnp.minimum(gu[:, :D_FF], SWIGLU_LIMIT)
        lin = jnp.clip(gu[:, D_FF:], -SWIGLU_LIMIT, SWIGLU_LIMIT)
        act = (lin + 1.0) * gate * _sigmoid(SWIGLU_ALPHA * gate)
        y = _dot(act.astype(BF16), wdn_bf[...]) + bdn_ref[...]
        o_ref[...] = _pack_rows(y)

    @pl.when(jnp.logical_not(used))
    def _():
        o_ref[...] = jnp.zeros_like(o_ref)


def _moe_experts(blk_exp, n_used, xs, w_gu, b_gu, w_dn, b_dn, l):
    def rowblk(i, be, nu):
        return (jnp.minimum(i, nu[0] - 1), 0)

    grid_spec = pltpu.PrefetchScalarGridSpec(
        num_scalar_prefetch=2,
        grid=(MOE_NBLK,),
        in_specs=[
            pl.BlockSpec((MOE_BLK, D_MODEL // 2), rowblk),
            pl.BlockSpec((None, None, D_MODEL, 2 * D_FF), lambda i, be, nu: (l, be[i], 0, 0)),
            pl.BlockSpec((None, None, 1, 2 * D_FF), lambda i, be, nu: (l, be[i], 0, 0)),
            pl.BlockSpec((None, None, D_FF, D_MODEL), lambda i, be, nu: (l, be[i], 0, 0)),
            pl.BlockSpec((None, None, 1, D_MODEL), lambda i, be, nu: (l, be[i], 0, 0)),
        ],
        out_specs=pl.BlockSpec((MOE_BLK, D_MODEL // 2), lambda i, be, nu: (i, 0)),
        scratch_shapes=[pltpu.VMEM((D_MODEL, 2 * D_FF), BF16), pltpu.VMEM((D_FF, D_MODEL), BF16)],
    )
    return pl.pallas_call(
        _moe_kernel,
        grid_spec=grid_spec,
        out_shape=jax.ShapeDtypeStruct((MOE_NBLK * MOE_BLK, D_MODEL // 2), jnp.uint32),
        compiler_params=pltpu.CompilerParams(dimension_semantics=("arbitrary",),
                                             vmem_limit_bytes=VMEM_LIMIT),
        name="moe_experts",
    )(blk_exp, n_used, xs, w_gu, b_gu.reshape(DEPTH, N_EXPERTS, 1, 2 * D_FF),
      w_dn, b_dn.reshape(DEPTH, N_EXPERTS, 1, D_MODEL))


def _route_kernel(lg_ref, exp_ref, gate_ref, rank_ref, cnt_ref, run_ref):
    i = pl.program_id(0)

    @pl.when(i == 0)
    def _():
        run_ref[...] = jnp.zeros_like(run_ref)

    lg = lg_ref[...]
    lane = lax.broadcasted_iota(jnp.int32, (1, N_EXPERTS), 1)
    lane_k = lax.broadcasted_iota(jnp.int32, (1, TOP_K), 1)
    vals, idxs, hots = [], [], []
    for _ in range(TOP_K):
        m = jnp.max(lg, axis=-1, keepdims=True)
        idx = jnp.min(jnp.where(lg == m, lane, N_EXPERTS), axis=-1, keepdims=True)
        hot = lane == idx
        vals.append(m)
        idxs.append(idx)
        hots.append(hot)
        lg = jnp.where(hot, -jnp.inf, lg)
    exps = [jnp.exp(v - vals[0]) for v in vals]
    denom = exps[0] + exps[1] + exps[2] + exps[3]
    tokhot = (hots[0] | hots[1] | hots[2] | hots[3]).astype(BF16)
    r = lax.broadcasted_iota(jnp.int32, (TM, TM), 0)
    cidx = lax.broadcasted_iota(jnp.int32, (TM, TM), 1)
    before = _dot((cidx < r).astype(BF16), tokhot) + run_ref[...]
    e_out = jnp.zeros((TM, TOP_K), jnp.int32)
    g_out = jnp.zeros((TM, TOP_K), F32)
    r_out = jnp.zeros((TM, TOP_K), jnp.int32)
    for k in range(TOP_K):
        rk = jnp.sum(jnp.where(hots[k], before, 0.0), axis=-1, keepdims=True).astype(jnp.int32)
        e_out = jnp.where(lane_k == k, idxs[k], e_out)
        g_out = jnp.where(lane_k == k, exps[k] / denom, g_out)
        r_out = jnp.where(lane_k == k, rk, r_out)
    exp_ref[...] = e_out
    gate_ref[...] = g_out
    rank_ref[...] = r_out
    run_ref[...] += jnp.sum(tokhot.astype(F32), axis=0, keepdims=True)
    cnt_ref[...] = run_ref[...].astype(jnp.int32)


def _route(logits):
    tk = pl.BlockSpec((TM, TOP_K), lambda i: (i, 0))
    top_exp, gates, rank, counts = pl.pallas_call(
        _route_kernel,
        grid=(N_TOK // TM,),
        in_specs=[pl.BlockSpec((TM, N_EXPERTS), lambda i: (i, 0))],
        out_specs=[tk, tk, tk, pl.BlockSpec((1, N_EXPERTS), lambda i: (0, 0))],
        out_shape=[jax.ShapeDtypeStruct((N_TOK, TOP_K), jnp.int32),
                   jax.ShapeDtypeStruct((N_TOK, TOP_K), F32),
                   jax.ShapeDtypeStruct((N_TOK, TOP_K), jnp.int32),
                   jax.ShapeDtypeStruct((1, N_EXPERTS), jnp.int32)],
        scratch_shapes=[pltpu.VMEM((1, N_EXPERTS), F32)],
        compiler_params=pltpu.CompilerParams(dimension_semantics=("arbitrary",)),
        name="route",
    )(logits)
    counts = counts[0]
    padded = (counts + MOE_BLK - 1) // MOE_BLK * MOE_BLK
    pad_end = jnp.cumsum(padded)
    pad_start = pad_end - padded
    start = jnp.cumsum(counts) - counts
    eq = top_exp[:, :, None] == jnp.arange(N_EXPERTS, dtype=jnp.int32)[None, None, :]
    dest = (jnp.sum(jnp.where(eq, pad_start[None, None, :], 0), axis=-1) + rank).T.reshape(-1)
    blk_exp = jnp.minimum(
        jnp.sum((pad_end[None, :] <= (jnp.arange(MOE_NBLK, dtype=jnp.int32) * MOE_BLK)[:, None]).astype(jnp.int32),
                axis=-1), N_EXPERTS - 1).astype(jnp.int32)
    n_used = (pad_end[-1:] // MOE_BLK).astype(jnp.int32)
    tok_ids = jnp.arange(N_ASG, dtype=jnp.int32) // TOP_K
    _, tok_sorted = lax.sort((top_exp.reshape(-1), tok_ids), num_keys=1, is_stable=True)
    slot = jnp.arange(MOE_NBLK * MOE_BLK, dtype=jnp.int32)
    e_slot = jnp.repeat(blk_exp, MOE_BLK)
    src = jnp.clip(start[e_slot] + slot - pad_start[e_slot], 0, N_ASG - 1)
    slot_tok = tok_sorted[src]
    return slot_tok, gates, dest, blk_exp, n_used


SC_IDX = 128
SC_ROWS = 64


def _sc_gather(table, idx):
    n = idx.shape[0]
    w = table.shape[1]
    mesh = plsc.VectorSubcoreMesh(core_axis_name="c", subcore_axis_name="s")
    n_workers = mesh.num_cores * mesh.num_subcores
    per = n // n_workers
    assert per * n_workers == n and per % SC_IDX == 0

    @pl.kernel(out_type=jax.ShapeDtypeStruct((n, w), table.dtype), mesh=mesh,
               scratch_types=[pltpu.VMEM((SC_IDX,), jnp.int32),
                              pltpu.VMEM((SC_ROWS, w), table.dtype),
                              pltpu.VMEM((SC_ROWS, w), table.dtype),
                              pltpu.SemaphoreType.DMA, pltpu.SemaphoreType.DMA])
    def gather_kernel(x_hbm, i_hbm, o_hbm, idx_v, buf0, buf1, sem0, sem1):
        wid = lax.axis_index("c") * mesh.num_subcores + lax.axis_index("s")

        @pl.loop(0, per // SC_IDX)
        def _(t):
            base = wid * per + t * SC_IDX
            pltpu.sync_copy(i_hbm.at[pl.ds(base, SC_IDX)], idx_v)
            g0 = pltpu.async_copy(x_hbm.at[idx_v.at[pl.ds(0, SC_ROWS)]], buf0, sem0)
            g1 = pltpu.async_copy(x_hbm.at[idx_v.at[pl.ds(SC_ROWS, SC_ROWS)]], buf1, sem1)
            g0.wait()
            pltpu.sync_copy(buf0, o_hbm.at[pl.ds(base, SC_ROWS)])
            g1.wait()
            pltpu.sync_copy(buf1, o_hbm.at[pl.ds(base + SC_ROWS, SC_ROWS)])

    return gather_kernel(table, idx)


def _combine_kernel(y0_ref, y1_ref, y2_ref, y3_ref, gate_ref, x_ref, mod_ref, o_ref):
    g = gate_ref[...]
    half = D_MODEL // 2
    acc_lo = jnp.zeros((TM, half), F32)
    acc_hi = jnp.zeros((TM, half), F32)
    for k, y_ref in enumerate((y0_ref, y1_ref, y2_ref, y3_ref)):
        lo, hi = _unpack_rows(y_ref[...])
        acc_lo = acc_lo + g[:, k:k + 1] * lo
        acc_hi = acc_hi + g[:, k:k + 1] * hi
    g2 = mod_ref[:, 5 * D_MODEL:]
    o_ref[:, :half] = x_ref[:, :half] + g2[:, :half] * acc_lo
    o_ref[:, half:] = x_ref[:, half:] + g2[:, half:] * acc_hi


def _combine(ys_tok, gates, x_mid, mods_l):
    n_i = N_TOK // TM

    def ysblk(k):
        return pl.BlockSpec((TM, D_MODEL // 2), lambda i: (k * n_i + i, 0))

    return pl.pallas_call(
        _combine_kernel,
        grid=(n_i,),
        in_specs=[ysblk(0), ysblk(1), ysblk(2), ysblk(3),
                  pl.BlockSpec((TM, TOP_K), lambda i: (i, 0)),
                  pl.BlockSpec((TM, D_MODEL), lambda i: (i, 0)),
                  pl.BlockSpec((None, 1, 6 * D_MODEL), lambda i: (i * TM // COND_GROUP, 0, 0))],
        out_specs=pl.BlockSpec((TM, D_MODEL), lambda i: (i, 0)),
        out_shape=jax.ShapeDtypeStruct((N_TOK, D_MODEL), F32),
        compiler_params=pltpu.CompilerParams(dimension_semantics=("arbitrary",), vmem_limit_bytes=VMEM_LIMIT),
        name="moe_combine",
    )(ys_tok, ys_tok, ys_tok, ys_tok, gates, x_mid, mods_l)


def _final_norm_kernel(x_ref, w_ref, oc_ref, ol_ref):
    y = _rms(x_ref[...], w_ref[...])
    is_ctx = pl.program_id(0) < N_CTX_TILES

    @pl.when(is_ctx)
    def _():
        oc_ref[...] = y

    @pl.when(jnp.logical_not(is_ctx))
    def _():
        ol_ref[...] = y


def _final_norm(x, w):
    n_lat_tiles = N_TOK // TM - N_CTX_TILES
    return pl.pallas_call(
        _final_norm_kernel,
        grid=(N_TOK // TM,),
        in_specs=[pl.BlockSpec((TM, D_MODEL), lambda i: (i, 0)), pl.BlockSpec((1, D_MODEL), lambda i: (0, 0))],
        out_specs=[pl.BlockSpec((TM, D_MODEL), lambda i: (jnp.minimum(i, N_CTX_TILES - 1), 0)),
                   pl.BlockSpec((TM, D_MODEL), lambda i: (jnp.clip(i - N_CTX_TILES, 0, n_lat_tiles - 1), 0))],
        out_shape=[jax.ShapeDtypeStruct((N_CTX_TOK, D_MODEL), F32),
                   jax.ShapeDtypeStruct((N_TOK - N_CTX_TOK, D_MODEL), F32)],
        compiler_params=pltpu.CompilerParams(dimension_semantics=("arbitrary",)),
        name="final_norm",
    )(x, w)


def kernel(x_prompt, x_sample, c, cache_k, cache_v, state_hgrn, c_ctx, norm1_w, norm2_w, w_ada, b_ada, w_in, hg_lb, hg_norm_w, lam_q1, lam_k1, lam_q2, lam_k2, da_norm_w, w_out, w_router, b_router, w_gu, b_gu, w_dn, b_dn, norm_f_w):
    lb_all = jnp.cumsum(jax.nn.softmax(hg_lb.astype(F32), axis=0), axis=0)
    lb_all = lb_all - lb_all[0:1]
    lam_init = [0.8 - 0.6 * math.exp(-0.3 * l) for l in range(DEPTH)]
    lam = (jnp.exp(jnp.sum(lam_q1 * lam_k1, axis=-1)) - jnp.exp(jnp.sum(lam_q2 * lam_k2, axis=-1))
           + jnp.asarray(lam_init, F32))
    cos_all, sin_all = _rope_tables()

    conds = jnp.concatenate([c_ctx[None, :], c, jnp.zeros((COND_ROWS - 1 - DEC_BATCH, D_MODEL), F32)], axis=0)
    mods = _ada_mod(conds, w_ada, b_ada)
    mods = mods[:, :N_COND].reshape(DEPTH, N_COND, 1, 6 * D_MODEL)

    x = jnp.concatenate([x_prompt.reshape(N_CTX_TOK, D_MODEL), x_sample.reshape(-1, D_MODEL)], axis=0)
    new_k = jnp.zeros((BATCH, DEPTH, SEQ, W512), F32)
    new_v = jnp.zeros((BATCH, DEPTH, SEQ, W512), F32)
    new_state = jnp.zeros((BATCH, DEPTH, 2, HG_HEADS, HG_DK, HG_DV), F32)
    for l in range(DEPTH):
        proj, new_k, new_v = _inproj(x, norm1_w[l][None, :], mods[l], w_in, new_k, new_v, l)
        o_fw, new_state = _hgrn_dir(proj, None, lb_all[l], state_hgrn, None, new_state, l, rev=False)
        o_hg, new_state = _hgrn_dir(proj, o_fw, lb_all[l], state_hgrn, hg_norm_w[l][None, :], new_state, l,
                                    rev=True)
        q_r, k_r, v_b, knorm = _da_prep(proj, cos_all, sin_all)
        scal = jnp.stack([lam[l], jnp.asarray(1.0 - lam_init[l], F32)])
        o_da_ctx, o_da_lat = _attention(scal, q_r, k_r, v_b, knorm, cache_k, cache_v, da_norm_w[l][None, :], l)
        x_mid, h2, logits = _outproj(o_hg, o_da_ctx, o_da_lat, x, mods[l], w_out, norm2_w[l][None, :],
                                     w_router[l], b_router[l][None, :], l)
        slot_tok, gates, dest, blk_exp, n_used = _route(logits)
        xs = _sc_gather(h2, slot_tok)
        ys = _moe_experts(blk_exp, n_used, xs, w_gu, b_gu, w_dn, b_dn, l)
        ys_tok = ys.at[dest].get(mode="promise_in_bounds")
        x = _combine(ys_tok, gates, x_mid, mods[l])
    y_ctx, y_lat = _final_norm(x, norm_f_w[None, :])
    return (y_ctx.reshape(BATCH, SEQ, D_MODEL), y_lat.reshape(DEC_BATCH, DEC_SEQ, D_MODEL),
            new_k.reshape(BATCH, DEPTH, SEQ, DA_HEADS, 2 * DA_QK), new_v.reshape(BATCH, DEPTH, SEQ, DA_HEADS, DA_V),
            new_state)
```

```python
import functools
import math

import jax
import jax.numpy as jnp
from jax import lax
from jax.experimental import pallas as pl
from jax.experimental.pallas import tpu as pltpu

F32 = jnp.float32
BF16 = jnp.bfloat16

D_MODEL = 1024
BATCH = 16
SEQ = 256
DEPTH = 4
DEC_BATCH = 2
DEC_SEQ = 4096
PAST_LEN = 256
GRID_W = 64
HG_HEADS = 4
HG_DK = 128
HG_DV = 128
HG_WIDTH = HG_HEADS * HG_DK
DA_HEADS = 4
DA_QK = 64
DA_V = 2 * DA_QK
DA_WIDTH = DA_HEADS * DA_V
ROPE_BASE = 10000.0
IN_DIM = 5 * HG_WIDTH + 3 * DA_WIDTH
N_EXPERTS = 32
TOP_K = 4
D_FF = D_MODEL
SWIGLU_ALPHA = 1.702
SWIGLU_LIMIT = 7.0
EPS = 1e-6

N_CTX_TOK = BATCH * SEQ
N_TOK = N_CTX_TOK + DEC_BATCH * DEC_SEQ
COND_GROUP = DEC_SEQ
assert N_CTX_TOK == COND_GROUP
N_COND = N_TOK // COND_GROUP
COND_ROWS = 8

COL_Q_HG, COL_F_FW, COL_F_BW, COL_I_HG, COL_G_HG, COL_Q_DA, COL_K_DA, COL_V_DA = range(8)
W512 = 512
LANES = 128

HG_CHUNK = 64
HG_HALF = HG_CHUNK // 2
HG_ROWS = 256
HG_EXP_CLAMP = 80.0

TM = 512
TN_IN = 2048
ATT_TQ = 256
Q_SCALE = math.log2(math.e) / math.sqrt(DA_QK)
ATT_KC = 512
ATT_MIN_SUM = 2.0 ** -40
MOE_BLK = 512
N_ASG = N_TOK * TOP_K
MOE_NBLK = N_ASG // MOE_BLK + N_EXPERTS
VMEM_LIMIT = 56 * 1024 * 1024


def _dot(a, b):
    return jnp.dot(a, b, preferred_element_type=F32)


def _dot_nt(a, b):
    return lax.dot_general(a, b, (((1,), (1,)), ((), ())), preferred_element_type=F32)


def _dot_tn(a, b):
    return lax.dot_general(a, b, (((0,), (0,)), ((), ())), preferred_element_type=F32)


def _split2(x):
    hi = x.astype(BF16)
    lo = (x - hi.astype(F32)).astype(BF16)
    return hi, lo


def _split3(x):
    hi = x.astype(BF16)
    r = x - hi.astype(F32)
    mid = r.astype(BF16)
    lo = (r - mid.astype(F32)).astype(BF16)
    return hi, mid, lo


def _dot_precise(a, b):
    a_hi, a_lo = _split2(a)
    b_hi, b_lo = _split2(b)
    return _dot(a_hi, b_hi) + (_dot(a_hi, b_lo) + _dot(a_lo, b_hi))


def _pack_rows(x):
    n = x.shape[1] // 2
    lo = pltpu.bitcast(x[:, :n].astype(BF16).astype(F32), jnp.uint32)
    hi = pltpu.bitcast(x[:, n:].astype(BF16).astype(F32), jnp.uint32)
    return (lo >> 16) | hi


def _unpack_rows(u):
    lo = pltpu.bitcast(u << 16, F32)
    hi = pltpu.bitcast(u & jnp.uint32(0xFFFF0000), F32)
    return lo, hi


def _rms(x, w):
    ms = jnp.mean(x * x, axis=-1, keepdims=True)
    return x * lax.rsqrt(ms + EPS) * w


def _sigmoid(x):
    return 1.0 / (1.0 + jnp.exp(-x))


def _ada_kernel(cond_ref, w_ref, b_ref, o_ref):
    cnd = cond_ref[...]
    s = cnd * _sigmoid(cnd)
    o_ref[...] = _dot_precise(s, w_ref[...]) + b_ref[...]


def _ada_mod(conds, w_ada, b_ada):
    nj = 6 * D_MODEL // D_MODEL
    return pl.pallas_call(
        _ada_kernel,
        grid=(DEPTH, nj),
        in_specs=[
            pl.BlockSpec((COND_ROWS, D_MODEL), lambda l, j: (0, 0)),
            pl.BlockSpec((None, D_MODEL, D_MODEL), lambda l, j: (l, 0, j)),
            pl.BlockSpec((None, 1, D_MODEL), lambda l, j: (l, 0, j)),
        ],
        out_specs=pl.BlockSpec((None, COND_ROWS, D_MODEL), lambda l, j: (l, 0, j)),
        out_shape=jax.ShapeDtypeStruct((DEPTH, COND_ROWS, 6 * D_MODEL), F32),
        compiler_params=pltpu.CompilerParams(dimension_semantics=("arbitrary", "arbitrary"),
                                             vmem_limit_bytes=VMEM_LIMIT),
        name="ada_mod",
    )(conds, w_ada, b_ada.reshape(DEPTH, 1, 6 * D_MODEL))


SEQ_PER_TILE = TM // SEQ
N_CTX_TILES = N_CTX_TOK // TM
KV_COL_BLK = (COL_K_DA * W512) // TN_IN
assert COL_K_DA * W512 == KV_COL_BLK * TN_IN + TN_IN - 2 * W512


def _inproj_kernel(x_ref, nw_ref, mod_ref, w_ref, kin_ref, vin_ref, o_ref, kc_ref, vc_ref, wbf_ref):
    del kin_ref, vin_ref
    n = pl.program_id(0)
    i = pl.program_id(1)

    @pl.when(i == 0)
    def _():
        wbf_ref[...] = w_ref[...].astype(BF16)

    x = x_ref[...]
    sh = mod_ref[:, 0:D_MODEL]
    sc = mod_ref[:, D_MODEL:2 * D_MODEL]
    h = _rms(x, nw_ref[...]) * (1.0 + sc) + sh
    o = _dot(h.astype(BF16), wbf_ref[...])
    o_ref[...] = o

    @pl.when(jnp.logical_and(n == KV_COL_BLK, i < N_CTX_TILES))
    def _():
        for s in range(SEQ_PER_TILE):
            kc_ref[s] = o[s * SEQ:(s + 1) * SEQ, TN_IN - 2 * W512:TN_IN - W512]
            vc_ref[s] = o[s * SEQ:(s + 1) * SEQ, TN_IN - W512:]


def _inproj(x, norm_w_l, mods_l, w_in, new_k, new_v, l):
    n_i = N_TOK // TM

    def cache_idx(n, i):
        return (jnp.where(n == KV_COL_BLK, jnp.minimum(i, N_CTX_TILES - 1), 0), l, 0, 0)

    cache_spec = pl.BlockSpec((SEQ_PER_TILE, None, SEQ, W512), cache_idx)
    cache_shape = jax.ShapeDtypeStruct((BATCH, DEPTH, SEQ, W512), F32)
    return pl.pallas_call(
        _inproj_kernel,
        grid=(IN_DIM // TN_IN, n_i),
        in_specs=[
            pl.BlockSpec((TM, D_MODEL), lambda n, i: (i, 0)),
            pl.BlockSpec((1, D_MODEL), lambda n, i: (0, 0)),
            pl.BlockSpec((None, 1, 6 * D_MODEL), lambda n, i: (i * TM // COND_GROUP, 0, 0)),
            pl.BlockSpec((None, D_MODEL, TN_IN), lambda n, i: (l, 0, n)),
            pl.BlockSpec(memory_space=pl.ANY),
            pl.BlockSpec(memory_space=pl.ANY),
        ],
        out_specs=[pl.BlockSpec((TM, TN_IN), lambda n, i: (i, n)), cache_spec, cache_spec],
        out_shape=[jax.ShapeDtypeStruct((N_TOK, IN_DIM), F32), cache_shape, cache_shape],
        input_output_aliases={4: 1, 5: 2},
        scratch_shapes=[pltpu.VMEM((D_MODEL, TN_IN), BF16)],
        compiler_params=pltpu.CompilerParams(dimension_semantics=("arbitrary", "arbitrary"),
                                             vmem_limit_bytes=VMEM_LIMIT),
        name="norm_inproj",
    )(x, norm_w_l, mods_l, w_in, new_k, new_v)


N_HG_BLK = N_TOK // HG_ROWS
N_HG_CTX_BLK = N_CTX_TOK // HG_ROWS
HG_BPS_CTX = SEQ // HG_ROWS
HG_BPS_LAT = DEC_SEQ // HG_ROWS


def _hg_block(step, rev):
    return (N_HG_BLK - 1 - step) if rev else step


def _hg_lat_seq(p):
    return jnp.clip((p - N_HG_CTX_BLK) // HG_BPS_LAT, 0, DEC_BATCH - 1)


def _hg_ctx_seq(p):
    return jnp.minimum(p // HG_BPS_CTX, BATCH - 1)


def _hgrn_kernel(*refs, rev):
    if rev:
        (q_ref, f_ref, v_ref, g_ref, ofw_ref, lb_ref, s0_ref, nw_ref, _, o_ref, sfin_ref, st_ref) = refs
    else:
        (q_ref, f_ref, v_ref, lb_ref, s0_ref, _, o_ref, sfin_ref, st_ref) = refs
    p = _hg_block(pl.program_id(0), rev)
    is_ctx = p < N_HG_CTX_BLK
    pos = jnp.where(is_ctx, p % HG_BPS_CTX, (p - N_HG_CTX_BLK) % HG_BPS_LAT)
    bps = jnp.where(is_ctx, HG_BPS_CTX, HG_BPS_LAT)
    first_pos = (bps - 1) if rev else 0
    last_pos = 0 if rev else (bps - 1)

    @pl.when(jnp.logical_and(pos == first_pos, is_ctx))
    def _():
        st_ref[...] = jnp.zeros_like(st_ref)

    @pl.when(jnp.logical_and(pos == first_pos, jnp.logical_not(is_ctx)))
    def _():
        for h in range(HG_HEADS):
            st_ref[h] = s0_ref[h].T

    lb = lb_ref[...]
    oml = 1.0 - lb
    log_oml = jnp.log1p(-lb)
    row = lax.broadcasted_iota(jnp.int32, (HG_CHUNK, HG_CHUNK), 0)
    col = lax.broadcasted_iota(jnp.int32, (HG_CHUNK, HG_CHUNK), 1)
    tri = ((col >= row) if rev else (col <= row)).astype(BF16)
    same_half = (row >= HG_HALF) == (col >= HG_HALF)
    if rev:
        m_diag = jnp.logical_and(same_half, col >= row)
        m_off = jnp.logical_and(row < HG_HALF, col >= HG_HALF)
    else:
        m_diag = jnp.logical_and(same_half, col <= row)
        m_off = jnp.logical_and(row >= HG_HALF, col < HG_HALF)
    row1 = lax.broadcasted_iota(jnp.int32, (HG_CHUNK, 1), 0)
    n_chunks = HG_ROWS // HG_CHUNK

    def chunk(ci):
        c = (n_chunks - 1 - ci) if rev else ci
        rows = slice(c * HG_CHUNK, (c + 1) * HG_CHUNK)
        fx = f_ref[rows, :]
        q = q_ref[rows, :]
        v = v_ref[rows, :]
        e = jnp.exp(-jnp.abs(fx))
        r = 1.0 / (1.0 + e)
        er = e * r
        nonneg = fx >= 0.0
        sig = jnp.where(nonneg, r, er)
        nsig = jnp.where(nonneg, er, r)
        f = lb + oml * sig
        logf = jnp.maximum(jnp.log(f), log_oml + (jnp.minimum(fx, 0.0) - jnp.log1p(e)))
        k = oml * nsig
        hi, mid, lo = _split3(logf)
        a = _dot(tri, hi) + (_dot(tri, mid) + _dot(tri, lo))
        a_mid = jnp.where(row1 < HG_HALF, a[HG_HALF // 2:HG_HALF // 2 + 1, :],
                          a[HG_HALF + HG_HALF // 2:HG_HALF + HG_HALF // 2 + 1, :])
        if rev:
            a_b = a[HG_HALF:HG_HALF + 1, :]
            a_last = a[0:1, :]
        else:
            a_b = a[HG_HALF - 1:HG_HALF, :]
            a_last = a[HG_CHUNK - 1:HG_CHUNK, :]
        qd = (q * jnp.exp(jnp.minimum(a - a_mid, HG_EXP_CLAMP))).astype(BF16)
        kd = (k * jnp.exp(jnp.minimum(a_mid - a, HG_EXP_CLAMP))).astype(BF16)
        qo = (q * jnp.exp(jnp.minimum(a - a_b, 0.0))).astype(BF16)
        ko = (k * jnp.exp(jnp.minimum(a_b - a, 0.0))).astype(BF16)
        qa = (q * jnp.exp(a)).astype(BF16)
        kdec = (k * jnp.exp(a_last - a)).astype(BF16)
        dec = jnp.exp(a_last)
        vb = v.astype(BF16)
        for h in range(HG_HEADS):
            hs = slice(h * HG_DK, (h + 1) * HG_DK)
            p1 = _dot_nt(qd[:, hs], kd[:, hs])
            p2 = _dot_nt(qo[:, hs], ko[:, hs])
            attn = jnp.where(m_diag, p1, jnp.where(m_off, p2, 0.0)).astype(BF16)
            st = st_ref[h]
            o_h = _dot_nt(qa[:, hs], st.astype(BF16)) + _dot(attn, vb[:, hs])
            st_ref[h] = dec[:, hs] * st + _dot_tn(vb[:, hs], kdec[:, hs])
            if rev:
                tot = ofw_ref[rows, hs] + o_h
                g = g_ref[rows, hs]
                y = _rms(tot, nw_ref[...]) * (g * _sigmoid(g))
                o_ref[rows, hs] = y.astype(o_ref.dtype)
            else:
                o_ref[rows, hs] = o_h

    for ci in range(n_chunks):
        chunk(ci)

    @pl.when(jnp.logical_and(pos == last_pos, is_ctx))
    def _():
        for h in range(HG_HEADS):
            sfin_ref[h] = st_ref[h].T


def _hgrn_dir(proj, o_fw, lb_l, state_hgrn, hg_norm_l, new_state, l, rev):
    d = 1 if rev else 0

    def tok(colblk):
        return pl.BlockSpec((HG_ROWS, W512), lambda s: (_hg_block(s, rev), colblk))

    in_specs = [tok(COL_Q_HG), tok(COL_F_BW if rev else COL_F_FW), tok(COL_I_HG)]
    args = [proj, proj, proj]
    if rev:
        in_specs += [tok(COL_G_HG), pl.BlockSpec((HG_ROWS, W512), lambda s: (_hg_block(s, rev), 0))]
        args += [proj, o_fw]
    in_specs += [
        pl.BlockSpec((None, 1, W512), lambda s: (d, 0, 0)),
        pl.BlockSpec((None, None, None, HG_HEADS, HG_DK, HG_DV),
                     lambda s: (_hg_lat_seq(_hg_block(s, rev)), l, d, 0, 0, 0)),
    ]
    args += [lb_l.reshape(2, 1, W512), state_hgrn]
    if rev:
        in_specs.append(pl.BlockSpec((1, HG_DV), lambda s: (0, 0)))
        args.append(hg_norm_l)
    in_specs.append(pl.BlockSpec(memory_space=pl.ANY))
    args.append(new_state)
    return pl.pallas_call(
        functools.partial(_hgrn_kernel, rev=rev),
        grid=(N_HG_BLK,),
        in_specs=in_specs,
        out_specs=[
            pl.BlockSpec((HG_ROWS, W512), lambda s: (_hg_block(s, rev), 0)),
            pl.BlockSpec((None, None, None, HG_HEADS, HG_DK, HG_DV),
                         lambda s: (_hg_ctx_seq(_hg_block(s, rev)), l, d, 0, 0, 0)),
        ],
        out_shape=[
            jax.ShapeDtypeStruct((N_TOK, W512), BF16 if rev else F32),
            jax.ShapeDtypeStruct((BATCH, DEPTH, 2, HG_HEADS, HG_DK, HG_DV), F32),
        ],
        input_output_aliases={len(args) - 1: 1},
        scratch_shapes=[pltpu.VMEM((HG_HEADS, HG_DV, HG_DK), F32)],
        compiler_params=pltpu.CompilerParams(dimension_semantics=("arbitrary",),
                                             vmem_limit_bytes=VMEM_LIMIT),
        name="hgrn_bwd" if rev else "hgrn_fwd",
    )(*args)


def _rope_tables():
    n_pairs = DA_QK // 4
    inv_freq = ROPE_BASE ** (-jnp.arange(n_pairs, dtype=F32) / n_pairs)
    t = jnp.arange(DEC_SEQ)
    row_idx = (t // GRID_W).astype(F32)
    col_idx = (t % GRID_W).astype(F32)
    lane = jnp.arange(LANES)
    jj = lane % DA_QK
    use_col = (jj // (DA_QK // 2)) == 1
    second = (jj % (DA_QK // 2)) >= n_pairs
    fr = inv_freq[jj % n_pairs]
    ang_r = row_idx[:, None] * inv_freq[None, :]
    ang_c = col_idx[:, None] * inv_freq[None, :]
    cos_r, sin_r, cos_c, sin_c = jnp.cos(ang_r), jnp.sin(ang_r), jnp.cos(ang_c), jnp.sin(ang_c)
    fi = jj % n_pairs
    cos_t = jnp.where(use_col[None, :], cos_c[:, fi], cos_r[:, fi])
    sin_t = jnp.where(use_col[None, :], sin_c[:, fi], sin_r[:, fi])
    sin_t = jnp.where(second[None, :], sin_t, -sin_t)
    del fr
    cos_all = jnp.concatenate([jnp.ones((N_CTX_TOK, LANES), F32)] + [cos_t] * DEC_BATCH, axis=0)
    sin_all = jnp.concatenate([jnp.zeros((N_CTX_TOK, LANES), F32)] + [sin_t] * DEC_BATCH, axis=0)
    return cos_all, sin_all


def _da_prep_kernel(q_ref, k_ref, v_ref, cos_ref, sin_ref, qo_ref, ko_ref, vo_ref, kn_ref):
    cos = cos_ref[...]
    sin = sin_ref[...]
    lane = lax.broadcasted_iota(jnp.int32, (1, LANES), 1)
    first = (lane % (DA_QK // 2)) < (DA_QK // 4)
    lane_n = lax.broadcasted_iota(jnp.int32, (1, 2 * DA_HEADS), 1)

    def rope(x):
        partner = jnp.where(first, pltpu.roll(x, LANES - DA_QK // 4, 1), pltpu.roll(x, DA_QK // 4, 1))
        return x * cos + partner * sin

    kn = jnp.zeros((TM, 2 * DA_HEADS), F32)
    ones = jnp.ones((TM, DA_V), BF16)
    for h in range(DA_HEADS):
        hs = slice(h * DA_V, (h + 1) * DA_V)
        qo_ref[h] = (rope(q_ref[:, hs]) * Q_SCALE).astype(BF16)
        kr = rope(k_ref[:, hs])
        ko_ref[h] = kr.astype(BF16)
        ksq = kr * kr
        n1 = jnp.sum(jnp.where(lane < DA_QK, ksq, 0.0), axis=-1, keepdims=True)
        n2 = jnp.sum(jnp.where(lane >= DA_QK, ksq, 0.0), axis=-1, keepdims=True)
        kn = jnp.where(lane_n == 2 * h, n1, jnp.where(lane_n == 2 * h + 1, n2, kn))
        vo_ref[h, :, 0:DA_V] = v_ref[:, hs].astype(BF16)
        vo_ref[h, :, DA_V:] = ones
    kn_ref[...] = kn


def _da_prep(proj, cos_all, sin_all):
    def tok(colblk):
        return pl.BlockSpec((TM, W512), lambda i: (i, colblk))

    tab = pl.BlockSpec((TM, LANES), lambda i: (i, 0))
    out = pl.BlockSpec((DA_HEADS, TM, DA_V), lambda i: (0, i, 0))
    shp = jax.ShapeDtypeStruct((DA_HEADS, N_TOK, DA_V), BF16)
    return pl.pallas_call(
        _da_prep_kernel,
        grid=(N_TOK // TM,),
        in_specs=[tok(COL_Q_DA), tok(COL_K_DA), tok(COL_V_DA), tab, tab],
        out_specs=[out, out, pl.BlockSpec((DA_HEADS, TM, 2 * DA_V), lambda i: (0, i, 0)),
                   pl.BlockSpec((TM, 2 * DA_HEADS), lambda i: (i, 0))],
        out_shape=[shp, shp, jax.ShapeDtypeStruct((DA_HEADS, N_TOK, 2 * DA_V), BF16),
                   jax.ShapeDtypeStruct((N_TOK, 2 * DA_HEADS), F32)],
        compiler_params=pltpu.CompilerParams(dimension_semantics=("arbitrary",),
                                             vmem_limit_bytes=VMEM_LIMIT),
        name="da_prep",
    )(proj, proj, proj, cos_all, sin_all)


def _attn_kernel(*refs, has_ctx):
    if has_ctx:
        scal_ref, kmax_ref, q_ref, k_ref, v_ref, kc_ref, vc_ref, nw_ref, o_ref = refs
    else:
        scal_ref, kmax_ref, q_ref, k_ref, v_ref, nw_ref, o_ref = refs
    lam = scal_ref[0]
    out_scale = scal_ref[1]
    lane = lax.broadcasted_iota(jnp.int32, (1, LANES), 1)
    n_heads = q_ref.shape[0]
    n_keys = k_ref.shape[1]
    chunk = min(ATT_KC, n_keys)
    b = pl.program_id(0)
    if has_ctx:
        kc = kc_ref[...].astype(BF16)
        vc = vc_ref[...].astype(BF16)
        vc_ext = jnp.concatenate([vc, jnp.ones((PAST_LEN, DA_V), BF16)], axis=1)

    def shifted(q_c, h, shift):
        acc = None
        for c in range(n_keys // chunk):
            ks = slice(c * chunk, (c + 1) * chunk)
            p = jnp.exp2(_dot_nt(q_c, k_ref[h, ks, :]) - shift).astype(BF16)
            part = _dot(p, v_ref[h, ks, :])
            acc = part if acc is None else acc + part
        if has_ctx:
            acc = acc + _dot(jnp.exp2(_dot_nt(q_c, kc) - shift).astype(BF16), vc_ext)
        l = acc[:, DA_V:DA_V + 1]
        return acc[:, :DA_V] / l, l

    def exact(q_c, h):
        s = _dot_nt(q_c, k_ref[h])
        m = jnp.max(s, axis=-1, keepdims=True)
        if has_ctx:
            s_c = _dot_nt(q_c, kc)
            m = jnp.maximum(m, jnp.max(s_c, axis=-1, keepdims=True))
        p = jnp.exp2(s - m)
        l = jnp.sum(p, axis=-1, keepdims=True)
        acc = _dot(p.astype(BF16), v_ref[h, :, 0:DA_V])
        if has_ctx:
            p_c = jnp.exp2(s_c - m)
            l = l + jnp.sum(p_c, axis=-1, keepdims=True)
            acc = acc + _dot(p_c.astype(BF16), vc)
        return acc / l

    def finish(o1, o2):
        return (_rms(o1 - lam * o2, nw_ref[...]) * out_scale).astype(o_ref.dtype)

    for h in range(n_heads):
        head = h if n_heads > 1 else pl.program_id(1)
        cols = slice(None) if n_heads == 1 else slice(h * DA_V, (h + 1) * DA_V)
        qh = q_ref[h]
        zero = jnp.zeros_like(qh)
        qs = (jnp.where(lane < DA_QK, qh, zero), jnp.where(lane >= DA_QK, qh, zero))
        outs, sums = [], []
        for comp, q_c in enumerate(qs):
            qf = q_c.astype(F32)
            qn = jnp.sqrt(jnp.sum(qf * qf, axis=-1, keepdims=True))
            shift = qn * (kmax_ref[(b * DA_HEADS + head) * 2 + comp] * 1.01) + 1e-3
            o_c, l_c = shifted(q_c, h, shift)
            outs.append(o_c)
            sums.append(l_c)
        o_ref[:, cols] = finish(outs[0], outs[1])
        ok = jnp.min(jnp.minimum(sums[0], sums[1])) >= ATT_MIN_SUM

        @pl.when(jnp.logical_not(ok))
        def _():
            o_ref[:, cols] = finish(exact(qs[0], h), exact(qs[1], h))


def _attention(scal, q, k, v, knorm, cache_k, cache_v, da_norm_l, l):
    smem = pl.BlockSpec(memory_space=pltpu.SMEM)
    nw = pl.BlockSpec((1, DA_V), lambda *a: (0, 0))
    kmax_ctx = jnp.sqrt(jnp.max(knorm[:N_CTX_TOK].reshape(BATCH, SEQ, 2 * DA_HEADS), axis=1)).reshape(-1)
    ckn = jnp.sum(jnp.square(cache_k[:, l].reshape(DEC_BATCH, PAST_LEN, 2 * DA_HEADS, DA_QK)), axis=-1)
    kmax_lat = jnp.sqrt(jnp.maximum(jnp.max(knorm[N_CTX_TOK:].reshape(DEC_BATCH, DEC_SEQ, 2 * DA_HEADS), axis=1),
                                    jnp.max(ckn, axis=1))).reshape(-1)
    blk = pl.BlockSpec((DA_HEADS, SEQ, DA_V), lambda b: (0, b, 0))
    o_ctx = pl.pallas_call(
        functools.partial(_attn_kernel, has_ctx=False),
        grid=(BATCH,),
        in_specs=[smem, smem, blk, blk, pl.BlockSpec((DA_HEADS, SEQ, 2 * DA_V), lambda b: (0, b, 0)), nw],
        out_specs=pl.BlockSpec((SEQ, W512), lambda b: (b, 0)),
        out_shape=jax.ShapeDtypeStruct((N_CTX_TOK, W512), BF16),
        compiler_params=pltpu.CompilerParams(dimension_semantics=("arbitrary",), vmem_limit_bytes=VMEM_LIMIT),
        name="attn_ctx",
    )(scal, kmax_ctx, q, k, v, da_norm_l)
    nq = DEC_SEQ // ATT_TQ
    off_q = N_CTX_TOK // ATT_TQ
    off_k = N_CTX_TOK // DEC_SEQ
    ck = cache_k.reshape(DEC_BATCH, DEPTH, PAST_LEN, W512)
    cv = cache_v.reshape(DEC_BATCH, DEPTH, PAST_LEN, W512)
    ckv = pl.BlockSpec((None, None, PAST_LEN, DA_V), lambda b, h, i: (b, l, 0, h))
    o_lat = pl.pallas_call(
        functools.partial(_attn_kernel, has_ctx=True),
        grid=(DEC_BATCH, DA_HEADS, nq),
        in_specs=[smem, smem,
                  pl.BlockSpec((1, ATT_TQ, DA_V), lambda b, h, i: (h, off_q + b * nq + i, 0)),
                  pl.BlockSpec((1, DEC_SEQ, DA_V), lambda b, h, i: (h, off_k + b, 0)),
                  pl.BlockSpec((1, DEC_SEQ, 2 * DA_V), lambda b, h, i: (h, off_k + b, 0)),
                  ckv, ckv, nw],
        out_specs=pl.BlockSpec((ATT_TQ, DA_V), lambda b, h, i: (b * nq + i, h)),
        out_shape=jax.ShapeDtypeStruct((DEC_BATCH * DEC_SEQ, W512), BF16),
        compiler_params=pltpu.CompilerParams(dimension_semantics=("arbitrary", "arbitrary", "arbitrary"),
                                             vmem_limit_bytes=VMEM_LIMIT),
        name="attn_lat",
    )(scal, kmax_lat, q, k, v, ck, cv, da_norm_l)
    return o_ctx, o_lat


def _outproj_kernel(hg_ref, dac_ref, dal_ref, x_ref, mod_ref, w_ref, nw_ref, wr_ref, br_ref,
                    xo_ref, h_ref, lg_ref, wbf_ref):
    i = pl.program_id(0)

    @pl.when(i == 0)
    def _():
        wbf_ref[...] = w_ref[...].astype(BF16)

    is_ctx = i < N_CTX_TOK // TM
    da = jnp.where(is_ctx, dac_ref[...], dal_ref[...])
    mix = _dot(hg_ref[...], wbf_ref[0:HG_WIDTH, :]) + _dot(da, wbf_ref[HG_WIDTH:, :])
    g1 = mod_ref[:, 2 * D_MODEL:3 * D_MODEL]
    sh2 = mod_ref[:, 3 * D_MODEL:4 * D_MODEL]
    sc2 = mod_ref[:, 4 * D_MODEL:5 * D_MODEL]
    x = x_ref[...] + g1 * mix
    xo_ref[...] = x
    h = _rms(x, nw_ref[...]) * (1.0 + sc2) + sh2
    h_ref[...] = _pack_rows(h)
    lg_ref[...] = _dot_precise(h, wr_ref[...]) + br_ref[...]


def _outproj(o_hg, o_da_ctx, o_da_lat, x, mods_l, w_out, norm2_l, w_router_l, b_router_l, l):
    n_ctx_t = N_CTX_TOK // TM
    n_lat_t = (N_TOK - N_CTX_TOK) // TM
    tokspec = lambda w: pl.BlockSpec((TM, w), lambda i: (i, 0))
    return pl.pallas_call(
        _outproj_kernel,
        grid=(N_TOK // TM,),
        in_specs=[
            tokspec(W512),
            pl.BlockSpec((TM, W512), lambda i: (jnp.minimum(i, n_ctx_t - 1), 0)),
            pl.BlockSpec((TM, W512), lambda i: (jnp.clip(i - n_ctx_t, 0, n_lat_t - 1), 0)),
            tokspec(D_MODEL),
            pl.BlockSpec((None, 1, 6 * D_MODEL), lambda i: (i * TM // COND_GROUP, 0, 0)),
            pl.BlockSpec((None, D_MODEL, D_MODEL), lambda i: (l, 0, 0)),
            pl.BlockSpec((1, D_MODEL), lambda i: (0, 0)),
            pl.BlockSpec((D_MODEL, N_EXPERTS), lambda i: (0, 0)),
            pl.BlockSpec((1, N_EXPERTS), lambda i: (0, 0)),
        ],
        out_specs=[tokspec(D_MODEL), tokspec(D_MODEL // 2), tokspec(N_EXPERTS)],
        out_shape=[jax.ShapeDtypeStruct((N_TOK, D_MODEL), F32),
                   jax.ShapeDtypeStruct((N_TOK, D_MODEL // 2), jnp.uint32),
                   jax.ShapeDtypeStruct((N_TOK, N_EXPERTS), F32)],
        scratch_shapes=[pltpu.VMEM((D_MODEL, D_MODEL), BF16)],
        compiler_params=pltpu.CompilerParams(dimension_semantics=("arbitrary",),
                                             vmem_limit_bytes=VMEM_LIMIT),
        name="outproj_norm_router",
    )(o_hg, o_da_ctx, o_da_lat, x, mods_l, w_out, norm2_l, w_router_l, b_router_l)


def _moe_kernel(be_ref, nu_ref, tok_ref, h_hbm, wgu_ref, bgu_ref, wdn_ref, bdn_ref, o_ref,
                wgu_bf, wdn_bf, xbuf, sem):
    i = pl.program_id(0)
    n_used = nu_ref[0]
    used = i < n_used
    cur = i % 2
    prev = be_ref[jnp.maximum(i - 1, 0)]
    fresh = jnp.logical_or(i == 0, be_ref[i] != prev)

    def start_rows(blk, buf):
        base = blk * MOE_BLK
        for r in range(MOE_BLK):
            pltpu.make_async_copy(h_hbm.at[pl.ds(tok_ref[base + r], 1), :], xbuf.at[buf, pl.ds(r, 1), :],
                                  sem.at[buf]).start()

    def wait_rows(buf):
        pltpu.make_async_copy(h_hbm.at[pl.ds(0, MOE_BLK), :], xbuf.at[buf], sem.at[buf]).wait()

    @pl.when(i == 0)
    def _():
        start_rows(0, 0)

    @pl.when(jnp.logical_and(used, fresh))
    def _():
        wgu_bf[...] = wgu_ref[...].astype(BF16)
        wdn_bf[...] = wdn_ref[...].astype(BF16)

    @pl.when(used)
    def _():
        wait_rows(cur)
        start_rows(jnp.minimum(i + 1, n_used - 1), 1 - cur)
        x_lo, x_hi = _unpack_rows(xbuf[cur])
        x = jnp.concatenate([x_lo.astype(BF16), x_hi.astype(BF16)], axis=1)
        gu = _dot(x, wgu_bf[...]) + bgu_ref[...]
        gate = jnp.minimum(gu[:, :D_FF], SWIGLU_LIMIT)
        lin = jnp.clip(gu[:, D_FF:], -SWIGLU_LIMIT, SWIGLU_LIMIT)
        act = (lin + 1.0) * gate * _sigmoid(SWIGLU_ALPHA * gate)
        y = _dot(act.astype(BF16), wdn_bf[...]) + bdn_ref[...]
        o_ref[...] = _pack_rows(y)

    @pl.when(i == n_used - 1)
    def _():
        wait_rows(1 - cur)

    @pl.when(jnp.logical_not(used))
    def _():
        o_ref[...] = jnp.zeros_like(o_ref)


def _moe_experts(blk_exp, n_used, slot_tok, h2, w_gu, b_gu, w_dn, b_dn, l):
    grid_spec = pltpu.PrefetchScalarGridSpec(
        num_scalar_prefetch=3,
        grid=(MOE_NBLK,),
        in_specs=[
            pl.BlockSpec(memory_space=pl.ANY),
            pl.BlockSpec((None, None, D_MODEL, 2 * D_FF), lambda i, be, nu, tk: (l, be[i], 0, 0)),
            pl.BlockSpec((None, None, 1, 2 * D_FF), lambda i, be, nu, tk: (l, be[i], 0, 0)),
            pl.BlockSpec((None, None, D_FF, D_MODEL), lambda i, be, nu, tk: (l, be[i], 0, 0)),
            pl.BlockSpec((None, None, 1, D_MODEL), lambda i, be, nu, tk: (l, be[i], 0, 0)),
        ],
        out_specs=pl.BlockSpec((MOE_BLK, D_MODEL // 2), lambda i, be, nu, tk: (i, 0)),
        scratch_shapes=[pltpu.VMEM((D_MODEL, 2 * D_FF), BF16), pltpu.VMEM((D_FF, D_MODEL), BF16),
                        pltpu.VMEM((2, MOE_BLK, D_MODEL // 2), jnp.uint32), pltpu.SemaphoreType.DMA((2,))],
    )
    return pl.pallas_call(
        _moe_kernel,
        grid_spec=grid_spec,
        out_shape=jax.ShapeDtypeStruct((MOE_NBLK * MOE_BLK, D_MODEL // 2), jnp.uint32),
        compiler_params=pltpu.CompilerParams(dimension_semantics=("arbitrary",),
                                             vmem_limit_bytes=VMEM_LIMIT),
        name="moe_experts",
    )(blk_exp, n_used, slot_tok, h2, w_gu, b_gu.reshape(DEPTH, N_EXPERTS, 1, 2 * D_FF),
      w_dn, b_dn.reshape(DEPTH, N_EXPERTS, 1, D_MODEL))


def _route_kernel(lg_ref, exp_ref, gate_ref, rank_ref, cnt_ref, run_ref):
    i = pl.program_id(0)

    @pl.when(i == 0)
    def _():
        run_ref[...] = jnp.zeros_like(run_ref)

    lg = lg_ref[...]
    lane = lax.broadcasted_iota(jnp.int32, (1, N_EXPERTS), 1)
    lane_k = lax.broadcasted_iota(jnp.int32, (1, TOP_K), 1)
    vals, idxs, hots = [], [], []
    for _ in range(TOP_K):
        m = jnp.max(lg, axis=-1, keepdims=True)
        idx = jnp.min(jnp.where(lg == m, lane, N_EXPERTS), axis=-1, keepdims=True)
        hot = lane == idx
        vals.append(m)
        idxs.append(idx)
        hots.append(hot)
        lg = jnp.where(hot, -jnp.inf, lg)
    exps = [jnp.exp(v - vals[0]) for v in vals]
    denom = exps[0] + exps[1] + exps[2] + exps[3]
    tokhot = (hots[0] | hots[1] | hots[2] | hots[3]).astype(BF16)
    r = lax.broadcasted_iota(jnp.int32, (TM, TM), 0)
    cidx = lax.broadcasted_iota(jnp.int32, (TM, TM), 1)
    before = _dot((cidx < r).astype(BF16), tokhot) + run_ref[...]
    e_out = jnp.zeros((TM, TOP_K), jnp.int32)
    g_out = jnp.zeros((TM, TOP_K), F32)
    r_out = jnp.zeros((TM, TOP_K), jnp.int32)
    for k in range(TOP_K):
        rk = jnp.sum(jnp.where(hots[k], before, 0.0), axis=-1, keepdims=True).astype(jnp.int32)
        e_out = jnp.where(lane_k == k, idxs[k], e_out)
        g_out = jnp.where(lane_k == k, exps[k] / denom, g_out)
        r_out = jnp.where(lane_k == k, rk, r_out)
    exp_ref[...] = e_out
    gate_ref[...] = g_out
    rank_ref[...] = r_out
    run_ref[...] += jnp.sum(tokhot.astype(F32), axis=0, keepdims=True)
    cnt_ref[...] = run_ref[...].astype(jnp.int32)


def _route(logits):
    tk = pl.BlockSpec((TM, TOP_K), lambda i: (i, 0))
    top_exp, gates, rank, counts = pl.pallas_call(
        _route_kernel,
        grid=(N_TOK // TM,),
        in_specs=[pl.BlockSpec((TM, N_EXPERTS), lambda i: (i, 0))],
        out_specs=[tk, tk, tk, pl.BlockSpec((1, N_EXPERTS), lambda i: (0, 0))],
        out_shape=[jax.ShapeDtypeStruct((N_TOK, TOP_K), jnp.int32),
                   jax.ShapeDtypeStruct((N_TOK, TOP_K), F32),
                   jax.ShapeDtypeStruct((N_TOK, TOP_K), jnp.int32),
                   jax.ShapeDtypeStruct((1, N_EXPERTS), jnp.int32)],
        scratch_shapes=[pltpu.VMEM((1, N_EXPERTS), F32)],
        compiler_params=pltpu.CompilerParams(dimension_semantics=("arbitrary",)),
        name="route",
    )(logits)
    counts = counts[0]
    padded = (counts + MOE_BLK - 1) // MOE_BLK * MOE_BLK
    pad_end = jnp.cumsum(padded)
    pad_start = pad_end - padded
    start = jnp.cumsum(counts) - counts
    eq = top_exp[:, :, None] == jnp.arange(N_EXPERTS, dtype=jnp.int32)[None, None, :]
    dest = (jnp.sum(jnp.where(eq, pad_start[None, None, :], 0), axis=-1) + rank).T.reshape(-1)
    blk_exp = jnp.minimum(
        jnp.sum((pad_end[None, :] <= (jnp.arange(MOE_NBLK, dtype=jnp.int32) * MOE_BLK)[:, None]).astype(jnp.int32),
                axis=-1), N_EXPERTS - 1).astype(jnp.int32)
    n_used = (pad_end[-1:] // MOE_BLK).astype(jnp.int32)
    tok_ids = jnp.arange(N_ASG, dtype=jnp.int32) // TOP_K
    _, tok_sorted = lax.sort((top_exp.reshape(-1), tok_ids), num_keys=1, is_stable=True)
    slot = jnp.arange(MOE_NBLK * MOE_BLK, dtype=jnp.int32)
    e_slot = jnp.repeat(blk_exp, MOE_BLK)
    src = jnp.clip(start[e_slot] + slot - pad_start[e_slot], 0, N_ASG - 1)
    slot_tok = tok_sorted[src]
    return slot_tok, gates, dest, blk_exp, n_used


def _combine_kernel(y0_ref, y1_ref, y2_ref, y3_ref, gate_ref, x_ref, mod_ref, o_ref):
    g = gate_ref[...]
    half = D_MODEL // 2
    acc_lo = jnp.zeros((TM, half), F32)
    acc_hi = jnp.zeros((TM, half), F32)
    for k, y_ref in enumerate((y0_ref, y1_ref, y2_ref, y3_ref)):
        lo, hi = _unpack_rows(y_ref[...])
        acc_lo = acc_lo + g[:, k:k + 1] * lo
        acc_hi = acc_hi + g[:, k:k + 1] * hi
    g2 = mod_ref[:, 5 * D_MODEL:]
    o_ref[:, :half] = x_ref[:, :half] + g2[:, :half] * acc_lo
    o_ref[:, half:] = x_ref[:, half:] + g2[:, half:] * acc_hi


def _combine(ys_tok, gates, x_mid, mods_l):
    n_i = N_TOK // TM

    def ysblk(k):
        return pl.BlockSpec((TM, D_MODEL // 2), lambda i: (k * n_i + i, 0))

    return pl.pallas_call(
        _combine_kernel,
        grid=(n_i,),
        in_specs=[ysblk(0), ysblk(1), ysblk(2), ysblk(3),
                  pl.BlockSpec((TM, TOP_K), lambda i: (i, 0)),
                  pl.BlockSpec((TM, D_MODEL), lambda i: (i, 0)),
                  pl.BlockSpec((None, 1, 6 * D_MODEL), lambda i: (i * TM // COND_GROUP, 0, 0))],
        out_specs=pl.BlockSpec((TM, D_MODEL), lambda i: (i, 0)),
        out_shape=jax.ShapeDtypeStruct((N_TOK, D_MODEL), F32),
        compiler_params=pltpu.CompilerParams(dimension_semantics=("arbitrary",), vmem_limit_bytes=VMEM_LIMIT),
        name="moe_combine",
    )(ys_tok, ys_tok, ys_tok, ys_tok, gates, x_mid, mods_l)


def _final_norm_kernel(x_ref, w_ref, oc_ref, ol_ref):
    y = _rms(x_ref[...], w_ref[...])
    is_ctx = pl.program_id(0) < N_CTX_TILES

    @pl.when(is_ctx)
    def _():
        oc_ref[...] = y

    @pl.when(jnp.logical_not(is_ctx))
    def _():
        ol_ref[...] = y


def _final_norm(x, w):
    n_lat_tiles = N_TOK // TM - N_CTX_TILES
    return pl.pallas_call(
        _final_norm_kernel,
        grid=(N_TOK // TM,),
        in_specs=[pl.BlockSpec((TM, D_MODEL), lambda i: (i, 0)), pl.BlockSpec((1, D_MODEL), lambda i: (0, 0))],
        out_specs=[pl.BlockSpec((TM, D_MODEL), lambda i: (jnp.minimum(i, N_CTX_TILES - 1), 0)),
                   pl.BlockSpec((TM, D_MODEL), lambda i: (jnp.clip(i - N_CTX_TILES, 0, n_lat_tiles - 1), 0))],
        out_shape=[jax.ShapeDtypeStruct((N_CTX_TOK, D_MODEL), F32),
                   jax.ShapeDtypeStruct((N_TOK - N_CTX_TOK, D_MODEL), F32)],
        compiler_params=pltpu.CompilerParams(dimension_semantics=("arbitrary",)),
        name="final_norm",
    )(x, w)


def kernel(x_prompt, x_sample, c, cache_k, cache_v, state_hgrn, c_ctx, norm1_w, norm2_w, w_ada, b_ada, w_in, hg_lb, hg_norm_w, lam_q1, lam_k1, lam_q2, lam_k2, da_norm_w, w_out, w_router, b_router, w_gu, b_gu, w_dn, b_dn, norm_f_w):
    lb_all = jnp.cumsum(jax.nn.softmax(hg_lb.astype(F32), axis=0), axis=0)
    lb_all = lb_all - lb_all[0:1]
    lam_init = [0.8 - 0.6 * math.exp(-0.3 * l) for l in range(DEPTH)]
    lam = (jnp.exp(jnp.sum(lam_q1 * lam_k1, axis=-1)) - jnp.exp(jnp.sum(lam_q2 * lam_k2, axis=-1))
           + jnp.asarray(lam_init, F32))
    cos_all, sin_all = _rope_tables()

    conds = jnp.concatenate([c_ctx[None, :], c, jnp.zeros((COND_ROWS - 1 - DEC_BATCH, D_MODEL), F32)], axis=0)
    mods = _ada_mod(conds, w_ada, b_ada)
    mods = mods[:, :N_COND].reshape(DEPTH, N_COND, 1, 6 * D_MODEL)

    x = jnp.concatenate([x_prompt.reshape(N_CTX_TOK, D_MODEL), x_sample.reshape(-1, D_MODEL)], axis=0)
    new_k = jnp.zeros((BATCH, DEPTH, SEQ, W512), F32)
    new_v = jnp.zeros((BATCH, DEPTH, SEQ, W512), F32)
    new_state = jnp.zeros((BATCH, DEPTH, 2, HG_HEADS, HG_DK, HG_DV), F32)
    for l in range(DEPTH):
        proj, new_k, new_v = _inproj(x, norm1_w[l][None, :], mods[l], w_in, new_k, new_v, l)
        o_fw, new_state = _hgrn_dir(proj, None, lb_all[l], state_hgrn, None, new_state, l, rev=False)
        o_hg, new_state = _hgrn_dir(proj, o_fw, lb_all[l], state_hgrn, hg_norm_w[l][None, :], new_state, l,
                                    rev=True)
        q_r, k_r, v_b, knorm = _da_prep(proj, cos_all, sin_all)
        scal = jnp.stack([lam[l], jnp.asarray(1.0 - lam_init[l], F32)])
        o_da_ctx, o_da_lat = _attention(scal, q_r, k_r, v_b, knorm, cache_k, cache_v, da_norm_w[l][None, :], l)
        x_mid, h2, logits = _outproj(o_hg, o_da_ctx, o_da_lat, x, mods[l], w_out, norm2_w[l][None, :],
                                     w_router[l], b_router[l][None, :], l)
        slot_tok, gates, dest, blk_exp, n_used = _route(logits)
        ys = _moe_experts(blk_exp, n_used, slot_tok, h2, w_gu, b_gu, w_dn, b_dn, l)
        ys_tok = ys.at[dest].get(mode="promise_in_bounds")
        x = _combine(ys_tok, gates, x_mid, mods[l])
    y_ctx, y_lat = _final_norm(x, norm_f_w[None, :])
    return (y_ctx.reshape(BATCH, SEQ, D_MODEL), y_lat.reshape(DEC_BATCH, DEC_SEQ, D_MODEL),
            new_k.reshape(BATCH, DEPTH, SEQ, DA_HEADS, 2 * DA_QK), new_v.reshape(BATCH, DEPTH, SEQ, DA_HEADS, DA_V),
            new_state)
```

```python
import functools
import math

import jax
import jax.numpy as jnp
from jax import lax
from jax.experimental import pallas as pl
from jax.experimental.pallas import tpu as pltpu

F32 = jnp.float32
BF16 = jnp.bfloat16

D_MODEL = 1024
BATCH = 16
SEQ = 256
DEPTH = 4
DEC_BATCH = 2
DEC_SEQ = 4096
PAST_LEN = 256
GRID_W = 64
HG_HEADS = 4
HG_DK = 128
HG_DV = 128
HG_WIDTH = HG_HEADS * HG_DK
DA_HEADS = 4
DA_QK = 64
DA_V = 2 * DA_QK
DA_WIDTH = DA_HEADS * DA_V
ROPE_BASE = 10000.0
IN_DIM = 5 * HG_WIDTH + 3 * DA_WIDTH
N_EXPERTS = 32
TOP_K = 4
D_FF = D_MODEL
SWIGLU_ALPHA = 1.702
SWIGLU_LIMIT = 7.0
EPS = 1e-6

N_CTX_TOK = BATCH * SEQ
N_TOK = N_CTX_TOK + DEC_BATCH * DEC_SEQ
COND_GROUP = DEC_SEQ
assert N_CTX_TOK == COND_GROUP
N_COND = N_TOK // COND_GROUP
COND_ROWS = 8

COL_Q_HG, COL_F_FW, COL_F_BW, COL_I_HG, COL_G_HG, COL_Q_DA, COL_K_DA, COL_V_DA = range(8)
W512 = 512
LANES = 128

HG_CHUNK = 64
HG_HALF = HG_CHUNK // 2
HG_ROWS = 256
HG_EXP_CLAMP = 80.0

TM = 512
TN_IN = 2048
ATT_TQ = 256
Q_SCALE = math.log2(math.e) / math.sqrt(DA_QK)
ATT_KC = 512
ATT_MIN_SUM = 2.0 ** -40
MOE_BLK = 512
N_ASG = N_TOK * TOP_K
MOE_NBLK = N_ASG // MOE_BLK + N_EXPERTS
VMEM_LIMIT = 56 * 1024 * 1024


def _dot(a, b):
    return jnp.dot(a, b, preferred_element_type=F32)


def _dot_nt(a, b):
    return lax.dot_general(a, b, (((1,), (1,)), ((), ())), preferred_element_type=F32)


def _dot_tn(a, b):
    return lax.dot_general(a, b, (((0,), (0,)), ((), ())), preferred_element_type=F32)


def _split2(x):
    hi = x.astype(BF16)
    lo = (x - hi.astype(F32)).astype(BF16)
    return hi, lo


def _split3(x):
    hi = x.astype(BF16)
    r = x - hi.astype(F32)
    mid = r.astype(BF16)
    lo = (r - mid.astype(F32)).astype(BF16)
    return hi, mid, lo


def _dot_precise(a, b):
    a_hi, a_lo = _split2(a)
    b_hi, b_lo = _split2(b)
    return _dot(a_hi, b_hi) + (_dot(a_hi, b_lo) + _dot(a_lo, b_hi))


def _pack_rows(x):
    n = x.shape[1] // 2
    lo = pltpu.bitcast(x[:, :n].astype(BF16).astype(F32), jnp.uint32)
    hi = pltpu.bitcast(x[:, n:].astype(BF16).astype(F32), jnp.uint32)
    return (lo >> 16) | hi


def _unpack_rows(u):
    lo = pltpu.bitcast(u << 16, F32)
    hi = pltpu.bitcast(u & jnp.uint32(0xFFFF0000), F32)
    return lo, hi


def _rms(x, w):
    ms = jnp.mean(x * x, axis=-1, keepdims=True)
    return x * lax.rsqrt(ms + EPS) * w


def _sigmoid(x):
    return 1.0 / (1.0 + jnp.exp(-x))


def _ada_kernel(cond_ref, w_ref, b_ref, o_ref):
    cnd = cond_ref[...]
    s = cnd * _sigmoid(cnd)
    o_ref[...] = _dot_precise(s, w_ref[...]) + b_ref[...]


def _ada_mod(conds, w_ada, b_ada):
    nj = 6 * D_MODEL // D_MODEL
    return pl.pallas_call(
        _ada_kernel,
        grid=(DEPTH, nj),
        in_specs=[
            pl.BlockSpec((COND_ROWS, D_MODEL), lambda l, j: (0, 0)),
            pl.BlockSpec((None, D_MODEL, D_MODEL), lambda l, j: (l, 0, j)),
            pl.BlockSpec((None, 1, D_MODEL), lambda l, j: (l, 0, j)),
        ],
        out_specs=pl.BlockSpec((None, COND_ROWS, D_MODEL), lambda l, j: (l, 0, j)),
        out_shape=jax.ShapeDtypeStruct((DEPTH, COND_ROWS, 6 * D_MODEL), F32),
        compiler_params=pltpu.CompilerParams(dimension_semantics=("arbitrary", "arbitrary"),
                                             vmem_limit_bytes=VMEM_LIMIT),
        name="ada_mod",
    )(conds, w_ada, b_ada.reshape(DEPTH, 1, 6 * D_MODEL))


SEQ_PER_TILE = TM // SEQ
N_CTX_TILES = N_CTX_TOK // TM
KV_COL_BLK = (COL_K_DA * W512) // TN_IN
assert COL_K_DA * W512 == KV_COL_BLK * TN_IN + TN_IN - 2 * W512


def _inproj_kernel(x_ref, nw_ref, mod_ref, w_ref, kin_ref, vin_ref, o_ref, kc_ref, vc_ref, wbf_ref):
    del kin_ref, vin_ref
    n = pl.program_id(0)
    i = pl.program_id(1)

    @pl.when(i == 0)
    def _():
        wbf_ref[...] = w_ref[...].astype(BF16)

    x = x_ref[...]
    sh = mod_ref[:, 0:D_MODEL]
    sc = mod_ref[:, D_MODEL:2 * D_MODEL]
    h = _rms(x, nw_ref[...]) * (1.0 + sc) + sh
    o = _dot(h.astype(BF16), wbf_ref[...])
    o_ref[...] = o

    @pl.when(jnp.logical_and(n == KV_COL_BLK, i < N_CTX_TILES))
    def _():
        for s in range(SEQ_PER_TILE):
            kc_ref[s] = o[s * SEQ:(s + 1) * SEQ, TN_IN - 2 * W512:TN_IN - W512]
            vc_ref[s] = o[s * SEQ:(s + 1) * SEQ, TN_IN - W512:]


def _inproj(x, norm_w_l, mods_l, w_in, new_k, new_v, l):
    n_i = N_TOK // TM

    def cache_idx(n, i):
        return (jnp.where(n == KV_COL_BLK, jnp.minimum(i, N_CTX_TILES - 1), 0), l, 0, 0)

    cache_spec = pl.BlockSpec((SEQ_PER_TILE, None, SEQ, W512), cache_idx)
    cache_shape = jax.ShapeDtypeStruct((BATCH, DEPTH, SEQ, W512), F32)
    return pl.pallas_call(
        _inproj_kernel,
        grid=(IN_DIM // TN_IN, n_i),
        in_specs=[
            pl.BlockSpec((TM, D_MODEL), lambda n, i: (i, 0)),
            pl.BlockSpec((1, D_MODEL), lambda n, i: (0, 0)),
            pl.BlockSpec((None, 1, 6 * D_MODEL), lambda n, i: (i * TM // COND_GROUP, 0, 0)),
            pl.BlockSpec((None, D_MODEL, TN_IN), lambda n, i: (l, 0, n)),
            pl.BlockSpec(memory_space=pl.ANY),
            pl.BlockSpec(memory_space=pl.ANY),
        ],
        out_specs=[pl.BlockSpec((TM, TN_IN), lambda n, i: (i, n)), cache_spec, cache_spec],
        out_shape=[jax.ShapeDtypeStruct((N_TOK, IN_DIM), F32), cache_shape, cache_shape],
        input_output_aliases={4: 1, 5: 2},
        scratch_shapes=[pltpu.VMEM((D_MODEL, TN_IN), BF16)],
        compiler_params=pltpu.CompilerParams(dimension_semantics=("arbitrary", "arbitrary"),
                                             vmem_limit_bytes=VMEM_LIMIT),
        name="norm_inproj",
    )(x, norm_w_l, mods_l, w_in, new_k, new_v)


N_HG_BLK = N_TOK // HG_ROWS
N_HG_CTX_BLK = N_CTX_TOK // HG_ROWS
HG_BPS_CTX = SEQ // HG_ROWS
HG_BPS_LAT = DEC_SEQ // HG_ROWS


def _hg_block(step, rev):
    return (N_HG_BLK - 1 - step) if rev else step


def _hg_lat_seq(p):
    return jnp.clip((p - N_HG_CTX_BLK) // HG_BPS_LAT, 0, DEC_BATCH - 1)


def _hg_ctx_seq(p):
    return jnp.minimum(p // HG_BPS_CTX, BATCH - 1)


def _hgrn_kernel(*refs, rev):
    if rev:
        (q_ref, f_ref, v_ref, g_ref, ofw_ref, lb_ref, s0_ref, nw_ref, _, o_ref, sfin_ref, st_ref) = refs
    else:
        (q_ref, f_ref, v_ref, lb_ref, s0_ref, _, o_ref, sfin_ref, st_ref) = refs
    p = _hg_block(pl.program_id(0), rev)
    is_ctx = p < N_HG_CTX_BLK
    pos = jnp.where(is_ctx, p % HG_BPS_CTX, (p - N_HG_CTX_BLK) % HG_BPS_LAT)
    bps = jnp.where(is_ctx, HG_BPS_CTX, HG_BPS_LAT)
    first_pos = (bps - 1) if rev else 0
    last_pos = 0 if rev else (bps - 1)

    @pl.when(jnp.logical_and(pos == first_pos, is_ctx))
    def _():
        st_ref[...] = jnp.zeros_like(st_ref)

    @pl.when(jnp.logical_and(pos == first_pos, jnp.logical_not(is_ctx)))
    def _():
        for h in range(HG_HEADS):
            st_ref[h] = s0_ref[h].T

    lb = lb_ref[...]
    oml = 1.0 - lb
    log_oml = jnp.log1p(-lb)
    row = lax.broadcasted_iota(jnp.int32, (HG_CHUNK, HG_CHUNK), 0)
    col = lax.broadcasted_iota(jnp.int32, (HG_CHUNK, HG_CHUNK), 1)
    tri = ((col >= row) if rev else (col <= row)).astype(BF16)
    same_half = (row >= HG_HALF) == (col >= HG_HALF)
    if rev:
        m_diag = jnp.logical_and(same_half, col >= row)
        m_off = jnp.logical_and(row < HG_HALF, col >= HG_HALF)
    else:
        m_diag = jnp.logical_and(same_half, col <= row)
        m_off = jnp.logical_and(row >= HG_HALF, col < HG_HALF)
    row1 = lax.broadcasted_iota(jnp.int32, (HG_CHUNK, 1), 0)
    n_chunks = HG_ROWS // HG_CHUNK

    def chunk(ci):
        c = (n_chunks - 1 - ci) if rev else ci
        rows = slice(c * HG_CHUNK, (c + 1) * HG_CHUNK)
        fx = f_ref[rows, :]
        q = q_ref[rows, :]
        v = v_ref[rows, :]
        e = jnp.exp(-jnp.abs(fx))
        r = 1.0 / (1.0 + e)
        er = e * r
        nonneg = fx >= 0.0
        sig = jnp.where(nonneg, r, er)
        nsig = jnp.where(nonneg, er, r)
        f = lb + oml * sig
        logf = jnp.maximum(jnp.log(f), log_oml + (jnp.minimum(fx, 0.0) - jnp.log1p(e)))
        k = oml * nsig
        hi, mid, lo = _split3(logf)
        a = _dot(tri, hi) + (_dot(tri, mid) + _dot(tri, lo))
        a_mid = jnp.where(row1 < HG_HALF, a[HG_HALF // 2:HG_HALF // 2 + 1, :],
                          a[HG_HALF + HG_HALF // 2:HG_HALF + HG_HALF // 2 + 1, :])
        if rev:
            a_b = a[HG_HALF:HG_HALF + 1, :]
            a_last = a[0:1, :]
        else:
            a_b = a[HG_HALF - 1:HG_HALF, :]
            a_last = a[HG_CHUNK - 1:HG_CHUNK, :]
        qd = (q * jnp.exp(jnp.minimum(a - a_mid, HG_EXP_CLAMP))).astype(BF16)
        kd = (k * jnp.exp(jnp.minimum(a_mid - a, HG_EXP_CLAMP))).astype(BF16)
        qo = (q * jnp.exp(jnp.minimum(a - a_b, 0.0))).astype(BF16)
        ko = (k * jnp.exp(jnp.minimum(a_b - a, 0.0))).astype(BF16)
        qa = (q * jnp.exp(a)).astype(BF16)
        kdec = (k * jnp.exp(a_last - a)).astype(BF16)
        dec = jnp.exp(a_last)
        vb = v.astype(BF16)
        for h in range(HG_HEADS):
            hs = slice(h * HG_DK, (h + 1) * HG_DK)
            p1 = _dot_nt(qd[:, hs], kd[:, hs])
            p2 = _dot_nt(qo[:, hs], ko[:, hs])
            attn = jnp.where(m_diag, p1, jnp.where(m_off, p2, 0.0)).astype(BF16)
            st = st_ref[h]
            o_h = _dot_nt(qa[:, hs], st.astype(BF16)) + _dot(attn, vb[:, hs])
            st_ref[h] = dec[:, hs] * st + _dot_tn(vb[:, hs], kdec[:, hs])
            if rev:
                tot = ofw_ref[rows, hs] + o_h
                g = g_ref[rows, hs]
                y = _rms(tot, nw_ref[...]) * (g * _sigmoid(g))
                o_ref[rows, hs] = y.astype(o_ref.dtype)
            else:
                o_ref[rows, hs] = o_h

    for ci in range(n_chunks):
        chunk(ci)

    @pl.when(jnp.logical_and(pos == last_pos, is_ctx))
    def _():
        for h in range(HG_HEADS):
            sfin_ref[h] = st_ref[h].T


def _hgrn_dir(proj, o_fw, lb_l, state_hgrn, hg_norm_l, new_state, l, rev):
    d = 1 if rev else 0

    def tok(colblk):
        return pl.BlockSpec((HG_ROWS, W512), lambda s: (_hg_block(s, rev), colblk))

    in_specs = [tok(COL_Q_HG), tok(COL_F_BW if rev else COL_F_FW), tok(COL_I_HG)]
    args = [proj, proj, proj]
    if rev:
        in_specs += [tok(COL_G_HG), pl.BlockSpec((HG_ROWS, W512), lambda s: (_hg_block(s, rev), 0))]
        args += [proj, o_fw]
    in_specs += [
        pl.BlockSpec((None, 1, W512), lambda s: (d, 0, 0)),
        pl.BlockSpec((None, None, None, HG_HEADS, HG_DK, HG_DV),
                     lambda s: (_hg_lat_seq(_hg_block(s, rev)), l, d, 0, 0, 0)),
    ]
    args += [lb_l.reshape(2, 1, W512), state_hgrn]
    if rev:
        in_specs.append(pl.BlockSpec((1, HG_DV), lambda s: (0, 0)))
        args.append(hg_norm_l)
    in_specs.append(pl.BlockSpec(memory_space=pl.ANY))
    args.append(new_state)
    return pl.pallas_call(
        functools.partial(_hgrn_kernel, rev=rev),
        grid=(N_HG_BLK,),
        in_specs=in_specs,
        out_specs=[
            pl.BlockSpec((HG_ROWS, W512), lambda s: (_hg_block(s, rev), 0)),
            pl.BlockSpec((None, None, None, HG_HEADS, HG_DK, HG_DV),
                         lambda s: (_hg_ctx_seq(_hg_block(s, rev)), l, d, 0, 0, 0)),
        ],
        out_shape=[
            jax.ShapeDtypeStruct((N_TOK, W512), BF16 if rev else F32),
            jax.ShapeDtypeStruct((BATCH, DEPTH, 2, HG_HEADS, HG_DK, HG_DV), F32),
        ],
        input_output_aliases={len(args) - 1: 1},
        scratch_shapes=[pltpu.VMEM((HG_HEADS, HG_DV, HG_DK), F32)],
        compiler_params=pltpu.CompilerParams(dimension_semantics=("arbitrary",),
                                             vmem_limit_bytes=VMEM_LIMIT),
        name="hgrn_bwd" if rev else "hgrn_fwd",
    )(*args)


def _rope_tables():
    n_pairs = DA_QK // 4
    inv_freq = ROPE_BASE ** (-jnp.arange(n_pairs, dtype=F32) / n_pairs)
    t = jnp.arange(DEC_SEQ)
    row_idx = (t // GRID_W).astype(F32)
    col_idx = (t % GRID_W).astype(F32)
    lane = jnp.arange(LANES)
    jj = lane % DA_QK
    use_col = (jj // (DA_QK // 2)) == 1
    second = (jj % (DA_QK // 2)) >= n_pairs
    fr = inv_freq[jj % n_pairs]
    ang_r = row_idx[:, None] * inv_freq[None, :]
    ang_c = col_idx[:, None] * inv_freq[None, :]
    cos_r, sin_r, cos_c, sin_c = jnp.cos(ang_r), jnp.sin(ang_r), jnp.cos(ang_c), jnp.sin(ang_c)
    fi = jj % n_pairs
    cos_t = jnp.where(use_col[None, :], cos_c[:, fi], cos_r[:, fi])
    sin_t = jnp.where(use_col[None, :], sin_c[:, fi], sin_r[:, fi])
    sin_t = jnp.where(second[None, :], sin_t, -sin_t)
    del fr
    cos_all = jnp.concatenate([jnp.ones((N_CTX_TOK, LANES), F32)] + [cos_t] * DEC_BATCH, axis=0)
    sin_all = jnp.concatenate([jnp.zeros((N_CTX_TOK, LANES), F32)] + [sin_t] * DEC_BATCH, axis=0)
    return cos_all, sin_all


def _da_prep_kernel(q_ref, k_ref, v_ref, cos_ref, sin_ref, qo_ref, ko_ref, vo_ref, kn_ref):
    cos = cos_ref[...]
    sin = sin_ref[...]
    lane = lax.broadcasted_iota(jnp.int32, (1, LANES), 1)
    first = (lane % (DA_QK // 2)) < (DA_QK // 4)
    lane_n = lax.broadcasted_iota(jnp.int32, (1, 2 * DA_HEADS), 1)

    def rope(x):
        partner = jnp.where(first, pltpu.roll(x, LANES - DA_QK // 4, 1), pltpu.roll(x, DA_QK // 4, 1))
        return x * cos + partner * sin

    kn = jnp.zeros((TM, 2 * DA_HEADS), F32)
    ones = jnp.ones((TM, DA_V), BF16)
    for h in range(DA_HEADS):
        hs = slice(h * DA_V, (h + 1) * DA_V)
        qo_ref[h] = (rope(q_ref[:, hs]) * Q_SCALE).astype(BF16)
        kr = rope(k_ref[:, hs])
        ko_ref[h] = kr.astype(BF16)
        ksq = kr * kr
        n1 = jnp.sum(jnp.where(lane < DA_QK, ksq, 0.0), axis=-1, keepdims=True)
        n2 = jnp.sum(jnp.where(lane >= DA_QK, ksq, 0.0), axis=-1, keepdims=True)
        kn = jnp.where(lane_n == 2 * h, n1, jnp.where(lane_n == 2 * h + 1, n2, kn))
        vo_ref[h, :, 0:DA_V] = v_ref[:, hs].astype(BF16)
        vo_ref[h, :, DA_V:] = ones
    kn_ref[...] = kn


def _da_prep(proj, cos_all, sin_all):
    def tok(colblk):
        return pl.BlockSpec((TM, W512), lambda i: (i, colblk))

    tab = pl.BlockSpec((TM, LANES), lambda i: (i, 0))
    out = pl.BlockSpec((DA_HEADS, TM, DA_V), lambda i: (0, i, 0))
    shp = jax.ShapeDtypeStruct((DA_HEADS, N_TOK, DA_V), BF16)
    return pl.pallas_call(
        _da_prep_kernel,
        grid=(N_TOK // TM,),
        in_specs=[tok(COL_Q_DA), tok(COL_K_DA), tok(COL_V_DA), tab, tab],
        out_specs=[out, out, pl.BlockSpec((DA_HEADS, TM, 2 * DA_V), lambda i: (0, i, 0)),
                   pl.BlockSpec((TM, 2 * DA_HEADS), lambda i: (i, 0))],
        out_shape=[shp, shp, jax.ShapeDtypeStruct((DA_HEADS, N_TOK, 2 * DA_V), BF16),
                   jax.ShapeDtypeStruct((N_TOK, 2 * DA_HEADS), F32)],
        compiler_params=pltpu.CompilerParams(dimension_semantics=("arbitrary",),
                                             vmem_limit_bytes=VMEM_LIMIT),
        name="da_prep",
    )(proj, proj, proj, cos_all, sin_all)


def _attn_kernel(*refs, has_ctx):
    if has_ctx:
        scal_ref, kmax_ref, q_ref, k_ref, v_ref, kc_ref, vc_ref, nw_ref, o_ref = refs
    else:
        scal_ref, kmax_ref, q_ref, k_ref, v_ref, nw_ref, o_ref = refs
    lam = scal_ref[0]
    out_scale = scal_ref[1]
    lane = lax.broadcasted_iota(jnp.int32, (1, LANES), 1)
    n_heads = q_ref.shape[0]
    n_keys = k_ref.shape[1]
    chunk = min(ATT_KC, n_keys)
    b = pl.program_id(0)
    if has_ctx:
        kc = kc_ref[...].astype(BF16)
        vc = vc_ref[...].astype(BF16)
        vc_ext = jnp.concatenate([vc, jnp.ones((PAST_LEN, DA_V), BF16)], axis=1)

    def shifted(q_c, h, shift):
        acc = None
        for c in range(n_keys // chunk):
            ks = slice(c * chunk, (c + 1) * chunk)
            p = jnp.exp2(_dot_nt(q_c, k_ref[h, ks, :]) - shift).astype(BF16)
            part = _dot(p, v_ref[h, ks, :])
            acc = part if acc is None else acc + part
        if has_ctx:
            acc = acc + _dot(jnp.exp2(_dot_nt(q_c, kc) - shift).astype(BF16), vc_ext)
        l = acc[:, DA_V:DA_V + 1]
        return acc[:, :DA_V] / l, l

    def exact(q_c, h):
        s = _dot_nt(q_c, k_ref[h])
        m = jnp.max(s, axis=-1, keepdims=True)
        if has_ctx:
            s_c = _dot_nt(q_c, kc)
            m = jnp.maximum(m, jnp.max(s_c, axis=-1, keepdims=True))
        p = jnp.exp2(s - m)
        l = jnp.sum(p, axis=-1, keepdims=True)
        acc = _dot(p.astype(BF16), v_ref[h, :, 0:DA_V])
        if has_ctx:
            p_c = jnp.exp2(s_c - m)
            l = l + jnp.sum(p_c, axis=-1, keepdims=True)
            acc = acc + _dot(p_c.astype(BF16), vc)
        return acc / l

    def finish(o1, o2):
        return (_rms(o1 - lam * o2, nw_ref[...]) * out_scale).astype(o_ref.dtype)

    for h in range(n_heads):
        head = h if n_heads > 1 else pl.program_id(1)
        cols = slice(None) if n_heads == 1 else slice(h * DA_V, (h + 1) * DA_V)
        qh = q_ref[h]
        zero = jnp.zeros_like(qh)
        qs = (jnp.where(lane < DA_QK, qh, zero), jnp.where(lane >= DA_QK, qh, zero))
        outs, sums = [], []
        for comp, q_c in enumerate(qs):
            qf = q_c.astype(F32)
            qn = jnp.sqrt(jnp.sum(qf * qf, axis=-1, keepdims=True))
            shift = qn * (kmax_ref[(b * DA_HEADS + head) * 2 + comp] * 1.01) + 1e-3
            o_c, l_c = shifted(q_c, h, shift)
            outs.append(o_c)
            sums.append(l_c)
        o_ref[:, cols] = finish(outs[0], outs[1])
        ok = jnp.min(jnp.minimum(sums[0], sums[1])) >= ATT_MIN_SUM

        @pl.when(jnp.logical_not(ok))
        def _():
            o_ref[:, cols] = finish(exact(qs[0], h), exact(qs[1], h))


def _attention(scal, q, k, v, knorm, cache_k, cache_v, da_norm_l, l):
    smem = pl.BlockSpec(memory_space=pltpu.SMEM)
    nw = pl.BlockSpec((1, DA_V), lambda *a: (0, 0))
    kmax_ctx = jnp.sqrt(jnp.max(knorm[:N_CTX_TOK].reshape(BATCH, SEQ, 2 * DA_HEADS), axis=1)).reshape(-1)
    ckn = jnp.sum(jnp.square(cache_k[:, l].reshape(DEC_BATCH, PAST_LEN, 2 * DA_HEADS, DA_QK)), axis=-1)
    kmax_lat = jnp.sqrt(jnp.maximum(jnp.max(knorm[N_CTX_TOK:].reshape(DEC_BATCH, DEC_SEQ, 2 * DA_HEADS), axis=1),
                                    jnp.max(ckn, axis=1))).reshape(-1)
    blk = pl.BlockSpec((DA_HEADS, SEQ, DA_V), lambda b: (0, b, 0))
    o_ctx = pl.pallas_call(
        functools.partial(_attn_kernel, has_ctx=False),
        grid=(BATCH,),
        in_specs=[smem, smem, blk, blk, pl.BlockSpec((DA_HEADS, SEQ, 2 * DA_V), lambda b: (0, b, 0)), nw],
        out_specs=pl.BlockSpec((SEQ, W512), lambda b: (b, 0)),
        out_shape=jax.ShapeDtypeStruct((N_CTX_TOK, W512), BF16),
        compiler_params=pltpu.CompilerParams(dimension_semantics=("arbitrary",), vmem_limit_bytes=VMEM_LIMIT),
        name="attn_ctx",
    )(scal, kmax_ctx, q, k, v, da_norm_l)
    nq = DEC_SEQ // ATT_TQ
    off_q = N_CTX_TOK // ATT_TQ
    off_k = N_CTX_TOK // DEC_SEQ
    ck = cache_k.reshape(DEC_BATCH, DEPTH, PAST_LEN, W512)
    cv = cache_v.reshape(DEC_BATCH, DEPTH, PAST_LEN, W512)
    ckv = pl.BlockSpec((None, None, PAST_LEN, DA_V), lambda b, h, i: (b, l, 0, h))
    o_lat = pl.pallas_call(
        functools.partial(_attn_kernel, has_ctx=True),
        grid=(DEC_BATCH, DA_HEADS, nq),
        in_specs=[smem, smem,
                  pl.BlockSpec((1, ATT_TQ, DA_V), lambda b, h, i: (h, off_q + b * nq + i, 0)),
                  pl.BlockSpec((1, DEC_SEQ, DA_V), lambda b, h, i: (h, off_k + b, 0)),
                  pl.BlockSpec((1, DEC_SEQ, 2 * DA_V), lambda b, h, i: (h, off_k + b, 0)),
                  ckv, ckv, nw],
        out_specs=pl.BlockSpec((ATT_TQ, DA_V), lambda b, h, i: (b * nq + i, h)),
        out_shape=jax.ShapeDtypeStruct((DEC_BATCH * DEC_SEQ, W512), BF16),
        compiler_params=pltpu.CompilerParams(dimension_semantics=("arbitrary", "arbitrary", "arbitrary"),
                                             vmem_limit_bytes=VMEM_LIMIT),
        name="attn_lat",
    )(scal, kmax_lat, q, k, v, ck, cv, da_norm_l)
    return o_ctx, o_lat


def _outproj_kernel(hg_ref, dac_ref, dal_ref, x_ref, mod_ref, w_ref, nw_ref, wr_ref, br_ref,
                    xo_ref, h_ref, lg_ref, wbf_ref):
    i = pl.program_id(0)

    @pl.when(i == 0)
    def _():
        wbf_ref[...] = w_ref[...].astype(BF16)

    is_ctx = i < N_CTX_TOK // TM
    da = jnp.where(is_ctx, dac_ref[...], dal_ref[...])
    mix = _dot(hg_ref[...], wbf_ref[0:HG_WIDTH, :]) + _dot(da, wbf_ref[HG_WIDTH:, :])
    g1 = mod_ref[:, 2 * D_MODEL:3 * D_MODEL]
    sh2 = mod_ref[:, 3 * D_MODEL:4 * D_MODEL]
    sc2 = mod_ref[:, 4 * D_MODEL:5 * D_MODEL]
    x = x_ref[...] + g1 * mix
    xo_ref[...] = x
    h = _rms(x, nw_ref[...]) * (1.0 + sc2) + sh2
    h_ref[...] = _pack_rows(h)
    lg_ref[...] = _dot_precise(h, wr_ref[...]) + br_ref[...]


def _outproj(o_hg, o_da_ctx, o_da_lat, x, mods_l, w_out, norm2_l, w_router_l, b_router_l, l):
    n_ctx_t = N_CTX_TOK // TM
    n_lat_t = (N_TOK - N_CTX_TOK) // TM
    tokspec = lambda w: pl.BlockSpec((TM, w), lambda i: (i, 0))
    return pl.pallas_call(
        _outproj_kernel,
        grid=(N_TOK // TM,),
        in_specs=[
            tokspec(W512),
            pl.BlockSpec((TM, W512), lambda i: (jnp.minimum(i, n_ctx_t - 1), 0)),
            pl.BlockSpec((TM, W512), lambda i: (jnp.clip(i - n_ctx_t, 0, n_lat_t - 1), 0)),
            tokspec(D_MODEL),
            pl.BlockSpec((None, 1, 6 * D_MODEL), lambda i: (i * TM // COND_GROUP, 0, 0)),
            pl.BlockSpec((None, D_MODEL, D_MODEL), lambda i: (l, 0, 0)),
            pl.BlockSpec((1, D_MODEL), lambda i: (0, 0)),
            pl.BlockSpec((D_MODEL, N_EXPERTS), lambda i: (0, 0)),
            pl.BlockSpec((1, N_EXPERTS), lambda i: (0, 0)),
        ],
        out_specs=[tokspec(D_MODEL), tokspec(D_MODEL // 2), tokspec(N_EXPERTS)],
        out_shape=[jax.ShapeDtypeStruct((N_TOK, D_MODEL), F32),
                   jax.ShapeDtypeStruct((N_TOK, D_MODEL // 2), jnp.uint32),
                   jax.ShapeDtypeStruct((N_TOK, N_EXPERTS), F32)],
        scratch_shapes=[pltpu.VMEM((D_MODEL, D_MODEL), BF16)],
        compiler_params=pltpu.CompilerParams(dimension_semantics=("arbitrary",),
                                             vmem_limit_bytes=VMEM_LIMIT),
        name="outproj_norm_router",
    )(o_hg, o_da_ctx, o_da_lat, x, mods_l, w_out, norm2_l, w_router_l, b_router_l)


def _moe_kernel(be_ref, nu_ref, tok_ref, h_hbm, wgu_ref, bgu_ref, wdn_ref, bdn_ref, o_ref,
                wgu_bf, wdn_bf, xbuf0, xbuf1, sem):
    i = pl.program_id(0)
    n_used = nu_ref[0]
    used = i < n_used
    odd = i % 2
    prev = be_ref[jnp.maximum(i - 1, 0)]
    fresh = jnp.logical_or(i == 0, be_ref[i] != prev)
    bufs = (xbuf0, xbuf1)

    def start_rows(blk, b):
        base = blk * MOE_BLK
        for r in range(MOE_BLK):
            pltpu.make_async_copy(h_hbm.at[pl.ds(tok_ref[base + r], 1), :], bufs[b].at[pl.ds(r, 1), :],
                                  sem.at[b]).start()

    def wait_rows(b):
        pltpu.make_async_copy(h_hbm.at[pl.ds(0, MOE_BLK), :], bufs[b], sem.at[b]).wait()

    @pl.when(i == 0)
    def _():
        start_rows(0, 0)

    @pl.when(jnp.logical_and(used, fresh))
    def _():
        wgu_bf[...] = wgu_ref[...].astype(BF16)
        wdn_bf[...] = wdn_ref[...].astype(BF16)

    def step(b):
        wait_rows(b)
        x_lo, x_hi = _unpack_rows(bufs[b][...])
        start_rows(jnp.minimum(i + 1, n_used - 1), 1 - b)
        x = jnp.concatenate([x_lo.astype(BF16), x_hi.astype(BF16)], axis=1)
        gu = _dot(x, wgu_bf[...]) + bgu_ref[...]
        gate = jnp.minimum(gu[:, :D_FF], SWIGLU_LIMIT)
        lin = jnp.clip(gu[:, D_FF:], -SWIGLU_LIMIT, SWIGLU_LIMIT)
        act = (lin + 1.0) * gate * _sigmoid(SWIGLU_ALPHA * gate)
        y = _dot(act.astype(BF16), wdn_bf[...]) + bdn_ref[...]
        o_ref[...] = _pack_rows(y)

    for b in range(2):
        @pl.when(jnp.logical_and(used, odd == b))
        def _():
            step(b)

        @pl.when(jnp.logical_and(i == n_used - 1, odd == b))
        def _():
            wait_rows(1 - b)

    @pl.when(jnp.logical_not(used))
    def _():
        o_ref[...] = jnp.zeros_like(o_ref)


def _moe_experts(blk_exp, n_used, slot_tok, h2, w_gu, b_gu, w_dn, b_dn, l):
    grid_spec = pltpu.PrefetchScalarGridSpec(
        num_scalar_prefetch=3,
        grid=(MOE_NBLK,),
        in_specs=[
            pl.BlockSpec(memory_space=pl.ANY),
            pl.BlockSpec((None, None, D_MODEL, 2 * D_FF), lambda i, be, nu, tk: (l, be[i], 0, 0)),
            pl.BlockSpec((None, None, 1, 2 * D_FF), lambda i, be, nu, tk: (l, be[i], 0, 0)),
            pl.BlockSpec((None, None, D_FF, D_MODEL), lambda i, be, nu, tk: (l, be[i], 0, 0)),
            pl.BlockSpec((None, None, 1, D_MODEL), lambda i, be, nu, tk: (l, be[i], 0, 0)),
        ],
        out_specs=pl.BlockSpec((MOE_BLK, D_MODEL // 2), lambda i, be, nu, tk: (i, 0)),
        scratch_shapes=[pltpu.VMEM((D_MODEL, 2 * D_FF), BF16), pltpu.VMEM((D_FF, D_MODEL), BF16),
                        pltpu.VMEM((MOE_BLK, D_MODEL // 2), jnp.uint32),
                        pltpu.VMEM((MOE_BLK, D_MODEL // 2), jnp.uint32), pltpu.SemaphoreType.DMA((2,))],
    )
    return pl.pallas_call(
        _moe_kernel,
        grid_spec=grid_spec,
        out_shape=jax.ShapeDtypeStruct((MOE_NBLK * MOE_BLK, D_MODEL // 2), jnp.uint32),
        compiler_params=pltpu.CompilerParams(dimension_semantics=("arbitrary",),
                                             vmem_limit_bytes=VMEM_LIMIT),
        name="moe_experts",
    )(blk_exp, n_used, slot_tok, h2, w_gu, b_gu.reshape(DEPTH, N_EXPERTS, 1, 2 * D_FF),
      w_dn, b_dn.reshape(DEPTH, N_EXPERTS, 1, D_MODEL))


def _route_kernel(lg_ref, exp_ref, gate_ref, rank_ref, cnt_ref, run_ref):
    i = pl.program_id(0)

    @pl.when(i == 0)
    def _():
        run_ref[...] = jnp.zeros_like(run_ref)

    lg = lg_ref[...]
    lane = lax.broadcasted_iota(jnp.int32, (1, N_EXPERTS), 1)
    lane_k = lax.broadcasted_iota(jnp.int32, (1, TOP_K), 1)
    vals, idxs, hots = [], [], []
    for _ in range(TOP_K):
        m = jnp.max(lg, axis=-1, keepdims=True)
        idx = jnp.min(jnp.where(lg == m, lane, N_EXPERTS), axis=-1, keepdims=True)
        hot = lane == idx
        vals.append(m)
        idxs.append(idx)
        hots.append(hot)
        lg = jnp.where(hot, -jnp.inf, lg)
    exps = [jnp.exp(v - vals[0]) for v in vals]
    denom = exps[0] + exps[1] + exps[2] + exps[3]
    tokhot = (hots[0] | hots[1] | hots[2] | hots[3]).astype(BF16)
    r = lax.broadcasted_iota(jnp.int32, (TM, TM), 0)
    cidx = lax.broadcasted_iota(jnp.int32, (TM, TM), 1)
    before = _dot((cidx < r).astype(BF16), tokhot) + run_ref[...]
    e_out = jnp.zeros((TM, TOP_K), jnp.int32)
    g_out = jnp.zeros((TM, TOP_K), F32)
    r_out = jnp.zeros((TM, TOP_K), jnp.int32)
    for k in range(TOP_K):
        rk = jnp.sum(jnp.where(hots[k], before, 0.0), axis=-1, keepdims=True).astype(jnp.int32)
        e_out = jnp.where(lane_k == k, idxs[k], e_out)
        g_out = jnp.where(lane_k == k, exps[k] / denom, g_out)
        r_out = jnp.where(lane_k == k, rk, r_out)
    exp_ref[...] = e_out
    gate_ref[...] = g_out
    rank_ref[...] = r_out
    run_ref[...] += jnp.sum(tokhot.astype(F32), axis=0, keepdims=True)
    cnt_ref[...] = run_ref[...].astype(jnp.int32)


def _route(logits):
    tk = pl.BlockSpec((TM, TOP_K), lambda i: (i, 0))
    top_exp, gates, rank, counts = pl.pallas_call(
        _route_kernel,
        grid=(N_TOK // TM,),
        in_specs=[pl.BlockSpec((TM, N_EXPERTS), lambda i: (i, 0))],
        out_specs=[tk, tk, tk, pl.BlockSpec((1, N_EXPERTS), lambda i: (0, 0))],
        out_shape=[jax.ShapeDtypeStruct((N_TOK, TOP_K), jnp.int32),
                   jax.ShapeDtypeStruct((N_TOK, TOP_K), F32),
                   jax.ShapeDtypeStruct((N_TOK, TOP_K), jnp.int32),
                   jax.ShapeDtypeStruct((1, N_EXPERTS), jnp.int32)],
        scratch_shapes=[pltpu.VMEM((1, N_EXPERTS), F32)],
        compiler_params=pltpu.CompilerParams(dimension_semantics=("arbitrary",)),
        name="route",
    )(logits)
    counts = counts[0]
    padded = (counts + MOE_BLK - 1) // MOE_BLK * MOE_BLK
    pad_end = jnp.cumsum(padded)
    pad_start = pad_end - padded
    start = jnp.cumsum(counts) - counts
    eq = top_exp[:, :, None] == jnp.arange(N_EXPERTS, dtype=jnp.int32)[None, None, :]
    dest = (jnp.sum(jnp.where(eq, pad_start[None, None, :], 0), axis=-1) + rank).T.reshape(-1)
    blk_exp = jnp.minimum(
        jnp.sum((pad_end[None, :] <= (jnp.arange(MOE_NBLK, dtype=jnp.int32) * MOE_BLK)[:, None]).astype(jnp.int32),
                axis=-1), N_EXPERTS - 1).astype(jnp.int32)
    n_used = (pad_end[-1:] // MOE_BLK).astype(jnp.int32)
    tok_ids = jnp.arange(N_ASG, dtype=jnp.int32) // TOP_K
    _, tok_sorted = lax.sort((top_exp.reshape(-1), tok_ids), num_keys=1, is_stable=True)
    slot = jnp.arange(MOE_NBLK * MOE_BLK, dtype=jnp.int32)
    e_slot = jnp.repeat(blk_exp, MOE_BLK)
    src = jnp.clip(start[e_slot] + slot - pad_start[e_slot], 0, N_ASG - 1)
    slot_tok = tok_sorted[src]
    return slot_tok, gates, dest, blk_exp, n_used


def _combine_kernel(y0_ref, y1_ref, y2_ref, y3_ref, gate_ref, x_ref, mod_ref, o_ref):
    g = gate_ref[...]
    half = D_MODEL // 2
    acc_lo = jnp.zeros((TM, half), F32)
    acc_hi = jnp.zeros((TM, half), F32)
    for k, y_ref in enumerate((y0_ref, y1_ref, y2_ref, y3_ref)):
        lo, hi = _unpack_rows(y_ref[...])
        acc_lo = acc_lo + g[:, k:k + 1] * lo
        acc_hi = acc_hi + g[:, k:k + 1] * hi
    g2 = mod_ref[:, 5 * D_MODEL:]
    o_ref[:, :half] = x_ref[:, :half] + g2[:, :half] * acc_lo
    o_ref[:, half:] = x_ref[:, half:] + g2[:, half:] * acc_hi


def _combine(ys_tok, gates, x_mid, mods_l):
    n_i = N_TOK // TM

    def ysblk(k):
        return pl.BlockSpec((TM, D_MODEL // 2), lambda i: (k * n_i + i, 0))

    return pl.pallas_call(
        _combine_kernel,
        grid=(n_i,),
        in_specs=[ysblk(0), ysblk(1), ysblk(2), ysblk(3),
                  pl.BlockSpec((TM, TOP_K), lambda i: (i, 0)),
                  pl.BlockSpec((TM, D_MODEL), lambda i: (i, 0)),
                  pl.BlockSpec((None, 1, 6 * D_MODEL), lambda i: (i * TM // COND_GROUP, 0, 0))],
        out_specs=pl.BlockSpec((TM, D_MODEL), lambda i: (i, 0)),
        out_shape=jax.ShapeDtypeStruct((N_TOK, D_MODEL), F32),
        compiler_params=pltpu.CompilerParams(dimension_semantics=("arbitrary",), vmem_limit_bytes=VMEM_LIMIT),
        name="moe_combine",
    )(ys_tok, ys_tok, ys_tok, ys_tok, gates, x_mid, mods_l)


def _final_norm_kernel(x_ref, w_ref, oc_ref, ol_ref):
    y = _rms(x_ref[...], w_ref[...])
    is_ctx = pl.program_id(0) < N_CTX_TILES

    @pl.when(is_ctx)
    def _():
        oc_ref[...] = y

    @pl.when(jnp.logical_not(is_ctx))
    def _():
        ol_ref[...] = y


def _final_norm(x, w):
    n_lat_tiles = N_TOK // TM - N_CTX_TILES
    return pl.pallas_call(
        _final_norm_kernel,
        grid=(N_TOK // TM,),
        in_specs=[pl.BlockSpec((TM, D_MODEL), lambda i: (i, 0)), pl.BlockSpec((1, D_MODEL), lambda i: (0, 0))],
        out_specs=[pl.BlockSpec((TM, D_MODEL), lambda i: (jnp.minimum(i, N_CTX_TILES - 1), 0)),
                   pl.BlockSpec((TM, D_MODEL), lambda i: (jnp.clip(i - N_CTX_TILES, 0, n_lat_tiles - 1), 0))],
        out_shape=[jax.ShapeDtypeStruct((N_CTX_TOK, D_MODEL), F32),
                   jax.ShapeDtypeStruct((N_TOK - N_CTX_TOK, D_MODEL), F32)],
        compiler_params=pltpu.CompilerParams(dimension_semantics=("arbitrary",)),
        name="final_norm",
    )(x, w)


def kernel(x_prompt, x_sample, c, cache_k, cache_v, state_hgrn, c_ctx, norm1_w, norm2_w, w_ada, b_ada, w_in, hg_lb, hg_norm_w, lam_q1, lam_k1, lam_q2, lam_k2, da_norm_w, w_out, w_router, b_router, w_gu, b_gu, w_dn, b_dn, norm_f_w):
    lb_all = jnp.cumsum(jax.nn.softmax(hg_lb.astype(F32), axis=0), axis=0)
    lb_all = lb_all - lb_all[0:1]
    lam_init = [0.8 - 0.6 * math.exp(-0.3 * l) for l in range(DEPTH)]
    lam = (jnp.exp(jnp.sum(lam_q1 * lam_k1, axis=-1)) - jnp.exp(jnp.sum(lam_q2 * lam_k2, axis=-1))
           + jnp.asarray(lam_init, F32))
    cos_all, sin_all = _rope_tables()

    conds = jnp.concatenate([c_ctx[None, :], c, jnp.zeros((COND_ROWS - 1 - DEC_BATCH, D_MODEL), F32)], axis=0)
    mods = _ada_mod(conds, w_ada, b_ada)
    mods = mods[:, :N_COND].reshape(DEPTH, N_COND, 1, 6 * D_MODEL)

    x = jnp.concatenate([x_prompt.reshape(N_CTX_TOK, D_MODEL), x_sample.reshape(-1, D_MODEL)], axis=0)
    new_k = jnp.zeros((BATCH, DEPTH, SEQ, W512), F32)
    new_v = jnp.zeros((BATCH, DEPTH, SEQ, W512), F32)
    new_state = jnp.zeros((BATCH, DEPTH, 2, HG_HEADS, HG_DK, HG_DV), F32)
    for l in range(DEPTH):
        proj, new_k, new_v = _inproj(x, norm1_w[l][None, :], mods[l], w_in, new_k, new_v, l)
        o_fw, new_state = _hgrn_dir(proj, None, lb_all[l], state_hgrn, None, new_state, l, rev=False)
        o_hg, new_state = _hgrn_dir(proj, o_fw, lb_all[l], state_hgrn, hg_norm_w[l][None, :], new_state, l,
                                    rev=True)
        q_r, k_r, v_b, knorm = _da_prep(proj, cos_all, sin_all)
        scal = jnp.stack([lam[l], jnp.asarray(1.0 - lam_init[l], F32)])
        o_da_ctx, o_da_lat = _attention(scal, q_r, k_r, v_b, knorm, cache_k, cache_v, da_norm_w[l][None, :], l)
        x_mid, h2, logits = _outproj(o_hg, o_da_ctx, o_da_lat, x, mods[l], w_out, norm2_w[l][None, :],
                                     w_router[l], b_router[l][None, :], l)
        slot_tok, gates, dest, blk_exp, n_used = _route(logits)
        ys = _moe_experts(blk_exp, n_used, slot_tok, h2, w_gu, b_gu, w_dn, b_dn, l)
        ys_tok = ys.at[dest].get(mode="promise_in_bounds")
        x = _combine(ys_tok, gates, x_mid, mods[l])
    y_ctx, y_lat = _final_norm(x, norm_f_w[None, :])
    return (y_ctx.reshape(BATCH, SEQ, D_MODEL), y_lat.reshape(DEC_BATCH, DEC_SEQ, D_MODEL),
            new_k.reshape(BATCH, DEPTH, SEQ, DA_HEADS, 2 * DA_QK), new_v.reshape(BATCH, DEPTH, SEQ, DA_HEADS, DA_V),
            new_state)
```

```python
import functools
import math

import jax
import jax.numpy as jnp
from jax import lax
from jax.experimental import pallas as pl
from jax.experimental.pallas import tpu as pltpu

F32 = jnp.float32
BF16 = jnp.bfloat16

D_MODEL = 1024
BATCH = 16
SEQ = 256
DEPTH = 4
DEC_BATCH = 2
DEC_SEQ = 4096
PAST_LEN = 256
GRID_W = 64
HG_HEADS = 4
HG_DK = 128
HG_DV = 128
HG_WIDTH = HG_HEADS * HG_DK
DA_HEADS = 4
DA_QK = 64
DA_V = 2 * DA_QK
DA_WIDTH = DA_HEADS * DA_V
ROPE_BASE = 10000.0
IN_DIM = 5 * HG_WIDTH + 3 * DA_WIDTH
N_EXPERTS = 32
TOP_K = 4
D_FF = D_MODEL
SWIGLU_ALPHA = 1.702
SWIGLU_LIMIT = 7.0
EPS = 1e-6

N_CTX_TOK = BATCH * SEQ
N_TOK = N_CTX_TOK + DEC_BATCH * DEC_SEQ
COND_GROUP = DEC_SEQ
assert N_CTX_TOK == COND_GROUP
N_COND = N_TOK // COND_GROUP
COND_ROWS = 8

COL_Q_HG, COL_F_FW, COL_F_BW, COL_I_HG, COL_G_HG, COL_Q_DA, COL_K_DA, COL_V_DA = range(8)
W512 = 512
LANES = 128

HG_CHUNK = 64
HG_HALF = HG_CHUNK // 2
HG_ROWS = 256
HG_EXP_CLAMP = 80.0

TM = 512
TN_IN = 2048
ATT_TQ = 256
Q_SCALE = math.log2(math.e) / math.sqrt(DA_QK)
ATT_KC = 512
ATT_MIN_SUM = 2.0 ** -40
MOE_BLK = 512
N_ASG = N_TOK * TOP_K
MOE_NBLK = N_ASG // MOE_BLK + N_EXPERTS
VMEM_LIMIT = 56 * 1024 * 1024


def _dot(a, b):
    return jnp.dot(a, b, preferred_element_type=F32)


def _dot_nt(a, b):
    return lax.dot_general(a, b, (((1,), (1,)), ((), ())), preferred_element_type=F32)


def _dot_tn(a, b):
    return lax.dot_general(a, b, (((0,), (0,)), ((), ())), preferred_element_type=F32)


def _split2(x):
    hi = x.astype(BF16)
    lo = (x - hi.astype(F32)).astype(BF16)
    return hi, lo


def _split3(x):
    hi = x.astype(BF16)
    r = x - hi.astype(F32)
    mid = r.astype(BF16)
    lo = (r - mid.astype(F32)).astype(BF16)
    return hi, mid, lo


def _dot_precise(a, b):
    a_hi, a_lo = _split2(a)
    b_hi, b_lo = _split2(b)
    return _dot(a_hi, b_hi) + (_dot(a_hi, b_lo) + _dot(a_lo, b_hi))


def _pack_rows(x):
    n = x.shape[1] // 2
    lo = pltpu.bitcast(x[:, :n].astype(BF16).astype(F32), jnp.uint32)
    hi = pltpu.bitcast(x[:, n:].astype(BF16).astype(F32), jnp.uint32)
    return (lo >> 16) | hi


def _unpack_rows(u):
    lo = pltpu.bitcast(u << 16, F32)
    hi = pltpu.bitcast(u & jnp.uint32(0xFFFF0000), F32)
    return lo, hi


def _rms(x, w):
    ms = jnp.mean(x * x, axis=-1, keepdims=True)
    return x * lax.rsqrt(ms + EPS) * w


def _sigmoid(x):
    return 1.0 / (1.0 + jnp.exp(-x))


def _ada_kernel(cond_ref, w_ref, b_ref, o_ref):
    cnd = cond_ref[...]
    s = cnd * _sigmoid(cnd)
    o_ref[...] = _dot_precise(s, w_ref[...]) + b_ref[...]


def _ada_mod(conds, w_ada, b_ada):
    nj = 6 * D_MODEL // D_MODEL
    return pl.pallas_call(
        _ada_kernel,
        grid=(DEPTH, nj),
        in_specs=[
            pl.BlockSpec((COND_ROWS, D_MODEL), lambda l, j: (0, 0)),
            pl.BlockSpec((None, D_MODEL, D_MODEL), lambda l, j: (l, 0, j)),
            pl.BlockSpec((None, 1, D_MODEL), lambda l, j: (l, 0, j)),
        ],
        out_specs=pl.BlockSpec((None, COND_ROWS, D_MODEL), lambda l, j: (l, 0, j)),
        out_shape=jax.ShapeDtypeStruct((DEPTH, COND_ROWS, 6 * D_MODEL), F32),
        compiler_params=pltpu.CompilerParams(dimension_semantics=("arbitrary", "arbitrary"),
                                             vmem_limit_bytes=VMEM_LIMIT),
        name="ada_mod",
    )(conds, w_ada, b_ada.reshape(DEPTH, 1, 6 * D_MODEL))


SEQ_PER_TILE = TM // SEQ
N_CTX_TILES = N_CTX_TOK // TM
KV_COL_BLK = (COL_K_DA * W512) // TN_IN
assert COL_K_DA * W512 == KV_COL_BLK * TN_IN + TN_IN - 2 * W512


def _inproj_kernel(x_ref, nw_ref, mod_ref, w_ref, kin_ref, vin_ref, o_ref, kc_ref, vc_ref, wbf_ref):
    del kin_ref, vin_ref
    n = pl.program_id(0)
    i = pl.program_id(1)

    @pl.when(i == 0)
    def _():
        wbf_ref[...] = w_ref[...].astype(BF16)

    x = x_ref[...]
    sh = mod_ref[:, 0:D_MODEL]
    sc = mod_ref[:, D_MODEL:2 * D_MODEL]
    h = _rms(x, nw_ref[...]) * (1.0 + sc) + sh
    o = _dot(h.astype(BF16), wbf_ref[...])
    o_ref[...] = o

    @pl.when(jnp.logical_and(n == KV_COL_BLK, i < N_CTX_TILES))
    def _():
        for s in range(SEQ_PER_TILE):
            kc_ref[s] = o[s * SEQ:(s + 1) * SEQ, TN_IN - 2 * W512:TN_IN - W512]
            vc_ref[s] = o[s * SEQ:(s + 1) * SEQ, TN_IN - W512:]


def _inproj(x, norm_w_l, mods_l, w_in, new_k, new_v, l):
    n_i = N_TOK // TM

    def cache_idx(n, i):
        return (jnp.where(n == KV_COL_BLK, jnp.minimum(i, N_CTX_TILES - 1), 0), l, 0, 0)

    cache_spec = pl.BlockSpec((SEQ_PER_TILE, None, SEQ, W512), cache_idx)
    cache_shape = jax.ShapeDtypeStruct((BATCH, DEPTH, SEQ, W512), F32)
    return pl.pallas_call(
        _inproj_kernel,
        grid=(IN_DIM // TN_IN, n_i),
        in_specs=[
            pl.BlockSpec((TM, D_MODEL), lambda n, i: (i, 0)),
            pl.BlockSpec((1, D_MODEL), lambda n, i: (0, 0)),
            pl.BlockSpec((None, 1, 6 * D_MODEL), lambda n, i: (i * TM // COND_GROUP, 0, 0)),
            pl.BlockSpec((None, D_MODEL, TN_IN), lambda n, i: (l, 0, n)),
            pl.BlockSpec(memory_space=pl.ANY),
            pl.BlockSpec(memory_space=pl.ANY),
        ],
        out_specs=[pl.BlockSpec((TM, TN_IN), lambda n, i: (i, n)), cache_spec, cache_spec],
        out_shape=[jax.ShapeDtypeStruct((N_TOK, IN_DIM), F32), cache_shape, cache_shape],
        input_output_aliases={4: 1, 5: 2},
        scratch_shapes=[pltpu.VMEM((D_MODEL, TN_IN), BF16)],
        compiler_params=pltpu.CompilerParams(dimension_semantics=("arbitrary", "arbitrary"),
                                             vmem_limit_bytes=VMEM_LIMIT),
        name="norm_inproj",
    )(x, norm_w_l, mods_l, w_in, new_k, new_v)


N_HG_BLK = N_TOK // HG_ROWS
N_HG_CTX_BLK = N_CTX_TOK // HG_ROWS
HG_BPS_CTX = SEQ // HG_ROWS
HG_BPS_LAT = DEC_SEQ // HG_ROWS


def _hg_block(step, rev):
    return (N_HG_BLK - 1 - step) if rev else step


def _hg_lat_seq(p):
    return jnp.clip((p - N_HG_CTX_BLK) // HG_BPS_LAT, 0, DEC_BATCH - 1)


def _hg_ctx_seq(p):
    return jnp.minimum(p // HG_BPS_CTX, BATCH - 1)


def _hgrn_kernel(*refs, rev):
    if rev:
        (q_ref, f_ref, v_ref, g_ref, ofw_ref, lb_ref, s0_ref, nw_ref, _, o_ref, sfin_ref,
         st_ref, a_ref, k_ref) = refs
    else:
        (q_ref, f_ref, v_ref, lb_ref, s0_ref, _, o_ref, sfin_ref, st_ref, a_ref, k_ref) = refs
    p = _hg_block(pl.program_id(0), rev)
    is_ctx = p < N_HG_CTX_BLK
    pos = jnp.where(is_ctx, p % HG_BPS_CTX, (p - N_HG_CTX_BLK) % HG_BPS_LAT)
    bps = jnp.where(is_ctx, HG_BPS_CTX, HG_BPS_LAT)
    first_pos = (bps - 1) if rev else 0
    last_pos = 0 if rev else (bps - 1)

    @pl.when(jnp.logical_and(pos == first_pos, is_ctx))
    def _():
        st_ref[...] = jnp.zeros_like(st_ref)

    @pl.when(jnp.logical_and(pos == first_pos, jnp.logical_not(is_ctx)))
    def _():
        for h in range(HG_HEADS):
            st_ref[h] = s0_ref[h].T

    lb = lb_ref[...]
    oml = 1.0 - lb
    log_oml = jnp.log1p(-lb)
    row = lax.broadcasted_iota(jnp.int32, (HG_CHUNK, HG_CHUNK), 0)
    col = lax.broadcasted_iota(jnp.int32, (HG_CHUNK, HG_CHUNK), 1)
    tri = ((col >= row) if rev else (col <= row)).astype(BF16)
    same_half = (row >= HG_HALF) == (col >= HG_HALF)
    if rev:
        m_diag = jnp.logical_and(same_half, col >= row)
        m_off = jnp.logical_and(row < HG_HALF, col >= HG_HALF)
    else:
        m_diag = jnp.logical_and(same_half, col <= row)
        m_off = jnp.logical_and(row >= HG_HALF, col < HG_HALF)
    row1 = lax.broadcasted_iota(jnp.int32, (HG_CHUNK, 1), 0)
    n_chunks = HG_ROWS // HG_CHUNK

    def mid_rows(a):
        return jnp.where(row1 < HG_HALF, a[HG_HALF // 2:HG_HALF // 2 + 1, :],
                         a[HG_HALF + HG_HALF // 2:HG_HALF + HG_HALF // 2 + 1, :])

    def gates(c):
        rows = slice(c * HG_CHUNK, (c + 1) * HG_CHUNK)
        fx = f_ref[rows, :]
        e = jnp.exp(-jnp.abs(fx))
        r = 1.0 / (1.0 + e)
        er = e * r
        nonneg = fx >= 0.0
        sig = jnp.where(nonneg, r, er)
        nsig = jnp.where(nonneg, er, r)
        f = lb + oml * sig
        logf = jnp.maximum(jnp.log(f), log_oml + (jnp.minimum(fx, 0.0) - jnp.log1p(e)))
        k = oml * nsig
        hi, mid, lo = _split3(logf)
        a = _dot(tri, hi) + (_dot(tri, mid) + _dot(tri, lo))
        a_ref[rows, :] = a
        k_ref[rows, :] = k
        return jnp.abs(a - mid_rows(a))

    def emit(rows, hs, o_h):
        if rev:
            tot = ofw_ref[rows, hs] + o_h
            g = g_ref[rows, hs]
            y = _rms(tot, nw_ref[...]) * (g * _sigmoid(g))
            o_ref[rows, hs] = y.astype(o_ref.dtype)
        else:
            o_ref[rows, hs] = o_h

    def chunk(c, exact):
        static = isinstance(c, int)
        r0 = c * HG_CHUNK if static else pl.multiple_of(c * HG_CHUNK, HG_CHUNK)
        rows = slice(r0, r0 + HG_CHUNK) if static else pl.ds(r0, HG_CHUNK)
        q = q_ref[rows, :]
        a = a_ref[rows, :]
        k = k_ref[rows, :]
        if rev:
            a_b = a[HG_HALF:HG_HALF + 1, :]
            a_last = a[0:1, :]
        else:
            a_b = a[HG_HALF - 1:HG_HALF, :]
            a_last = a[HG_CHUNK - 1:HG_CHUNK, :]
        qa = (q * jnp.exp(a)).astype(BF16)
        kdec = (k * jnp.exp(a_last - a)).astype(BF16)
        dec = jnp.exp(a_last)
        vb = v_ref[rows, :].astype(BF16)
        if exact:
            def key_row(s, attns):
                a_s = a_ref[pl.ds(r0 + s, 1), :]
                k_s = k_ref[pl.ds(r0 + s, 1), :]
                w = q * k_s * jnp.exp(jnp.minimum(a - a_s, 0.0))
                return tuple(
                    jnp.where(col == s, jnp.sum(w[:, h * HG_DK:(h + 1) * HG_DK], axis=-1, keepdims=True), attns[h])
                    for h in range(HG_HEADS))

            zero = jnp.zeros((HG_CHUNK, HG_CHUNK), F32)
            attns = lax.fori_loop(0, HG_CHUNK, key_row, (zero,) * HG_HEADS)
            m_all = jnp.logical_or(m_diag, m_off)
        else:
            a_mid = mid_rows(a)
            qd = (q * jnp.exp(jnp.minimum(a - a_mid, HG_EXP_CLAMP))).astype(BF16)
            kd = (k * jnp.exp(jnp.minimum(a_mid - a, HG_EXP_CLAMP))).astype(BF16)
            qo = (q * jnp.exp(jnp.minimum(a - a_b, 0.0))).astype(BF16)
            ko = (k * jnp.exp(jnp.minimum(a_b - a, 0.0))).astype(BF16)
        for h in range(HG_HEADS):
            hs = slice(h * HG_DK, (h + 1) * HG_DK)
            if exact:
                attn = jnp.where(m_all, attns[h], 0.0).astype(BF16)
            else:
                p1 = _dot_nt(qd[:, hs], kd[:, hs])
                p2 = _dot_nt(qo[:, hs], ko[:, hs])
                attn = jnp.where(m_diag, p1, jnp.where(m_off, p2, 0.0)).astype(BF16)
            st = st_ref[h]
            o_h = _dot_nt(qa[:, hs], st.astype(BF16)) + _dot(attn, vb[:, hs])
            st_ref[h] = dec[:, hs] * st + _dot_tn(vb[:, hs], kdec[:, hs])
            emit(rows, hs, o_h)

    spread = gates(0)
    for c in range(1, n_chunks):
        spread = jnp.maximum(spread, gates(c))
    safe = jnp.max(spread) <= HG_EXP_CLAMP

    @pl.when(safe)
    def _():
        for ci in range(n_chunks):
            chunk((n_chunks - 1 - ci) if rev else ci, exact=False)

    @pl.when(jnp.logical_not(safe))
    def _():
        def body(ci, carry):
            chunk((n_chunks - 1 - ci) if rev else ci, exact=True)
            return carry

        lax.fori_loop(0, n_chunks, body, 0)

    @pl.when(jnp.logical_and(pos == last_pos, is_ctx))
    def _():
        for h in range(HG_HEADS):
            sfin_ref[h] = st_ref[h].T


def _hgrn_dir(proj, o_fw, lb_l, state_hgrn, hg_norm_l, new_state, l, rev):
    d = 1 if rev else 0

    def tok(colblk):
        return pl.BlockSpec((HG_ROWS, W512), lambda s: (_hg_block(s, rev), colblk))

    in_specs = [tok(COL_Q_HG), tok(COL_F_BW if rev else COL_F_FW), tok(COL_I_HG)]
    args = [proj, proj, proj]
    if rev:
        in_specs += [tok(COL_G_HG), pl.BlockSpec((HG_ROWS, W512), lambda s: (_hg_block(s, rev), 0))]
        args += [proj, o_fw]
    in_specs += [
        pl.BlockSpec((None, 1, W512), lambda s: (d, 0, 0)),
        pl.BlockSpec((None, None, None, HG_HEADS, HG_DK, HG_DV),
                     lambda s: (_hg_lat_seq(_hg_block(s, rev)), l, d, 0, 0, 0)),
    ]
    args += [lb_l.reshape(2, 1, W512), state_hgrn]
    if rev:
        in_specs.append(pl.BlockSpec((1, HG_DV), lambda s: (0, 0)))
        args.append(hg_norm_l)
    in_specs.append(pl.BlockSpec(memory_space=pl.ANY))
    args.append(new_state)
    return pl.pallas_call(
        functools.partial(_hgrn_kernel, rev=rev),
        grid=(N_HG_BLK,),
        in_specs=in_specs,
        out_specs=[
            pl.BlockSpec((HG_ROWS, W512), lambda s: (_hg_block(s, rev), 0)),
            pl.BlockSpec((None, None, None, HG_HEADS, HG_DK, HG_DV),
                         lambda s: (_hg_ctx_seq(_hg_block(s, rev)), l, d, 0, 0, 0)),
        ],
        out_shape=[
            jax.ShapeDtypeStruct((N_TOK, W512), BF16 if rev else F32),
            jax.ShapeDtypeStruct((BATCH, DEPTH, 2, HG_HEADS, HG_DK, HG_DV), F32),
        ],
        input_output_aliases={len(args) - 1: 1},
        scratch_shapes=[pltpu.VMEM((HG_HEADS, HG_DV, HG_DK), F32),
                        pltpu.VMEM((HG_ROWS, W512), F32), pltpu.VMEM((HG_ROWS, W512), F32)],
        compiler_params=pltpu.CompilerParams(dimension_semantics=("arbitrary",),
                                             vmem_limit_bytes=VMEM_LIMIT),
        name="hgrn_bwd" if rev else "hgrn_fwd",
    )(*args)


def _rope_tables():
    n_pairs = DA_QK // 4
    inv_freq = ROPE_BASE ** (-jnp.arange(n_pairs, dtype=F32) / n_pairs)
    t = jnp.arange(DEC_SEQ)
    row_idx = (t // GRID_W).astype(F32)
    col_idx = (t % GRID_W).astype(F32)
    lane = jnp.arange(LANES)
    jj = lane % DA_QK
    use_col = (jj // (DA_QK // 2)) == 1
    second = (jj % (DA_QK // 2)) >= n_pairs
    fr = inv_freq[jj % n_pairs]
    ang_r = row_idx[:, None] * inv_freq[None, :]
    ang_c = col_idx[:, None] * inv_freq[None, :]
    cos_r, sin_r, cos_c, sin_c = jnp.cos(ang_r), jnp.sin(ang_r), jnp.cos(ang_c), jnp.sin(ang_c)
    fi = jj % n_pairs
    cos_t = jnp.where(use_col[None, :], cos_c[:, fi], cos_r[:, fi])
    sin_t = jnp.where(use_col[None, :], sin_c[:, fi], sin_r[:, fi])
    sin_t = jnp.where(second[None, :], sin_t, -sin_t)
    del fr
    cos_all = jnp.concatenate([jnp.ones((N_CTX_TOK, LANES), F32)] + [cos_t] * DEC_BATCH, axis=0)
    sin_all = jnp.concatenate([jnp.zeros((N_CTX_TOK, LANES), F32)] + [sin_t] * DEC_BATCH, axis=0)
    return cos_all, sin_all


def _da_prep_kernel(q_ref, k_ref, v_ref, cos_ref, sin_ref, qo_ref, ko_ref, vo_ref, kn_ref):
    cos = cos_ref[...]
    sin = sin_ref[...]
    lane = lax.broadcasted_iota(jnp.int32, (1, LANES), 1)
    first = (lane % (DA_QK // 2)) < (DA_QK // 4)
    lane_n = lax.broadcasted_iota(jnp.int32, (1, 2 * DA_HEADS), 1)

    def rope(x):
        partner = jnp.where(first, pltpu.roll(x, LANES - DA_QK // 4, 1), pltpu.roll(x, DA_QK // 4, 1))
        return x * cos + partner * sin

    kn = jnp.zeros((TM, 2 * DA_HEADS), F32)
    ones = jnp.ones((TM, DA_V), BF16)
    for h in range(DA_HEADS):
        hs = slice(h * DA_V, (h + 1) * DA_V)
        qo_ref[h] = (rope(q_ref[:, hs]) * Q_SCALE).astype(BF16)
        kr = rope(k_ref[:, hs])
        ko_ref[h] = kr.astype(BF16)
        ksq = kr * kr
        n1 = jnp.sum(jnp.where(lane < DA_QK, ksq, 0.0), axis=-1, keepdims=True)
        n2 = jnp.sum(jnp.where(lane >= DA_QK, ksq, 0.0), axis=-1, keepdims=True)
        kn = jnp.where(lane_n == 2 * h, n1, jnp.where(lane_n == 2 * h + 1, n2, kn))
        vo_ref[h, :, 0:DA_V] = v_ref[:, hs].astype(BF16)
        vo_ref[h, :, DA_V:] = ones
    kn_ref[...] = kn


def _da_prep(proj, cos_all, sin_all):
    def tok(colblk):
        return pl.BlockSpec((TM, W512), lambda i: (i, colblk))

    tab = pl.BlockSpec((TM, LANES), lambda i: (i, 0))
    out = pl.BlockSpec((DA_HEADS, TM, DA_V), lambda i: (0, i, 0))
    shp = jax.ShapeDtypeStruct((DA_HEADS, N_TOK, DA_V), BF16)
    return pl.pallas_call(
        _da_prep_kernel,
        grid=(N_TOK // TM,),
        in_specs=[tok(COL_Q_DA), tok(COL_K_DA), tok(COL_V_DA), tab, tab],
        out_specs=[out, out, pl.BlockSpec((DA_HEADS, TM, 2 * DA_V), lambda i: (0, i, 0)),
                   pl.BlockSpec((TM, 2 * DA_HEADS), lambda i: (i, 0))],
        out_shape=[shp, shp, jax.ShapeDtypeStruct((DA_HEADS, N_TOK, 2 * DA_V), BF16),
                   jax.ShapeDtypeStruct((N_TOK, 2 * DA_HEADS), F32)],
        compiler_params=pltpu.CompilerParams(dimension_semantics=("arbitrary",),
                                             vmem_limit_bytes=VMEM_LIMIT),
        name="da_prep",
    )(proj, proj, proj, cos_all, sin_all)


def _attn_kernel(*refs, has_ctx):
    if has_ctx:
        scal_ref, kmax_ref, q_ref, k_ref, v_ref, kc_ref, vc_ref, nw_ref, o_ref = refs
    else:
        scal_ref, kmax_ref, q_ref, k_ref, v_ref, nw_ref, o_ref = refs
    lam = scal_ref[0]
    out_scale = scal_ref[1]
    lane = lax.broadcasted_iota(jnp.int32, (1, LANES), 1)
    n_heads = q_ref.shape[0]
    n_keys = k_ref.shape[1]
    chunk = min(ATT_KC, n_keys)
    b = pl.program_id(0)
    if has_ctx:
        kc = kc_ref[...].astype(BF16)
        vc = vc_ref[...].astype(BF16)
        vc_ext = jnp.concatenate([vc, jnp.ones((PAST_LEN, DA_V), BF16)], axis=1)

    def shifted(q_c, h, shift):
        acc = None
        for c in range(n_keys // chunk):
            ks = slice(c * chunk, (c + 1) * chunk)
            p = jnp.exp2(_dot_nt(q_c, k_ref[h, ks, :]) - shift).astype(BF16)
            part = _dot(p, v_ref[h, ks, :])
            acc = part if acc is None else acc + part
        if has_ctx:
            acc = acc + _dot(jnp.exp2(_dot_nt(q_c, kc) - shift).astype(BF16), vc_ext)
        l = acc[:, DA_V:DA_V + 1]
        return acc[:, :DA_V] / l, l

    def exact(q_c, h):
        s = _dot_nt(q_c, k_ref[h])
        m = jnp.max(s, axis=-1, keepdims=True)
        if has_ctx:
            s_c = _dot_nt(q_c, kc)
            m = jnp.maximum(m, jnp.max(s_c, axis=-1, keepdims=True))
        p = jnp.exp2(s - m)
        l = jnp.sum(p, axis=-1, keepdims=True)
        acc = _dot(p.astype(BF16), v_ref[h, :, 0:DA_V])
        if has_ctx:
            p_c = jnp.exp2(s_c - m)
            l = l + jnp.sum(p_c, axis=-1, keepdims=True)
            acc = acc + _dot(p_c.astype(BF16), vc)
        return acc / l

    def finish(o1, o2):
        return (_rms(o1 - lam * o2, nw_ref[...]) * out_scale).astype(o_ref.dtype)

    for h in range(n_heads):
        head = h if n_heads > 1 else pl.program_id(1)
        cols = slice(None) if n_heads == 1 else slice(h * DA_V, (h + 1) * DA_V)
        qh = q_ref[h]
        zero = jnp.zeros_like(qh)
        qs = (jnp.where(lane < DA_QK, qh, zero), jnp.where(lane >= DA_QK, qh, zero))
        outs, sums = [], []
        for comp, q_c in enumerate(qs):
            qf = q_c.astype(F32)
            qn = jnp.sqrt(jnp.sum(qf * qf, axis=-1, keepdims=True))
            shift = qn * (kmax_ref[(b * DA_HEADS + head) * 2 + comp] * 1.01) + 1e-3
            o_c, l_c = shifted(q_c, h, shift)
            outs.append(o_c)
            sums.append(l_c)
        o_ref[:, cols] = finish(outs[0], outs[1])
        ok = jnp.min(jnp.minimum(sums[0], sums[1])) >= ATT_MIN_SUM

        @pl.when(jnp.logical_not(ok))
        def _():
            o_ref[:, cols] = finish(exact(qs[0], h), exact(qs[1], h))


def _attention(scal, q, k, v, knorm, cache_k, cache_v, da_norm_l, l):
    smem = pl.BlockSpec(memory_space=pltpu.SMEM)
    nw = pl.BlockSpec((1, DA_V), lambda *a: (0, 0))
    kmax_ctx = jnp.sqrt(jnp.max(knorm[:N_CTX_TOK].reshape(BATCH, SEQ, 2 * DA_HEADS), axis=1)).reshape(-1)
    ckn = jnp.sum(jnp.square(cache_k[:, l].reshape(DEC_BATCH, PAST_LEN, 2 * DA_HEADS, DA_QK)), axis=-1)
    kmax_lat = jnp.sqrt(jnp.maximum(jnp.max(knorm[N_CTX_TOK:].reshape(DEC_BATCH, DEC_SEQ, 2 * DA_HEADS), axis=1),
                                    jnp.max(ckn, axis=1))).reshape(-1)
    blk = pl.BlockSpec((DA_HEADS, SEQ, DA_V), lambda b: (0, b, 0))
    o_ctx = pl.pallas_call(
        functools.partial(_attn_kernel, has_ctx=False),
        grid=(BATCH,),
        in_specs=[smem, smem, blk, blk, pl.BlockSpec((DA_HEADS, SEQ, 2 * DA_V), lambda b: (0, b, 0)), nw],
        out_specs=pl.BlockSpec((SEQ, W512), lambda b: (b, 0)),
        out_shape=jax.ShapeDtypeStruct((N_CTX_TOK, W512), BF16),
        compiler_params=pltpu.CompilerParams(dimension_semantics=("arbitrary",), vmem_limit_bytes=VMEM_LIMIT),
        name="attn_ctx",
    )(scal, kmax_ctx, q, k, v, da_norm_l)
    nq = DEC_SEQ // ATT_TQ
    off_q = N_CTX_TOK // ATT_TQ
    off_k = N_CTX_TOK // DEC_SEQ
    ck = cache_k.reshape(DEC_BATCH, DEPTH, PAST_LEN, W512)
    cv = cache_v.reshape(DEC_BATCH, DEPTH, PAST_LEN, W512)
    ckv = pl.BlockSpec((None, None, PAST_LEN, DA_V), lambda b, h, i: (b, l, 0, h))
    o_lat = pl.pallas_call(
        functools.partial(_attn_kernel, has_ctx=True),
        grid=(DEC_BATCH, DA_HEADS, nq),
        in_specs=[smem, smem,
                  pl.BlockSpec((1, ATT_TQ, DA_V), lambda b, h, i: (h, off_q + b * nq + i, 0)),
                  pl.BlockSpec((1, DEC_SEQ, DA_V), lambda b, h, i: (h, off_k + b, 0)),
                  pl.BlockSpec((1, DEC_SEQ, 2 * DA_V), lambda b, h, i: (h, off_k + b, 0)),
                  ckv, ckv, nw],
        out_specs=pl.BlockSpec((ATT_TQ, DA_V), lambda b, h, i: (b * nq + i, h)),
        out_shape=jax.ShapeDtypeStruct((DEC_BATCH * DEC_SEQ, W512), BF16),
        compiler_params=pltpu.CompilerParams(dimension_semantics=("arbitrary", "arbitrary", "arbitrary"),
                                             vmem_limit_bytes=VMEM_LIMIT),
        name="attn_lat",
    )(scal, kmax_lat, q, k, v, ck, cv, da_norm_l)
    return o_ctx, o_lat


def _outproj_kernel(hg_ref, dac_ref, dal_ref, x_ref, mod_ref, w_ref, nw_ref, wr_ref, br_ref,
                    xo_ref, h_ref, lg_ref, wbf_ref):
    i = pl.program_id(0)

    @pl.when(i == 0)
    def _():
        wbf_ref[...] = w_ref[...].astype(BF16)

    is_ctx = i < N_CTX_TOK // TM
    da = jnp.where(is_ctx, dac_ref[...], dal_ref[...])
    mix = _dot(hg_ref[...], wbf_ref[0:HG_WIDTH, :]) + _dot(da, wbf_ref[HG_WIDTH:, :])
    g1 = mod_ref[:, 2 * D_MODEL:3 * D_MODEL]
    sh2 = mod_ref[:, 3 * D_MODEL:4 * D_MODEL]
    sc2 = mod_ref[:, 4 * D_MODEL:5 * D_MODEL]
    x = x_ref[...] + g1 * mix
    xo_ref[...] = x
    h = _rms(x, nw_ref[...]) * (1.0 + sc2) + sh2
    h_ref[...] = _pack_rows(h)
    lg_ref[...] = _dot_precise(h, wr_ref[...]) + br_ref[...]


def _outproj(o_hg, o_da_ctx, o_da_lat, x, mods_l, w_out, norm2_l, w_router_l, b_router_l, l):
    n_ctx_t = N_CTX_TOK // TM
    n_lat_t = (N_TOK - N_CTX_TOK) // TM
    tokspec = lambda w: pl.BlockSpec((TM, w), lambda i: (i, 0))
    return pl.pallas_call(
        _outproj_kernel,
        grid=(N_TOK // TM,),
        in_specs=[
            tokspec(W512),
            pl.BlockSpec((TM, W512), lambda i: (jnp.minimum(i, n_ctx_t - 1), 0)),
            pl.BlockSpec((TM, W512), lambda i: (jnp.clip(i - n_ctx_t, 0, n_lat_t - 1), 0)),
            tokspec(D_MODEL),
            pl.BlockSpec((None, 1, 6 * D_MODEL), lambda i: (i * TM // COND_GROUP, 0, 0)),
            pl.BlockSpec((None, D_MODEL, D_MODEL), lambda i: (l, 0, 0)),
            pl.BlockSpec((1, D_MODEL), lambda i: (0, 0)),
            pl.BlockSpec((D_MODEL, N_EXPERTS), lambda i: (0, 0)),
            pl.BlockSpec((1, N_EXPERTS), lambda i: (0, 0)),
        ],
        out_specs=[tokspec(D_MODEL), tokspec(D_MODEL // 2), tokspec(N_EXPERTS)],
        out_shape=[jax.ShapeDtypeStruct((N_TOK, D_MODEL), F32),
                   jax.ShapeDtypeStruct((N_TOK, D_MODEL // 2), jnp.uint32),
                   jax.ShapeDtypeStruct((N_TOK, N_EXPERTS), F32)],
        scratch_shapes=[pltpu.VMEM((D_MODEL, D_MODEL), BF16)],
        compiler_params=pltpu.CompilerParams(dimension_semantics=("arbitrary",),
                                             vmem_limit_bytes=VMEM_LIMIT),
        name="outproj_norm_router",
    )(o_hg, o_da_ctx, o_da_lat, x, mods_l, w_out, norm2_l, w_router_l, b_router_l)


def _moe_kernel(be_ref, nu_ref, tok_ref, h_hbm, wgu_ref, bgu_ref, wdn_ref, bdn_ref, o_ref,
                wgu_bf, wdn_bf, xbuf0, xbuf1, sem):
    i = pl.program_id(0)
    n_used = nu_ref[0]
    used = i < n_used
    odd = i % 2
    prev = be_ref[jnp.maximum(i - 1, 0)]
    fresh = jnp.logical_or(i == 0, be_ref[i] != prev)
    bufs = (xbuf0, xbuf1)

    def start_rows(blk, b):
        base = blk * MOE_BLK
        for r in range(MOE_BLK):
            pltpu.make_async_copy(h_hbm.at[pl.ds(tok_ref[base + r], 1), :], bufs[b].at[pl.ds(r, 1), :],
                                  sem.at[b]).start()

    def wait_rows(b):
        pltpu.make_async_copy(h_hbm.at[pl.ds(0, MOE_BLK), :], bufs[b], sem.at[b]).wait()

    @pl.when(i == 0)
    def _():
        start_rows(0, 0)

    @pl.when(jnp.logical_and(used, fresh))
    def _():
        wgu_bf[...] = wgu_ref[...].astype(BF16)
        wdn_bf[...] = wdn_ref[...].astype(BF16)

    def step(b):
        wait_rows(b)
        x_lo, x_hi = _unpack_rows(bufs[b][...])
        start_rows(jnp.minimum(i + 1, n_used - 1), 1 - b)
        x = jnp.concatenate([x_lo.astype(BF16), x_hi.astype(BF16)], axis=1)
        gu = _dot(x, wgu_bf[...]) + bgu_ref[...]
        gate = jnp.minimum(gu[:, :D_FF], SWIGLU_LIMIT)
        lin = jnp.clip(gu[:, D_FF:], -SWIGLU_LIMIT, SWIGLU_LIMIT)
        act = (lin + 1.0) * gate * _sigmoid(SWIGLU_ALPHA * gate)
        y = _dot(act.astype(BF16), wdn_bf[...]) + bdn_ref[...]
        o_ref[...] = _pack_rows(y)

    for b in range(2):
        @pl.when(jnp.logical_and(used, odd == b))
        def _():
            step(b)

        @pl.when(jnp.logical_and(i == n_used - 1, odd == b))
        def _():
            wait_rows(1 - b)

    @pl.when(jnp.logical_not(used))
    def _():
        o_ref[...] = jnp.zeros_like(o_ref)


def _moe_experts(blk_exp, n_used, slot_tok, h2, w_gu, b_gu, w_dn, b_dn, l):
    grid_spec = pltpu.PrefetchScalarGridSpec(
        num_scalar_prefetch=3,
        grid=(MOE_NBLK,),
        in_specs=[
            pl.BlockSpec(memory_space=pl.ANY),
            pl.BlockSpec((None, None, D_MODEL, 2 * D_FF), lambda i, be, nu, tk: (l, be[i], 0, 0)),
            pl.BlockSpec((None, None, 1, 2 * D_FF), lambda i, be, nu, tk: (l, be[i], 0, 0)),
            pl.BlockSpec((None, None, D_FF, D_MODEL), lambda i, be, nu, tk: (l, be[i], 0, 0)),
            pl.BlockSpec((None, None, 1, D_MODEL), lambda i, be, nu, tk: (l, be[i], 0, 0)),
        ],
        out_specs=pl.BlockSpec((MOE_BLK, D_MODEL // 2), lambda i, be, nu, tk: (i, 0)),
        scratch_shapes=[pltpu.VMEM((D_MODEL, 2 * D_FF), BF16), pltpu.VMEM((D_FF, D_MODEL), BF16),
                        pltpu.VMEM((MOE_BLK, D_MODEL // 2), jnp.uint32),
                        pltpu.VMEM((MOE_BLK, D_MODEL // 2), jnp.uint32), pltpu.SemaphoreType.DMA((2,))],
    )
    return pl.pallas_call(
        _moe_kernel,
        grid_spec=grid_spec,
        out_shape=jax.ShapeDtypeStruct((MOE_NBLK * MOE_BLK, D_MODEL // 2), jnp.uint32),
        compiler_params=pltpu.CompilerParams(dimension_semantics=("arbitrary",),
                                             vmem_limit_bytes=VMEM_LIMIT),
        name="moe_experts",
    )(blk_exp, n_used, slot_tok, h2, w_gu, b_gu.reshape(DEPTH, N_EXPERTS, 1, 2 * D_FF),
      w_dn, b_dn.reshape(DEPTH, N_EXPERTS, 1, D_MODEL))


def _route_kernel(lg_ref, exp_ref, gate_ref, rank_ref, cnt_ref, run_ref):
    i = pl.program_id(0)

    @pl.when(i == 0)
    def _():
        run_ref[...] = jnp.zeros_like(run_ref)

    lg = lg_ref[...]
    lane = lax.broadcasted_iota(jnp.int32, (1, N_EXPERTS), 1)
    lane_k = lax.broadcasted_iota(jnp.int32, (1, TOP_K), 1)
    vals, idxs, hots = [], [], []
    for _ in range(TOP_K):
        m = jnp.max(lg, axis=-1, keepdims=True)
        idx = jnp.min(jnp.where(lg == m, lane, N_EXPERTS), axis=-1, keepdims=True)
        hot = lane == idx
        vals.append(m)
        idxs.append(idx)
        hots.append(hot)
        lg = jnp.where(hot, -jnp.inf, lg)
    exps = [jnp.exp(v - vals[0]) for v in vals]
    denom = exps[0] + exps[1] + exps[2] + exps[3]
    tokhot = (hots[0] | hots[1] | hots[2] | hots[3]).astype(BF16)
    r = lax.broadcasted_iota(jnp.int32, (TM, TM), 0)
    cidx = lax.broadcasted_iota(jnp.int32, (TM, TM), 1)
    before = _dot((cidx < r).astype(BF16), tokhot) + run_ref[...]
    e_out = jnp.zeros((TM, TOP_K), jnp.int32)
    g_out = jnp.zeros((TM, TOP_K), F32)
    r_out = jnp.zeros((TM, TOP_K), jnp.int32)
    for k in range(TOP_K):
        rk = jnp.sum(jnp.where(hots[k], before, 0.0), axis=-1, keepdims=True).astype(jnp.int32)
        e_out = jnp.where(lane_k == k, idxs[k], e_out)
        g_out = jnp.where(lane_k == k, exps[k] / denom, g_out)
        r_out = jnp.where(lane_k == k, rk, r_out)
    exp_ref[...] = e_out
    gate_ref[...] = g_out
    rank_ref[...] = r_out
    run_ref[...] += jnp.sum(tokhot.astype(F32), axis=0, keepdims=True)
    cnt_ref[...] = run_ref[...].astype(jnp.int32)


def _route(logits):
    tk = pl.BlockSpec((TM, TOP_K), lambda i: (i, 0))
    top_exp, gates, rank, counts = pl.pallas_call(
        _route_kernel,
        grid=(N_TOK // TM,),
        in_specs=[pl.BlockSpec((TM, N_EXPERTS), lambda i: (i, 0))],
        out_specs=[tk, tk, tk, pl.BlockSpec((1, N_EXPERTS), lambda i: (0, 0))],
        out_shape=[jax.ShapeDtypeStruct((N_TOK, TOP_K), jnp.int32),
                   jax.ShapeDtypeStruct((N_TOK, TOP_K), F32),
                   jax.ShapeDtypeStruct((N_TOK, TOP_K), jnp.int32),
                   jax.ShapeDtypeStruct((1, N_EXPERTS), jnp.int32)],
        scratch_shapes=[pltpu.VMEM((1, N_EXPERTS), F32)],
        compiler_params=pltpu.CompilerParams(dimension_semantics=("arbitrary",)),
        name="route",
    )(logits)
    counts = counts[0]
    padded = (counts + MOE_BLK - 1) // MOE_BLK * MOE_BLK
    pad_end = jnp.cumsum(padded)
    pad_start = pad_end - padded
    start = jnp.cumsum(counts) - counts
    eq = top_exp[:, :, None] == jnp.arange(N_EXPERTS, dtype=jnp.int32)[None, None, :]
    dest = (jnp.sum(jnp.where(eq, pad_start[None, None, :], 0), axis=-1) + rank).T.reshape(-1)
    blk_exp = jnp.minimum(
        jnp.sum((pad_end[None, :] <= (jnp.arange(MOE_NBLK, dtype=jnp.int32) * MOE_BLK)[:, None]).astype(jnp.int32),
                axis=-1), N_EXPERTS - 1).astype(jnp.int32)
    n_used = (pad_end[-1:] // MOE_BLK).astype(jnp.int32)
    tok_ids = jnp.arange(N_ASG, dtype=jnp.int32) // TOP_K
    _, tok_sorted = lax.sort((top_exp.reshape(-1), tok_ids), num_keys=1, is_stable=True)
    slot = jnp.arange(MOE_NBLK * MOE_BLK, dtype=jnp.int32)
    e_slot = jnp.repeat(blk_exp, MOE_BLK)
    src = jnp.clip(start[e_slot] + slot - pad_start[e_slot], 0, N_ASG - 1)
    slot_tok = tok_sorted[src]
    return slot_tok, gates, dest, blk_exp, n_used


def _combine_kernel(y0_ref, y1_ref, y2_ref, y3_ref, gate_ref, x_ref, mod_ref, o_ref):
    g = gate_ref[...]
    half = D_MODEL // 2
    acc_lo = jnp.zeros((TM, half), F32)
    acc_hi = jnp.zeros((TM, half), F32)
    for k, y_ref in enumerate((y0_ref, y1_ref, y2_ref, y3_ref)):
        lo, hi = _unpack_rows(y_ref[...])
        acc_lo = acc_lo + g[:, k:k + 1] * lo
        acc_hi = acc_hi + g[:, k:k + 1] * hi
    g2 = mod_ref[:, 5 * D_MODEL:]
    o_ref[:, :half] = x_ref[:, :half] + g2[:, :half] * acc_lo
    o_ref[:, half:] = x_ref[:, half:] + g2[:, half:] * acc_hi


def _combine(ys_tok, gates, x_mid, mods_l):
    n_i = N_TOK // TM

    def ysblk(k):
        return pl.BlockSpec((TM, D_MODEL // 2), lambda i: (k * n_i + i, 0))

    return pl.pallas_call(
        _combine_kernel,
        grid=(n_i,),
        in_specs=[ysblk(0), ysblk(1), ysblk(2), ysblk(3),
                  pl.BlockSpec((TM, TOP_K), lambda i: (i, 0)),
                  pl.BlockSpec((TM, D_MODEL), lambda i: (i, 0)),
                  pl.BlockSpec((None, 1, 6 * D_MODEL), lambda i: (i * TM // COND_GROUP, 0, 0))],
        out_specs=pl.BlockSpec((TM, D_MODEL), lambda i: (i, 0)),
        out_shape=jax.ShapeDtypeStruct((N_TOK, D_MODEL), F32),
        compiler_params=pltpu.CompilerParams(dimension_semantics=("arbitrary",), vmem_limit_bytes=VMEM_LIMIT),
        name="moe_combine",
    )(ys_tok, ys_tok, ys_tok, ys_tok, gates, x_mid, mods_l)


def _final_norm_kernel(x_ref, w_ref, oc_ref, ol_ref):
    y = _rms(x_ref[...], w_ref[...])
    is_ctx = pl.program_id(0) < N_CTX_TILES

    @pl.when(is_ctx)
    def _():
        oc_ref[...] = y

    @pl.when(jnp.logical_not(is_ctx))
    def _():
        ol_ref[...] = y


def _final_norm(x, w):
    n_lat_tiles = N_TOK // TM - N_CTX_TILES
    return pl.pallas_call(
        _final_norm_kernel,
        grid=(N_TOK // TM,),
        in_specs=[pl.BlockSpec((TM, D_MODEL), lambda i: (i, 0)), pl.BlockSpec((1, D_MODEL), lambda i: (0, 0))],
        out_specs=[pl.BlockSpec((TM, D_MODEL), lambda i: (jnp.minimum(i, N_CTX_TILES - 1), 0)),
                   pl.BlockSpec((TM, D_MODEL), lambda i: (jnp.clip(i - N_CTX_TILES, 0, n_lat_tiles - 1), 0))],
        out_shape=[jax.ShapeDtypeStruct((N_CTX_TOK, D_MODEL), F32),
                   jax.ShapeDtypeStruct((N_TOK - N_CTX_TOK, D_MODEL), F32)],
        compiler_params=pltpu.CompilerParams(dimension_semantics=("arbitrary",)),
        name="final_norm",
    )(x, w)


def kernel(x_prompt, x_sample, c, cache_k, cache_v, state_hgrn, c_ctx, norm1_w, norm2_w, w_ada, b_ada, w_in, hg_lb, hg_norm_w, lam_q1, lam_k1, lam_q2, lam_k2, da_norm_w, w_out, w_router, b_router, w_gu, b_gu, w_dn, b_dn, norm_f_w):
    lb_all = jnp.cumsum(jax.nn.softmax(hg_lb.astype(F32), axis=0), axis=0)
    lb_all = lb_all - lb_all[0:1]
    lam_init = [0.8 - 0.6 * math.exp(-0.3 * l) for l in range(DEPTH)]
    lam = (jnp.exp(jnp.sum(lam_q1 * lam_k1, axis=-1)) - jnp.exp(jnp.sum(lam_q2 * lam_k2, axis=-1))
           + jnp.asarray(lam_init, F32))
    cos_all, sin_all = _rope_tables()

    conds = jnp.concatenate([c_ctx[None, :], c, jnp.zeros((COND_ROWS - 1 - DEC_BATCH, D_MODEL), F32)], axis=0)
    mods = _ada_mod(conds, w_ada, b_ada)
    mods = mods[:, :N_COND].reshape(DEPTH, N_COND, 1, 6 * D_MODEL)

    x = jnp.concatenate([x_prompt.reshape(N_CTX_TOK, D_MODEL), x_sample.reshape(-1, D_MODEL)], axis=0)
    new_k = jnp.zeros((BATCH, DEPTH, SEQ, W512), F32)
    new_v = jnp.zeros((BATCH, DEPTH, SEQ, W512), F32)
    new_state = jnp.zeros((BATCH, DEPTH, 2, HG_HEADS, HG_DK, HG_DV), F32)
    for l in range(DEPTH):
        proj, new_k, new_v = _inproj(x, norm1_w[l][None, :], mods[l], w_in, new_k, new_v, l)
        o_fw, new_state = _hgrn_dir(proj, None, lb_all[l], state_hgrn, None, new_state, l, rev=False)
        o_hg, new_state = _hgrn_dir(proj, o_fw, lb_all[l], state_hgrn, hg_norm_w[l][None, :], new_state, l,
                                    rev=True)
        q_r, k_r, v_b, knorm = _da_prep(proj, cos_all, sin_all)
        scal = jnp.stack([lam[l], jnp.asarray(1.0 - lam_init[l], F32)])
        o_da_ctx, o_da_lat = _attention(scal, q_r, k_r, v_b, knorm, cache_k, cache_v, da_norm_w[l][None, :], l)
        x_mid, h2, logits = _outproj(o_hg, o_da_ctx, o_da_lat, x, mods[l], w_out, norm2_w[l][None, :],
                                     w_router[l], b_router[l][None, :], l)
        slot_tok, gates, dest, blk_exp, n_used = _route(logits)
        ys = _moe_experts(blk_exp, n_used, slot_tok, h2, w_gu, b_gu, w_dn, b_dn, l)
        ys_tok = ys.at[dest].get(mode="promise_in_bounds")
        x = _combine(ys_tok, gates, x_mid, mods[l])
    y_ctx, y_lat = _final_norm(x, norm_f_w[None, :])
    return (y_ctx.reshape(BATCH, SEQ, D_MODEL), y_lat.reshape(DEC_BATCH, DEC_SEQ, D_MODEL),
            new_k.reshape(BATCH, DEPTH, SEQ, DA_HEADS, 2 * DA_QK), new_v.reshape(BATCH, DEPTH, SEQ, DA_HEADS, DA_V),
            new_state)
```

```python
import functools
import math

import jax
import jax.numpy as jnp
from jax import lax
from jax.experimental import pallas as pl
from jax.experimental.pallas import tpu as pltpu

F32 = jnp.float32
BF16 = jnp.bfloat16

D_MODEL = 1024
BATCH = 16
SEQ = 256
DEPTH = 4
DEC_BATCH = 2
DEC_SEQ = 4096
PAST_LEN = 256
GRID_W = 64
HG_HEADS = 4
HG_DK = 128
HG_DV = 128
HG_WIDTH = HG_HEADS * HG_DK
DA_HEADS = 4
DA_QK = 64
DA_V = 2 * DA_QK
DA_WIDTH = DA_HEADS * DA_V
ROPE_BASE = 10000.0
IN_DIM = 5 * HG_WIDTH + 3 * DA_WIDTH
N_EXPERTS = 32
TOP_K = 4
D_FF = D_MODEL
SWIGLU_ALPHA = 1.702
SWIGLU_LIMIT = 7.0
EPS = 1e-6

N_CTX_TOK = BATCH * SEQ
N_TOK = N_CTX_TOK + DEC_BATCH * DEC_SEQ
COND_GROUP = DEC_SEQ
assert N_CTX_TOK == COND_GROUP
N_COND = N_TOK // COND_GROUP
COND_ROWS = 8

COL_Q_HG, COL_F_FW, COL_F_BW, COL_I_HG, COL_G_HG, COL_Q_DA, COL_K_DA, COL_V_DA = range(8)
W512 = 512
LANES = 128

HG_CHUNK = 64
HG_HALF = HG_CHUNK // 2
HG_ROWS = 256
HG_EXP_CLAMP = 80.0

TM = 512
TN_IN = 2048
ATT_TQ = 256
Q_SCALE = math.log2(math.e) / math.sqrt(DA_QK)
ATT_KC = 512
ATT_MIN_SUM = 2.0 ** -40
MOE_BLK = 512
N_ASG = N_TOK * TOP_K
MOE_NBLK = N_ASG // MOE_BLK + N_EXPERTS
VMEM_LIMIT = 56 * 1024 * 1024


def _dot(a, b):
    return jnp.dot(a, b, preferred_element_type=F32)


def _dot_nt(a, b):
    return lax.dot_general(a, b, (((1,), (1,)), ((), ())), preferred_element_type=F32)


def _dot_tn(a, b):
    return lax.dot_general(a, b, (((0,), (0,)), ((), ())), preferred_element_type=F32)


def _split2(x):
    hi = x.astype(BF16)
    lo = (x - hi.astype(F32)).astype(BF16)
    return hi, lo


def _split3(x):
    hi = x.astype(BF16)
    r = x - hi.astype(F32)
    mid = r.astype(BF16)
    lo = (r - mid.astype(F32)).astype(BF16)
    return hi, mid, lo


def _dot_precise(a, b):
    a_hi, a_lo = _split2(a)
    b_hi, b_lo = _split2(b)
    return _dot(a_hi, b_hi) + (_dot(a_hi, b_lo) + _dot(a_lo, b_hi))


def _pack_rows(x):
    n = x.shape[1] // 2
    lo = pltpu.bitcast(x[:, :n].astype(BF16).astype(F32), jnp.uint32)
    hi = pltpu.bitcast(x[:, n:].astype(BF16).astype(F32), jnp.uint32)
    return (lo >> 16) | hi


def _unpack_rows(u):
    lo = pltpu.bitcast(u << 16, F32)
    hi = pltpu.bitcast(u & jnp.uint32(0xFFFF0000), F32)
    return lo, hi


def _rms(x, w):
    ms = jnp.mean(x * x, axis=-1, keepdims=True)
    return x * lax.rsqrt(ms + EPS) * w


def _sigmoid(x):
    return 1.0 / (1.0 + jnp.exp(-x))


def _ada_kernel(cond_ref, w_ref, b_ref, o_ref):
    cnd = cond_ref[...]
    s = cnd * _sigmoid(cnd)
    o_ref[...] = _dot_precise(s, w_ref[...]) + b_ref[...]


def _ada_mod(conds, w_ada, b_ada):
    nj = 6 * D_MODEL // D_MODEL
    return pl.pallas_call(
        _ada_kernel,
        grid=(DEPTH, nj),
        in_specs=[
            pl.BlockSpec((COND_ROWS, D_MODEL), lambda l, j: (0, 0)),
            pl.BlockSpec((None, D_MODEL, D_MODEL), lambda l, j: (l, 0, j)),
            pl.BlockSpec((None, 1, D_MODEL), lambda l, j: (l, 0, j)),
        ],
        out_specs=pl.BlockSpec((None, COND_ROWS, D_MODEL), lambda l, j: (l, 0, j)),
        out_shape=jax.ShapeDtypeStruct((DEPTH, COND_ROWS, 6 * D_MODEL), F32),
        compiler_params=pltpu.CompilerParams(dimension_semantics=("arbitrary", "arbitrary"),
                                             vmem_limit_bytes=VMEM_LIMIT),
        name="ada_mod",
    )(conds, w_ada, b_ada.reshape(DEPTH, 1, 6 * D_MODEL))


SEQ_PER_TILE = TM // SEQ
N_CTX_TILES = N_CTX_TOK // TM
KV_COL_BLK = (COL_K_DA * W512) // TN_IN
assert COL_K_DA * W512 == KV_COL_BLK * TN_IN + TN_IN - 2 * W512


assert IN_DIM // TN_IN == 2 and KV_COL_BLK == 1 and COL_G_HG * W512 == TN_IN


def _inproj_kernel(x_ref, nw_ref, mod_ref, w_ref, cos_ref, sin_ref, kin_ref, vin_ref,
                   hg_ref, g_ref, qo_ref, ko_ref, vo_ref, kn_ref, kc_ref, vc_ref, wbf_ref):
    del kin_ref, vin_ref
    n = pl.program_id(0)
    i = pl.program_id(1)

    @pl.when(i == 0)
    def _():
        wbf_ref[...] = w_ref[...].astype(BF16)

    x = x_ref[...]
    sh = mod_ref[:, 0:D_MODEL]
    sc = mod_ref[:, D_MODEL:2 * D_MODEL]
    h = _rms(x, nw_ref[...]) * (1.0 + sc) + sh
    o = _dot(h.astype(BF16), wbf_ref[...])

    @pl.when(n == 0)
    def _():
        hg_ref[...] = o

    @pl.when(n == 1)
    def _():
        g_ref[...] = o[:, 0:W512]
        cos = cos_ref[...]
        sin = sin_ref[...]
        lane = lax.broadcasted_iota(jnp.int32, (1, LANES), 1)
        first = (lane % (DA_QK // 2)) < (DA_QK // 4)
        lane_n = lax.broadcasted_iota(jnp.int32, (1, 2 * DA_HEADS), 1)

        def rope(t):
            partner = jnp.where(first, pltpu.roll(t, LANES - DA_QK // 4, 1), pltpu.roll(t, DA_QK // 4, 1))
            return t * cos + partner * sin

        kn = jnp.zeros((TM, 2 * DA_HEADS), F32)
        ones = jnp.ones((TM, DA_V), BF16)
        for hd in range(DA_HEADS):
            qs = slice(W512 + hd * DA_V, W512 + (hd + 1) * DA_V)
            ks = slice(2 * W512 + hd * DA_V, 2 * W512 + (hd + 1) * DA_V)
            vs = slice(3 * W512 + hd * DA_V, 3 * W512 + (hd + 1) * DA_V)
            qo_ref[hd] = (rope(o[:, qs]) * Q_SCALE).astype(BF16)
            kr = rope(o[:, ks])
            ko_ref[hd] = kr.astype(BF16)
            ksq = kr * kr
            n1 = jnp.sum(jnp.where(lane < DA_QK, ksq, 0.0), axis=-1, keepdims=True)
            n2 = jnp.sum(jnp.where(lane >= DA_QK, ksq, 0.0), axis=-1, keepdims=True)
            kn = jnp.where(lane_n == 2 * hd, n1, jnp.where(lane_n == 2 * hd + 1, n2, kn))
            vo_ref[hd, :, 0:DA_V] = o[:, vs].astype(BF16)
            vo_ref[hd, :, DA_V:] = ones
        kn_ref[...] = kn

    @pl.when(jnp.logical_and(n == 1, i < N_CTX_TILES))
    def _():
        for s in range(SEQ_PER_TILE):
            kc_ref[s] = o[s * SEQ:(s + 1) * SEQ, 2 * W512:3 * W512]
            vc_ref[s] = o[s * SEQ:(s + 1) * SEQ, 3 * W512:]


def _inproj(x, norm_w_l, mods_l, w_in, cos_all, sin_all, new_k, new_v, l):
    n_i = N_TOK // TM

    def second(n, i):
        return jnp.where(n == 1, i, 0)

    def cache_idx(n, i):
        return (jnp.where(n == 1, jnp.minimum(i, N_CTX_TILES - 1), 0), l, 0, 0)

    cache_spec = pl.BlockSpec((SEQ_PER_TILE, None, SEQ, W512), cache_idx)
    cache_shape = jax.ShapeDtypeStruct((BATCH, DEPTH, SEQ, W512), F32)
    tab = pl.BlockSpec((TM, LANES), lambda n, i: (second(n, i), 0))
    head_spec = pl.BlockSpec((DA_HEADS, TM, DA_V), lambda n, i: (0, second(n, i), 0))
    head_shape = jax.ShapeDtypeStruct((DA_HEADS, N_TOK, DA_V), BF16)
    return pl.pallas_call(
        _inproj_kernel,
        grid=(2, n_i),
        in_specs=[
            pl.BlockSpec((TM, D_MODEL), lambda n, i: (i, 0)),
            pl.BlockSpec((1, D_MODEL), lambda n, i: (0, 0)),
            pl.BlockSpec((None, 1, 6 * D_MODEL), lambda n, i: (i * TM // COND_GROUP, 0, 0)),
            pl.BlockSpec((None, D_MODEL, TN_IN), lambda n, i: (l, 0, n)),
            tab, tab,
            pl.BlockSpec(memory_space=pl.ANY),
            pl.BlockSpec(memory_space=pl.ANY),
        ],
        out_specs=[pl.BlockSpec((TM, TN_IN), lambda n, i: (jnp.where(n == 0, i, n_i - 1), 0)),
                   pl.BlockSpec((TM, W512), lambda n, i: (second(n, i), 0)),
                   head_spec, head_spec,
                   pl.BlockSpec((DA_HEADS, TM, 2 * DA_V), lambda n, i: (0, second(n, i), 0)),
                   pl.BlockSpec((TM, 2 * DA_HEADS), lambda n, i: (second(n, i), 0)),
                   cache_spec, cache_spec],
        out_shape=[jax.ShapeDtypeStruct((N_TOK, TN_IN), F32),
                   jax.ShapeDtypeStruct((N_TOK, W512), F32),
                   head_shape, head_shape,
                   jax.ShapeDtypeStruct((DA_HEADS, N_TOK, 2 * DA_V), BF16),
                   jax.ShapeDtypeStruct((N_TOK, 2 * DA_HEADS), F32),
                   cache_shape, cache_shape],
        input_output_aliases={6: 6, 7: 7},
        scratch_shapes=[pltpu.VMEM((D_MODEL, TN_IN), BF16)],
        compiler_params=pltpu.CompilerParams(dimension_semantics=("arbitrary", "arbitrary"),
                                             vmem_limit_bytes=VMEM_LIMIT),
        name="norm_inproj",
    )(x, norm_w_l, mods_l, w_in, cos_all, sin_all, new_k, new_v)


N_HG_BLK = N_TOK // HG_ROWS
N_HG_CTX_BLK = N_CTX_TOK // HG_ROWS
HG_BPS_CTX = SEQ // HG_ROWS
HG_BPS_LAT = DEC_SEQ // HG_ROWS


def _hg_block(step, rev):
    return (N_HG_BLK - 1 - step) if rev else step


def _hg_lat_seq(p):
    return jnp.clip((p - N_HG_CTX_BLK) // HG_BPS_LAT, 0, DEC_BATCH - 1)


def _hg_ctx_seq(p):
    return jnp.minimum(p // HG_BPS_CTX, BATCH - 1)


def _hgrn_kernel(*refs, rev):
    if rev:
        (q_ref, f_ref, v_ref, g_ref, ofw_ref, lb_ref, s0_ref, nw_ref, _, o_ref, sfin_ref,
         st_ref, a_ref, k_ref) = refs
    else:
        (q_ref, f_ref, v_ref, lb_ref, s0_ref, _, o_ref, sfin_ref, st_ref, a_ref, k_ref) = refs
    p = _hg_block(pl.program_id(0), rev)
    is_ctx = p < N_HG_CTX_BLK
    pos = jnp.where(is_ctx, p % HG_BPS_CTX, (p - N_HG_CTX_BLK) % HG_BPS_LAT)
    bps = jnp.where(is_ctx, HG_BPS_CTX, HG_BPS_LAT)
    first_pos = (bps - 1) if rev else 0
    last_pos = 0 if rev else (bps - 1)

    @pl.when(jnp.logical_and(pos == first_pos, is_ctx))
    def _():
        st_ref[...] = jnp.zeros_like(st_ref)

    @pl.when(jnp.logical_and(pos == first_pos, jnp.logical_not(is_ctx)))
    def _():
        for h in range(HG_HEADS):
            st_ref[h] = s0_ref[h].T

    lb = lb_ref[...]
    oml = 1.0 - lb
    log_oml = jnp.log1p(-lb)
    row = lax.broadcasted_iota(jnp.int32, (HG_CHUNK, HG_CHUNK), 0)
    col = lax.broadcasted_iota(jnp.int32, (HG_CHUNK, HG_CHUNK), 1)
    tri = ((col >= row) if rev else (col <= row)).astype(BF16)
    same_half = (row >= HG_HALF) == (col >= HG_HALF)
    if rev:
        m_diag = jnp.logical_and(same_half, col >= row)
        m_off = jnp.logical_and(row < HG_HALF, col >= HG_HALF)
    else:
        m_diag = jnp.logical_and(same_half, col <= row)
        m_off = jnp.logical_and(row >= HG_HALF, col < HG_HALF)
    row1 = lax.broadcasted_iota(jnp.int32, (HG_CHUNK, 1), 0)
    n_chunks = HG_ROWS // HG_CHUNK

    def mid_rows(a):
        return jnp.where(row1 < HG_HALF, a[HG_HALF // 2:HG_HALF // 2 + 1, :],
                         a[HG_HALF + HG_HALF // 2:HG_HALF + HG_HALF // 2 + 1, :])

    def gates(c):
        rows = slice(c * HG_CHUNK, (c + 1) * HG_CHUNK)
        fx = f_ref[rows, :]
        e = jnp.exp(-jnp.abs(fx))
        r = 1.0 / (1.0 + e)
        er = e * r
        nonneg = fx >= 0.0
        sig = jnp.where(nonneg, r, er)
        nsig = jnp.where(nonneg, er, r)
        f = lb + oml * sig
        logf = jnp.maximum(jnp.log(f), log_oml + (jnp.minimum(fx, 0.0) - jnp.log1p(e)))
        k = oml * nsig
        hi, mid, lo = _split3(logf)
        a = _dot(tri, hi) + (_dot(tri, mid) + _dot(tri, lo))
        a_ref[rows, :] = a
        k_ref[rows, :] = k
        return jnp.abs(a - mid_rows(a))

    def emit(rows, hs, o_h):
        if rev:
            tot = ofw_ref[rows, hs] + o_h
            g = g_ref[rows, hs]
            y = _rms(tot, nw_ref[...]) * (g * _sigmoid(g))
            o_ref[rows, hs] = y.astype(o_ref.dtype)
        else:
            o_ref[rows, hs] = o_h

    def chunk(c, exact):
        static = isinstance(c, int)
        r0 = c * HG_CHUNK if static else pl.multiple_of(c * HG_CHUNK, HG_CHUNK)
        rows = slice(r0, r0 + HG_CHUNK) if static else pl.ds(r0, HG_CHUNK)
        q = q_ref[rows, :]
        a = a_ref[rows, :]
        k = k_ref[rows, :]
        if rev:
            a_b = a[HG_HALF:HG_HALF + 1, :]
            a_last = a[0:1, :]
        else:
            a_b = a[HG_HALF - 1:HG_HALF, :]
            a_last = a[HG_CHUNK - 1:HG_CHUNK, :]
        qa = (q * jnp.exp(a)).astype(BF16)
        kdec = (k * jnp.exp(a_last - a)).astype(BF16)
        dec = jnp.exp(a_last)
        vb = v_ref[rows, :].astype(BF16)
        if exact:
            def key_row(s, attns):
                a_s = a_ref[pl.ds(r0 + s, 1), :]
                k_s = k_ref[pl.ds(r0 + s, 1), :]
                w = q * k_s * jnp.exp(jnp.minimum(a - a_s, 0.0))
                return tuple(
                    jnp.where(col == s, jnp.sum(w[:, h * HG_DK:(h + 1) * HG_DK], axis=-1, keepdims=True), attns[h])
                    for h in range(HG_HEADS))

            zero = jnp.zeros((HG_CHUNK, HG_CHUNK), F32)
            attns = lax.fori_loop(0, HG_CHUNK, key_row, (zero,) * HG_HEADS)
            m_all = jnp.logical_or(m_diag, m_off)
        else:
            a_mid = mid_rows(a)
            qd = (q * jnp.exp(jnp.minimum(a - a_mid, HG_EXP_CLAMP))).astype(BF16)
            kd = (k * jnp.exp(jnp.minimum(a_mid - a, HG_EXP_CLAMP))).astype(BF16)
            qo = (q * jnp.exp(jnp.minimum(a - a_b, 0.0))).astype(BF16)
            ko = (k * jnp.exp(jnp.minimum(a_b - a, 0.0))).astype(BF16)
        for h in range(HG_HEADS):
            hs = slice(h * HG_DK, (h + 1) * HG_DK)
            if exact:
                attn = jnp.where(m_all, attns[h], 0.0).astype(BF16)
            else:
                p1 = _dot_nt(qd[:, hs], kd[:, hs])
                p2 = _dot_nt(qo[:, hs], ko[:, hs])
                attn = jnp.where(m_diag, p1, jnp.where(m_off, p2, 0.0)).astype(BF16)
            st = st_ref[h]
            o_h = _dot_nt(qa[:, hs], st.astype(BF16)) + _dot(attn, vb[:, hs])
            st_ref[h] = dec[:, hs] * st + _dot_tn(vb[:, hs], kdec[:, hs])
            emit(rows, hs, o_h)

    spread = gates(0)
    for c in range(1, n_chunks):
        spread = jnp.maximum(spread, gates(c))
    safe = jnp.max(spread) <= HG_EXP_CLAMP

    @pl.when(safe)
    def _():
        for ci in range(n_chunks):
            chunk((n_chunks - 1 - ci) if rev else ci, exact=False)

    @pl.when(jnp.logical_not(safe))
    def _():
        def body(ci, carry):
            chunk((n_chunks - 1 - ci) if rev else ci, exact=True)
            return carry

        lax.fori_loop(0, n_chunks, body, 0)

    @pl.when(jnp.logical_and(pos == last_pos, is_ctx))
    def _():
        for h in range(HG_HEADS):
            sfin_ref[h] = st_ref[h].T


def _hgrn_dir(hg, g_hg, o_fw, lb_l, state_hgrn, hg_norm_l, new_state, l, rev):
    d = 1 if rev else 0

    def tok(colblk):
        return pl.BlockSpec((HG_ROWS, W512), lambda s: (_hg_block(s, rev), colblk))

    in_specs = [tok(COL_Q_HG), tok(COL_F_BW if rev else COL_F_FW), tok(COL_I_HG)]
    args = [hg, hg, hg]
    if rev:
        in_specs += [tok(0), tok(0)]
        args += [g_hg, o_fw]
    in_specs += [
        pl.BlockSpec((None, 1, W512), lambda s: (d, 0, 0)),
        pl.BlockSpec((None, None, None, HG_HEADS, HG_DK, HG_DV),
                     lambda s: (_hg_lat_seq(_hg_block(s, rev)), l, d, 0, 0, 0)),
    ]
    args += [lb_l.reshape(2, 1, W512), state_hgrn]
    if rev:
        in_specs.append(pl.BlockSpec((1, HG_DV), lambda s: (0, 0)))
        args.append(hg_norm_l)
    in_specs.append(pl.BlockSpec(memory_space=pl.ANY))
    args.append(new_state)
    return pl.pallas_call(
        functools.partial(_hgrn_kernel, rev=rev),
        grid=(N_HG_BLK,),
        in_specs=in_specs,
        out_specs=[
            pl.BlockSpec((HG_ROWS, W512), lambda s: (_hg_block(s, rev), 0)),
            pl.BlockSpec((None, None, None, HG_HEADS, HG_DK, HG_DV),
                         lambda s: (_hg_ctx_seq(_hg_block(s, rev)), l, d, 0, 0, 0)),
        ],
        out_shape=[
            jax.ShapeDtypeStruct((N_TOK, W512), BF16 if rev else F32),
            jax.ShapeDtypeStruct((BATCH, DEPTH, 2, HG_HEADS, HG_DK, HG_DV), F32),
        ],
        input_output_aliases={len(args) - 1: 1},
        scratch_shapes=[pltpu.VMEM((HG_HEADS, HG_DV, HG_DK), F32),
                        pltpu.VMEM((HG_ROWS, W512), F32), pltpu.VMEM((HG_ROWS, W512), F32)],
        compiler_params=pltpu.CompilerParams(dimension_semantics=("arbitrary",),
                                             vmem_limit_bytes=VMEM_LIMIT),
        name="hgrn_bwd" if rev else "hgrn_fwd",
    )(*args)


def _rope_tables():
    n_pairs = DA_QK // 4
    inv_freq = ROPE_BASE ** (-jnp.arange(n_pairs, dtype=F32) / n_pairs)
    t = jnp.arange(DEC_SEQ)
    row_idx = (t // GRID_W).astype(F32)
    col_idx = (t % GRID_W).astype(F32)
    lane = jnp.arange(LANES)
    jj = lane % DA_QK
    use_col = (jj // (DA_QK // 2)) == 1
    second = (jj % (DA_QK // 2)) >= n_pairs
    fr = inv_freq[jj % n_pairs]
    ang_r = row_idx[:, None] * inv_freq[None, :]
    ang_c = col_idx[:, None] * inv_freq[None, :]
    cos_r, sin_r, cos_c, sin_c = jnp.cos(ang_r), jnp.sin(ang_r), jnp.cos(ang_c), jnp.sin(ang_c)
    fi = jj % n_pairs
    cos_t = jnp.where(use_col[None, :], cos_c[:, fi], cos_r[:, fi])
    sin_t = jnp.where(use_col[None, :], sin_c[:, fi], sin_r[:, fi])
    sin_t = jnp.where(second[None, :], sin_t, -sin_t)
    del fr
    cos_all = jnp.concatenate([jnp.ones((N_CTX_TOK, LANES), F32)] + [cos_t] * DEC_BATCH, axis=0)
    sin_all = jnp.concatenate([jnp.zeros((N_CTX_TOK, LANES), F32)] + [sin_t] * DEC_BATCH, axis=0)
    return cos_all, sin_all


def _attn_kernel(*refs, has_ctx):
    if has_ctx:
        scal_ref, kmax_ref, q_ref, k_ref, v_ref, kc_ref, vc_ref, nw_ref, o_ref = refs
    else:
        scal_ref, kmax_ref, q_ref, k_ref, v_ref, nw_ref, o_ref = refs
    lam = scal_ref[0]
    out_scale = scal_ref[1]
    lane = lax.broadcasted_iota(jnp.int32, (1, LANES), 1)
    n_heads = q_ref.shape[0]
    n_keys = k_ref.shape[1]
    chunk = min(ATT_KC, n_keys)
    b = pl.program_id(0)
    if has_ctx:
        kc = kc_ref[...].astype(BF16)
        vc = vc_ref[...].astype(BF16)
        vc_ext = jnp.concatenate([vc, jnp.ones((PAST_LEN, DA_V), BF16)], axis=1)

    def shifted(q_c, h, shift):
        acc = None
        for c in range(n_keys // chunk):
            ks = slice(c * chunk, (c + 1) * chunk)
            p = jnp.exp2(_dot_nt(q_c, k_ref[h, ks, :]) - shift).astype(BF16)
            part = _dot(p, v_ref[h, ks, :])
            acc = part if acc is None else acc + part
        if has_ctx:
            acc = acc + _dot(jnp.exp2(_dot_nt(q_c, kc) - shift).astype(BF16), vc_ext)
        l = acc[:, DA_V:DA_V + 1]
        return acc[:, :DA_V] / l, l

    def exact(q_c, h):
        s = _dot_nt(q_c, k_ref[h])
        m = jnp.max(s, axis=-1, keepdims=True)
        if has_ctx:
            s_c = _dot_nt(q_c, kc)
            m = jnp.maximum(m, jnp.max(s_c, axis=-1, keepdims=True))
        p = jnp.exp2(s - m)
        l = jnp.sum(p, axis=-1, keepdims=True)
        acc = _dot(p.astype(BF16), v_ref[h, :, 0:DA_V])
        if has_ctx:
            p_c = jnp.exp2(s_c - m)
            l = l + jnp.sum(p_c, axis=-1, keepdims=True)
            acc = acc + _dot(p_c.astype(BF16), vc)
        return acc / l

    def finish(o1, o2):
        return (_rms(o1 - lam * o2, nw_ref[...]) * out_scale).astype(o_ref.dtype)

    for h in range(n_heads):
        head = h if n_heads > 1 else pl.program_id(1)
        cols = slice(None) if n_heads == 1 else slice(h * DA_V, (h + 1) * DA_V)
        qh = q_ref[h]
        zero = jnp.zeros_like(qh)
        qs = (jnp.where(lane < DA_QK, qh, zero), jnp.where(lane >= DA_QK, qh, zero))
        outs, sums = [], []
        for comp, q_c in enumerate(qs):
            qf = q_c.astype(F32)
            qn = jnp.sqrt(jnp.sum(qf * qf, axis=-1, keepdims=True))
            shift = qn * (kmax_ref[(b * DA_HEADS + head) * 2 + comp] * 1.01) + 1e-3
            o_c, l_c = shifted(q_c, h, shift)
            outs.append(o_c)
            sums.append(l_c)
        o_ref[:, cols] = finish(outs[0], outs[1])
        ok = jnp.min(jnp.minimum(sums[0], sums[1])) >= ATT_MIN_SUM

        @pl.when(jnp.logical_not(ok))
        def _():
            o_ref[:, cols] = finish(exact(qs[0], h), exact(qs[1], h))


def _attention(scal, q, k, v, knorm, cache_k, cache_v, da_norm_l, l):
    smem = pl.BlockSpec(memory_space=pltpu.SMEM)
    nw = pl.BlockSpec((1, DA_V), lambda *a: (0, 0))
    kmax_ctx = jnp.sqrt(jnp.max(knorm[:N_CTX_TOK].reshape(BATCH, SEQ, 2 * DA_HEADS), axis=1)).reshape(-1)
    ckn = jnp.sum(jnp.square(cache_k[:, l].reshape(DEC_BATCH, PAST_LEN, 2 * DA_HEADS, DA_QK)), axis=-1)
    kmax_lat = jnp.sqrt(jnp.maximum(jnp.max(knorm[N_CTX_TOK:].reshape(DEC_BATCH, DEC_SEQ, 2 * DA_HEADS), axis=1),
                                    jnp.max(ckn, axis=1))).reshape(-1)
    blk = pl.BlockSpec((DA_HEADS, SEQ, DA_V), lambda b: (0, b, 0))
    o_ctx = pl.pallas_call(
        functools.partial(_attn_kernel, has_ctx=False),
        grid=(BATCH,),
        in_specs=[smem, smem, blk, blk, pl.BlockSpec((DA_HEADS, SEQ, 2 * DA_V), lambda b: (0, b, 0)), nw],
        out_specs=pl.BlockSpec((SEQ, W512), lambda b: (b, 0)),
        out_shape=jax.ShapeDtypeStruct((N_CTX_TOK, W512), BF16),
        compiler_params=pltpu.CompilerParams(dimension_semantics=("arbitrary",), vmem_limit_bytes=VMEM_LIMIT),
        name="attn_ctx",
    )(scal, kmax_ctx, q, k, v, da_norm_l)
    nq = DEC_SEQ // ATT_TQ
    off_q = N_CTX_TOK // ATT_TQ
    off_k = N_CTX_TOK // DEC_SEQ
    ck = cache_k.reshape(DEC_BATCH, DEPTH, PAST_LEN, W512)
    cv = cache_v.reshape(DEC_BATCH, DEPTH, PAST_LEN, W512)
    ckv = pl.BlockSpec((None, None, PAST_LEN, DA_V), lambda b, h, i: (b, l, 0, h))
    o_lat = pl.pallas_call(
        functools.partial(_attn_kernel, has_ctx=True),
        grid=(DEC_BATCH, DA_HEADS, nq),
        in_specs=[smem, smem,
                  pl.BlockSpec((1, ATT_TQ, DA_V), lambda b, h, i: (h, off_q + b * nq + i, 0)),
                  pl.BlockSpec((1, DEC_SEQ, DA_V), lambda b, h, i: (h, off_k + b, 0)),
                  pl.BlockSpec((1, DEC_SEQ, 2 * DA_V), lambda b, h, i: (h, off_k + b, 0)),
                  ckv, ckv, nw],
        out_specs=pl.BlockSpec((ATT_TQ, DA_V), lambda b, h, i: (b * nq + i, h)),
        out_shape=jax.ShapeDtypeStruct((DEC_BATCH * DEC_SEQ, W512), BF16),
        compiler_params=pltpu.CompilerParams(dimension_semantics=("arbitrary", "arbitrary", "arbitrary"),
                                             vmem_limit_bytes=VMEM_LIMIT),
        name="attn_lat",
    )(scal, kmax_lat, q, k, v, ck, cv, da_norm_l)
    return o_ctx, o_lat


def _outproj_kernel(hg_ref, dac_ref, dal_ref, x_ref, mod_ref, w_ref, nw_ref, wr_ref, br_ref,
                    xo_ref, h_ref, lg_ref, wbf_ref):
    i = pl.program_id(0)

    @pl.when(i == 0)
    def _():
        wbf_ref[...] = w_ref[...].astype(BF16)

    is_ctx = i < N_CTX_TOK // TM
    da = jnp.where(is_ctx, dac_ref[...], dal_ref[...])
    mix = _dot(hg_ref[...], wbf_ref[0:HG_WIDTH, :]) + _dot(da, wbf_ref[HG_WIDTH:, :])
    g1 = mod_ref[:, 2 * D_MODEL:3 * D_MODEL]
    sh2 = mod_ref[:, 3 * D_MODEL:4 * D_MODEL]
    sc2 = mod_ref[:, 4 * D_MODEL:5 * D_MODEL]
    x = x_ref[...] + g1 * mix
    xo_ref[...] = x
    h = _rms(x, nw_ref[...]) * (1.0 + sc2) + sh2
    h_ref[...] = _pack_rows(h)
    lg_ref[...] = _dot_precise(h, wr_ref[...]) + br_ref[...]


def _outproj(o_hg, o_da_ctx, o_da_lat, x, mods_l, w_out, norm2_l, w_router_l, b_router_l, l):
    n_ctx_t = N_CTX_TOK // TM
    n_lat_t = (N_TOK - N_CTX_TOK) // TM
    tokspec = lambda w: pl.BlockSpec((TM, w), lambda i: (i, 0))
    return pl.pallas_call(
        _outproj_kernel,
        grid=(N_TOK // TM,),
        in_specs=[
            tokspec(W512),
            pl.BlockSpec((TM, W512), lambda i: (jnp.minimum(i, n_ctx_t - 1), 0)),
            pl.BlockSpec((TM, W512), lambda i: (jnp.clip(i - n_ctx_t, 0, n_lat_t - 1), 0)),
            tokspec(D_MODEL),
            pl.BlockSpec((None, 1, 6 * D_MODEL), lambda i: (i * TM // COND_GROUP, 0, 0)),
            pl.BlockSpec((None, D_MODEL, D_MODEL), lambda i: (l, 0, 0)),
            pl.BlockSpec((1, D_MODEL), lambda i: (0, 0)),
            pl.BlockSpec((D_MODEL, N_EXPERTS), lambda i: (0, 0)),
            pl.BlockSpec((1, N_EXPERTS), lambda i: (0, 0)),
        ],
        out_specs=[tokspec(D_MODEL), tokspec(D_MODEL // 2), tokspec(N_EXPERTS)],
        out_shape=[jax.ShapeDtypeStruct((N_TOK, D_MODEL), F32),
                   jax.ShapeDtypeStruct((N_TOK, D_MODEL // 2), jnp.uint32),
                   jax.ShapeDtypeStruct((N_TOK, N_EXPERTS), F32)],
        scratch_shapes=[pltpu.VMEM((D_MODEL, D_MODEL), BF16)],
        compiler_params=pltpu.CompilerParams(dimension_semantics=("arbitrary",),
                                             vmem_limit_bytes=VMEM_LIMIT),
        name="outproj_norm_router",
    )(o_hg, o_da_ctx, o_da_lat, x, mods_l, w_out, norm2_l, w_router_l, b_router_l)


def _moe_kernel(be_ref, nu_ref, tok_ref, h_hbm, wgu_ref, bgu_ref, wdn_ref, bdn_ref, o_ref,
                wgu_bf, wdn_bf, xbuf0, xbuf1, sem):
    i = pl.program_id(0)
    n_used = nu_ref[0]
    used = i < n_used
    odd = i % 2
    prev = be_ref[jnp.maximum(i - 1, 0)]
    fresh = jnp.logical_or(i == 0, be_ref[i] != prev)
    bufs = (xbuf0, xbuf1)

    def start_rows(blk, b):
        base = blk * MOE_BLK
        for r in range(MOE_BLK):
            pltpu.make_async_copy(h_hbm.at[pl.ds(tok_ref[base + r], 1), :], bufs[b].at[pl.ds(r, 1), :],
                                  sem.at[b]).start()

    def wait_rows(b):
        pltpu.make_async_copy(h_hbm.at[pl.ds(0, MOE_BLK), :], bufs[b], sem.at[b]).wait()

    @pl.when(i == 0)
    def _():
        start_rows(0, 0)

    @pl.when(jnp.logical_and(used, fresh))
    def _():
        wgu_bf[...] = wgu_ref[...].astype(BF16)
        wdn_bf[...] = wdn_ref[...].astype(BF16)

    def step(b):
        wait_rows(b)
        x_lo, x_hi = _unpack_rows(bufs[b][...])
        start_rows(jnp.minimum(i + 1, n_used - 1), 1 - b)
        x = jnp.concatenate([x_lo.astype(BF16), x_hi.astype(BF16)], axis=1)
        gu = _dot(x, wgu_bf[...]) + bgu_ref[...]
        gate = jnp.minimum(gu[:, :D_FF], SWIGLU_LIMIT)
        lin = jnp.clip(gu[:, D_FF:], -SWIGLU_LIMIT, SWIGLU_LIMIT)
        act = (lin + 1.0) * gate * _sigmoid(SWIGLU_ALPHA * gate)
        y = _dot(act.astype(BF16), wdn_bf[...]) + bdn_ref[...]
        o_ref[...] = _pack_rows(y)

    for b in range(2):
        @pl.when(jnp.logical_and(used, odd == b))
        def _():
            step(b)

        @pl.when(jnp.logical_and(i == n_used - 1, odd == b))
        def _():
            wait_rows(1 - b)

    @pl.when(jnp.logical_not(used))
    def _():
        o_ref[...] = jnp.zeros_like(o_ref)


def _moe_experts(blk_exp, n_used, slot_tok, h2, w_gu, b_gu, w_dn, b_dn, l):
    grid_spec = pltpu.PrefetchScalarGridSpec(
        num_scalar_prefetch=3,
        grid=(MOE_NBLK,),
        in_specs=[
            pl.BlockSpec(memory_space=pl.ANY),
            pl.BlockSpec((None, None, D_MODEL, 2 * D_FF), lambda i, be, nu, tk: (l, be[i], 0, 0)),
            pl.BlockSpec((None, None, 1, 2 * D_FF), lambda i, be, nu, tk: (l, be[i], 0, 0)),
            pl.BlockSpec((None, None, D_FF, D_MODEL), lambda i, be, nu, tk: (l, be[i], 0, 0)),
            pl.BlockSpec((None, None, 1, D_MODEL), lambda i, be, nu, tk: (l, be[i], 0, 0)),
        ],
        out_specs=pl.BlockSpec((MOE_BLK, D_MODEL // 2), lambda i, be, nu, tk: (i, 0)),
        scratch_shapes=[pltpu.VMEM((D_MODEL, 2 * D_FF), BF16), pltpu.VMEM((D_FF, D_MODEL), BF16),
                        pltpu.VMEM((MOE_BLK, D_MODEL // 2), jnp.uint32),
                        pltpu.VMEM((MOE_BLK, D_MODEL // 2), jnp.uint32), pltpu.SemaphoreType.DMA((2,))],
    )
    return pl.pallas_call(
        _moe_kernel,
        grid_spec=grid_spec,
        out_shape=jax.ShapeDtypeStruct((MOE_NBLK * MOE_BLK, D_MODEL // 2), jnp.uint32),
        compiler_params=pltpu.CompilerParams(dimension_semantics=("arbitrary",),
                                             vmem_limit_bytes=VMEM_LIMIT),
        name="moe_experts",
    )(blk_exp, n_used, slot_tok, h2, w_gu, b_gu.reshape(DEPTH, N_EXPERTS, 1, 2 * D_FF),
      w_dn, b_dn.reshape(DEPTH, N_EXPERTS, 1, D_MODEL))


def _route_kernel(lg_ref, exp_ref, gate_ref, rank_ref, cnt_ref, run_ref):
    i = pl.program_id(0)

    @pl.when(i == 0)
    def _():
        run_ref[...] = jnp.zeros_like(run_ref)

    lg = lg_ref[...]
    lane = lax.broadcasted_iota(jnp.int32, (1, N_EXPERTS), 1)
    lane_k = lax.broadcasted_iota(jnp.int32, (1, TOP_K), 1)
    vals, idxs, hots = [], [], []
    for _ in range(TOP_K):
        m = jnp.max(lg, axis=-1, keepdims=True)
        idx = jnp.min(jnp.where(lg == m, lane, N_EXPERTS), axis=-1, keepdims=True)
        hot = lane == idx
        vals.append(m)
        idxs.append(idx)
        hots.append(hot)
        lg = jnp.where(hot, -jnp.inf, lg)
    exps = [jnp.exp(v - vals[0]) for v in vals]
    denom = exps[0] + exps[1] + exps[2] + exps[3]
    tokhot = (hots[0] | hots[1] | hots[2] | hots[3]).astype(BF16)
    r = lax.broadcasted_iota(jnp.int32, (TM, TM), 0)
    cidx = lax.broadcasted_iota(jnp.int32, (TM, TM), 1)
    before = _dot((cidx < r).astype(BF16), tokhot) + run_ref[...]
    e_out = jnp.zeros((TM, TOP_K), jnp.int32)
    g_out = jnp.zeros((TM, TOP_K), F32)
    r_out = jnp.zeros((TM, TOP_K), jnp.int32)
    for k in range(TOP_K):
        rk = jnp.sum(jnp.where(hots[k], before, 0.0), axis=-1, keepdims=True).astype(jnp.int32)
        e_out = jnp.where(lane_k == k, idxs[k], e_out)
        g_out = jnp.where(lane_k == k, exps[k] / denom, g_out)
        r_out = jnp.where(lane_k == k, rk, r_out)
    exp_ref[...] = e_out
    gate_ref[...] = g_out
    rank_ref[...] = r_out
    run_ref[...] += jnp.sum(tokhot.astype(F32), axis=0, keepdims=True)
    cnt_ref[...] = run_ref[...].astype(jnp.int32)


def _route(logits):
    tk = pl.BlockSpec((TM, TOP_K), lambda i: (i, 0))
    top_exp, gates, rank, counts = pl.pallas_call(
        _route_kernel,
        grid=(N_TOK // TM,),
        in_specs=[pl.BlockSpec((TM, N_EXPERTS), lambda i: (i, 0))],
        out_specs=[tk, tk, tk, pl.BlockSpec((1, N_EXPERTS), lambda i: (0, 0))],
        out_shape=[jax.ShapeDtypeStruct((N_TOK, TOP_K), jnp.int32),
                   jax.ShapeDtypeStruct((N_TOK, TOP_K), F32),
                   jax.ShapeDtypeStruct((N_TOK, TOP_K), jnp.int32),
                   jax.ShapeDtypeStruct((1, N_EXPERTS), jnp.int32)],
        scratch_shapes=[pltpu.VMEM((1, N_EXPERTS), F32)],
        compiler_params=pltpu.CompilerParams(dimension_semantics=("arbitrary",)),
        name="route",
    )(logits)
    counts = counts[0]
    padded = (counts + MOE_BLK - 1) // MOE_BLK * MOE_BLK
    pad_end = jnp.cumsum(padded)
    pad_start = pad_end - padded
    start = jnp.cumsum(counts) - counts
    eq = top_exp[:, :, None] == jnp.arange(N_EXPERTS, dtype=jnp.int32)[None, None, :]
    dest = (jnp.sum(jnp.where(eq, pad_start[None, None, :], 0), axis=-1) + rank).T.reshape(-1)
    blk_exp = jnp.minimum(
        jnp.sum((pad_end[None, :] <= (jnp.arange(MOE_NBLK, dtype=jnp.int32) * MOE_BLK)[:, None]).astype(jnp.int32),
                axis=-1), N_EXPERTS - 1).astype(jnp.int32)
    n_used = (pad_end[-1:] // MOE_BLK).astype(jnp.int32)
    tok_ids = jnp.arange(N_ASG, dtype=jnp.int32) // TOP_K
    _, tok_sorted = lax.sort((top_exp.reshape(-1), tok_ids), num_keys=1, is_stable=True)
    slot = jnp.arange(MOE_NBLK * MOE_BLK, dtype=jnp.int32)
    e_slot = jnp.repeat(blk_exp, MOE_BLK)
    src = jnp.clip(start[e_slot] + slot - pad_start[e_slot], 0, N_ASG - 1)
    slot_tok = tok_sorted[src]
    return slot_tok, gates, dest, blk_exp, n_used


def _combine_kernel(y0_ref, y1_ref, y2_ref, y3_ref, gate_ref, x_ref, mod_ref, o_ref):
    g = gate_ref[...]
    half = D_MODEL // 2
    acc_lo = jnp.zeros((TM, half), F32)
    acc_hi = jnp.zeros((TM, half), F32)
    for k, y_ref in enumerate((y0_ref, y1_ref, y2_ref, y3_ref)):
        lo, hi = _unpack_rows(y_ref[...])
        acc_lo = acc_lo + g[:, k:k + 1] * lo
        acc_hi = acc_hi + g[:, k:k + 1] * hi
    g2 = mod_ref[:, 5 * D_MODEL:]
    o_ref[:, :half] = x_ref[:, :half] + g2[:, :half] * acc_lo
    o_ref[:, half:] = x_ref[:, half:] + g2[:, half:] * acc_hi


def _combine(ys_tok, gates, x_mid, mods_l):
    n_i = N_TOK // TM

    def ysblk(k):
        return pl.BlockSpec((TM, D_MODEL // 2), lambda i: (k * n_i + i, 0))

    return pl.pallas_call(
        _combine_kernel,
        grid=(n_i,),
        in_specs=[ysblk(0), ysblk(1), ysblk(2), ysblk(3),
                  pl.BlockSpec((TM, TOP_K), lambda i: (i, 0)),
                  pl.BlockSpec((TM, D_MODEL), lambda i: (i, 0)),
                  pl.BlockSpec((None, 1, 6 * D_MODEL), lambda i: (i * TM // COND_GROUP, 0, 0))],
        out_specs=pl.BlockSpec((TM, D_MODEL), lambda i: (i, 0)),
        out_shape=jax.ShapeDtypeStruct((N_TOK, D_MODEL), F32),
        compiler_params=pltpu.CompilerParams(dimension_semantics=("arbitrary",), vmem_limit_bytes=VMEM_LIMIT),
        name="moe_combine",
    )(ys_tok, ys_tok, ys_tok, ys_tok, gates, x_mid, mods_l)


def _final_norm_kernel(x_ref, w_ref, oc_ref, ol_ref):
    y = _rms(x_ref[...], w_ref[...])
    is_ctx = pl.program_id(0) < N_CTX_TILES

    @pl.when(is_ctx)
    def _():
        oc_ref[...] = y

    @pl.when(jnp.logical_not(is_ctx))
    def _():
        ol_ref[...] = y


def _final_norm(x, w):
    n_lat_tiles = N_TOK // TM - N_CTX_TILES
    return pl.pallas_call(
        _final_norm_kernel,
        grid=(N_TOK // TM,),
        in_specs=[pl.BlockSpec((TM, D_MODEL), lambda i: (i, 0)), pl.BlockSpec((1, D_MODEL), lambda i: (0, 0))],
        out_specs=[pl.BlockSpec((TM, D_MODEL), lambda i: (jnp.minimum(i, N_CTX_TILES - 1), 0)),
                   pl.BlockSpec((TM, D_MODEL), lambda i: (jnp.clip(i - N_CTX_TILES, 0, n_lat_tiles - 1), 0))],
        out_shape=[jax.ShapeDtypeStruct((N_CTX_TOK, D_MODEL), F32),
                   jax.ShapeDtypeStruct((N_TOK - N_CTX_TOK, D_MODEL), F32)],
        compiler_params=pltpu.CompilerParams(dimension_semantics=("arbitrary",)),
        name="final_norm",
    )(x, w)


def kernel(x_prompt, x_sample, c, cache_k, cache_v, state_hgrn, c_ctx, norm1_w, norm2_w, w_ada, b_ada, w_in, hg_lb, hg_norm_w, lam_q1, lam_k1, lam_q2, lam_k2, da_norm_w, w_out, w_router, b_router, w_gu, b_gu, w_dn, b_dn, norm_f_w):
    lb_all = jnp.cumsum(jax.nn.softmax(hg_lb.astype(F32), axis=0), axis=0)
    lb_all = lb_all - lb_all[0:1]
    lam_init = [0.8 - 0.6 * math.exp(-0.3 * l) for l in range(DEPTH)]
    lam = (jnp.exp(jnp.sum(lam_q1 * lam_k1, axis=-1)) - jnp.exp(jnp.sum(lam_q2 * lam_k2, axis=-1))
           + jnp.asarray(lam_init, F32))
    cos_all, sin_all = _rope_tables()

    conds = jnp.concatenate([c_ctx[None, :], c, jnp.zeros((COND_ROWS - 1 - DEC_BATCH, D_MODEL), F32)], axis=0)
    mods = _ada_mod(conds, w_ada, b_ada)
    mods = mods[:, :N_COND].reshape(DEPTH, N_COND, 1, 6 * D_MODEL)

    x = jnp.concatenate([x_prompt.reshape(N_CTX_TOK, D_MODEL), x_sample.reshape(-1, D_MODEL)], axis=0)
    new_k = jnp.zeros((BATCH, DEPTH, SEQ, W512), F32)
    new_v = jnp.zeros((BATCH, DEPTH, SEQ, W512), F32)
    new_state = jnp.zeros((BATCH, DEPTH, 2, HG_HEADS, HG_DK, HG_DV), F32)
    for l in range(DEPTH):
        hg, g_hg, q_r, k_r, v_b, knorm, new_k, new_v = _inproj(x, norm1_w[l][None, :], mods[l], w_in,
                                                               cos_all, sin_all, new_k, new_v, l)
        o_fw, new_state = _hgrn_dir(hg, g_hg, None, lb_all[l], state_hgrn, None, new_state, l, rev=False)
        o_hg, new_state = _hgrn_dir(hg, g_hg, o_fw, lb_all[l], state_hgrn, hg_norm_w[l][None, :], new_state, l,
                                    rev=True)
        scal = jnp.stack([lam[l], jnp.asarray(1.0 - lam_init[l], F32)])
        o_da_ctx, o_da_lat = _attention(scal, q_r, k_r, v_b, knorm, cache_k, cache_v, da_norm_w[l][None, :], l)
        x_mid, h2, logits = _outproj(o_hg, o_da_ctx, o_da_lat, x, mods[l], w_out, norm2_w[l][None, :],
                                     w_router[l], b_router[l][None, :], l)
        slot_tok, gates, dest, blk_exp, n_used = _route(logits)
        ys = _moe_experts(blk_exp, n_used, slot_tok, h2, w_gu, b_gu, w_dn, b_dn, l)
        ys_tok = ys.at[dest].get(mode="promise_in_bounds")
        x = _combine(ys_tok, gates, x_mid, mods[l])
    y_ctx, y_lat = _final_norm(x, norm_f_w[None, :])
    return (y_ctx.reshape(BATCH, SEQ, D_MODEL), y_lat.reshape(DEC_BATCH, DEC_SEQ, D_MODEL),
            new_k.reshape(BATCH, DEPTH, SEQ, DA_HEADS, 2 * DA_QK), new_v.reshape(BATCH, DEPTH, SEQ, DA_HEADS, DA_V),
            new_state)
```

```python
import functools
import math

import jax
import jax.numpy as jnp
from jax import lax
from jax.experimental import pallas as pl
from jax.experimental.pallas import tpu as pltpu

F32 = jnp.float32
BF16 = jnp.bfloat16

D_MODEL = 1024
BATCH = 16
SEQ = 256
DEPTH = 4
DEC_BATCH = 2
DEC_SEQ = 4096
PAST_LEN = 256
GRID_W = 64
HG_HEADS = 4
HG_DK = 128
HG_DV = 128
HG_WIDTH = HG_HEADS * HG_DK
DA_HEADS = 4
DA_QK = 64
DA_V = 2 * DA_QK
DA_WIDTH = DA_HEADS * DA_V
ROPE_BASE = 10000.0
IN_DIM = 5 * HG_WIDTH + 3 * DA_WIDTH
N_EXPERTS = 32
TOP_K = 4
D_FF = D_MODEL
SWIGLU_ALPHA = 1.702
SWIGLU_LIMIT = 7.0
EPS = 1e-6

N_CTX_TOK = BATCH * SEQ
N_TOK = N_CTX_TOK + DEC_BATCH * DEC_SEQ
COND_GROUP = DEC_SEQ
assert N_CTX_TOK == COND_GROUP
N_COND = N_TOK // COND_GROUP
COND_ROWS = 8

COL_Q_HG, COL_F_FW, COL_F_BW, COL_I_HG, COL_G_HG, COL_Q_DA, COL_K_DA, COL_V_DA = range(8)
W512 = 512
LANES = 128

HG_CHUNK = 64
HG_HALF = HG_CHUNK // 2
HG_ROWS = 256
HG_EXP_CLAMP = 80.0

TM = 512
TN_IN = 2048
ATT_TQ = 256
Q_SCALE = math.log2(math.e) / math.sqrt(DA_QK)
ATT_KC = 512
ATT_MIN_SUM = 2.0 ** -40
MOE_BLK = 512
N_ASG = N_TOK * TOP_K
MOE_NBLK = N_ASG // MOE_BLK + N_EXPERTS
VMEM_LIMIT = 56 * 1024 * 1024


def _dot(a, b):
    return jnp.dot(a, b, preferred_element_type=F32)


def _dot_nt(a, b):
    return lax.dot_general(a, b, (((1,), (1,)), ((), ())), preferred_element_type=F32)


def _dot_tn(a, b):
    return lax.dot_general(a, b, (((0,), (0,)), ((), ())), preferred_element_type=F32)


def _split2(x):
    hi = x.astype(BF16)
    lo = (x - hi.astype(F32)).astype(BF16)
    return hi, lo


def _split3(x):
    hi = x.astype(BF16)
    r = x - hi.astype(F32)
    mid = r.astype(BF16)
    lo = (r - mid.astype(F32)).astype(BF16)
    return hi, mid, lo


def _dot_precise(a, b):
    a_hi, a_lo = _split2(a)
    b_hi, b_lo = _split2(b)
    return _dot(a_hi, b_hi) + (_dot(a_hi, b_lo) + _dot(a_lo, b_hi))


def _pack_rows(x):
    n = x.shape[1] // 2
    lo = pltpu.bitcast(x[:, :n].astype(BF16).astype(F32), jnp.uint32)
    hi = pltpu.bitcast(x[:, n:].astype(BF16).astype(F32), jnp.uint32)
    return (lo >> 16) | hi


def _unpack_rows(u):
    lo = pltpu.bitcast(u << 16, F32)
    hi = pltpu.bitcast(u & jnp.uint32(0xFFFF0000), F32)
    return lo, hi


def _rms(x, w):
    ms = jnp.mean(x * x, axis=-1, keepdims=True)
    return x * lax.rsqrt(ms + EPS) * w


def _sigmoid(x):
    return 1.0 / (1.0 + jnp.exp(-x))


def _ada_kernel(cond_ref, w_ref, b_ref, o_ref):
    cnd = cond_ref[...]
    s = cnd * _sigmoid(cnd)
    o_ref[...] = _dot_precise(s, w_ref[...]) + b_ref[...]


def _ada_mod(conds, w_ada, b_ada):
    nj = 6 * D_MODEL // D_MODEL
    return pl.pallas_call(
        _ada_kernel,
        grid=(DEPTH, nj),
        in_specs=[
            pl.BlockSpec((COND_ROWS, D_MODEL), lambda l, j: (0, 0)),
            pl.BlockSpec((None, D_MODEL, D_MODEL), lambda l, j: (l, 0, j)),
            pl.BlockSpec((None, 1, D_MODEL), lambda l, j: (l, 0, j)),
        ],
        out_specs=pl.BlockSpec((None, COND_ROWS, D_MODEL), lambda l, j: (l, 0, j)),
        out_shape=jax.ShapeDtypeStruct((DEPTH, COND_ROWS, 6 * D_MODEL), F32),
        compiler_params=pltpu.CompilerParams(dimension_semantics=("arbitrary", "arbitrary"),
                                             vmem_limit_bytes=VMEM_LIMIT),
        name="ada_mod",
    )(conds, w_ada, b_ada.reshape(DEPTH, 1, 6 * D_MODEL))


SEQ_PER_TILE = TM // SEQ
N_CTX_TILES = N_CTX_TOK // TM
KV_COL_BLK = (COL_K_DA * W512) // TN_IN
assert COL_K_DA * W512 == KV_COL_BLK * TN_IN + TN_IN - 2 * W512


assert IN_DIM // TN_IN == 2 and KV_COL_BLK == 1 and COL_G_HG * W512 == TN_IN


def _inproj_kernel(x_ref, nw_ref, mod_ref, w_ref, cos_ref, sin_ref, kin_ref, vin_ref,
                   hg_ref, g_ref, qo_ref, ko_ref, vo_ref, kn_ref, kc_ref, vc_ref, wbf_ref):
    del kin_ref, vin_ref
    n = pl.program_id(0)
    i = pl.program_id(1)

    @pl.when(i == 0)
    def _():
        wbf_ref[...] = w_ref[...].astype(BF16)

    x = x_ref[...]
    sh = mod_ref[:, 0:D_MODEL]
    sc = mod_ref[:, D_MODEL:2 * D_MODEL]
    h = _rms(x, nw_ref[...]) * (1.0 + sc) + sh
    o = _dot(h.astype(BF16), wbf_ref[...])

    @pl.when(n == 0)
    def _():
        hg_ref[...] = o

    @pl.when(n == 1)
    def _():
        g_ref[...] = o[:, 0:W512]
        cos = cos_ref[...]
        sin = sin_ref[...]
        lane = lax.broadcasted_iota(jnp.int32, (1, LANES), 1)
        first = (lane % (DA_QK // 2)) < (DA_QK // 4)
        lane_n = lax.broadcasted_iota(jnp.int32, (1, 2 * DA_HEADS), 1)

        def rope(t):
            partner = jnp.where(first, pltpu.roll(t, LANES - DA_QK // 4, 1), pltpu.roll(t, DA_QK // 4, 1))
            return t * cos + partner * sin

        kn = jnp.zeros((TM, 2 * DA_HEADS), F32)
        ones = jnp.ones((TM, DA_V), BF16)
        for hd in range(DA_HEADS):
            qs = slice(W512 + hd * DA_V, W512 + (hd + 1) * DA_V)
            ks = slice(2 * W512 + hd * DA_V, 2 * W512 + (hd + 1) * DA_V)
            vs = slice(3 * W512 + hd * DA_V, 3 * W512 + (hd + 1) * DA_V)
            qo_ref[hd] = (rope(o[:, qs]) * Q_SCALE).astype(BF16)
            kr = rope(o[:, ks])
            ko_ref[hd] = kr.astype(BF16)
            ksq = kr * kr
            n1 = jnp.sum(jnp.where(lane < DA_QK, ksq, 0.0), axis=-1, keepdims=True)
            n2 = jnp.sum(jnp.where(lane >= DA_QK, ksq, 0.0), axis=-1, keepdims=True)
            kn = jnp.where(lane_n == 2 * hd, n1, jnp.where(lane_n == 2 * hd + 1, n2, kn))
            vo_ref[hd, :, 0:DA_V] = o[:, vs].astype(BF16)
            vo_ref[hd, :, DA_V:] = ones
        kn_ref[...] = kn

    @pl.when(jnp.logical_and(n == 1, i < N_CTX_TILES))
    def _():
        for s in range(SEQ_PER_TILE):
            kc_ref[s] = o[s * SEQ:(s + 1) * SEQ, 2 * W512:3 * W512]
            vc_ref[s] = o[s * SEQ:(s + 1) * SEQ, 3 * W512:]


def _inproj(x, norm_w_l, mods_l, w_in, cos_all, sin_all, new_k, new_v, l):
    n_i = N_TOK // TM

    def second(n, i):
        return jnp.where(n == 1, i, 0)

    def cache_idx(n, i):
        return (jnp.where(n == 1, jnp.minimum(i, N_CTX_TILES - 1), 0), l, 0, 0)

    cache_spec = pl.BlockSpec((SEQ_PER_TILE, None, SEQ, W512), cache_idx)
    cache_shape = jax.ShapeDtypeStruct((BATCH, DEPTH, SEQ, W512), F32)
    tab = pl.BlockSpec((TM, LANES), lambda n, i: (second(n, i), 0))
    head_spec = pl.BlockSpec((DA_HEADS, TM, DA_V), lambda n, i: (0, second(n, i), 0))
    head_shape = jax.ShapeDtypeStruct((DA_HEADS, N_TOK, DA_V), BF16)
    return pl.pallas_call(
        _inproj_kernel,
        grid=(2, n_i),
        in_specs=[
            pl.BlockSpec((TM, D_MODEL), lambda n, i: (i, 0)),
            pl.BlockSpec((1, D_MODEL), lambda n, i: (0, 0)),
            pl.BlockSpec((None, 1, 6 * D_MODEL), lambda n, i: (i * TM // COND_GROUP, 0, 0)),
            pl.BlockSpec((None, D_MODEL, TN_IN), lambda n, i: (l, 0, n)),
            tab, tab,
            pl.BlockSpec(memory_space=pl.ANY),
            pl.BlockSpec(memory_space=pl.ANY),
        ],
        out_specs=[pl.BlockSpec((TM, TN_IN), lambda n, i: (jnp.where(n == 0, i, n_i - 1), 0)),
                   pl.BlockSpec((TM, W512), lambda n, i: (second(n, i), 0)),
                   head_spec, head_spec,
                   pl.BlockSpec((DA_HEADS, TM, 2 * DA_V), lambda n, i: (0, second(n, i), 0)),
                   pl.BlockSpec((TM, 2 * DA_HEADS), lambda n, i: (second(n, i), 0)),
                   cache_spec, cache_spec],
        out_shape=[jax.ShapeDtypeStruct((N_TOK, TN_IN), F32),
                   jax.ShapeDtypeStruct((N_TOK, W512), F32),
                   head_shape, head_shape,
                   jax.ShapeDtypeStruct((DA_HEADS, N_TOK, 2 * DA_V), BF16),
                   jax.ShapeDtypeStruct((N_TOK, 2 * DA_HEADS), F32),
                   cache_shape, cache_shape],
        input_output_aliases={6: 6, 7: 7},
        scratch_shapes=[pltpu.VMEM((D_MODEL, TN_IN), BF16)],
        compiler_params=pltpu.CompilerParams(dimension_semantics=("arbitrary", "arbitrary"),
                                             vmem_limit_bytes=VMEM_LIMIT),
        name="norm_inproj",
    )(x, norm_w_l, mods_l, w_in, cos_all, sin_all, new_k, new_v)


N_HG_BLK = N_TOK // HG_ROWS
N_HG_CTX_BLK = N_CTX_TOK // HG_ROWS
HG_BPS_CTX = SEQ // HG_ROWS
HG_BPS_LAT = DEC_SEQ // HG_ROWS


def _hg_block(step, rev):
    return (N_HG_BLK - 1 - step) if rev else step


def _hg_lat_seq(p):
    return jnp.clip((p - N_HG_CTX_BLK) // HG_BPS_LAT, 0, DEC_BATCH - 1)


def _hg_ctx_seq(p):
    return jnp.minimum(p // HG_BPS_CTX, BATCH - 1)


def _hgrn_kernel(*refs, rev):
    if rev:
        (q_ref, f_ref, v_ref, g_ref, ofw_ref, lb_ref, s0_ref, nw_ref, _, o_ref, sfin_ref,
         st_ref, a_ref, k_ref) = refs
    else:
        (q_ref, f_ref, v_ref, lb_ref, s0_ref, _, o_ref, sfin_ref, st_ref, a_ref, k_ref) = refs
    p = _hg_block(pl.program_id(0), rev)
    is_ctx = p < N_HG_CTX_BLK
    pos = jnp.where(is_ctx, p % HG_BPS_CTX, (p - N_HG_CTX_BLK) % HG_BPS_LAT)
    bps = jnp.where(is_ctx, HG_BPS_CTX, HG_BPS_LAT)
    first_pos = (bps - 1) if rev else 0
    last_pos = 0 if rev else (bps - 1)

    @pl.when(jnp.logical_and(pos == first_pos, is_ctx))
    def _():
        st_ref[...] = jnp.zeros_like(st_ref)

    @pl.when(jnp.logical_and(pos == first_pos, jnp.logical_not(is_ctx)))
    def _():
        for h in range(HG_HEADS):
            st_ref[h] = s0_ref[h].T

    lb = lb_ref[...]
    oml = 1.0 - lb
    log_oml = jnp.log1p(-lb)
    row = lax.broadcasted_iota(jnp.int32, (HG_CHUNK, HG_CHUNK), 0)
    col = lax.broadcasted_iota(jnp.int32, (HG_CHUNK, HG_CHUNK), 1)
    tri = ((col >= row) if rev else (col <= row)).astype(BF16)
    same_half = (row >= HG_HALF) == (col >= HG_HALF)
    if rev:
        m_diag = jnp.logical_and(same_half, col >= row)
        m_off = jnp.logical_and(row < HG_HALF, col >= HG_HALF)
    else:
        m_diag = jnp.logical_and(same_half, col <= row)
        m_off = jnp.logical_and(row >= HG_HALF, col < HG_HALF)
    row1 = lax.broadcasted_iota(jnp.int32, (HG_CHUNK, 1), 0)
    n_chunks = HG_ROWS // HG_CHUNK

    def mid_rows(a):
        return jnp.where(row1 < HG_HALF, a[HG_HALF // 2:HG_HALF // 2 + 1, :],
                         a[HG_HALF + HG_HALF // 2:HG_HALF + HG_HALF // 2 + 1, :])

    def gates(c):
        rows = slice(c * HG_CHUNK, (c + 1) * HG_CHUNK)
        fx = f_ref[rows, :]
        e = jnp.exp(-jnp.abs(fx))
        r = 1.0 / (1.0 + e)
        er = e * r
        nonneg = fx >= 0.0
        sig = jnp.where(nonneg, r, er)
        nsig = jnp.where(nonneg, er, r)
        f = lb + oml * sig
        logf = jnp.maximum(jnp.log(f), log_oml + (jnp.minimum(fx, 0.0) - jnp.log1p(e)))
        k = oml * nsig
        hi, mid, lo = _split3(logf)
        a = _dot(tri, hi) + (_dot(tri, mid) + _dot(tri, lo))
        a_ref[rows, :] = a
        k_ref[rows, :] = k
        return jnp.abs(a - mid_rows(a))

    def emit(rows, hs, o_h):
        if rev:
            tot = ofw_ref[rows, hs] + o_h
            g = g_ref[rows, hs]
            y = _rms(tot, nw_ref[...]) * (g * _sigmoid(g))
            o_ref[rows, hs] = y.astype(o_ref.dtype)
        else:
            o_ref[rows, hs] = o_h

    def chunk(c, exact):
        static = isinstance(c, int)
        r0 = c * HG_CHUNK if static else pl.multiple_of(c * HG_CHUNK, HG_CHUNK)
        rows = slice(r0, r0 + HG_CHUNK) if static else pl.ds(r0, HG_CHUNK)
        q = q_ref[rows, :]
        a = a_ref[rows, :]
        k = k_ref[rows, :]
        if rev:
            a_b = a[HG_HALF:HG_HALF + 1, :]
            a_last = a[0:1, :]
        else:
            a_b = a[HG_HALF - 1:HG_HALF, :]
            a_last = a[HG_CHUNK - 1:HG_CHUNK, :]
        qa = (q * jnp.exp(a)).astype(BF16)
        kdec = (k * jnp.exp(a_last - a)).astype(BF16)
        dec = jnp.exp(a_last)
        vb = v_ref[rows, :].astype(BF16)
        if exact:
            def key_row(s, attns):
                a_s = a_ref[pl.ds(r0 + s, 1), :]
                k_s = k_ref[pl.ds(r0 + s, 1), :]
                w = q * k_s * jnp.exp(jnp.minimum(a - a_s, 0.0))
                return tuple(
                    jnp.where(col == s, jnp.sum(w[:, h * HG_DK:(h + 1) * HG_DK], axis=-1, keepdims=True), attns[h])
                    for h in range(HG_HEADS))

            zero = jnp.zeros((HG_CHUNK, HG_CHUNK), F32)
            attns = lax.fori_loop(0, HG_CHUNK, key_row, (zero,) * HG_HEADS)
            m_all = jnp.logical_or(m_diag, m_off)
        else:
            a_mid = mid_rows(a)
            qd = (q * jnp.exp(jnp.minimum(a - a_mid, HG_EXP_CLAMP))).astype(BF16)
            kd = (k * jnp.exp(jnp.minimum(a_mid - a, HG_EXP_CLAMP))).astype(BF16)
            qo = (q * jnp.exp(jnp.minimum(a - a_b, 0.0))).astype(BF16)
            ko = (k * jnp.exp(jnp.minimum(a_b - a, 0.0))).astype(BF16)
        for h in range(HG_HEADS):
            hs = slice(h * HG_DK, (h + 1) * HG_DK)
            if exact:
                attn = jnp.where(m_all, attns[h], 0.0).astype(BF16)
            else:
                p1 = _dot_nt(qd[:, hs], kd[:, hs])
                p2 = _dot_nt(qo[:, hs], ko[:, hs])
                attn = jnp.where(m_diag, p1, jnp.where(m_off, p2, 0.0)).astype(BF16)
            st = st_ref[h]
            o_h = _dot_nt(qa[:, hs], st.astype(BF16)) + _dot(attn, vb[:, hs])
            st_ref[h] = dec[:, hs] * st + _dot_tn(vb[:, hs], kdec[:, hs])
            emit(rows, hs, o_h)

    spread = gates(0)
    for c in range(1, n_chunks):
        spread = jnp.maximum(spread, gates(c))
    safe = jnp.max(spread) <= HG_EXP_CLAMP

    @pl.when(safe)
    def _():
        for ci in range(n_chunks):
            chunk((n_chunks - 1 - ci) if rev else ci, exact=False)

    @pl.when(jnp.logical_not(safe))
    def _():
        def body(ci, carry):
            chunk((n_chunks - 1 - ci) if rev else ci, exact=True)
            return carry

        lax.fori_loop(0, n_chunks, body, 0)

    @pl.when(jnp.logical_and(pos == last_pos, is_ctx))
    def _():
        for h in range(HG_HEADS):
            sfin_ref[h] = st_ref[h].T


def _hgrn_dir(hg, g_hg, o_fw, lb_l, state_hgrn, hg_norm_l, new_state, l, rev):
    d = 1 if rev else 0

    def tok(colblk):
        return pl.BlockSpec((HG_ROWS, W512), lambda s: (_hg_block(s, rev), colblk))

    in_specs = [tok(COL_Q_HG), tok(COL_F_BW if rev else COL_F_FW), tok(COL_I_HG)]
    args = [hg, hg, hg]
    if rev:
        in_specs += [tok(0), tok(0)]
        args += [g_hg, o_fw]
    in_specs += [
        pl.BlockSpec((None, 1, W512), lambda s: (d, 0, 0)),
        pl.BlockSpec((None, None, None, HG_HEADS, HG_DK, HG_DV),
                     lambda s: (_hg_lat_seq(_hg_block(s, rev)), l, d, 0, 0, 0)),
    ]
    args += [lb_l.reshape(2, 1, W512), state_hgrn]
    if rev:
        in_specs.append(pl.BlockSpec((1, HG_DV), lambda s: (0, 0)))
        args.append(hg_norm_l)
    in_specs.append(pl.BlockSpec(memory_space=pl.ANY))
    args.append(new_state)
    return pl.pallas_call(
        functools.partial(_hgrn_kernel, rev=rev),
        grid=(N_HG_BLK,),
        in_specs=in_specs,
        out_specs=[
            pl.BlockSpec((HG_ROWS, W512), lambda s: (_hg_block(s, rev), 0)),
            pl.BlockSpec((None, None, None, HG_HEADS, HG_DK, HG_DV),
                         lambda s: (_hg_ctx_seq(_hg_block(s, rev)), l, d, 0, 0, 0)),
        ],
        out_shape=[
            jax.ShapeDtypeStruct((N_TOK, W512), BF16 if rev else F32),
            jax.ShapeDtypeStruct((BATCH, DEPTH, 2, HG_HEADS, HG_DK, HG_DV), F32),
        ],
        input_output_aliases={len(args) - 1: 1},
        scratch_shapes=[pltpu.VMEM((HG_HEADS, HG_DV, HG_DK), F32),
                        pltpu.VMEM((HG_ROWS, W512), F32), pltpu.VMEM((HG_ROWS, W512), F32)],
        compiler_params=pltpu.CompilerParams(dimension_semantics=("arbitrary",),
                                             vmem_limit_bytes=VMEM_LIMIT),
        name="hgrn_bwd" if rev else "hgrn_fwd",
    )(*args)


def _rope_tables():
    n_pairs = DA_QK // 4
    inv_freq = ROPE_BASE ** (-jnp.arange(n_pairs, dtype=F32) / n_pairs)
    t = jnp.arange(DEC_SEQ)
    row_idx = (t // GRID_W).astype(F32)
    col_idx = (t % GRID_W).astype(F32)
    lane = jnp.arange(LANES)
    jj = lane % DA_QK
    use_col = (jj // (DA_QK // 2)) == 1
    second = (jj % (DA_QK // 2)) >= n_pairs
    fr = inv_freq[jj % n_pairs]
    ang_r = row_idx[:, None] * inv_freq[None, :]
    ang_c = col_idx[:, None] * inv_freq[None, :]
    cos_r, sin_r, cos_c, sin_c = jnp.cos(ang_r), jnp.sin(ang_r), jnp.cos(ang_c), jnp.sin(ang_c)
    fi = jj % n_pairs
    cos_t = jnp.where(use_col[None, :], cos_c[:, fi], cos_r[:, fi])
    sin_t = jnp.where(use_col[None, :], sin_c[:, fi], sin_r[:, fi])
    sin_t = jnp.where(second[None, :], sin_t, -sin_t)
    del fr
    cos_all = jnp.concatenate([jnp.ones((N_CTX_TOK, LANES), F32)] + [cos_t] * DEC_BATCH, axis=0)
    sin_all = jnp.concatenate([jnp.zeros((N_CTX_TOK, LANES), F32)] + [sin_t] * DEC_BATCH, axis=0)
    return cos_all, sin_all


def _attn_kernel(*refs, has_ctx):
    if has_ctx:
        scal_ref, kmax_ref, q_ref, k_ref, v_ref, kc_ref, vc_ref, nw_ref, o_ref = refs
    else:
        scal_ref, kmax_ref, q_ref, k_ref, v_ref, nw_ref, o_ref = refs
    lam = scal_ref[0]
    out_scale = scal_ref[1]
    lane = lax.broadcasted_iota(jnp.int32, (1, LANES), 1)
    n_heads = q_ref.shape[0]
    n_keys = k_ref.shape[1]
    chunk = min(ATT_KC, n_keys)
    b = pl.program_id(0)
    if has_ctx:
        kc = kc_ref[...].astype(BF16)
        vc = vc_ref[...].astype(BF16)
        vc_ext = jnp.concatenate([vc, jnp.ones((PAST_LEN, DA_V), BF16)], axis=1)

    def shifted(q_c, h, shift):
        acc = None
        for c in range(n_keys // chunk):
            ks = slice(c * chunk, (c + 1) * chunk)
            p = jnp.exp2(_dot_nt(q_c, k_ref[h, ks, :]) - shift).astype(BF16)
            part = _dot(p, v_ref[h, ks, :])
            acc = part if acc is None else acc + part
        if has_ctx:
            acc = acc + _dot(jnp.exp2(_dot_nt(q_c, kc) - shift).astype(BF16), vc_ext)
        l = acc[:, DA_V:DA_V + 1]
        return acc[:, :DA_V] / l, l

    def exact(q_c, h):
        s = _dot_nt(q_c, k_ref[h])
        m = jnp.max(s, axis=-1, keepdims=True)
        if has_ctx:
            s_c = _dot_nt(q_c, kc)
            m = jnp.maximum(m, jnp.max(s_c, axis=-1, keepdims=True))
        p = jnp.exp2(s - m)
        l = jnp.sum(p, axis=-1, keepdims=True)
        acc = _dot(p.astype(BF16), v_ref[h, :, 0:DA_V])
        if has_ctx:
            p_c = jnp.exp2(s_c - m)
            l = l + jnp.sum(p_c, axis=-1, keepdims=True)
            acc = acc + _dot(p_c.astype(BF16), vc)
        return acc / l

    def finish(o1, o2):
        return (_rms(o1 - lam * o2, nw_ref[...]) * out_scale).astype(o_ref.dtype)

    for h in range(n_heads):
        head = h if n_heads > 1 else pl.program_id(1)
        cols = slice(None) if n_heads == 1 else slice(h * DA_V, (h + 1) * DA_V)
        qh = q_ref[h]
        zero = jnp.zeros_like(qh)
        qs = (jnp.where(lane < DA_QK, qh, zero), jnp.where(lane >= DA_QK, qh, zero))
        outs, sums = [], []
        for comp, q_c in enumerate(qs):
            qf = q_c.astype(F32)
            qn = jnp.sqrt(jnp.sum(qf * qf, axis=-1, keepdims=True))
            shift = qn * (kmax_ref[(b * DA_HEADS + head) * 2 + comp] * 1.01) + 1e-3
            o_c, l_c = shifted(q_c, h, shift)
            outs.append(o_c)
            sums.append(l_c)
        o_ref[:, cols] = finish(outs[0], outs[1])
        ok = jnp.min(jnp.minimum(sums[0], sums[1])) >= ATT_MIN_SUM

        @pl.when(jnp.logical_not(ok))
        def _():
            o_ref[:, cols] = finish(exact(qs[0], h), exact(qs[1], h))


def _attention(scal, q, k, v, knorm, cache_k, cache_v, da_norm_l, l):
    smem = pl.BlockSpec(memory_space=pltpu.SMEM)
    nw = pl.BlockSpec((1, DA_V), lambda *a: (0, 0))
    kmax_ctx = jnp.sqrt(jnp.max(knorm[:N_CTX_TOK].reshape(BATCH, SEQ, 2 * DA_HEADS), axis=1)).reshape(-1)
    ckn = jnp.sum(jnp.square(cache_k[:, l].reshape(DEC_BATCH, PAST_LEN, 2 * DA_HEADS, DA_QK)), axis=-1)
    kmax_lat = jnp.sqrt(jnp.maximum(jnp.max(knorm[N_CTX_TOK:].reshape(DEC_BATCH, DEC_SEQ, 2 * DA_HEADS), axis=1),
                                    jnp.max(ckn, axis=1))).reshape(-1)
    blk = pl.BlockSpec((DA_HEADS, SEQ, DA_V), lambda b: (0, b, 0))
    o_ctx = pl.pallas_call(
        functools.partial(_attn_kernel, has_ctx=False),
        grid=(BATCH,),
        in_specs=[smem, smem, blk, blk, pl.BlockSpec((DA_HEADS, SEQ, 2 * DA_V), lambda b: (0, b, 0)), nw],
        out_specs=pl.BlockSpec((SEQ, W512), lambda b: (b, 0)),
        out_shape=jax.ShapeDtypeStruct((N_CTX_TOK, W512), BF16),
        compiler_params=pltpu.CompilerParams(dimension_semantics=("arbitrary",), vmem_limit_bytes=VMEM_LIMIT),
        name="attn_ctx",
    )(scal, kmax_ctx, q, k, v, da_norm_l)
    nq = DEC_SEQ // ATT_TQ
    off_q = N_CTX_TOK // ATT_TQ
    off_k = N_CTX_TOK // DEC_SEQ
    ck = cache_k.reshape(DEC_BATCH, DEPTH, PAST_LEN, W512)
    cv = cache_v.reshape(DEC_BATCH, DEPTH, PAST_LEN, W512)
    ckv = pl.BlockSpec((None, None, PAST_LEN, DA_V), lambda b, h, i: (b, l, 0, h))
    o_lat = pl.pallas_call(
        functools.partial(_attn_kernel, has_ctx=True),
        grid=(DEC_BATCH, DA_HEADS, nq),
        in_specs=[smem, smem,
                  pl.BlockSpec((1, ATT_TQ, DA_V), lambda b, h, i: (h, off_q + b * nq + i, 0)),
                  pl.BlockSpec((1, DEC_SEQ, DA_V), lambda b, h, i: (h, off_k + b, 0)),
                  pl.BlockSpec((1, DEC_SEQ, 2 * DA_V), lambda b, h, i: (h, off_k + b, 0)),
                  ckv, ckv, nw],
        out_specs=pl.BlockSpec((ATT_TQ, DA_V), lambda b, h, i: (b * nq + i, h)),
        out_shape=jax.ShapeDtypeStruct((DEC_BATCH * DEC_SEQ, W512), BF16),
        compiler_params=pltpu.CompilerParams(dimension_semantics=("arbitrary", "arbitrary", "arbitrary"),
                                             vmem_limit_bytes=VMEM_LIMIT),
        name="attn_lat",
    )(scal, kmax_lat, q, k, v, ck, cv, da_norm_l)
    return o_ctx, o_lat


def _outproj_kernel(hg_ref, dac_ref, dal_ref, x_ref, mod_ref, w_ref, nw_ref, wr_ref, br_ref,
                    xo_ref, h_ref, lg_ref, wbf_ref):
    i = pl.program_id(0)

    @pl.when(i == 0)
    def _():
        wbf_ref[...] = w_ref[...].astype(BF16)

    is_ctx = i < N_CTX_TOK // TM
    da = jnp.where(is_ctx, dac_ref[...], dal_ref[...])
    mix = _dot(hg_ref[...], wbf_ref[0:HG_WIDTH, :]) + _dot(da, wbf_ref[HG_WIDTH:, :])
    g1 = mod_ref[:, 2 * D_MODEL:3 * D_MODEL]
    sh2 = mod_ref[:, 3 * D_MODEL:4 * D_MODEL]
    sc2 = mod_ref[:, 4 * D_MODEL:5 * D_MODEL]
    x = x_ref[...] + g1 * mix
    xo_ref[...] = x
    h = _rms(x, nw_ref[...]) * (1.0 + sc2) + sh2
    h_ref[...] = _pack_rows(h)
    lg_ref[...] = _dot_precise(h, wr_ref[...]) + br_ref[...]


def _outproj(o_hg, o_da_ctx, o_da_lat, x, mods_l, w_out, norm2_l, w_router_l, b_router_l, l):
    n_ctx_t = N_CTX_TOK // TM
    n_lat_t = (N_TOK - N_CTX_TOK) // TM
    tokspec = lambda w: pl.BlockSpec((TM, w), lambda i: (i, 0))
    return pl.pallas_call(
        _outproj_kernel,
        grid=(N_TOK // TM,),
        in_specs=[
            tokspec(W512),
            pl.BlockSpec((TM, W512), lambda i: (jnp.minimum(i, n_ctx_t - 1), 0)),
            pl.BlockSpec((TM, W512), lambda i: (jnp.clip(i - n_ctx_t, 0, n_lat_t - 1), 0)),
            tokspec(D_MODEL),
            pl.BlockSpec((None, 1, 6 * D_MODEL), lambda i: (i * TM // COND_GROUP, 0, 0)),
            pl.BlockSpec((None, D_MODEL, D_MODEL), lambda i: (l, 0, 0)),
            pl.BlockSpec((1, D_MODEL), lambda i: (0, 0)),
            pl.BlockSpec((D_MODEL, N_EXPERTS), lambda i: (0, 0)),
            pl.BlockSpec((1, N_EXPERTS), lambda i: (0, 0)),
        ],
        out_specs=[tokspec(D_MODEL), tokspec(D_MODEL // 2), tokspec(N_EXPERTS)],
        out_shape=[jax.ShapeDtypeStruct((N_TOK, D_MODEL), F32),
                   jax.ShapeDtypeStruct((N_TOK, D_MODEL // 2), jnp.uint32),
                   jax.ShapeDtypeStruct((N_TOK, N_EXPERTS), F32)],
        scratch_shapes=[pltpu.VMEM((D_MODEL, D_MODEL), BF16)],
        compiler_params=pltpu.CompilerParams(dimension_semantics=("arbitrary",),
                                             vmem_limit_bytes=VMEM_LIMIT),
        name="outproj_norm_router",
    )(o_hg, o_da_ctx, o_da_lat, x, mods_l, w_out, norm2_l, w_router_l, b_router_l)


def _moe_kernel(be_ref, nu_ref, src_ref, tok_ref, h_hbm, wgu_ref, bgu_ref, wdn_ref, bdn_ref, o_ref,
                wgu_bf, wdn_bf, xbuf0, xbuf1, sem):
    i = pl.program_id(0)
    n_used = nu_ref[0]
    used = i < n_used
    odd = i % 2
    prev = be_ref[jnp.maximum(i - 1, 0)]
    fresh = jnp.logical_or(i == 0, be_ref[i] != prev)
    bufs = (xbuf0, xbuf1)

    def start_rows(blk, b):
        base = src_ref[blk]
        for r in range(MOE_BLK):
            pltpu.make_async_copy(h_hbm.at[pl.ds(tok_ref[base + r], 1), :], bufs[b].at[pl.ds(r, 1), :],
                                  sem.at[b]).start()

    def wait_rows(b):
        pltpu.make_async_copy(h_hbm.at[pl.ds(0, MOE_BLK), :], bufs[b], sem.at[b]).wait()

    @pl.when(i == 0)
    def _():
        start_rows(0, 0)

    @pl.when(jnp.logical_and(used, fresh))
    def _():
        wgu_bf[...] = wgu_ref[...].astype(BF16)
        wdn_bf[...] = wdn_ref[...].astype(BF16)

    def step(b):
        wait_rows(b)
        x_lo, x_hi = _unpack_rows(bufs[b][...])
        start_rows(jnp.minimum(i + 1, n_used - 1), 1 - b)
        x = jnp.concatenate([x_lo.astype(BF16), x_hi.astype(BF16)], axis=1)
        gu = _dot(x, wgu_bf[...]) + bgu_ref[...]
        gate = jnp.minimum(gu[:, :D_FF], SWIGLU_LIMIT)
        lin = jnp.clip(gu[:, D_FF:], -SWIGLU_LIMIT, SWIGLU_LIMIT)
        act = (lin + 1.0) * gate * _sigmoid(SWIGLU_ALPHA * gate)
        y = _dot(act.astype(BF16), wdn_bf[...]) + bdn_ref[...]
        o_ref[...] = _pack_rows(y)

    for b in range(2):
        @pl.when(jnp.logical_and(used, odd == b))
        def _():
            step(b)

        @pl.when(jnp.logical_and(i == n_used - 1, odd == b))
        def _():
            wait_rows(1 - b)

    @pl.when(jnp.logical_not(used))
    def _():
        o_ref[...] = jnp.zeros_like(o_ref)


def _moe_experts(blk_exp, n_used, blk_src, tok_sorted, h2, w_gu, b_gu, w_dn, b_dn, l):
    grid_spec = pltpu.PrefetchScalarGridSpec(
        num_scalar_prefetch=4,
        grid=(MOE_NBLK,),
        in_specs=[
            pl.BlockSpec(memory_space=pl.ANY),
            pl.BlockSpec((None, None, D_MODEL, 2 * D_FF), lambda i, be, nu, sr, tk: (l, be[i], 0, 0)),
            pl.BlockSpec((None, None, 1, 2 * D_FF), lambda i, be, nu, sr, tk: (l, be[i], 0, 0)),
            pl.BlockSpec((None, None, D_FF, D_MODEL), lambda i, be, nu, sr, tk: (l, be[i], 0, 0)),
            pl.BlockSpec((None, None, 1, D_MODEL), lambda i, be, nu, sr, tk: (l, be[i], 0, 0)),
        ],
        out_specs=pl.BlockSpec((MOE_BLK, D_MODEL // 2), lambda i, be, nu, sr, tk: (i, 0)),
        scratch_shapes=[pltpu.VMEM((D_MODEL, 2 * D_FF), BF16), pltpu.VMEM((D_FF, D_MODEL), BF16),
                        pltpu.VMEM((MOE_BLK, D_MODEL // 2), jnp.uint32),
                        pltpu.VMEM((MOE_BLK, D_MODEL // 2), jnp.uint32), pltpu.SemaphoreType.DMA((2,))],
    )
    return pl.pallas_call(
        _moe_kernel,
        grid_spec=grid_spec,
        out_shape=jax.ShapeDtypeStruct((MOE_NBLK * MOE_BLK, D_MODEL // 2), jnp.uint32),
        compiler_params=pltpu.CompilerParams(dimension_semantics=("arbitrary",),
                                             vmem_limit_bytes=VMEM_LIMIT),
        name="moe_experts",
    )(blk_exp, n_used, blk_src, tok_sorted, h2, w_gu, b_gu.reshape(DEPTH, N_EXPERTS, 1, 2 * D_FF),
      w_dn, b_dn.reshape(DEPTH, N_EXPERTS, 1, D_MODEL))


def _route_kernel(lg_ref, exp_ref, gate_ref, rank_ref, cnt_ref, run_ref):
    i = pl.program_id(0)

    @pl.when(i == 0)
    def _():
        run_ref[...] = jnp.zeros_like(run_ref)

    lg = lg_ref[...]
    lane = lax.broadcasted_iota(jnp.int32, (1, N_EXPERTS), 1)
    lane_k = lax.broadcasted_iota(jnp.int32, (1, TOP_K), 1)
    vals, idxs, hots = [], [], []
    for _ in range(TOP_K):
        m = jnp.max(lg, axis=-1, keepdims=True)
        idx = jnp.min(jnp.where(lg == m, lane, N_EXPERTS), axis=-1, keepdims=True)
        hot = lane == idx
        vals.append(m)
        idxs.append(idx)
        hots.append(hot)
        lg = jnp.where(hot, -jnp.inf, lg)
    exps = [jnp.exp(v - vals[0]) for v in vals]
    denom = exps[0] + exps[1] + exps[2] + exps[3]
    tokhot = (hots[0] | hots[1] | hots[2] | hots[3]).astype(BF16)
    r = lax.broadcasted_iota(jnp.int32, (TM, TM), 0)
    cidx = lax.broadcasted_iota(jnp.int32, (TM, TM), 1)
    before = _dot((cidx < r).astype(BF16), tokhot) + run_ref[...]
    e_out = jnp.zeros((TM, TOP_K), jnp.int32)
    g_out = jnp.zeros((TM, TOP_K), F32)
    r_out = jnp.zeros((TM, TOP_K), jnp.int32)
    for k in range(TOP_K):
        rk = jnp.sum(jnp.where(hots[k], before, 0.0), axis=-1, keepdims=True).astype(jnp.int32)
        e_out = jnp.where(lane_k == k, idxs[k], e_out)
        g_out = jnp.where(lane_k == k, exps[k] / denom, g_out)
        r_out = jnp.where(lane_k == k, rk, r_out)
    exp_ref[...] = e_out
    gate_ref[...] = g_out
    rank_ref[...] = r_out
    run_ref[...] += jnp.sum(tokhot.astype(F32), axis=0, keepdims=True)
    cnt_ref[...] = run_ref[...].astype(jnp.int32)


def _route(logits):
    tk = pl.BlockSpec((TM, TOP_K), lambda i: (i, 0))
    top_exp, gates, rank, counts = pl.pallas_call(
        _route_kernel,
        grid=(N_TOK // TM,),
        in_specs=[pl.BlockSpec((TM, N_EXPERTS), lambda i: (i, 0))],
        out_specs=[tk, tk, tk, pl.BlockSpec((1, N_EXPERTS), lambda i: (0, 0))],
        out_shape=[jax.ShapeDtypeStruct((N_TOK, TOP_K), jnp.int32),
                   jax.ShapeDtypeStruct((N_TOK, TOP_K), F32),
                   jax.ShapeDtypeStruct((N_TOK, TOP_K), jnp.int32),
                   jax.ShapeDtypeStruct((1, N_EXPERTS), jnp.int32)],
        scratch_shapes=[pltpu.VMEM((1, N_EXPERTS), F32)],
        compiler_params=pltpu.CompilerParams(dimension_semantics=("arbitrary",)),
        name="route",
    )(logits)
    counts = counts[0]
    padded = (counts + MOE_BLK - 1) // MOE_BLK * MOE_BLK
    pad_end = jnp.cumsum(padded)
    pad_start = pad_end - padded
    start = jnp.cumsum(counts) - counts
    eq = top_exp[:, :, None] == jnp.arange(N_EXPERTS, dtype=jnp.int32)[None, None, :]
    dest = (jnp.sum(jnp.where(eq, pad_start[None, None, :], 0), axis=-1) + rank).T.reshape(-1)
    blk_exp = jnp.minimum(
        jnp.sum((pad_end[None, :] <= (jnp.arange(MOE_NBLK, dtype=jnp.int32) * MOE_BLK)[:, None]).astype(jnp.int32),
                axis=-1), N_EXPERTS - 1).astype(jnp.int32)
    n_used = (pad_end[-1:] // MOE_BLK).astype(jnp.int32)
    tok_ids = jnp.arange(N_ASG, dtype=jnp.int32) // TOP_K
    _, tok_sorted = lax.sort((top_exp.reshape(-1), tok_ids), num_keys=1, is_stable=True)
    blk_src = (start[blk_exp] + jnp.arange(MOE_NBLK, dtype=jnp.int32) * MOE_BLK - pad_start[blk_exp]).astype(jnp.int32)
    tok_sorted = jnp.concatenate([tok_sorted, jnp.zeros((MOE_BLK,), jnp.int32)])
    return tok_sorted, blk_src, gates, dest, blk_exp, n_used


def _combine_kernel(y0_ref, y1_ref, y2_ref, y3_ref, gate_ref, x_ref, mod_ref, o_ref):
    g = gate_ref[...]
    half = D_MODEL // 2
    acc_lo = jnp.zeros((TM, half), F32)
    acc_hi = jnp.zeros((TM, half), F32)
    for k, y_ref in enumerate((y0_ref, y1_ref, y2_ref, y3_ref)):
        lo, hi = _unpack_rows(y_ref[...])
        acc_lo = acc_lo + g[:, k:k + 1] * lo
        acc_hi = acc_hi + g[:, k:k + 1] * hi
    g2 = mod_ref[:, 5 * D_MODEL:]
    o_ref[:, :half] = x_ref[:, :half] + g2[:, :half] * acc_lo
    o_ref[:, half:] = x_ref[:, half:] + g2[:, half:] * acc_hi


def _combine(ys_tok, gates, x_mid, mods_l):
    n_i = N_TOK // TM

    def ysblk(k):
        return pl.BlockSpec((TM, D_MODEL // 2), lambda i: (k * n_i + i, 0))

    return pl.pallas_call(
        _combine_kernel,
        grid=(n_i,),
        in_specs=[ysblk(0), ysblk(1), ysblk(2), ysblk(3),
                  pl.BlockSpec((TM, TOP_K), lambda i: (i, 0)),
                  pl.BlockSpec((TM, D_MODEL), lambda i: (i, 0)),
                  pl.BlockSpec((None, 1, 6 * D_MODEL), lambda i: (i * TM // COND_GROUP, 0, 0))],
        out_specs=pl.BlockSpec((TM, D_MODEL), lambda i: (i, 0)),
        out_shape=jax.ShapeDtypeStruct((N_TOK, D_MODEL), F32),
        compiler_params=pltpu.CompilerParams(dimension_semantics=("arbitrary",), vmem_limit_bytes=VMEM_LIMIT),
        name="moe_combine",
    )(ys_tok, ys_tok, ys_tok, ys_tok, gates, x_mid, mods_l)


def _final_norm_kernel(x_ref, w_ref, oc_ref, ol_ref):
    y = _rms(x_ref[...], w_ref[...])
    is_ctx = pl.program_id(0) < N_CTX_TILES

    @pl.when(is_ctx)
    def _():
        oc_ref[...] = y

    @pl.when(jnp.logical_not(is_ctx))
    def _():
        ol_ref[...] = y


def _final_norm(x, w):
    n_lat_tiles = N_TOK // TM - N_CTX_TILES
    return pl.pallas_call(
        _final_norm_kernel,
        grid=(N_TOK // TM,),
        in_specs=[pl.BlockSpec((TM, D_MODEL), lambda i: (i, 0)), pl.BlockSpec((1, D_MODEL), lambda i: (0, 0))],
        out_specs=[pl.BlockSpec((TM, D_MODEL), lambda i: (jnp.minimum(i, N_CTX_TILES - 1), 0)),
                   pl.BlockSpec((TM, D_MODEL), lambda i: (jnp.clip(i - N_CTX_TILES, 0, n_lat_tiles - 1), 0))],
        out_shape=[jax.ShapeDtypeStruct((N_CTX_TOK, D_MODEL), F32),
                   jax.ShapeDtypeStruct((N_TOK - N_CTX_TOK, D_MODEL), F32)],
        compiler_params=pltpu.CompilerParams(dimension_semantics=("arbitrary",)),
        name="final_norm",
    )(x, w)


def kernel(x_prompt, x_sample, c, cache_k, cache_v, state_hgrn, c_ctx, norm1_w, norm2_w, w_ada, b_ada, w_in, hg_lb, hg_norm_w, lam_q1, lam_k1, lam_q2, lam_k2, da_norm_w, w_out, w_router, b_router, w_gu, b_gu, w_dn, b_dn, norm_f_w):
    lb_all = jnp.cumsum(jax.nn.softmax(hg_lb.astype(F32), axis=0), axis=0)
    lb_all = lb_all - lb_all[0:1]
    lam_init = [0.8 - 0.6 * math.exp(-0.3 * l) for l in range(DEPTH)]
    lam = (jnp.exp(jnp.sum(lam_q1 * lam_k1, axis=-1)) - jnp.exp(jnp.sum(lam_q2 * lam_k2, axis=-1))
           + jnp.asarray(lam_init, F32))
    cos_all, sin_all = _rope_tables()

    conds = jnp.concatenate([c_ctx[None, :], c, jnp.zeros((COND_ROWS - 1 - DEC_BATCH, D_MODEL), F32)], axis=0)
    mods = _ada_mod(conds, w_ada, b_ada)
    mods = mods[:, :N_COND].reshape(DEPTH, N_COND, 1, 6 * D_MODEL)

    x = jnp.concatenate([x_prompt.reshape(N_CTX_TOK, D_MODEL), x_sample.reshape(-1, D_MODEL)], axis=0)
    new_k = jnp.zeros((BATCH, DEPTH, SEQ, W512), F32)
    new_v = jnp.zeros((BATCH, DEPTH, SEQ, W512), F32)
    new_state = jnp.zeros((BATCH, DEPTH, 2, HG_HEADS, HG_DK, HG_DV), F32)
    for l in range(DEPTH):
        hg, g_hg, q_r, k_r, v_b, knorm, new_k, new_v = _inproj(x, norm1_w[l][None, :], mods[l], w_in,
                                                               cos_all, sin_all, new_k, new_v, l)
        o_fw, new_state = _hgrn_dir(hg, g_hg, None, lb_all[l], state_hgrn, None, new_state, l, rev=False)
        o_hg, new_state = _hgrn_dir(hg, g_hg, o_fw, lb_all[l], state_hgrn, hg_norm_w[l][None, :], new_state, l,
                                    rev=True)
        scal = jnp.stack([lam[l], jnp.asarray(1.0 - lam_init[l], F32)])
        o_da_ctx, o_da_lat = _attention(scal, q_r, k_r, v_b, knorm, cache_k, cache_v, da_norm_w[l][None, :], l)
        x_mid, h2, logits = _outproj(o_hg, o_da_ctx, o_da_lat, x, mods[l], w_out, norm2_w[l][None, :],
                                     w_router[l], b_router[l][None, :], l)
        tok_sorted, blk_src, gates, dest, blk_exp, n_used = _route(logits)
        ys = _moe_experts(blk_exp, n_used, blk_src, tok_sorted, h2, w_gu, b_gu, w_dn, b_dn, l)
        ys_tok = ys.at[dest].get(mode="promise_in_bounds")
        x = _combine(ys_tok, gates, x_mid, mods[l])
    y_ctx, y_lat = _final_norm(x, norm_f_w[None, :])
    return (y_ctx.reshape(BATCH, SEQ, D_MODEL), y_lat.reshape(DEC_BATCH, DEC_SEQ, D_MODEL),
            new_k.reshape(BATCH, DEPTH, SEQ, DA_HEADS, 2 * DA_QK), new_v.reshape(BATCH, DEPTH, SEQ, DA_HEADS, DA_V),
            new_state)
```

```python
import functools
import math

import jax
import jax.numpy as jnp
from jax import lax
from jax.experimental import pallas as pl
from jax.experimental.pallas import tpu as pltpu

F32 = jnp.float32
BF16 = jnp.bfloat16

D_MODEL = 1024
BATCH = 16
SEQ = 256
DEPTH = 4
DEC_BATCH = 2
DEC_SEQ = 4096
PAST_LEN = 256
GRID_W = 64
HG_HEADS = 4
HG_DK = 128
HG_DV = 128
HG_WIDTH = HG_HEADS * HG_DK
DA_HEADS = 4
DA_QK = 64
DA_V = 2 * DA_QK
DA_WIDTH = DA_HEADS * DA_V
ROPE_BASE = 10000.0
IN_DIM = 5 * HG_WIDTH + 3 * DA_WIDTH
N_EXPERTS = 32
TOP_K = 4
D_FF = D_MODEL
SWIGLU_ALPHA = 1.702
SWIGLU_LIMIT = 7.0
EPS = 1e-6

N_CTX_TOK = BATCH * SEQ
N_TOK = N_CTX_TOK + DEC_BATCH * DEC_SEQ
COND_GROUP = DEC_SEQ
assert N_CTX_TOK == COND_GROUP
N_COND = N_TOK // COND_GROUP
COND_ROWS = 8

COL_Q_HG, COL_F_FW, COL_F_BW, COL_I_HG, COL_G_HG, COL_Q_DA, COL_K_DA, COL_V_DA = range(8)
W512 = 512
LANES = 128

HG_CHUNK = 64
HG_HALF = HG_CHUNK // 2
HG_ROWS = 256
HG_EXP_CLAMP = 80.0

TM = 512
TN_IN = 2048
ATT_TQ = 256
Q_SCALE = math.log2(math.e) / math.sqrt(DA_QK)
ATT_KC = 1024
ATT_MIN_SUM = 2.0 ** -40
MOE_BLK = 512
N_ASG = N_TOK * TOP_K
MOE_NBLK = N_ASG // MOE_BLK + N_EXPERTS
VMEM_LIMIT = 56 * 1024 * 1024


def _dot(a, b):
    return jnp.dot(a, b, preferred_element_type=F32)


def _dot_nt(a, b):
    return lax.dot_general(a, b, (((1,), (1,)), ((), ())), preferred_element_type=F32)


def _dot_tn(a, b):
    return lax.dot_general(a, b, (((0,), (0,)), ((), ())), preferred_element_type=F32)


def _split2(x):
    hi = x.astype(BF16)
    lo = (x - hi.astype(F32)).astype(BF16)
    return hi, lo


def _split3(x):
    hi = x.astype(BF16)
    r = x - hi.astype(F32)
    mid = r.astype(BF16)
    lo = (r - mid.astype(F32)).astype(BF16)
    return hi, mid, lo


def _dot_precise(a, b):
    a_hi, a_lo = _split2(a)
    b_hi, b_lo = _split2(b)
    return _dot(a_hi, b_hi) + (_dot(a_hi, b_lo) + _dot(a_lo, b_hi))


def _pack_rows(x):
    n = x.shape[1] // 2
    lo = pltpu.bitcast(x[:, :n].astype(BF16).astype(F32), jnp.uint32)
    hi = pltpu.bitcast(x[:, n:].astype(BF16).astype(F32), jnp.uint32)
    return (lo >> 16) | hi


def _unpack_rows(u):
    lo = pltpu.bitcast(u << 16, F32)
    hi = pltpu.bitcast(u & jnp.uint32(0xFFFF0000), F32)
    return lo, hi


def _rms(x, w):
    ms = jnp.mean(x * x, axis=-1, keepdims=True)
    return x * lax.rsqrt(ms + EPS) * w


def _sigmoid(x):
    return 1.0 / (1.0 + jnp.exp(-x))


def _ada_kernel(cond_ref, w_ref, b_ref, o_ref):
    cnd = cond_ref[...]
    s = cnd * _sigmoid(cnd)
    o_ref[...] = _dot_precise(s, w_ref[...]) + b_ref[...]


def _ada_mod(conds, w_ada, b_ada):
    nj = 6 * D_MODEL // D_MODEL
    return pl.pallas_call(
        _ada_kernel,
        grid=(DEPTH, nj),
        in_specs=[
            pl.BlockSpec((COND_ROWS, D_MODEL), lambda l, j: (0, 0)),
            pl.BlockSpec((None, D_MODEL, D_MODEL), lambda l, j: (l, 0, j)),
            pl.BlockSpec((None, 1, D_MODEL), lambda l, j: (l, 0, j)),
        ],
        out_specs=pl.BlockSpec((None, COND_ROWS, D_MODEL), lambda l, j: (l, 0, j)),
        out_shape=jax.ShapeDtypeStruct((DEPTH, COND_ROWS, 6 * D_MODEL), F32),
        compiler_params=pltpu.CompilerParams(dimension_semantics=("arbitrary", "arbitrary"),
                                             vmem_limit_bytes=VMEM_LIMIT),
        name="ada_mod",
    )(conds, w_ada, b_ada.reshape(DEPTH, 1, 6 * D_MODEL))


SEQ_PER_TILE = TM // SEQ
N_CTX_TILES = N_CTX_TOK // TM
KV_COL_BLK = (COL_K_DA * W512) // TN_IN
assert COL_K_DA * W512 == KV_COL_BLK * TN_IN + TN_IN - 2 * W512


assert IN_DIM // TN_IN == 2 and KV_COL_BLK == 1 and COL_G_HG * W512 == TN_IN


def _inproj_kernel(x_ref, nw_ref, mod_ref, w_ref, cos_ref, sin_ref, kin_ref, vin_ref,
                   hg_ref, g_ref, qo_ref, ko_ref, vo_ref, kn_ref, kc_ref, vc_ref, wbf_ref):
    del kin_ref, vin_ref
    n = pl.program_id(0)
    i = pl.program_id(1)

    @pl.when(i == 0)
    def _():
        wbf_ref[...] = w_ref[...].astype(BF16)

    x = x_ref[...]
    sh = mod_ref[:, 0:D_MODEL]
    sc = mod_ref[:, D_MODEL:2 * D_MODEL]
    h = _rms(x, nw_ref[...]) * (1.0 + sc) + sh
    o = _dot(h.astype(BF16), wbf_ref[...])

    @pl.when(n == 0)
    def _():
        hg_ref[...] = o

    @pl.when(n == 1)
    def _():
        g_ref[...] = o[:, 0:W512]
        cos = cos_ref[...]
        sin = sin_ref[...]
        lane = lax.broadcasted_iota(jnp.int32, (1, LANES), 1)
        first = (lane % (DA_QK // 2)) < (DA_QK // 4)
        lane_n = lax.broadcasted_iota(jnp.int32, (1, 2 * DA_HEADS), 1)

        def rope(t):
            partner = jnp.where(first, pltpu.roll(t, LANES - DA_QK // 4, 1), pltpu.roll(t, DA_QK // 4, 1))
            return t * cos + partner * sin

        kn = jnp.zeros((TM, 2 * DA_HEADS), F32)
        ones = jnp.ones((TM, DA_V), BF16)
        for hd in range(DA_HEADS):
            qs = slice(W512 + hd * DA_V, W512 + (hd + 1) * DA_V)
            ks = slice(2 * W512 + hd * DA_V, 2 * W512 + (hd + 1) * DA_V)
            vs = slice(3 * W512 + hd * DA_V, 3 * W512 + (hd + 1) * DA_V)
            qo_ref[hd] = (rope(o[:, qs]) * Q_SCALE).astype(BF16)
            kr = rope(o[:, ks])
            ko_ref[hd] = kr.astype(BF16)
            ksq = kr * kr
            n1 = jnp.sum(jnp.where(lane < DA_QK, ksq, 0.0), axis=-1, keepdims=True)
            n2 = jnp.sum(jnp.where(lane >= DA_QK, ksq, 0.0), axis=-1, keepdims=True)
            kn = jnp.where(lane_n == 2 * hd, n1, jnp.where(lane_n == 2 * hd + 1, n2, kn))
            vo_ref[hd, :, 0:DA_V] = o[:, vs].astype(BF16)
            vo_ref[hd, :, DA_V:] = ones
        kn_ref[...] = kn

    @pl.when(jnp.logical_and(n == 1, i < N_CTX_TILES))
    def _():
        for s in range(SEQ_PER_TILE):
            kc_ref[s] = o[s * SEQ:(s + 1) * SEQ, 2 * W512:3 * W512]
            vc_ref[s] = o[s * SEQ:(s + 1) * SEQ, 3 * W512:]


def _inproj(x, norm_w_l, mods_l, w_in, cos_all, sin_all, new_k, new_v, l):
    n_i = N_TOK // TM

    def second(n, i):
        return jnp.where(n == 1, i, 0)

    def cache_idx(n, i):
        return (jnp.where(n == 1, jnp.minimum(i, N_CTX_TILES - 1), 0), l, 0, 0)

    cache_spec = pl.BlockSpec((SEQ_PER_TILE, None, SEQ, W512), cache_idx)
    cache_shape = jax.ShapeDtypeStruct((BATCH, DEPTH, SEQ, W512), F32)
    tab = pl.BlockSpec((TM, LANES), lambda n, i: (second(n, i), 0))
    head_spec = pl.BlockSpec((DA_HEADS, TM, DA_V), lambda n, i: (0, second(n, i), 0))
    head_shape = jax.ShapeDtypeStruct((DA_HEADS, N_TOK, DA_V), BF16)
    return pl.pallas_call(
        _inproj_kernel,
        grid=(2, n_i),
        in_specs=[
            pl.BlockSpec((TM, D_MODEL), lambda n, i: (i, 0)),
            pl.BlockSpec((1, D_MODEL), lambda n, i: (0, 0)),
            pl.BlockSpec((None, 1, 6 * D_MODEL), lambda n, i: (i * TM // COND_GROUP, 0, 0)),
            pl.BlockSpec((None, D_MODEL, TN_IN), lambda n, i: (l, 0, n)),
            tab, tab,
            pl.BlockSpec(memory_space=pl.ANY),
            pl.BlockSpec(memory_space=pl.ANY),
        ],
        out_specs=[pl.BlockSpec((TM, TN_IN), lambda n, i: (jnp.where(n == 0, i, n_i - 1), 0)),
                   pl.BlockSpec((TM, W512), lambda n, i: (second(n, i), 0)),
                   head_spec, head_spec,
                   pl.BlockSpec((DA_HEADS, TM, 2 * DA_V), lambda n, i: (0, second(n, i), 0)),
                   pl.BlockSpec((TM, 2 * DA_HEADS), lambda n, i: (second(n, i), 0)),
                   cache_spec, cache_spec],
        out_shape=[jax.ShapeDtypeStruct((N_TOK, TN_IN), F32),
                   jax.ShapeDtypeStruct((N_TOK, W512), F32),
                   head_shape, head_shape,
                   jax.ShapeDtypeStruct((DA_HEADS, N_TOK, 2 * DA_V), BF16),
                   jax.ShapeDtypeStruct((N_TOK, 2 * DA_HEADS), F32),
                   cache_shape, cache_shape],
        input_output_aliases={6: 6, 7: 7},
        scratch_shapes=[pltpu.VMEM((D_MODEL, TN_IN), BF16)],
        compiler_params=pltpu.CompilerParams(dimension_semantics=("arbitrary", "arbitrary"),
                                             vmem_limit_bytes=VMEM_LIMIT),
        name="norm_inproj",
    )(x, norm_w_l, mods_l, w_in, cos_all, sin_all, new_k, new_v)


N_HG_BLK = N_TOK // HG_ROWS
N_HG_CTX_BLK = N_CTX_TOK // HG_ROWS
HG_BPS_CTX = SEQ // HG_ROWS
HG_BPS_LAT = DEC_SEQ // HG_ROWS


def _hg_block(step, rev):
    return (N_HG_BLK - 1 - step) if rev else step


def _hg_lat_seq(p):
    return jnp.clip((p - N_HG_CTX_BLK) // HG_BPS_LAT, 0, DEC_BATCH - 1)


def _hg_ctx_seq(p):
    return jnp.minimum(p // HG_BPS_CTX, BATCH - 1)


def _hgrn_kernel(*refs, rev):
    if rev:
        (q_ref, f_ref, v_ref, g_ref, ofw_ref, lb_ref, s0_ref, nw_ref, _, o_ref, sfin_ref,
         st_ref, a_ref, k_ref) = refs
    else:
        (q_ref, f_ref, v_ref, lb_ref, s0_ref, _, o_ref, sfin_ref, st_ref, a_ref, k_ref) = refs
    p = _hg_block(pl.program_id(0), rev)
    is_ctx = p < N_HG_CTX_BLK
    pos = jnp.where(is_ctx, p % HG_BPS_CTX, (p - N_HG_CTX_BLK) % HG_BPS_LAT)
    bps = jnp.where(is_ctx, HG_BPS_CTX, HG_BPS_LAT)
    first_pos = (bps - 1) if rev else 0
    last_pos = 0 if rev else (bps - 1)

    @pl.when(jnp.logical_and(pos == first_pos, is_ctx))
    def _():
        st_ref[...] = jnp.zeros_like(st_ref)

    @pl.when(jnp.logical_and(pos == first_pos, jnp.logical_not(is_ctx)))
    def _():
        for h in range(HG_HEADS):
            st_ref[h] = s0_ref[h].T

    lb = lb_ref[...]
    oml = 1.0 - lb
    log_oml = jnp.log1p(-lb)
    row = lax.broadcasted_iota(jnp.int32, (HG_CHUNK, HG_CHUNK), 0)
    col = lax.broadcasted_iota(jnp.int32, (HG_CHUNK, HG_CHUNK), 1)
    tri = ((col >= row) if rev else (col <= row)).astype(BF16)
    same_half = (row >= HG_HALF) == (col >= HG_HALF)
    if rev:
        m_diag = jnp.logical_and(same_half, col >= row)
        m_off = jnp.logical_and(row < HG_HALF, col >= HG_HALF)
    else:
        m_diag = jnp.logical_and(same_half, col <= row)
        m_off = jnp.logical_and(row >= HG_HALF, col < HG_HALF)
    row1 = lax.broadcasted_iota(jnp.int32, (HG_CHUNK, 1), 0)
    n_chunks = HG_ROWS // HG_CHUNK

    def mid_rows(a):
        return jnp.where(row1 < HG_HALF, a[HG_HALF // 2:HG_HALF // 2 + 1, :],
                         a[HG_HALF + HG_HALF // 2:HG_HALF + HG_HALF // 2 + 1, :])

    def gates(c):
        rows = slice(c * HG_CHUNK, (c + 1) * HG_CHUNK)
        fx = f_ref[rows, :]
        e = jnp.exp(-jnp.abs(fx))
        r = 1.0 / (1.0 + e)
        er = e * r
        nonneg = fx >= 0.0
        sig = jnp.where(nonneg, r, er)
        nsig = jnp.where(nonneg, er, r)
        f = lb + oml * sig
        logf = jnp.maximum(jnp.log(f), log_oml + (jnp.minimum(fx, 0.0) - jnp.log1p(e)))
        k = oml * nsig
        hi, mid, lo = _split3(logf)
        a = _dot(tri, hi) + (_dot(tri, mid) + _dot(tri, lo))
        a_ref[rows, :] = a
        k_ref[rows, :] = k
        return jnp.abs(a - mid_rows(a))

    def emit(rows, hs, o_h):
        if rev:
            tot = ofw_ref[rows, hs] + o_h
            g = g_ref[rows, hs]
            y = _rms(tot, nw_ref[...]) * (g * _sigmoid(g))
            o_ref[rows, hs] = y.astype(o_ref.dtype)
        else:
            o_ref[rows, hs] = o_h

    def chunk(c, exact):
        static = isinstance(c, int)
        r0 = c * HG_CHUNK if static else pl.multiple_of(c * HG_CHUNK, HG_CHUNK)
        rows = slice(r0, r0 + HG_CHUNK) if static else pl.ds(r0, HG_CHUNK)
        q = q_ref[rows, :]
        a = a_ref[rows, :]
        k = k_ref[rows, :]
        if rev:
            a_b = a[HG_HALF:HG_HALF + 1, :]
            a_last = a[0:1, :]
        else:
            a_b = a[HG_HALF - 1:HG_HALF, :]
            a_last = a[HG_CHUNK - 1:HG_CHUNK, :]
        qa = (q * jnp.exp(a)).astype(BF16)
        kdec = (k * jnp.exp(a_last - a)).astype(BF16)
        dec = jnp.exp(a_last)
        vb = v_ref[rows, :].astype(BF16)
        if exact:
            def key_row(s, attns):
                a_s = a_ref[pl.ds(r0 + s, 1), :]
                k_s = k_ref[pl.ds(r0 + s, 1), :]
                w = q * k_s * jnp.exp(jnp.minimum(a - a_s, 0.0))
                return tuple(
                    jnp.where(col == s, jnp.sum(w[:, h * HG_DK:(h + 1) * HG_DK], axis=-1, keepdims=True), attns[h])
                    for h in range(HG_HEADS))

            zero = jnp.zeros((HG_CHUNK, HG_CHUNK), F32)
            attns = lax.fori_loop(0, HG_CHUNK, key_row, (zero,) * HG_HEADS)
            m_all = jnp.logical_or(m_diag, m_off)
        else:
            a_mid = mid_rows(a)
            qd = (q * jnp.exp(jnp.minimum(a - a_mid, HG_EXP_CLAMP))).astype(BF16)
            kd = (k * jnp.exp(jnp.minimum(a_mid - a, HG_EXP_CLAMP))).astype(BF16)
            qo = (q * jnp.exp(jnp.minimum(a - a_b, 0.0))).astype(BF16)
            ko = (k * jnp.exp(jnp.minimum(a_b - a, 0.0))).astype(BF16)
        for h in range(HG_HEADS):
            hs = slice(h * HG_DK, (h + 1) * HG_DK)
            if exact:
                attn = jnp.where(m_all, attns[h], 0.0).astype(BF16)
            else:
                p1 = _dot_nt(qd[:, hs], kd[:, hs])
                p2 = _dot_nt(qo[:, hs], ko[:, hs])
                attn = jnp.where(m_diag, p1, jnp.where(m_off, p2, 0.0)).astype(BF16)
            st = st_ref[h]
            o_h = _dot_nt(qa[:, hs], st.astype(BF16)) + _dot(attn, vb[:, hs])
            st_ref[h] = dec[:, hs] * st + _dot_tn(vb[:, hs], kdec[:, hs])
            emit(rows, hs, o_h)

    spread = gates(0)
    for c in range(1, n_chunks):
        spread = jnp.maximum(spread, gates(c))
    safe = jnp.max(spread) <= HG_EXP_CLAMP

    @pl.when(safe)
    def _():
        for ci in range(n_chunks):
            chunk((n_chunks - 1 - ci) if rev else ci, exact=False)

    @pl.when(jnp.logical_not(safe))
    def _():
        def body(ci, carry):
            chunk((n_chunks - 1 - ci) if rev else ci, exact=True)
            return carry

        lax.fori_loop(0, n_chunks, body, 0)

    @pl.when(jnp.logical_and(pos == last_pos, is_ctx))
    def _():
        for h in range(HG_HEADS):
            sfin_ref[h] = st_ref[h].T


def _hgrn_dir(hg, g_hg, o_fw, lb_l, state_hgrn, hg_norm_l, new_state, l, rev):
    d = 1 if rev else 0

    def tok(colblk):
        return pl.BlockSpec((HG_ROWS, W512), lambda s: (_hg_block(s, rev), colblk))

    in_specs = [tok(COL_Q_HG), tok(COL_F_BW if rev else COL_F_FW), tok(COL_I_HG)]
    args = [hg, hg, hg]
    if rev:
        in_specs += [tok(0), tok(0)]
        args += [g_hg, o_fw]
    in_specs += [
        pl.BlockSpec((None, 1, W512), lambda s: (d, 0, 0)),
        pl.BlockSpec((None, None, None, HG_HEADS, HG_DK, HG_DV),
                     lambda s: (_hg_lat_seq(_hg_block(s, rev)), l, d, 0, 0, 0)),
    ]
    args += [lb_l.reshape(2, 1, W512), state_hgrn]
    if rev:
        in_specs.append(pl.BlockSpec((1, HG_DV), lambda s: (0, 0)))
        args.append(hg_norm_l)
    in_specs.append(pl.BlockSpec(memory_space=pl.ANY))
    args.append(new_state)
    return pl.pallas_call(
        functools.partial(_hgrn_kernel, rev=rev),
        grid=(N_HG_BLK,),
        in_specs=in_specs,
        out_specs=[
            pl.BlockSpec((HG_ROWS, W512), lambda s: (_hg_block(s, rev), 0)),
            pl.BlockSpec((None, None, None, HG_HEADS, HG_DK, HG_DV),
                         lambda s: (_hg_ctx_seq(_hg_block(s, rev)), l, d, 0, 0, 0)),
        ],
        out_shape=[
            jax.ShapeDtypeStruct((N_TOK, W512), BF16 if rev else F32),
            jax.ShapeDtypeStruct((BATCH, DEPTH, 2, HG_HEADS, HG_DK, HG_DV), F32),
        ],
        input_output_aliases={len(args) - 1: 1},
        scratch_shapes=[pltpu.VMEM((HG_HEADS, HG_DV, HG_DK), F32),
                        pltpu.VMEM((HG_ROWS, W512), F32), pltpu.VMEM((HG_ROWS, W512), F32)],
        compiler_params=pltpu.CompilerParams(dimension_semantics=("arbitrary",),
                                             vmem_limit_bytes=VMEM_LIMIT),
        name="hgrn_bwd" if rev else "hgrn_fwd",
    )(*args)


def _rope_tables():
    n_pairs = DA_QK // 4
    inv_freq = ROPE_BASE ** (-jnp.arange(n_pairs, dtype=F32) / n_pairs)
    t = jnp.arange(DEC_SEQ)
    row_idx = (t // GRID_W).astype(F32)
    col_idx = (t % GRID_W).astype(F32)
    lane = jnp.arange(LANES)
    jj = lane % DA_QK
    use_col = (jj // (DA_QK // 2)) == 1
    second = (jj % (DA_QK // 2)) >= n_pairs
    fr = inv_freq[jj % n_pairs]
    ang_r = row_idx[:, None] * inv_freq[None, :]
    ang_c = col_idx[:, None] * inv_freq[None, :]
    cos_r, sin_r, cos_c, sin_c = jnp.cos(ang_r), jnp.sin(ang_r), jnp.cos(ang_c), jnp.sin(ang_c)
    fi = jj % n_pairs
    cos_t = jnp.where(use_col[None, :], cos_c[:, fi], cos_r[:, fi])
    sin_t = jnp.where(use_col[None, :], sin_c[:, fi], sin_r[:, fi])
    sin_t = jnp.where(second[None, :], sin_t, -sin_t)
    del fr
    cos_all = jnp.concatenate([jnp.ones((N_CTX_TOK, LANES), F32)] + [cos_t] * DEC_BATCH, axis=0)
    sin_all = jnp.concatenate([jnp.zeros((N_CTX_TOK, LANES), F32)] + [sin_t] * DEC_BATCH, axis=0)
    return cos_all, sin_all


def _attn_kernel(*refs, has_ctx):
    if has_ctx:
        scal_ref, kmax_ref, q_ref, k_ref, v_ref, kc_ref, vc_ref, nw_ref, o_ref = refs
    else:
        scal_ref, kmax_ref, q_ref, k_ref, v_ref, nw_ref, o_ref = refs
    lam = scal_ref[0]
    out_scale = scal_ref[1]
    lane = lax.broadcasted_iota(jnp.int32, (1, LANES), 1)
    n_heads = q_ref.shape[0]
    n_keys = k_ref.shape[1]
    chunk = min(ATT_KC, n_keys)
    b = pl.program_id(0)
    if has_ctx:
        kc = kc_ref[...].astype(BF16)
        vc = vc_ref[...].astype(BF16)
        vc_ext = jnp.concatenate([vc, jnp.ones((PAST_LEN, DA_V), BF16)], axis=1)

    def shifted(q_c, h, shift):
        acc = None
        for c in range(n_keys // chunk):
            ks = slice(c * chunk, (c + 1) * chunk)
            p = jnp.exp2(_dot_nt(q_c, k_ref[h, ks, :]) - shift).astype(BF16)
            part = _dot(p, v_ref[h, ks, :])
            acc = part if acc is None else acc + part
        if has_ctx:
            acc = acc + _dot(jnp.exp2(_dot_nt(q_c, kc) - shift).astype(BF16), vc_ext)
        l = acc[:, DA_V:DA_V + 1]
        return acc[:, :DA_V] / l, l

    def exact(q_c, h):
        s = _dot_nt(q_c, k_ref[h])
        m = jnp.max(s, axis=-1, keepdims=True)
        if has_ctx:
            s_c = _dot_nt(q_c, kc)
            m = jnp.maximum(m, jnp.max(s_c, axis=-1, keepdims=True))
        p = jnp.exp2(s - m)
        l = jnp.sum(p, axis=-1, keepdims=True)
        acc = _dot(p.astype(BF16), v_ref[h, :, 0:DA_V])
        if has_ctx:
            p_c = jnp.exp2(s_c - m)
            l = l + jnp.sum(p_c, axis=-1, keepdims=True)
            acc = acc + _dot(p_c.astype(BF16), vc)
        return acc / l

    def finish(o1, o2):
        return (_rms(o1 - lam * o2, nw_ref[...]) * out_scale).astype(o_ref.dtype)

    for h in range(n_heads):
        head = h if n_heads > 1 else pl.program_id(1)
        cols = slice(None) if n_heads == 1 else slice(h * DA_V, (h + 1) * DA_V)
        qh = q_ref[h]
        zero = jnp.zeros_like(qh)
        qs = (jnp.where(lane < DA_QK, qh, zero), jnp.where(lane >= DA_QK, qh, zero))
        outs, sums = [], []
        for comp, q_c in enumerate(qs):
            qf = q_c.astype(F32)
            qn = jnp.sqrt(jnp.sum(qf * qf, axis=-1, keepdims=True))
            shift = qn * (kmax_ref[(b * DA_HEADS + head) * 2 + comp] * 1.01) + 1e-3
            o_c, l_c = shifted(q_c, h, shift)
            outs.append(o_c)
            sums.append(l_c)
        o_ref[:, cols] = finish(outs[0], outs[1])
        ok = jnp.min(jnp.minimum(sums[0], sums[1])) >= ATT_MIN_SUM

        @pl.when(jnp.logical_not(ok))
        def _():
            o_ref[:, cols] = finish(exact(qs[0], h), exact(qs[1], h))


def _attention(scal, q, k, v, knorm, cache_k, cache_v, da_norm_l, l):
    smem = pl.BlockSpec(memory_space=pltpu.SMEM)
    nw = pl.BlockSpec((1, DA_V), lambda *a: (0, 0))
    kmax_ctx = jnp.sqrt(jnp.max(knorm[:N_CTX_TOK].reshape(BATCH, SEQ, 2 * DA_HEADS), axis=1)).reshape(-1)
    ckn = jnp.sum(jnp.square(cache_k[:, l].reshape(DEC_BATCH, PAST_LEN, 2 * DA_HEADS, DA_QK)), axis=-1)
    kmax_lat = jnp.sqrt(jnp.maximum(jnp.max(knorm[N_CTX_TOK:].reshape(DEC_BATCH, DEC_SEQ, 2 * DA_HEADS), axis=1),
                                    jnp.max(ckn, axis=1))).reshape(-1)
    blk = pl.BlockSpec((DA_HEADS, SEQ, DA_V), lambda b: (0, b, 0))
    o_ctx = pl.pallas_call(
        functools.partial(_attn_kernel, has_ctx=False),
        grid=(BATCH,),
        in_specs=[smem, smem, blk, blk, pl.BlockSpec((DA_HEADS, SEQ, 2 * DA_V), lambda b: (0, b, 0)), nw],
        out_specs=pl.BlockSpec((SEQ, W512), lambda b: (b, 0)),
        out_shape=jax.ShapeDtypeStruct((N_CTX_TOK, W512), BF16),
        compiler_params=pltpu.CompilerParams(dimension_semantics=("arbitrary",), vmem_limit_bytes=VMEM_LIMIT),
        name="attn_ctx",
    )(scal, kmax_ctx, q, k, v, da_norm_l)
    nq = DEC_SEQ // ATT_TQ
    off_q = N_CTX_TOK // ATT_TQ
    off_k = N_CTX_TOK // DEC_SEQ
    ck = cache_k.reshape(DEC_BATCH, DEPTH, PAST_LEN, W512)
    cv = cache_v.reshape(DEC_BATCH, DEPTH, PAST_LEN, W512)
    ckv = pl.BlockSpec((None, None, PAST_LEN, DA_V), lambda b, h, i: (b, l, 0, h))
    o_lat = pl.pallas_call(
        functools.partial(_attn_kernel, has_ctx=True),
        grid=(DEC_BATCH, DA_HEADS, nq),
        in_specs=[smem, smem,
                  pl.BlockSpec((1, ATT_TQ, DA_V), lambda b, h, i: (h, off_q + b * nq + i, 0)),
                  pl.BlockSpec((1, DEC_SEQ, DA_V), lambda b, h, i: (h, off_k + b, 0)),
                  pl.BlockSpec((1, DEC_SEQ, 2 * DA_V), lambda b, h, i: (h, off_k + b, 0)),
                  ckv, ckv, nw],
        out_specs=pl.BlockSpec((ATT_TQ, DA_V), lambda b, h, i: (b * nq + i, h)),
        out_shape=jax.ShapeDtypeStruct((DEC_BATCH * DEC_SEQ, W512), BF16),
        compiler_params=pltpu.CompilerParams(dimension_semantics=("arbitrary", "arbitrary", "arbitrary"),
                                             vmem_limit_bytes=VMEM_LIMIT),
        name="attn_lat",
    )(scal, kmax_lat, q, k, v, ck, cv, da_norm_l)
    return o_ctx, o_lat


def _outproj_kernel(hg_ref, dac_ref, dal_ref, x_ref, mod_ref, w_ref, nw_ref, wr_ref, br_ref,
                    xo_ref, h_ref, lg_ref, wbf_ref):
    i = pl.program_id(0)

    @pl.when(i == 0)
    def _():
        wbf_ref[...] = w_ref[...].astype(BF16)

    is_ctx = i < N_CTX_TOK // TM
    da = jnp.where(is_ctx, dac_ref[...], dal_ref[...])
    mix = _dot(hg_ref[...], wbf_ref[0:HG_WIDTH, :]) + _dot(da, wbf_ref[HG_WIDTH:, :])
    g1 = mod_ref[:, 2 * D_MODEL:3 * D_MODEL]
    sh2 = mod_ref[:, 3 * D_MODEL:4 * D_MODEL]
    sc2 = mod_ref[:, 4 * D_MODEL:5 * D_MODEL]
    x = x_ref[...] + g1 * mix
    xo_ref[...] = x
    h = _rms(x, nw_ref[...]) * (1.0 + sc2) + sh2
    h_ref[...] = _pack_rows(h)
    lg_ref[...] = _dot_precise(h, wr_ref[...]) + br_ref[...]


def _outproj(o_hg, o_da_ctx, o_da_lat, x, mods_l, w_out, norm2_l, w_router_l, b_router_l, l):
    n_ctx_t = N_CTX_TOK // TM
    n_lat_t = (N_TOK - N_CTX_TOK) // TM
    tokspec = lambda w: pl.BlockSpec((TM, w), lambda i: (i, 0))
    return pl.pallas_call(
        _outproj_kernel,
        grid=(N_TOK // TM,),
        in_specs=[
            tokspec(W512),
            pl.BlockSpec((TM, W512), lambda i: (jnp.minimum(i, n_ctx_t - 1), 0)),
            pl.BlockSpec((TM, W512), lambda i: (jnp.clip(i - n_ctx_t, 0, n_lat_t - 1), 0)),
            tokspec(D_MODEL),
            pl.BlockSpec((None, 1, 6 * D_MODEL), lambda i: (i * TM // COND_GROUP, 0, 0)),
            pl.BlockSpec((None, D_MODEL, D_MODEL), lambda i: (l, 0, 0)),
            pl.BlockSpec((1, D_MODEL), lambda i: (0, 0)),
            pl.BlockSpec((D_MODEL, N_EXPERTS), lambda i: (0, 0)),
            pl.BlockSpec((1, N_EXPERTS), lambda i: (0, 0)),
        ],
        out_specs=[tokspec(D_MODEL), tokspec(D_MODEL // 2), tokspec(N_EXPERTS)],
        out_shape=[jax.ShapeDtypeStruct((N_TOK, D_MODEL), F32),
                   jax.ShapeDtypeStruct((N_TOK, D_MODEL // 2), jnp.uint32),
                   jax.ShapeDtypeStruct((N_TOK, N_EXPERTS), F32)],
        scratch_shapes=[pltpu.VMEM((D_MODEL, D_MODEL), BF16)],
        compiler_params=pltpu.CompilerParams(dimension_semantics=("arbitrary",),
                                             vmem_limit_bytes=VMEM_LIMIT),
        name="outproj_norm_router",
    )(o_hg, o_da_ctx, o_da_lat, x, mods_l, w_out, norm2_l, w_router_l, b_router_l)


def _moe_kernel(be_ref, nu_ref, src_ref, tok_ref, h_hbm, wgu_ref, bgu_ref, wdn_ref, bdn_ref, o_ref,
                wgu_bf, wdn_bf, xbuf0, xbuf1, sem):
    i = pl.program_id(0)
    n_used = nu_ref[0]
    used = i < n_used
    odd = i % 2
    prev = be_ref[jnp.maximum(i - 1, 0)]
    fresh = jnp.logical_or(i == 0, be_ref[i] != prev)
    bufs = (xbuf0, xbuf1)

    def start_rows(blk, b):
        base = src_ref[blk]
        for r in range(MOE_BLK):
            pltpu.make_async_copy(h_hbm.at[pl.ds(tok_ref[base + r], 1), :], bufs[b].at[pl.ds(r, 1), :],
                                  sem.at[b]).start()

    def wait_rows(b):
        pltpu.make_async_copy(h_hbm.at[pl.ds(0, MOE_BLK), :], bufs[b], sem.at[b]).wait()

    @pl.when(i == 0)
    def _():
        start_rows(0, 0)

    @pl.when(jnp.logical_and(used, fresh))
    def _():
        wgu_bf[...] = wgu_ref[...].astype(BF16)
        wdn_bf[...] = wdn_ref[...].astype(BF16)

    def step(b):
        wait_rows(b)
        x_lo, x_hi = _unpack_rows(bufs[b][...])
        start_rows(jnp.minimum(i + 1, n_used - 1), 1 - b)
        x = jnp.concatenate([x_lo.astype(BF16), x_hi.astype(BF16)], axis=1)
        gu = _dot(x, wgu_bf[...]) + bgu_ref[...]
        gate = jnp.minimum(gu[:, :D_FF], SWIGLU_LIMIT)
        lin = jnp.clip(gu[:, D_FF:], -SWIGLU_LIMIT, SWIGLU_LIMIT)
        act = (lin + 1.0) * gate * _sigmoid(SWIGLU_ALPHA * gate)
        y = _dot(act.astype(BF16), wdn_bf[...]) + bdn_ref[...]
        o_ref[...] = _pack_rows(y)

    for b in range(2):
        @pl.when(jnp.logical_and(used, odd == b))
        def _():
            step(b)

        @pl.when(jnp.logical_and(i == n_used - 1, odd == b))
        def _():
            wait_rows(1 - b)

    @pl.when(jnp.logical_not(used))
    def _():
        o_ref[...] = jnp.zeros_like(o_ref)


def _moe_experts(blk_exp, n_used, blk_src, tok_sorted, h2, w_gu, b_gu, w_dn, b_dn, l):
    grid_spec = pltpu.PrefetchScalarGridSpec(
        num_scalar_prefetch=4,
        grid=(MOE_NBLK,),
        in_specs=[
            pl.BlockSpec(memory_space=pl.ANY),
            pl.BlockSpec((None, None, D_MODEL, 2 * D_FF), lambda i, be, nu, sr, tk: (l, be[i], 0, 0)),
            pl.BlockSpec((None, None, 1, 2 * D_FF), lambda i, be, nu, sr, tk: (l, be[i], 0, 0)),
            pl.BlockSpec((None, None, D_FF, D_MODEL), lambda i, be, nu, sr, tk: (l, be[i], 0, 0)),
            pl.BlockSpec((None, None, 1, D_MODEL), lambda i, be, nu, sr, tk: (l, be[i], 0, 0)),
        ],
        out_specs=pl.BlockSpec((MOE_BLK, D_MODEL // 2), lambda i, be, nu, sr, tk: (i, 0)),
        scratch_shapes=[pltpu.VMEM((D_MODEL, 2 * D_FF), BF16), pltpu.VMEM((D_FF, D_MODEL), BF16),
                        pltpu.VMEM((MOE_BLK, D_MODEL // 2), jnp.uint32),
                        pltpu.VMEM((MOE_BLK, D_MODEL // 2), jnp.uint32), pltpu.SemaphoreType.DMA((2,))],
    )
    return pl.pallas_call(
        _moe_kernel,
        grid_spec=grid_spec,
        out_shape=jax.ShapeDtypeStruct((MOE_NBLK * MOE_BLK, D_MODEL // 2), jnp.uint32),
        compiler_params=pltpu.CompilerParams(dimension_semantics=("arbitrary",),
                                             vmem_limit_bytes=VMEM_LIMIT),
        name="moe_experts",
    )(blk_exp, n_used, blk_src, tok_sorted, h2, w_gu, b_gu.reshape(DEPTH, N_EXPERTS, 1, 2 * D_FF),
      w_dn, b_dn.reshape(DEPTH, N_EXPERTS, 1, D_MODEL))


def _route_kernel(lg_ref, exp_ref, gate_ref, rank_ref, cnt_ref, run_ref):
    i = pl.program_id(0)

    @pl.when(i == 0)
    def _():
        run_ref[...] = jnp.zeros_like(run_ref)

    lg = lg_ref[...]
    lane = lax.broadcasted_iota(jnp.int32, (1, N_EXPERTS), 1)
    lane_k = lax.broadcasted_iota(jnp.int32, (1, TOP_K), 1)
    vals, idxs, hots = [], [], []
    for _ in range(TOP_K):
        m = jnp.max(lg, axis=-1, keepdims=True)
        idx = jnp.min(jnp.where(lg == m, lane, N_EXPERTS), axis=-1, keepdims=True)
        hot = lane == idx
        vals.append(m)
        idxs.append(idx)
        hots.append(hot)
        lg = jnp.where(hot, -jnp.inf, lg)
    exps = [jnp.exp(v - vals[0]) for v in vals]
    denom = exps[0] + exps[1] + exps[2] + exps[3]
    tokhot = (hots[0] | hots[1] | hots[2] | hots[3]).astype(BF16)
    r = lax.broadcasted_iota(jnp.int32, (TM, TM), 0)
    cidx = lax.broadcasted_iota(jnp.int32, (TM, TM), 1)
    before = _dot((cidx < r).astype(BF16), tokhot) + run_ref[...]
    e_out = jnp.zeros((TM, TOP_K), jnp.int32)
    g_out = jnp.zeros((TM, TOP_K), F32)
    r_out = jnp.zeros((TM, TOP_K), jnp.int32)
    for k in range(TOP_K):
        rk = jnp.sum(jnp.where(hots[k], before, 0.0), axis=-1, keepdims=True).astype(jnp.int32)
        e_out = jnp.where(lane_k == k, idxs[k], e_out)
        g_out = jnp.where(lane_k == k, exps[k] / denom, g_out)
        r_out = jnp.where(lane_k == k, rk, r_out)
    exp_ref[...] = e_out
    gate_ref[...] = g_out
    rank_ref[...] = r_out
    run_ref[...] += jnp.sum(tokhot.astype(F32), axis=0, keepdims=True)
    cnt_ref[...] = run_ref[...].astype(jnp.int32)


def _route(logits):
    tk = pl.BlockSpec((TM, TOP_K), lambda i: (i, 0))
    top_exp, gates, rank, counts = pl.pallas_call(
        _route_kernel,
        grid=(N_TOK // TM,),
        in_specs=[pl.BlockSpec((TM, N_EXPERTS), lambda i: (i, 0))],
        out_specs=[tk, tk, tk, pl.BlockSpec((1, N_EXPERTS), lambda i: (0, 0))],
        out_shape=[jax.ShapeDtypeStruct((N_TOK, TOP_K), jnp.int32),
                   jax.ShapeDtypeStruct((N_TOK, TOP_K), F32),
                   jax.ShapeDtypeStruct((N_TOK, TOP_K), jnp.int32),
                   jax.ShapeDtypeStruct((1, N_EXPERTS), jnp.int32)],
        scratch_shapes=[pltpu.VMEM((1, N_EXPERTS), F32)],
        compiler_params=pltpu.CompilerParams(dimension_semantics=("arbitrary",)),
        name="route",
    )(logits)
    counts = counts[0]
    padded = (counts + MOE_BLK - 1) // MOE_BLK * MOE_BLK
    pad_end = jnp.cumsum(padded)
    pad_start = pad_end - padded
    start = jnp.cumsum(counts) - counts
    eq = top_exp[:, :, None] == jnp.arange(N_EXPERTS, dtype=jnp.int32)[None, None, :]
    dest = (jnp.sum(jnp.where(eq, pad_start[None, None, :], 0), axis=-1) + rank).T.reshape(-1)
    blk_exp = jnp.minimum(
        jnp.sum((pad_end[None, :] <= (jnp.arange(MOE_NBLK, dtype=jnp.int32) * MOE_BLK)[:, None]).astype(jnp.int32),
                axis=-1), N_EXPERTS - 1).astype(jnp.int32)
    n_used = (pad_end[-1:] // MOE_BLK).astype(jnp.int32)
    tok_ids = jnp.arange(N_ASG, dtype=jnp.int32) // TOP_K
    _, tok_sorted = lax.sort((top_exp.reshape(-1), tok_ids), num_keys=1, is_stable=True)
    blk_src = (start[blk_exp] + jnp.arange(MOE_NBLK, dtype=jnp.int32) * MOE_BLK - pad_start[blk_exp]).astype(jnp.int32)
    tok_sorted = jnp.concatenate([tok_sorted, jnp.zeros((MOE_BLK,), jnp.int32)])
    return tok_sorted, blk_src, gates, dest, blk_exp, n_used


def _combine_kernel(y0_ref, y1_ref, y2_ref, y3_ref, gate_ref, x_ref, mod_ref, o_ref):
    g = gate_ref[...]
    half = D_MODEL // 2
    acc_lo = jnp.zeros((TM, half), F32)
    acc_hi = jnp.zeros((TM, half), F32)
    for k, y_ref in enumerate((y0_ref, y1_ref, y2_ref, y3_ref)):
        lo, hi = _unpack_rows(y_ref[...])
        acc_lo = acc_lo + g[:, k:k + 1] * lo
        acc_hi = acc_hi + g[:, k:k + 1] * hi
    g2 = mod_ref[:, 5 * D_MODEL:]
    o_ref[:, :half] = x_ref[:, :half] + g2[:, :half] * acc_lo
    o_ref[:, half:] = x_ref[:, half:] + g2[:, half:] * acc_hi


def _combine(ys_tok, gates, x_mid, mods_l):
    n_i = N_TOK // TM

    def ysblk(k):
        return pl.BlockSpec((TM, D_MODEL // 2), lambda i: (k * n_i + i, 0))

    return pl.pallas_call(
        _combine_kernel,
        grid=(n_i,),
        in_specs=[ysblk(0), ysblk(1), ysblk(2), ysblk(3),
                  pl.BlockSpec((TM, TOP_K), lambda i: (i, 0)),
                  pl.BlockSpec((TM, D_MODEL), lambda i: (i, 0)),
                  pl.BlockSpec((None, 1, 6 * D_MODEL), lambda i: (i * TM // COND_GROUP, 0, 0))],
        out_specs=pl.BlockSpec((TM, D_MODEL), lambda i: (i, 0)),
        out_shape=jax.ShapeDtypeStruct((N_TOK, D_MODEL), F32),
        compiler_params=pltpu.CompilerParams(dimension_semantics=("arbitrary",), vmem_limit_bytes=VMEM_LIMIT),
        name="moe_combine",
    )(ys_tok, ys_tok, ys_tok, ys_tok, gates, x_mid, mods_l)


def _final_norm_kernel(x_ref, w_ref, oc_ref, ol_ref):
    y = _rms(x_ref[...], w_ref[...])
    is_ctx = pl.program_id(0) < N_CTX_TILES

    @pl.when(is_ctx)
    def _():
        oc_ref[...] = y

    @pl.when(jnp.logical_not(is_ctx))
    def _():
        ol_ref[...] = y


def _final_norm(x, w):
    n_lat_tiles = N_TOK // TM - N_CTX_TILES
    return pl.pallas_call(
        _final_norm_kernel,
        grid=(N_TOK // TM,),
        in_specs=[pl.BlockSpec((TM, D_MODEL), lambda i: (i, 0)), pl.BlockSpec((1, D_MODEL), lambda i: (0, 0))],
        out_specs=[pl.BlockSpec((TM, D_MODEL), lambda i: (jnp.minimum(i, N_CTX_TILES - 1), 0)),
                   pl.BlockSpec((TM, D_MODEL), lambda i: (jnp.clip(i - N_CTX_TILES, 0, n_lat_tiles - 1), 0))],
        out_shape=[jax.ShapeDtypeStruct((N_CTX_TOK, D_MODEL), F32),
                   jax.ShapeDtypeStruct((N_TOK - N_CTX_TOK, D_MODEL), F32)],
        compiler_params=pltpu.CompilerParams(dimension_semantics=("arbitrary",)),
        name="final_norm",
    )(x, w)


def kernel(x_prompt, x_sample, c, cache_k, cache_v, state_hgrn, c_ctx, norm1_w, norm2_w, w_ada, b_ada, w_in, hg_lb, hg_norm_w, lam_q1, lam_k1, lam_q2, lam_k2, da_norm_w, w_out, w_router, b_router, w_gu, b_gu, w_dn, b_dn, norm_f_w):
    lb_all = jnp.cumsum(jax.nn.softmax(hg_lb.astype(F32), axis=0), axis=0)
    lb_all = lb_all - lb_all[0:1]
    lam_init = [0.8 - 0.6 * math.exp(-0.3 * l) for l in range(DEPTH)]
    lam = (jnp.exp(jnp.sum(lam_q1 * lam_k1, axis=-1)) - jnp.exp(jnp.sum(lam_q2 * lam_k2, axis=-1))
           + jnp.asarray(lam_init, F32))
    cos_all, sin_all = _rope_tables()

    conds = jnp.concatenate([c_ctx[None, :], c, jnp.zeros((COND_ROWS - 1 - DEC_BATCH, D_MODEL), F32)], axis=0)
    mods = _ada_mod(conds, w_ada, b_ada)
    mods = mods[:, :N_COND].reshape(DEPTH, N_COND, 1, 6 * D_MODEL)

    x = jnp.concatenate([x_prompt.reshape(N_CTX_TOK, D_MODEL), x_sample.reshape(-1, D_MODEL)], axis=0)
    new_k = jnp.zeros((BATCH, DEPTH, SEQ, W512), F32)
    new_v = jnp.zeros((BATCH, DEPTH, SEQ, W512), F32)
    new_state = jnp.zeros((BATCH, DEPTH, 2, HG_HEADS, HG_DK, HG_DV), F32)
    for l in range(DEPTH):
        hg, g_hg, q_r, k_r, v_b, knorm, new_k, new_v = _inproj(x, norm1_w[l][None, :], mods[l], w_in,
                                                               cos_all, sin_all, new_k, new_v, l)
        o_fw, new_state = _hgrn_dir(hg, g_hg, None, lb_all[l], state_hgrn, None, new_state, l, rev=False)
        o_hg, new_state = _hgrn_dir(hg, g_hg, o_fw, lb_all[l], state_hgrn, hg_norm_w[l][None, :], new_state, l,
                                    rev=True)
        scal = jnp.stack([lam[l], jnp.asarray(1.0 - lam_init[l], F32)])
        o_da_ctx, o_da_lat = _attention(scal, q_r, k_r, v_b, knorm, cache_k, cache_v, da_norm_w[l][None, :], l)
        x_mid, h2, logits = _outproj(o_hg, o_da_ctx, o_da_lat, x, mods[l], w_out, norm2_w[l][None, :],
                                     w_router[l], b_router[l][None, :], l)
        tok_sorted, blk_src, gates, dest, blk_exp, n_used = _route(logits)
        ys = _moe_experts(blk_exp, n_used, blk_src, tok_sorted, h2, w_gu, b_gu, w_dn, b_dn, l)
        ys_tok = ys.at[dest].get(mode="promise_in_bounds")
        x = _combine(ys_tok, gates, x_mid, mods[l])
    y_ctx, y_lat = _final_norm(x, norm_f_w[None, :])
    return (y_ctx.reshape(BATCH, SEQ, D_MODEL), y_lat.reshape(DEC_BATCH, DEC_SEQ, D_MODEL),
            new_k.reshape(BATCH, DEPTH, SEQ, DA_HEADS, 2 * DA_QK), new_v.reshape(BATCH, DEPTH, SEQ, DA_HEADS, DA_V),
            new_state)
```
